```python
import jax, jax.numpy as jnp
from jax import lax
import numpy as np

D_MODEL = 1024
BATCH = 8
SEQ = 4096
DEPTH = 1

MIX_WIDTH = D_MODEL
CONV_CH = MIX_WIDTH // 2
RWKV_CH = MIX_WIDTH - CONV_CH
RWKV_HEAD = 64
RWKV_HEADS = RWKV_CH // RWKV_HEAD
CONV_WIDTH = 31
LORA_W = 64
LORA_A = 64
LORA_G = 128
RWKV_PROJ = 3 * RWKV_CH + LORA_W + LORA_A + LORA_G
IN_PROJ = 2 * CONV_CH + RWKV_PROJ
RWKV_SPLITS = [RWKV_CH, 2 * RWKV_CH, 3 * RWKV_CH, 3 * RWKV_CH + LORA_W, 3 * RWKV_CH + LORA_W + LORA_A]
PEER_HEADS = 8
PEER_NKEYS = 128
PEER_EXPERTS = PEER_NKEYS * PEER_NKEYS
PEER_DQ = 256
PEER_TOPK = 16
PEER_CHUNK = 128
RMS_EPS = 1e-6
LN_EPS = 1e-5
GN_EPS = 64e-5

kernel_name = 'hymba_conv_rwkv7_peer_adaln_block'

F32 = jnp.float32


def rmsnorm(x, g):
    xf = x.astype(F32)
    y = xf * lax.rsqrt(jnp.mean(xf * xf, axis=-1, keepdims=True) + RMS_EPS)
    return (y * g.astype(F32)).astype(x.dtype)


def modulate(x, shift, scale):
    return x * (1 + scale[:, None, :]) + shift[:, None, :]


def conformer_conv(p_conv, w_dw, b_dw, ln_w, ln_b):
    a, b = jnp.split(p_conv, 2, axis=-1)
    y = a * jax.nn.sigmoid(b)
    y = lax.conv_general_dilated(
        y, w_dw[:, None, :], window_strides=(1,), padding=[(CONV_WIDTH - 1, 0)],
        dimension_numbers=('NWC', 'WIO', 'NWC'), feature_group_count=CONV_CH) + b_dw
    yf = y.astype(F32)
    mu = jnp.mean(yf, axis=-1, keepdims=True)
    var = jnp.mean(jnp.square(yf - mu), axis=-1, keepdims=True)
    yf = (yf - mu) * lax.rsqrt(var + LN_EPS) * ln_w.astype(F32) + ln_b.astype(F32)
    return jax.nn.silu(yf).astype(p_conv.dtype)


def token_shift(p, mu):
    prev = jnp.pad(p, ((0, 0), (1, 0), (0, 0)))[:, :-1]
    return p + mu * (prev - p)


def rwkv7_mix(p_rwkv, mu, w0, w2, a0, a2, g2, k_k, k_a, r_k, gn_w, gn_b):
    B, S, _ = p_rwkv.shape
    xs = token_shift(p_rwkv, mu)
    r, k, v, wl, al, gl = jnp.split(xs, RWKV_SPLITS, axis=-1)
    w = -jax.nn.softplus(-(w0 + jnp.tanh(wl) @ w2)) - 0.5
    decay = jnp.exp(-jnp.exp(w.astype(F32)))
    a = jax.nn.sigmoid(a0 + al @ a2)
    g = jax.nn.sigmoid(gl) @ g2
    kk = k * k_k
    k = k * (1 + (a - 1) * k_a)

    def heads(t):
        return t.reshape(B, S, RWKV_HEADS, RWKV_HEAD).astype(F32)

    kk_h = heads(kk)
    kk_h = kk_h / jnp.maximum(jnp.sqrt(jnp.sum(kk_h * kk_h, axis=-1, keepdims=True)), 1e-12)
    r_h, k_h, v_h, w_h, a_h = heads(r), heads(k), heads(v), decay.reshape(B, S, RWKV_HEADS, RWKV_HEAD), heads(a)

    def step(state, inp):
        r_t, k_t, v_t, w_t, kk_t, a_t = inp
        sa = jnp.einsum('bhij,bhj->bhi', state, -kk_t)
        state = (state * w_t[:, :, None, :]
                 + sa[..., None] * (kk_t * a_t)[:, :, None, :]
                 + v_t[..., None] * k_t[:, :, None, :])
        y_t = jnp.einsum('bhij,bhj->bhi', state, r_t)
        return state, y_t

    def seq_first(t):
        return jnp.moveaxis(t, 1, 0)

    state0 = jnp.zeros((B, RWKV_HEADS, RWKV_HEAD, RWKV_HEAD), F32)
    _, y = lax.scan(step, state0, (seq_first(r_h), seq_first(k_h), seq_first(v_h),
                                   seq_first(w_h), seq_first(kk_h), seq_first(a_h)))
    y = jnp.moveaxis(y, 0, 1)
    mean = jnp.mean(y, axis=-1, keepdims=True)
    var = jnp.mean(jnp.square(y - mean), axis=-1, keepdims=True)
    y = ((y - mean) * lax.rsqrt(var + GN_EPS)).reshape(B, S, RWKV_CH) * gn_w.astype(F32) + gn_b.astype(F32)
    bonus = jnp.sum(r_h * k_h * r_k.astype(F32), axis=-1, keepdims=True) * v_h
    y = y + bonus.reshape(B, S, RWKV_CH)
    return (y * g.astype(F32)).astype(p_rwkv.dtype)


def peer_ffn(u, w_q, sub_keys, expert_u, expert_v):
    B, S, D = u.shape
    tokens = u.reshape(-1, PEER_CHUNK, D)

    def chunk(xc):
        tc = xc.shape[0]
        q = (xc @ w_q).reshape(tc, PEER_HEADS, 2, PEER_DQ // 2)
        s = jnp.einsum('thpd,hpkd->thpk', q, sub_keys).astype(F32)
        s_top, i_top = lax.top_k(s, PEER_TOPK)
        cand = s_top[:, :, 0, :, None] + s_top[:, :, 1, None, :]
        cand_idx = i_top[:, :, 0, :, None] * PEER_NKEYS + i_top[:, :, 1, None, :]
        best, pos = lax.top_k(cand.reshape(tc, PEER_HEADS, PEER_TOPK * PEER_TOPK), PEER_TOPK)
        expert = jnp.take_along_axis(cand_idx.reshape(tc, PEER_HEADS, PEER_TOPK * PEER_TOPK), pos, axis=-1)
        gate = jax.nn.softmax(best, axis=-1)
        ue = jnp.take(expert_u, expert, axis=0)
        ve = jnp.take(expert_v, expert, axis=0)
        h = jax.nn.gelu(jnp.einsum('thkd,td->thk', ue, xc).astype(F32), approximate=False)
        return jnp.einsum('thk,thkd->td', (gate * h).astype(xc.dtype), ve)

    out = lax.map(chunk, tokens)
    return out.reshape(B, S, D)


def setup_inputs(seed: int = 0) -> dict:
    key = jax.random.key(seed)
    ks = iter(jax.random.split(key, 32))

    def nrm(shape, scale):
        return scale * jax.random.normal(next(ks), shape, jnp.float32)

    def gain(shape):
        return 1.0 + nrm(shape, 0.02)

    L = DEPTH
    return {
        'x': nrm((BATCH, SEQ, D_MODEL), 1.0),
        'c': nrm((BATCH, D_MODEL), 1.0),
        'ada_w': nrm((L, D_MODEL, 6 * D_MODEL), D_MODEL ** -0.5),
        'ada_b': nrm((L, 6 * D_MODEL), 0.02),
        'norm_mix_g': gain((L, D_MODEL)),
        'w_in': nrm((L, D_MODEL, IN_PROJ), D_MODEL ** -0.5),
        'conv_dw_w': nrm((L, CONV_WIDTH, CONV_CH), CONV_WIDTH ** -0.5),
        'conv_dw_b': nrm((L, CONV_CH), 0.02),
        'conv_ln_w': gain((L, CONV_CH)),
        'conv_ln_b': nrm((L, CONV_CH), 0.02),
        'rwkv_mu': jax.random.uniform(next(ks), (L, RWKV_PROJ), jnp.float32),
        'rwkv_w0': jax.random.uniform(next(ks), (L, RWKV_CH), jnp.float32, -6.0, 1.0),
        'rwkv_w2': nrm((L, LORA_W, RWKV_CH), 0.5 * LORA_W ** -0.5),
        'rwkv_a0': nrm((L, RWKV_CH), 0.1),
        'rwkv_a2': nrm((L, LORA_A, RWKV_CH), 0.5 * LORA_A ** -0.5),
        'rwkv_g2': nrm((L, LORA_G, RWKV_CH), LORA_G ** -0.5),
        'rwkv_k_k': 0.85 + nrm((L, RWKV_CH), 0.02),
        'rwkv_k_a': gain((L, RWKV_CH)),
        'rwkv_r_k': nrm((L, RWKV_HEADS, RWKV_HEAD), 0.1),
        'rwkv_gn_w': gain((L, RWKV_CH)),
        'rwkv_gn_b': nrm((L, RWKV_CH), 0.02),
        'w_out': nrm((L, MIX_WIDTH, D_MODEL), MIX_WIDTH ** -0.5),
        'norm_ffn_g': gain((L, D_MODEL)),
        'peer_w_q': nrm((L, D_MODEL, PEER_HEADS * PEER_DQ), D_MODEL ** -0.5),
        'peer_sub_keys': nrm((L, PEER_HEADS, 2, PEER_NKEYS, PEER_DQ // 2), (PEER_DQ // 2) ** -0.5),
        'peer_u': nrm((L, PEER_EXPERTS, D_MODEL), D_MODEL ** -0.5),
        'peer_v': nrm((L, PEER_EXPERTS, D_MODEL), (PEER_HEADS * PEER_TOPK) ** -0.5),
        'final_g': gain((D_MODEL,)),
    }


def reference(x, c, ada_w, ada_b, norm_mix_g, w_in, conv_dw_w, conv_dw_b, conv_ln_w, conv_ln_b,
              rwkv_mu, rwkv_w0, rwkv_w2, rwkv_a0, rwkv_a2, rwkv_g2, rwkv_k_k, rwkv_k_a, rwkv_r_k,
              rwkv_gn_w, rwkv_gn_b, w_out, norm_ffn_g, peer_w_q, peer_sub_keys, peer_u, peer_v, final_g):
    for l in range(DEPTH):
        mod = jax.nn.silu(c) @ ada_w[l] + ada_b[l]
        sh_mix, sc_mix, gt_mix, sh_ffn, sc_ffn, gt_ffn = jnp.split(mod, 6, axis=-1)

        u = modulate(rmsnorm(x, norm_mix_g[l]), sh_mix, sc_mix)
        p = u @ w_in[l]
        y_conv = conformer_conv(p[..., :2 * CONV_CH], conv_dw_w[l], conv_dw_b[l], conv_ln_w[l], conv_ln_b[l])
        y_rwkv = rwkv7_mix(p[..., 2 * CONV_CH:], rwkv_mu[l], rwkv_w0[l], rwkv_w2[l], rwkv_a0[l], rwkv_a2[l],
                           rwkv_g2[l], rwkv_k_k[l], rwkv_k_a[l], rwkv_r_k[l], rwkv_gn_w[l], rwkv_gn_b[l])
        mix = jnp.concatenate([y_conv, y_rwkv], axis=-1) @ w_out[l]
        x = x + gt_mix[:, None, :] * mix

        u = modulate(rmsnorm(x, norm_ffn_g[l]), sh_ffn, sc_ffn)
        x = x + gt_ffn[:, None, :] * peer_ffn(u, peer_w_q[l], peer_sub_keys[l], peer_u[l], peer_v[l])
    return rmsnorm(x, final_g)
```

```python
import functools

import jax
import jax.numpy as jnp
from jax import lax
from jax.experimental import pallas as pl
from jax.experimental.pallas import tpu as pltpu

F32 = jnp.float32
BF16 = jnp.bfloat16
I32 = jnp.int32
HI = lax.Precision.HIGHEST

D_MODEL = 1024
CONV_CH = 512
RWKV_CH = 512
HEAD = 64
CONV_WIDTH = 31
LORA_W = 64
LORA_A = 64
LORA_G = 128
RWKV_PROJ = 3 * RWKV_CH + LORA_W + LORA_A + LORA_G
IN_PROJ = 2 * CONV_CH + RWKV_PROJ
PEER_HEADS = 8
PEER_NKEYS = 128
PEER_EXPERTS = PEER_NKEYS * PEER_NKEYS
PEER_DQ = 256
PEER_TOPK = 16
NPAIR = PEER_HEADS * PEER_TOPK
RMS_EPS = 1e-6
LN_EPS = 1e-5
GN_EPS = 64e-5

LANES = 128
CHUNKS = D_MODEL // LANES
WORD_ROWS = CHUNKS // 2
CHUNK_T = 64
VMEM_LIMIT = 56 * 1024 * 1024


def _cparams(sem, vmem=None):
    return pltpu.CompilerParams(dimension_semantics=sem, vmem_limit_bytes=vmem or VMEM_LIMIT)


def _dot(a, b, precision=None):
    return jnp.dot(a, b, precision=precision, preferred_element_type=F32)


def _dot_nt(a, b, precision=None):
    return lax.dot_general(a, b, (((1,), (1,)), ((), ())), precision=precision, preferred_element_type=F32)


def _dot_tn(a, b, precision=None):
    return lax.dot_general(a, b, (((0,), (0,)), ((), ())), precision=precision, preferred_element_type=F32)


def _rms(x, g):
    return x * lax.rsqrt(jnp.mean(x * x, axis=-1, keepdims=True) + RMS_EPS) * g


def _mod_kernel(c_ref, w_ref, b_ref, o_ref):
    c = c_ref[...]
    o_ref[...] = _dot(c * jax.nn.sigmoid(c), w_ref[...], HI) + b_ref[...]


def _mod(c, w, b):
    bsz = c.shape[0]
    n = w.shape[1]
    tn = 1024
    return pl.pallas_call(
        _mod_kernel,
        grid=(n // tn,),
        in_specs=[pl.BlockSpec((bsz, D_MODEL), lambda j: (0, 0)),
                  pl.BlockSpec((D_MODEL, tn), lambda j: (0, j)),
                  pl.BlockSpec((1, tn), lambda j: (0, j))],
        out_specs=pl.BlockSpec((bsz, tn), lambda j: (0, j)),
        out_shape=jax.ShapeDtypeStruct((bsz, n), F32),
        compiler_params=_cparams(("parallel",)),
    )(c, w, b.reshape(1, n))


def _in_proj_kernel(x_ref, sh_ref, sc_ref, g_ref, w_ref, yglu_ref, prw_ref):
    u = _rms(x_ref[0], g_ref[...]) * (1.0 + sc_ref[0]) + sh_ref[0]
    p = _dot(u.astype(BF16), w_ref[...])
    yglu_ref[0] = p[:, :CONV_CH] * jax.nn.sigmoid(p[:, CONV_CH:2 * CONV_CH])
    prw_ref[0] = p[:, 2 * CONV_CH:]


def _in_proj(x, sh, sc, g, w_bf16, tm=256):
    bsz, s, _ = x.shape
    vec = pl.BlockSpec((1, 1, D_MODEL), lambda b, i: (b, 0, 0))
    return pl.pallas_call(
        _in_proj_kernel,
        grid=(bsz, s // tm),
        in_specs=[pl.BlockSpec((1, tm, D_MODEL), lambda b, i: (b, i, 0)), vec, vec,
                  pl.BlockSpec((1, D_MODEL), lambda b, i: (0, 0)),
                  pl.BlockSpec((D_MODEL, IN_PROJ), lambda b, i: (0, 0))],
        out_specs=[pl.BlockSpec((1, tm, CONV_CH), lambda b, i: (b, i, 0)),
                   pl.BlockSpec((1, tm, RWKV_PROJ), lambda b, i: (b, i, 0))],
        out_shape=[jax.ShapeDtypeStruct((bsz, s, CONV_CH), F32),
                   jax.ShapeDtypeStruct((bsz, s, RWKV_PROJ), F32)],
        compiler_params=_cparams(("parallel", "parallel")),
    )(x, sh, sc, g.reshape(1, D_MODEL), w_bf16)


CONV_HALO = 32
CONV_ROWS = 64


def _conv_kernel(cur_ref, prev_ref, w_ref, b_ref, lnw_ref, lnb_ref, o_ref, pad_ref):
    tc = cur_ref.shape[1]
    pad_ref[0:CONV_HALO, :] = jnp.where(pl.program_id(1) > 0, prev_ref[0], 0.0)
    pad_ref[CONV_HALO:CONV_HALO + tc, :] = cur_ref[0]
    off = CONV_HALO - (CONV_WIDTH - 1)
    for r0 in range(0, tc, CONV_ROWS):
        acc = jnp.zeros((CONV_ROWS, CONV_CH), F32)
        for j in range(CONV_WIDTH):
            acc = acc + w_ref[j:j + 1, :] * pad_ref[r0 + off + j:r0 + off + j + CONV_ROWS, :]
        y = acc + b_ref[...]
        mu = jnp.mean(y, axis=-1, keepdims=True)
        yc = y - mu
        var = jnp.mean(yc * yc, axis=-1, keepdims=True)
        yn = yc * lax.rsqrt(var + LN_EPS) * lnw_ref[...] + lnb_ref[...]
        o_ref[0, r0:r0 + CONV_ROWS, :] = (yn * jax.nn.sigmoid(yn)).astype(o_ref.dtype)


def _conv(yglu, w, b, lnw, lnb, tc=256):
    bsz, s, _ = yglu.shape
    hb = tc // CONV_HALO
    row = lambda a: a.reshape(1, CONV_CH)
    const = lambda shp: pl.BlockSpec(shp, lambda bb, i: (0, 0))
    return pl.pallas_call(
        _conv_kernel,
        grid=(bsz, s // tc),
        in_specs=[pl.BlockSpec((1, tc, CONV_CH), lambda bb, i: (bb, i, 0)),
                  pl.BlockSpec((1, CONV_HALO, CONV_CH), lambda bb, i: (bb, jnp.maximum(i * hb - 1, 0), 0)),
                  const((CONV_WIDTH, CONV_CH)), const((1, CONV_CH)), const((1, CONV_CH)), const((1, CONV_CH))],
        out_specs=pl.BlockSpec((1, tc, CONV_CH), lambda bb, i: (bb, i, 0)),
        out_shape=jax.ShapeDtypeStruct((bsz, s, CONV_CH), BF16),
        scratch_shapes=[pltpu.VMEM((CONV_HALO + tc, CONV_CH), F32)],
        compiler_params=_cparams(("parallel", "parallel")),
    )(yglu, yglu, w, row(b), row(lnw), row(lnb))


def _softplus(z):
    return jnp.maximum(z, 0.0) + jnp.log1p(jnp.exp(-jnp.abs(z)))


def _rwkv_pre_kernel(cur_ref, prev_ref, mu_ref, w0_ref, w2_ref, a0_ref, a2_ref, g2_ref, kk_ref, ka_ref, rk_ref,
                     bd_ref, r_o, k_o, v_o, lw_o, a_o, b_o, g_o, bonus_o):
    cur = cur_ref[0]
    prow = jnp.where(pl.program_id(1) > 0, prev_ref[0][7:8, :], 0.0)
    rows = lax.broadcasted_iota(I32, cur.shape, 0)
    prev = jnp.where(rows == 0, prow, pltpu.roll(cur, 1, axis=0))
    xs = cur + mu_ref[...] * (prev - cur)
    r = xs[:, 0:RWKV_CH]
    k = xs[:, RWKV_CH:2 * RWKV_CH]
    v = xs[:, 2 * RWKV_CH:3 * RWKV_CH]
    wa = xs[:, 3 * RWKV_CH:3 * RWKV_CH + LORA_W + LORA_A]
    gl = xs[:, 3 * RWKV_CH + LORA_W + LORA_A:]
    w = -_softplus(-(w0_ref[...] + _dot(jnp.tanh(wa), w2_ref[...], HI))) - 0.5
    a = jax.nn.sigmoid(a0_ref[...] + _dot(wa, a2_ref[...], HI))
    g = _dot(jax.nn.sigmoid(gl), g2_ref[...], HI)
    kk = k * kk_ref[...]
    kkn = kk / jnp.maximum(jnp.sqrt(_dot(kk * kk, bd_ref[...], HI)), 1e-12)
    k2 = k * (1.0 + (a - 1.0) * ka_ref[...])
    r_o[0] = r
    k_o[0] = k2
    v_o[0] = v
    lw_o[0] = -jnp.exp(w)
    a_o[0] = -kkn
    b_o[0] = kkn * a
    g_o[0] = g
    bonus_o[0] = _dot(r * k2 * rk_ref[...], bd_ref[...], HI) * v


def _rwkv_pre(prw, mu, w0, w2p, a0, a2p, g2, k_k, k_a, r_k, bd, tr=256):
    bsz, s, _ = prw.shape
    row = lambda a: a.reshape(1, -1)
    const = lambda shp: pl.BlockSpec(shp, lambda bb, i: (0, 0))
    oblk = pl.BlockSpec((1, tr, RWKV_CH), lambda bb, i: (bb, i, 0))
    return pl.pallas_call(
        _rwkv_pre_kernel,
        grid=(bsz, s // tr),
        in_specs=[pl.BlockSpec((1, tr, RWKV_PROJ), lambda bb, i: (bb, i, 0)),
                  pl.BlockSpec((1, 8, RWKV_PROJ), lambda bb, i: (bb, jnp.maximum(i * (tr // 8) - 1, 0), 0)),
                  const((1, RWKV_PROJ)), const((1, RWKV_CH)), const((LORA_W + LORA_A, RWKV_CH)),
                  const((1, RWKV_CH)), const((LORA_W + LORA_A, RWKV_CH)), const((LORA_G, RWKV_CH)),
                  const((1, RWKV_CH)), const((1, RWKV_CH)), const((1, RWKV_CH)), const((RWKV_CH, RWKV_CH))],
        out_specs=[oblk] * 8,
        out_shape=[jax.ShapeDtypeStruct((bsz, s, RWKV_CH), F32)] * 8,
        compiler_params=_cparams(("parallel", "parallel")),
    )(prw, prw, row(mu), row(w0), w2p, row(a0), a2p, g2, row(k_k), row(k_a), row(r_k), bd)


def _expand(x, lane_lo):
    return jnp.concatenate([jnp.where(lane_lo, x, 0.0), jnp.where(lane_lo, 0.0, x)], axis=0)


def _rwkv_kernel(r_ref, k_ref, v_ref, lw_ref, a_ref, b_ref, g_ref, bonus_ref, bdm_ref, gnw_ref, gnb_ref,
                 y_ref, s_ref):
    C = CHUNK_T
    H2 = 2 * C

    @pl.when(pl.program_id(1) == 0)
    def _():
        s_ref[...] = jnp.zeros_like(s_ref)

    lw = lw_ref[0]
    tri = (lax.broadcasted_iota(I32, (C, C), 0) >= lax.broadcasted_iota(I32, (C, C), 1)).astype(F32)
    cum = _dot(tri, lw, HI)
    tot = cum[C - 1:C, :]
    e_pos = jnp.exp(cum)
    e_neg = jnp.exp(-cum)
    e_rem = jnp.exp(tot - cum)
    rt = r_ref[0] * e_pos
    at = a_ref[0] * jnp.exp(cum - lw)
    kt = k_ref[0] * e_neg
    bt = b_ref[0] * e_neg
    kp = k_ref[0] * e_rem
    bp = b_ref[0] * e_rem
    pc = jnp.exp(tot)
    vv = v_ref[0]

    lane_lo = lax.broadcasted_iota(I32, (C, LANES), 1) < HEAD
    tt = lax.broadcasted_iota(I32, (H2, H2), 0) % C
    ss = lax.broadcasted_iota(I32, (H2, H2), 1) % C
    strict = tt > ss
    incl = tt >= ss
    eye = lax.broadcasted_iota(I32, (LANES, LANES), 0) == lax.broadcasted_iota(I32, (LANES, LANES), 1)

    ys = []
    for hp in range(RWKV_CH // LANES):
        sl = slice(hp * LANES, (hp + 1) * LANES)
        ax, rx, bx, kx, vx = (_expand(t[:, sl], lane_lo) for t in (at, rt, bt, kt, vv))
        bpx, kpx = _expand(bp[:, sl], lane_lo), _expand(kp[:, sl], lane_lo)
        gram = _dot_nt(jnp.concatenate([ax, rx], axis=0), jnp.concatenate([bx, kx], axis=0), HI)
        l_ab = jnp.where(strict, gram[:H2, :H2], 0.0)
        l_ak = jnp.where(strict, gram[:H2, H2:], 0.0)
        m_rb = jnp.where(incl, gram[H2:, :H2], 0.0)
        m_rk = jnp.where(incl, gram[H2:, H2:], 0.0)
        x = jnp.concatenate([ax, _dot(l_ak, vx, HI)], axis=1)
        lp = l_ab
        n_sq = C.bit_length() - 1
        for it in range(n_sq):
            x = x + _dot(lp, x, HI)
            if it + 1 < n_sq:
                lp = _dot(lp, lp, HI)
        z = jnp.concatenate([x, jnp.concatenate([jnp.zeros((H2, LANES), F32), vx], axis=1)], axis=0)
        w1 = _dot(jnp.concatenate([m_rb, m_rk], axis=1), z, HI)
        w2 = _dot_tn(jnp.concatenate([bpx, kpx], axis=0), z, HI)
        ra = rx + w1[:, :LANES]
        ra = ra[:C] + ra[C:]
        y0 = w1[:C, LANES:] + w1[C:, LANES:]
        mt = w2[:, :LANES] + jnp.where(eye, pc[:, sl], 0.0)
        sadd = w2[:, LANES:]
        s0 = s_ref[hp]
        ys.append(_dot(ra, s0, HI) + y0)
        s_ref[hp] = _dot(mt, s0, HI) + sadd

    y = jnp.concatenate(ys, axis=1)
    mean = _dot(y, bdm_ref[...], HI)
    yc = y - mean
    var = _dot(yc * yc, bdm_ref[...], HI)
    yn = yc * lax.rsqrt(var + GN_EPS) * gnw_ref[...] + gnb_ref[...]
    y_ref[0] = ((yn + bonus_ref[0]) * g_ref[0]).astype(y_ref.dtype)


def _rwkv(r, k, v, lw, a, b, g, bonus, bdm, gnw, gnb):
    bsz, s, _ = r.shape
    blk = pl.BlockSpec((1, CHUNK_T, RWKV_CH), lambda bb, c: (bb, c, 0))
    const = lambda shp: pl.BlockSpec(shp, lambda bb, c: (0, 0))
    return pl.pallas_call(
        _rwkv_kernel,
        grid=(bsz, s // CHUNK_T),
        in_specs=[blk] * 8 + [const((RWKV_CH, RWKV_CH)), const((1, RWKV_CH)), const((1, RWKV_CH))],
        out_specs=blk,
        out_shape=jax.ShapeDtypeStruct((bsz, s, RWKV_CH), BF16),
        scratch_shapes=[pltpu.VMEM((RWKV_CH // LANES, LANES, LANES), F32)],
        compiler_params=_cparams(("parallel", "arbitrary")),
    )(r, k, v, lw, a, b, g, bonus, bdm, gnw.reshape(1, -1), gnb.reshape(1, -1))


def _out_proj_kernel(yc_ref, yr_ref, wo_ref, x_ref, gt_ref, g_ref, sh_ref, sc_ref, x1_o, u2_o):
    mix = _dot(yc_ref[0], wo_ref[0:CONV_CH, :]) + _dot(yr_ref[0], wo_ref[CONV_CH:, :])
    x1 = x_ref[0] + gt_ref[0] * mix
    x1_o[0] = x1
    u2_o[0] = _rms(x1, g_ref[...]) * (1.0 + sc_ref[0]) + sh_ref[0]


def _out_proj(yc, yr, wo_bf16, x, gt, g, sh, sc, tm=256):
    bsz, s, _ = x.shape
    vec = pl.BlockSpec((1, 1, D_MODEL), lambda b, i: (b, 0, 0))
    tile = lambda w: pl.BlockSpec((1, tm, w), lambda b, i: (b, i, 0))
    return pl.pallas_call(
        _out_proj_kernel,
        grid=(bsz, s // tm),
        in_specs=[tile(CONV_CH), tile(RWKV_CH), pl.BlockSpec((D_MODEL, D_MODEL), lambda b, i: (0, 0)),
                  tile(D_MODEL), vec, pl.BlockSpec((1, D_MODEL), lambda b, i: (0, 0)), vec, vec],
        out_specs=[tile(D_MODEL), tile(D_MODEL)],
        out_shape=[jax.ShapeDtypeStruct((bsz, s, D_MODEL), F32)] * 2,
        compiler_params=_cparams(("parallel", "parallel")),
    )(yc, yr, wo_bf16, x, gt, g.reshape(1, D_MODEL), sh, sc)


def _topk_rows(s, k):
    n = s.shape[0]
    rid = lax.broadcasted_iota(I32, s.shape, 0)
    vals, ids = [], []
    for _ in range(k):
        m = jnp.max(s, axis=0, keepdims=True)
        j = jnp.min(jnp.where(s == m, rid, n), axis=0, keepdims=True)
        vals.append(m)
        ids.append(j)
        s = jnp.where(rid == j, -jnp.inf, s)
    return jnp.concatenate(vals, axis=0), jnp.concatenate(ids, axis=0)


def _route_kernel(u_ref, wq_ref, keys_ref, idx_o, gate_o, q_ref):
    tm = u_ref.shape[0]
    q = _dot(u_ref[...].astype(BF16), wq_ref[...])
    for j in range(2 * PEER_HEADS):
        q_ref[j] = q[:, j * LANES:(j + 1) * LANES]
    K = PEER_TOPK
    row8 = lax.broadcasted_iota(I32, (8, tm), 0)

    def head(h, carry):
        av, ai = _topk_rows(_dot_nt(keys_ref[2 * h], q_ref[2 * h], HI), K)
        bv, bi = _topk_rows(_dot_nt(keys_ref[2 * h + 1], q_ref[2 * h + 1], HI), K)
        cs, cf, ce = [av[0:1] + bv], [lax.broadcasted_iota(I32, (K, tm), 0)], [ai[0:1] * PEER_NKEYS + bi]
        for x in range(1, 8):
            cs.append(jnp.where(row8 < K // (x + 1), av[x:x + 1] + bv[0:8], -jnp.inf))
            cf.append(x * K + row8)
            ce.append(ai[x:x + 1] * PEER_NKEYS + bi[0:8])
        cs.append(av[8:16] + bv[0:1])
        cf.append((row8 + 8) * K)
        ce.append(ai[8:16] * PEER_NKEYS + bi[0:1])
        s = jnp.concatenate(cs, axis=0)
        flat = jnp.concatenate(cf, axis=0)
        eid = jnp.concatenate(ce, axis=0)
        best, experts = [], []
        for _ in range(K):
            m = jnp.max(s, axis=0, keepdims=True)
            f = jnp.min(jnp.where(s == m, flat, K * K), axis=0, keepdims=True)
            sel = flat == f
            best.append(m)
            experts.append(jnp.sum(jnp.where(sel, eid, 0), axis=0, keepdims=True))
            s = jnp.where(sel, -jnp.inf, s)
        best = jnp.concatenate(best, axis=0)
        e = jnp.exp(best - best[0:1])
        gate_o[0, h] = e / jnp.sum(e, axis=0, keepdims=True)
        idx_o[0, h] = jnp.concatenate(experts, axis=0)
        return carry

    lax.fori_loop(0, PEER_HEADS, head, 0)


def _route(u2, wq_bf16, keys, tm=256):
    n = u2.shape[0]
    nb = n // tm
    oblk = pl.BlockSpec((1, PEER_HEADS, PEER_TOPK, tm), lambda i: (i, 0, 0, 0))
    return pl.pallas_call(
        _route_kernel,
        grid=(nb,),
        in_specs=[pl.BlockSpec((tm, D_MODEL), lambda i: (i, 0)),
                  pl.BlockSpec((D_MODEL, PEER_HEADS * PEER_DQ), lambda i: (0, 0)),
                  pl.BlockSpec((2 * PEER_HEADS, PEER_NKEYS, PEER_DQ // 2), lambda i: (0, 0, 0))],
        out_specs=[oblk, oblk],
        out_shape=[jax.ShapeDtypeStruct((nb, PEER_HEADS, PEER_TOPK, tm), I32),
                   jax.ShapeDtypeStruct((nb, PEER_HEADS, PEER_TOPK, tm), F32)],
        scratch_shapes=[pltpu.VMEM((2 * PEER_HEADS, tm, LANES), F32)],
        compiler_params=_cparams(("parallel",)),
    )(u2, wq_bf16, keys)


def _pack_table(t):
    e = t.shape[0]
    tb = t.astype(BF16).reshape(e, WORD_ROWS, 2, LANES)
    return lax.bitcast_convert_type(jnp.swapaxes(tb, 2, 3), I32)


def _gather_rows(idx_ref, t, tab_ref, g_ref):
    for kk in range(NPAIR):
        g_ref[kk * WORD_ROWS:(kk + 1) * WORD_ROWS, :] = tab_ref[idx_ref[0, kk, t]]


def _peer_u_kernel(idx_ref, x_ref, gate_ref, tab_ref, w_o, g_ref, h_ref):
    tp = x_ref.shape[0]
    rows = lax.broadcasted_iota(I32, (NPAIR * CHUNKS, LANES), 0)
    lanes = lax.broadcasted_iota(I32, (NPAIR * CHUNKS, LANES), 1)
    keep = (lanes // 16 == rows % CHUNKS) & (lanes % 16 == (rows // CHUNKS) % 16)
    grp = lax.broadcasted_iota(I32, (8, LANES), 0) == lax.broadcasted_iota(I32, (8, LANES), 1) // 16

    def token(t, carry):
        _gather_rows(idx_ref, t, tab_ref, g_ref)
        g = pltpu.bitcast(g_ref[...], BF16)
        xc = x_ref[t].astype(BF16)
        xt = jnp.concatenate([jnp.broadcast_to(xc[c:c + 1], (16, LANES)) for c in range(CHUNKS)], axis=0)
        r = jnp.where(keep, _dot_nt(g, xt), 0.0)
        part = jnp.sum(r.reshape(NPAIR // 16, 16 * CHUNKS, LANES), axis=1)
        part = part + pltpu.roll(part, 64, axis=1)
        part = part + pltpu.roll(part, 32, axis=1)
        part = part + pltpu.roll(part, 16, axis=1)
        h_ref[pl.ds(t, 1), :] = jnp.sum(jnp.where(grp, part, 0.0), axis=0, keepdims=True)
        return carry

    lax.fori_loop(0, tp, token, 0)
    h = h_ref[...]
    gelu = 0.5 * h * (1.0 + lax.erf(h * (2.0 ** -0.5)))
    w_o[...] = jnp.transpose(gate_ref[0]) * gelu


def _peer_u(idx, x3, gate, tab, tp=128):
    n = x3.shape[0]
    return pl.pallas_call(
        _peer_u_kernel,
        grid=(n // tp,),
        in_specs=[pl.BlockSpec((1, NPAIR, tp), lambda i: (i, 0, 0), memory_space=pltpu.SMEM),
                  pl.BlockSpec((tp, CHUNKS, LANES), lambda i: (i, 0, 0)),
                  pl.BlockSpec((1, NPAIR, tp), lambda i: (i, 0, 0)),
                  pl.BlockSpec(memory_space=pltpu.VMEM)],
        out_specs=pl.BlockSpec((tp, NPAIR), lambda i: (i, 0)),
        out_shape=jax.ShapeDtypeStruct((n, NPAIR), F32),
        scratch_shapes=[pltpu.VMEM((NPAIR * WORD_ROWS, LANES), I32), pltpu.VMEM((tp, NPAIR), F32)],
        compiler_params=_cparams(("arbitrary",)),
    )(idx, x3, gate, tab)


def _peer_v_kernel(idx_ref, w_ref, rep_ref, tab_ref, o_ref, g_ref, wx_ref):
    tp = w_ref.shape[0]
    wx_ref[...] = _dot(w_ref[...], rep_ref[...], HI)
    diag = lax.broadcasted_iota(I32, (CHUNKS, NPAIR * CHUNKS), 0) == lax.broadcasted_iota(
        I32, (CHUNKS, NPAIR * CHUNKS), 1) % CHUNKS

    def token(t, carry):
        _gather_rows(idx_ref, t, tab_ref, g_ref)
        g = pltpu.bitcast(g_ref[...], BF16)
        wm = jnp.where(diag, wx_ref[pl.ds(t, 1), :], 0.0).astype(BF16)
        o_ref[t] = _dot(wm, g)
        return carry

    lax.fori_loop(0, tp, token, 0)


def _peer_v(idx, w, rep, tab, tp=128):
    n = w.shape[0]
    return pl.pallas_call(
        _peer_v_kernel,
        grid=(n // tp,),
        in_specs=[pl.BlockSpec((1, NPAIR, tp), lambda i: (i, 0, 0), memory_space=pltpu.SMEM),
                  pl.BlockSpec((tp, NPAIR), lambda i: (i, 0)),
                  pl.BlockSpec((NPAIR, NPAIR * CHUNKS), lambda i: (0, 0)),
                  pl.BlockSpec(memory_space=pltpu.VMEM)],
        out_specs=pl.BlockSpec((tp, CHUNKS, LANES), lambda i: (i, 0, 0)),
        out_shape=jax.ShapeDtypeStruct((n, CHUNKS, LANES), F32),
        scratch_shapes=[pltpu.VMEM((NPAIR * WORD_ROWS, LANES), I32), pltpu.VMEM((tp, NPAIR * CHUNKS), F32)],
        compiler_params=_cparams(("arbitrary",)),
    )(idx, w, rep, tab)


def _final_kernel(x1_ref, p_ref, gt_ref, g_ref, o_ref):
    o_ref[0] = _rms(x1_ref[0] + gt_ref[0] * p_ref[0], g_ref[...])


def _final(x1, peer, gt, g, tm=512):
    bsz, s, _ = x1.shape
    tile = pl.BlockSpec((1, tm, D_MODEL), lambda b, i: (b, i, 0))
    return pl.pallas_call(
        _final_kernel,
        grid=(bsz, s // tm),
        in_specs=[tile, tile, pl.BlockSpec((1, 1, D_MODEL), lambda b, i: (b, 0, 0)),
                  pl.BlockSpec((1, D_MODEL), lambda b, i: (0, 0))],
        out_specs=tile,
        out_shape=jax.ShapeDtypeStruct((bsz, s, D_MODEL), F32),
        compiler_params=_cparams(("parallel", "parallel")),
    )(x1, peer, gt, g.reshape(1, D_MODEL))


def _block_diag(width, group, value):
    i = jnp.arange(width) // group
    return jnp.where(i[:, None] == i[None, :], value, 0.0).astype(F32)


def _layer(x, mod, final_g, norm_mix_g, w_in, conv_dw_w, conv_dw_b, conv_ln_w, conv_ln_b, rwkv_mu, rwkv_w0, rwkv_w2,
           rwkv_a0, rwkv_a2, rwkv_g2, rwkv_k_k, rwkv_k_a, rwkv_r_k, rwkv_gn_w, rwkv_gn_b, w_out, norm_ffn_g,
           peer_w_q, peer_sub_keys, peer_u, peer_v):
    bsz, s, _ = x.shape
    sh_mix, sc_mix, gt_mix, sh_ffn, sc_ffn, gt_ffn = (
        mod[:, i * D_MODEL:(i + 1) * D_MODEL].reshape(bsz, 1, D_MODEL) for i in range(6))

    yglu, prw = _in_proj(x, sh_mix, sc_mix, norm_mix_g, w_in.astype(BF16))
    y_conv = _conv(yglu, conv_dw_w, conv_dw_b, conv_ln_w, conv_ln_b)

    zpad = jnp.zeros((LORA_W, RWKV_CH), F32)
    r, k, v, lw, a, b, g, bonus = _rwkv_pre(
        prw, rwkv_mu, rwkv_w0, jnp.concatenate([rwkv_w2, zpad], axis=0), rwkv_a0,
        jnp.concatenate([zpad, rwkv_a2], axis=0), rwkv_g2, rwkv_k_k, rwkv_k_a, rwkv_r_k.reshape(-1),
        _block_diag(RWKV_CH, HEAD, 1.0))
    y_rwkv = _rwkv(r, k, v, lw, a, b, g, bonus, _block_diag(RWKV_CH, HEAD, 1.0 / HEAD), rwkv_gn_w, rwkv_gn_b)

    x1, u2 = _out_proj(y_conv, y_rwkv, w_out.astype(BF16), x, gt_mix, norm_ffn_g, sh_ffn, sc_ffn)

    n = bsz * s
    u2f = u2.reshape(n, D_MODEL)
    keys = peer_sub_keys.reshape(2 * PEER_HEADS, PEER_NKEYS, PEER_DQ // 2)
    idx, gate = _route(u2f, peer_w_q.astype(BF16), keys)
    nb, tm = idx.shape[0], idx.shape[3]
    idx = idx.reshape(nb, NPAIR, tm)
    gate = gate.reshape(nb, NPAIR, tm)
    tp = 128
    if tm != tp:
        regroup = lambda z: jnp.swapaxes(z.reshape(nb, NPAIR, tm // tp, tp), 1, 2).reshape(n // tp, NPAIR, tp)
        idx, gate = regroup(idx), regroup(gate)
    wts = _peer_u(idx, u2f.reshape(n, CHUNKS, LANES), gate, _pack_table(peer_u), tp)
    rep = jnp.repeat(jnp.eye(NPAIR, dtype=F32), CHUNKS, axis=1)
    peer = _peer_v(idx, wts, rep, _pack_table(peer_v), tp)
    return _final(x1, peer.reshape(bsz, s, D_MODEL), gt_ffn, final_g)


def kernel(x, c, ada_w, ada_b, norm_mix_g, w_in, conv_dw_w, conv_dw_b, conv_ln_w, conv_ln_b, rwkv_mu, rwkv_w0,
           rwkv_w2, rwkv_a0, rwkv_a2, rwkv_g2, rwkv_k_k, rwkv_k_a, rwkv_r_k, rwkv_gn_w, rwkv_gn_b, w_out,
           norm_ffn_g, peer_w_q, peer_sub_keys, peer_u, peer_v, final_g):
    depth = ada_w.shape[0]
    assert depth == 1, "one layer: the final norm is fused into the last layer's residual"
    mod = _mod(c, ada_w[0], ada_b[0])
    return _layer(x, mod, final_g, norm_mix_g[0], w_in[0], conv_dw_w[0], conv_dw_b[0], conv_ln_w[0],
                        conv_ln_b[0], rwkv_mu[0], rwkv_w0[0], rwkv_w2[0], rwkv_a0[0], rwkv_a2[0], rwkv_g2[0],
                        rwkv_k_k[0], rwkv_k_a[0], rwkv_r_k[0], rwkv_gn_w[0], rwkv_gn_b[0], w_out[0],
                        norm_ffn_g[0], peer_w_q[0], peer_sub_keys[0], peer_u[0], peer_v[0])
```

```python
import functools

import jax
import jax.numpy as jnp
from jax import lax
from jax.experimental import pallas as pl
from jax.experimental.pallas import tpu as pltpu

F32 = jnp.float32
BF16 = jnp.bfloat16
I32 = jnp.int32
HI = lax.Precision.HIGHEST

D_MODEL = 1024
CONV_CH = 512
RWKV_CH = 512
HEAD = 64
CONV_WIDTH = 31
LORA_W = 64
LORA_A = 64
LORA_G = 128
RWKV_PROJ = 3 * RWKV_CH + LORA_W + LORA_A + LORA_G
IN_PROJ = 2 * CONV_CH + RWKV_PROJ
PEER_HEADS = 8
PEER_NKEYS = 128
PEER_EXPERTS = PEER_NKEYS * PEER_NKEYS
PEER_DQ = 256
PEER_TOPK = 16
NPAIR = PEER_HEADS * PEER_TOPK
RMS_EPS = 1e-6
LN_EPS = 1e-5
GN_EPS = 64e-5

LANES = 128
CHUNKS = D_MODEL // LANES
WORD_ROWS = CHUNKS // 2
CHUNK_T = 64
VMEM_LIMIT = 56 * 1024 * 1024


def _cparams(sem, vmem=None):
    return pltpu.CompilerParams(dimension_semantics=sem, vmem_limit_bytes=vmem or VMEM_LIMIT)


def _dot(a, b, precision=None):
    return jnp.dot(a, b, precision=precision, preferred_element_type=F32)


def _dot_nt(a, b, precision=None):
    return lax.dot_general(a, b, (((1,), (1,)), ((), ())), precision=precision, preferred_element_type=F32)


def _dot_tn(a, b, precision=None):
    return lax.dot_general(a, b, (((0,), (0,)), ((), ())), precision=precision, preferred_element_type=F32)


def _rms(x, g):
    return x * lax.rsqrt(jnp.mean(x * x, axis=-1, keepdims=True) + RMS_EPS) * g


def _mod_kernel(c_ref, w_ref, b_ref, o_ref):
    c = c_ref[...]
    o_ref[...] = _dot(c * jax.nn.sigmoid(c), w_ref[...], HI) + b_ref[...]


def _mod(c, w, b):
    bsz = c.shape[0]
    n = w.shape[1]
    tn = 1024
    return pl.pallas_call(
        _mod_kernel,
        grid=(n // tn,),
        in_specs=[pl.BlockSpec((bsz, D_MODEL), lambda j: (0, 0)),
                  pl.BlockSpec((D_MODEL, tn), lambda j: (0, j)),
                  pl.BlockSpec((1, tn), lambda j: (0, j))],
        out_specs=pl.BlockSpec((bsz, tn), lambda j: (0, j)),
        out_shape=jax.ShapeDtypeStruct((bsz, n), F32),
        compiler_params=_cparams(("parallel",)),
    )(c, w, b.reshape(1, n))


def _in_proj_kernel(x_ref, sh_ref, sc_ref, g_ref, w_ref, yglu_ref, prw_ref):
    u = _rms(x_ref[0], g_ref[...]) * (1.0 + sc_ref[0]) + sh_ref[0]
    p = _dot(u.astype(BF16), w_ref[...])
    yglu_ref[0] = p[:, :CONV_CH] * jax.nn.sigmoid(p[:, CONV_CH:2 * CONV_CH])
    prw_ref[0] = p[:, 2 * CONV_CH:]


def _in_proj(x, sh, sc, g, w_bf16, tm=256):
    bsz, s, _ = x.shape
    vec = pl.BlockSpec((1, 1, D_MODEL), lambda b, i: (b, 0, 0))
    return pl.pallas_call(
        _in_proj_kernel,
        grid=(bsz, s // tm),
        in_specs=[pl.BlockSpec((1, tm, D_MODEL), lambda b, i: (b, i, 0)), vec, vec,
                  pl.BlockSpec((1, D_MODEL), lambda b, i: (0, 0)),
                  pl.BlockSpec((D_MODEL, IN_PROJ), lambda b, i: (0, 0))],
        out_specs=[pl.BlockSpec((1, tm, CONV_CH), lambda b, i: (b, i, 0)),
                   pl.BlockSpec((1, tm, RWKV_PROJ), lambda b, i: (b, i, 0))],
        out_shape=[jax.ShapeDtypeStruct((bsz, s, CONV_CH), F32),
                   jax.ShapeDtypeStruct((bsz, s, RWKV_PROJ), F32)],
        compiler_params=_cparams(("parallel", "parallel")),
    )(x, sh, sc, g.reshape(1, D_MODEL), w_bf16)


CONV_HALO = 32
CONV_ROWS = 64


def _conv_kernel(cur_ref, prev_ref, w_ref, b_ref, lnw_ref, lnb_ref, o_ref, pad_ref):
    tc = cur_ref.shape[1]
    pad_ref[0:CONV_HALO, :] = jnp.where(pl.program_id(1) > 0, prev_ref[0], 0.0)
    pad_ref[CONV_HALO:CONV_HALO + tc, :] = cur_ref[0]
    off = CONV_HALO - (CONV_WIDTH - 1)
    for r0 in range(0, tc, CONV_ROWS):
        acc = jnp.zeros((CONV_ROWS, CONV_CH), F32)
        for j in range(CONV_WIDTH):
            acc = acc + w_ref[j:j + 1, :] * pad_ref[r0 + off + j:r0 + off + j + CONV_ROWS, :]
        y = acc + b_ref[...]
        mu = jnp.mean(y, axis=-1, keepdims=True)
        yc = y - mu
        var = jnp.mean(yc * yc, axis=-1, keepdims=True)
        yn = yc * lax.rsqrt(var + LN_EPS) * lnw_ref[...] + lnb_ref[...]
        o_ref[0, r0:r0 + CONV_ROWS, :] = (yn * jax.nn.sigmoid(yn)).astype(o_ref.dtype)


def _conv(yglu, w, b, lnw, lnb, tc=256):
    bsz, s, _ = yglu.shape
    hb = tc // CONV_HALO
    row = lambda a: a.reshape(1, CONV_CH)
    const = lambda shp: pl.BlockSpec(shp, lambda bb, i: (0, 0))
    return pl.pallas_call(
        _conv_kernel,
        grid=(bsz, s // tc),
        in_specs=[pl.BlockSpec((1, tc, CONV_CH), lambda bb, i: (bb, i, 0)),
                  pl.BlockSpec((1, CONV_HALO, CONV_CH), lambda bb, i: (bb, jnp.maximum(i * hb - 1, 0), 0)),
                  const((CONV_WIDTH, CONV_CH)), const((1, CONV_CH)), const((1, CONV_CH)), const((1, CONV_CH))],
        out_specs=pl.BlockSpec((1, tc, CONV_CH), lambda bb, i: (bb, i, 0)),
        out_shape=jax.ShapeDtypeStruct((bsz, s, CONV_CH), BF16),
        scratch_shapes=[pltpu.VMEM((CONV_HALO + tc, CONV_CH), F32)],
        compiler_params=_cparams(("parallel", "parallel")),
    )(yglu, yglu, w, row(b), row(lnw), row(lnb))


def _softplus(z):
    return jnp.maximum(z, 0.0) + jnp.log1p(jnp.exp(-jnp.abs(z)))


def _rwkv_pre_kernel(cur_ref, prev_ref, mu_ref, w0_ref, w2_ref, a0_ref, a2_ref, g2_ref, kk_ref, ka_ref, rk_ref,
                     bd_ref, r_o, k_o, v_o, lw_o, a_o, b_o, g_o, bonus_o):
    cur = cur_ref[0]
    prow = jnp.where(pl.program_id(1) > 0, prev_ref[0][7:8, :], 0.0)
    rows = lax.broadcasted_iota(I32, cur.shape, 0)
    prev = jnp.where(rows == 0, prow, pltpu.roll(cur, 1, axis=0))
    xs = cur + mu_ref[...] * (prev - cur)
    r = xs[:, 0:RWKV_CH]
    k = xs[:, RWKV_CH:2 * RWKV_CH]
    v = xs[:, 2 * RWKV_CH:3 * RWKV_CH]
    wa = xs[:, 3 * RWKV_CH:3 * RWKV_CH + LORA_W + LORA_A]
    gl = xs[:, 3 * RWKV_CH + LORA_W + LORA_A:]
    w = -_softplus(-(w0_ref[...] + _dot(jnp.tanh(wa), w2_ref[...], HI))) - 0.5
    a = jax.nn.sigmoid(a0_ref[...] + _dot(wa, a2_ref[...], HI))
    g = _dot(jax.nn.sigmoid(gl), g2_ref[...], HI)
    kk = k * kk_ref[...]
    kkn = kk / jnp.maximum(jnp.sqrt(_dot(kk * kk, bd_ref[...], HI)), 1e-12)
    k2 = k * (1.0 + (a - 1.0) * ka_ref[...])
    r_o[0] = r
    k_o[0] = k2
    v_o[0] = v
    lw_o[0] = -jnp.exp(w)
    a_o[0] = -kkn
    b_o[0] = kkn * a
    g_o[0] = g
    bonus_o[0] = _dot(r * k2 * rk_ref[...], bd_ref[...], HI) * v


def _rwkv_pre(prw, mu, w0, w2p, a0, a2p, g2, k_k, k_a, r_k, bd, tr=256):
    bsz, s, _ = prw.shape
    row = lambda a: a.reshape(1, -1)
    const = lambda shp: pl.BlockSpec(shp, lambda bb, i: (0, 0))
    oblk = pl.BlockSpec((1, tr, RWKV_CH), lambda bb, i: (bb, i, 0))
    return pl.pallas_call(
        _rwkv_pre_kernel,
        grid=(bsz, s // tr),
        in_specs=[pl.BlockSpec((1, tr, RWKV_PROJ), lambda bb, i: (bb, i, 0)),
                  pl.BlockSpec((1, 8, RWKV_PROJ), lambda bb, i: (bb, jnp.maximum(i * (tr // 8) - 1, 0), 0)),
                  const((1, RWKV_PROJ)), const((1, RWKV_CH)), const((LORA_W + LORA_A, RWKV_CH)),
                  const((1, RWKV_CH)), const((LORA_W + LORA_A, RWKV_CH)), const((LORA_G, RWKV_CH)),
                  const((1, RWKV_CH)), const((1, RWKV_CH)), const((1, RWKV_CH)), const((RWKV_CH, RWKV_CH))],
        out_specs=[oblk] * 8,
        out_shape=[jax.ShapeDtypeStruct((bsz, s, RWKV_CH), F32)] * 8,
        compiler_params=_cparams(("parallel", "parallel")),
    )(prw, prw, row(mu), row(w0), w2p, row(a0), a2p, g2, row(k_k), row(k_a), row(r_k), bd)


def _expand(x, lane_lo):
    return jnp.concatenate([jnp.where(lane_lo, x, 0.0), jnp.where(lane_lo, 0.0, x)], axis=0)


def _rwkv_kernel(r_ref, k_ref, v_ref, lw_ref, a_ref, b_ref, g_ref, bonus_ref, bdm_ref, gnw_ref, gnb_ref,
                 y_ref, s_ref):
    C = CHUNK_T
    H2 = 2 * C

    @pl.when(pl.program_id(1) == 0)
    def _():
        s_ref[...] = jnp.zeros_like(s_ref)

    lw = lw_ref[0]
    tri = (lax.broadcasted_iota(I32, (C, C), 0) >= lax.broadcasted_iota(I32, (C, C), 1)).astype(F32)
    cum = _dot(tri, lw, HI)
    tot = cum[C - 1:C, :]
    e_pos = jnp.exp(cum)
    e_neg = jnp.exp(-cum)
    e_rem = jnp.exp(tot - cum)
    rt = r_ref[0] * e_pos
    at = a_ref[0] * jnp.exp(cum - lw)
    kt = k_ref[0] * e_neg
    bt = b_ref[0] * e_neg
    kp = k_ref[0] * e_rem
    bp = b_ref[0] * e_rem
    pc = jnp.exp(tot)
    vv = v_ref[0]

    lane_lo = lax.broadcasted_iota(I32, (C, LANES), 1) < HEAD
    tt = lax.broadcasted_iota(I32, (H2, H2), 0) % C
    ss = lax.broadcasted_iota(I32, (H2, H2), 1) % C
    strict = tt > ss
    incl = tt >= ss
    eye = lax.broadcasted_iota(I32, (LANES, LANES), 0) == lax.broadcasted_iota(I32, (LANES, LANES), 1)

    ys = []
    for hp in range(RWKV_CH // LANES):
        sl = slice(hp * LANES, (hp + 1) * LANES)
        ax, rx, bx, kx, vx = (_expand(t[:, sl], lane_lo) for t in (at, rt, bt, kt, vv))
        bpx, kpx = _expand(bp[:, sl], lane_lo), _expand(kp[:, sl], lane_lo)
        gram = _dot_nt(jnp.concatenate([ax, rx], axis=0), jnp.concatenate([bx, kx], axis=0), HI)
        l_ab = jnp.where(strict, gram[:H2, :H2], 0.0)
        l_ak = jnp.where(strict, gram[:H2, H2:], 0.0)
        m_rb = jnp.where(incl, gram[H2:, :H2], 0.0)
        m_rk = jnp.where(incl, gram[H2:, H2:], 0.0)
        x = jnp.concatenate([ax, _dot(l_ak, vx, HI)], axis=1)
        lp = l_ab
        n_sq = C.bit_length() - 1
        for it in range(n_sq):
            x = x + _dot(lp, x, HI)
            if it + 1 < n_sq:
                lp = _dot(lp, lp, HI)
        z = jnp.concatenate([x, jnp.concatenate([jnp.zeros((H2, LANES), F32), vx], axis=1)], axis=0)
        w1 = _dot(jnp.concatenate([m_rb, m_rk], axis=1), z, HI)
        w2 = _dot_tn(jnp.concatenate([bpx, kpx], axis=0), z, HI)
        ra = rx + w1[:, :LANES]
        ra = ra[:C] + ra[C:]
        y0 = w1[:C, LANES:] + w1[C:, LANES:]
        mt = w2[:, :LANES] + jnp.where(eye, pc[:, sl], 0.0)
        sadd = w2[:, LANES:]
        s0 = s_ref[hp]
        ys.append(_dot(ra, s0, HI) + y0)
        s_ref[hp] = _dot(mt, s0, HI) + sadd

    y = jnp.concatenate(ys, axis=1)
    mean = _dot(y, bdm_ref[...], HI)
    yc = y - mean
    var = _dot(yc * yc, bdm_ref[...], HI)
    yn = yc * lax.rsqrt(var + GN_EPS) * gnw_ref[...] + gnb_ref[...]
    y_ref[0] = ((yn + bonus_ref[0]) * g_ref[0]).astype(y_ref.dtype)


def _rwkv(r, k, v, lw, a, b, g, bonus, bdm, gnw, gnb):
    bsz, s, _ = r.shape
    blk = pl.BlockSpec((1, CHUNK_T, RWKV_CH), lambda bb, c: (bb, c, 0))
    const = lambda shp: pl.BlockSpec(shp, lambda bb, c: (0, 0))
    return pl.pallas_call(
        _rwkv_kernel,
        grid=(bsz, s // CHUNK_T),
        in_specs=[blk] * 8 + [const((RWKV_CH, RWKV_CH)), const((1, RWKV_CH)), const((1, RWKV_CH))],
        out_specs=blk,
        out_shape=jax.ShapeDtypeStruct((bsz, s, RWKV_CH), BF16),
        scratch_shapes=[pltpu.VMEM((RWKV_CH // LANES, LANES, LANES), F32)],
        compiler_params=_cparams(("parallel", "arbitrary")),
    )(r, k, v, lw, a, b, g, bonus, bdm, gnw.reshape(1, -1), gnb.reshape(1, -1))


def _out_proj_kernel(yc_ref, yr_ref, wo_ref, x_ref, gt_ref, g_ref, sh_ref, sc_ref, x1_o, u2_o):
    mix = _dot(yc_ref[0], wo_ref[0:CONV_CH, :]) + _dot(yr_ref[0], wo_ref[CONV_CH:, :])
    x1 = x_ref[0] + gt_ref[0] * mix
    x1_o[0] = x1
    u2_o[0] = _rms(x1, g_ref[...]) * (1.0 + sc_ref[0]) + sh_ref[0]


def _out_proj(yc, yr, wo_bf16, x, gt, g, sh, sc, tm=256):
    bsz, s, _ = x.shape
    vec = pl.BlockSpec((1, 1, D_MODEL), lambda b, i: (b, 0, 0))
    tile = lambda w: pl.BlockSpec((1, tm, w), lambda b, i: (b, i, 0))
    return pl.pallas_call(
        _out_proj_kernel,
        grid=(bsz, s // tm),
        in_specs=[tile(CONV_CH), tile(RWKV_CH), pl.BlockSpec((D_MODEL, D_MODEL), lambda b, i: (0, 0)),
                  tile(D_MODEL), vec, pl.BlockSpec((1, D_MODEL), lambda b, i: (0, 0)), vec, vec],
        out_specs=[tile(D_MODEL), tile(D_MODEL)],
        out_shape=[jax.ShapeDtypeStruct((bsz, s, D_MODEL), F32)] * 2,
        compiler_params=_cparams(("parallel", "parallel")),
    )(yc, yr, wo_bf16, x, gt, g.reshape(1, D_MODEL), sh, sc)


def _topk_rows(s, k):
    n = s.shape[0]
    rid = lax.broadcasted_iota(I32, s.shape, 0)
    vals, ids = [], []
    for _ in range(k):
        m = jnp.max(s, axis=0, keepdims=True)
        j = jnp.min(jnp.where(s == m, rid, n), axis=0, keepdims=True)
        vals.append(m)
        ids.append(j)
        s = jnp.where(rid == j, -jnp.inf, s)
    return jnp.concatenate(vals, axis=0), jnp.concatenate(ids, axis=0)


def _route_kernel(u_ref, wq_ref, keys_ref, idx_o, gate_o, q_ref, idx_s, gate_s):
    tm = u_ref.shape[0]
    q = _dot(u_ref[...].astype(BF16), wq_ref[...])
    for j in range(2 * PEER_HEADS):
        q_ref[j] = q[:, j * LANES:(j + 1) * LANES]
    K = PEER_TOPK
    row8 = lax.broadcasted_iota(I32, (8, tm), 0)

    def head(h, carry):
        av, ai = _topk_rows(_dot_nt(keys_ref[2 * h], q_ref[2 * h], HI), K)
        bv, bi = _topk_rows(_dot_nt(keys_ref[2 * h + 1], q_ref[2 * h + 1], HI), K)
        cs, cf, ce = [av[0:1] + bv], [lax.broadcasted_iota(I32, (K, tm), 0)], [ai[0:1] * PEER_NKEYS + bi]
        for x in range(1, 8):
            cs.append(jnp.where(row8 < K // (x + 1), av[x:x + 1] + bv[0:8], -jnp.inf))
            cf.append(x * K + row8)
            ce.append(ai[x:x + 1] * PEER_NKEYS + bi[0:8])
        cs.append(av[8:16] + bv[0:1])
        cf.append((row8 + 8) * K)
        ce.append(ai[8:16] * PEER_NKEYS + bi[0:1])
        s = jnp.concatenate(cs, axis=0)
        flat = jnp.concatenate(cf, axis=0)
        eid = jnp.concatenate(ce, axis=0)
        best, experts = [], []
        for _ in range(K):
            m = jnp.max(s, axis=0, keepdims=True)
            f = jnp.min(jnp.where(s == m, flat, K * K), axis=0, keepdims=True)
            sel = flat == f
            best.append(m)
            experts.append(jnp.sum(jnp.where(sel, eid, 0), axis=0, keepdims=True))
            s = jnp.where(sel, -jnp.inf, s)
        best = jnp.concatenate(best, axis=0)
        e = jnp.exp(best - best[0:1])
        gate_s[h] = e / jnp.sum(e, axis=0, keepdims=True)
        idx_s[h] = jnp.concatenate(experts, axis=0) * WORD_ROWS
        return carry

    lax.fori_loop(0, PEER_HEADS, head, 0)
    idx_o[...] = jnp.transpose(idx_s[...].reshape(NPAIR, tm))
    gate_o[...] = jnp.transpose(gate_s[...].reshape(NPAIR, tm))


def _route(u2, wq_bf16, keys, tm=256):
    n = u2.shape[0]
    oblk = pl.BlockSpec((tm, NPAIR), lambda i: (i, 0))
    return pl.pallas_call(
        _route_kernel,
        grid=(n // tm,),
        in_specs=[pl.BlockSpec((tm, D_MODEL), lambda i: (i, 0)),
                  pl.BlockSpec((D_MODEL, PEER_HEADS * PEER_DQ), lambda i: (0, 0)),
                  pl.BlockSpec((2 * PEER_HEADS, PEER_NKEYS, PEER_DQ // 2), lambda i: (0, 0, 0))],
        out_specs=[oblk, oblk],
        out_shape=[jax.ShapeDtypeStruct((n, NPAIR), I32), jax.ShapeDtypeStruct((n, NPAIR), F32)],
        scratch_shapes=[pltpu.VMEM((2 * PEER_HEADS, tm, LANES), F32),
                        pltpu.VMEM((PEER_HEADS, PEER_TOPK, tm), I32),
                        pltpu.VMEM((PEER_HEADS, PEER_TOPK, tm), F32)],
        compiler_params=_cparams(("parallel",)),
        name="route",
    )(u2, wq_bf16, keys)


TOK_UNROLL = 4


def _pack_table(t):
    e = t.shape[0]
    tb = t.astype(BF16).reshape(e, WORD_ROWS, 2, LANES)
    return lax.bitcast_convert_type(jnp.swapaxes(tb, 2, 3), I32).reshape(e * WORD_ROWS, LANES)


def _gather_rows(idx_ref, t, tab_ref, g_ref):
    for kk in range(NPAIR):
        row = pl.multiple_of(idx_ref[t, kk], WORD_ROWS)
        g_ref[kk * WORD_ROWS:(kk + 1) * WORD_ROWS, :] = tab_ref[pl.ds(row, WORD_ROWS), :]


def _peer_u_kernel(idx_ref, x_ref, gate_ref, keep_ref, sel_ref, tab_ref, w_o, g_ref, p_ref):
    tp = x_ref.shape[0]

    def tokens(i, carry):
        t0 = i * TOK_UNROLL
        for u in range(TOK_UNROLL):
            _gather_rows(idx_ref, t0 + u, tab_ref, g_ref.at[u])
        for u in range(TOK_UNROLL):
            g = pltpu.bitcast(g_ref[u], BF16)
            xc = x_ref[t0 + u]
            xt = jnp.concatenate([xc] * (LANES // CHUNKS), axis=0).astype(BF16)
            r = _dot_nt(g, xt) * keep_ref[...]
            part = jnp.sum(r.reshape(NPAIR // 16, 16 * CHUNKS, LANES), axis=1)
            p_ref[pl.ds(pl.multiple_of((t0 + u) * 8, 8), 8), :] = part
        return carry

    lax.fori_loop(0, tp // TOK_UNROLL, tokens, 0)
    z = _dot(p_ref[...], sel_ref[...], HI)
    grp = lax.broadcasted_iota(I32, z.shape, 0) % 8 == lax.broadcasted_iota(I32, z.shape, 1) // 16
    h = jnp.sum(jnp.where(grp, z, 0.0).reshape(tp, 8, LANES), axis=1)
    w_o[...] = gate_ref[...] * (0.5 * h * (1.0 + lax.erf(h * (2.0 ** -0.5))))


def _peer_u(idx, x3, gate, tab, tp=128):
    n = x3.shape[0]
    rows = jnp.arange(NPAIR * CHUNKS)[:, None]
    lanes = jnp.arange(LANES)[None, :]
    keep = ((lanes % CHUNKS == rows % CHUNKS) & (lanes // CHUNKS == (rows // CHUNKS) % 16)).astype(F32)
    sel = (jnp.arange(LANES)[:, None] // CHUNKS == jnp.arange(LANES)[None, :] % 16).astype(F32)
    return pl.pallas_call(
        _peer_u_kernel,
        grid=(n // tp,),
        in_specs=[pl.BlockSpec((tp, NPAIR), lambda i: (i, 0), memory_space=pltpu.SMEM),
                  pl.BlockSpec((tp, CHUNKS, LANES), lambda i: (i, 0, 0)),
                  pl.BlockSpec((tp, NPAIR), lambda i: (i, 0)),
                  pl.BlockSpec((NPAIR * CHUNKS, LANES), lambda i: (0, 0)),
                  pl.BlockSpec((LANES, LANES), lambda i: (0, 0)),
                  pl.BlockSpec(memory_space=pltpu.VMEM)],
        out_specs=pl.BlockSpec((tp, NPAIR), lambda i: (i, 0)),
        out_shape=jax.ShapeDtypeStruct((n, NPAIR), F32),
        scratch_shapes=[pltpu.VMEM((TOK_UNROLL, NPAIR * WORD_ROWS, LANES), I32),
                        pltpu.VMEM((tp * 8, LANES), F32)],
        compiler_params=_cparams(("arbitrary",)),
        name="peer_u",
    )(idx, x3, gate, keep, sel, tab)


def _peer_v_kernel(idx_ref, w_ref, rep_ref, diag_ref, tab_ref, o_ref, g_ref, wx_ref):
    tp = w_ref.shape[0]
    wx_ref[...] = _dot(w_ref[...], rep_ref[...], HI)

    def tokens(i, carry):
        t0 = i * TOK_UNROLL
        for u in range(TOK_UNROLL):
            _gather_rows(idx_ref, t0 + u, tab_ref, g_ref.at[u])
        for u in range(TOK_UNROLL):
            g = pltpu.bitcast(g_ref[u], BF16)
            wm = (wx_ref[pl.ds(t0 + u, 1), :] * diag_ref[...]).astype(BF16)
            o_ref[t0 + u] = _dot(wm, g)
        return carry

    lax.fori_loop(0, tp // TOK_UNROLL, tokens, 0)


def _peer_v(idx, w, tab, tp=128):
    n = w.shape[0]
    rep = jnp.repeat(jnp.eye(NPAIR, dtype=F32), CHUNKS, axis=1)
    diag = (jnp.arange(CHUNKS)[:, None] == jnp.arange(NPAIR * CHUNKS)[None, :] % CHUNKS).astype(F32)
    return pl.pallas_call(
        _peer_v_kernel,
        grid=(n // tp,),
        in_specs=[pl.BlockSpec((tp, NPAIR), lambda i: (i, 0), memory_space=pltpu.SMEM),
                  pl.BlockSpec((tp, NPAIR), lambda i: (i, 0)),
                  pl.BlockSpec((NPAIR, NPAIR * CHUNKS), lambda i: (0, 0)),
                  pl.BlockSpec((CHUNKS, NPAIR * CHUNKS), lambda i: (0, 0)),
                  pl.BlockSpec(memory_space=pltpu.VMEM)],
        out_specs=pl.BlockSpec((tp, CHUNKS, LANES), lambda i: (i, 0, 0)),
        out_shape=jax.ShapeDtypeStruct((n, CHUNKS, LANES), F32),
        scratch_shapes=[pltpu.VMEM((TOK_UNROLL, NPAIR * WORD_ROWS, LANES), I32),
                        pltpu.VMEM((tp, NPAIR * CHUNKS), F32)],
        compiler_params=_cparams(("arbitrary",)),
        name="peer_v",
    )(idx, w, rep, diag, tab)


def _final_kernel(x1_ref, p_ref, gt_ref, g_ref, o_ref):
    o_ref[0] = _rms(x1_ref[0] + gt_ref[0] * p_ref[0], g_ref[...])


def _final(x1, peer, gt, g, tm=512):
    bsz, s, _ = x1.shape
    tile = pl.BlockSpec((1, tm, D_MODEL), lambda b, i: (b, i, 0))
    return pl.pallas_call(
        _final_kernel,
        grid=(bsz, s // tm),
        in_specs=[tile, tile, pl.BlockSpec((1, 1, D_MODEL), lambda b, i: (b, 0, 0)),
                  pl.BlockSpec((1, D_MODEL), lambda b, i: (0, 0))],
        out_specs=tile,
        out_shape=jax.ShapeDtypeStruct((bsz, s, D_MODEL), F32),
        compiler_params=_cparams(("parallel", "parallel")),
    )(x1, peer, gt, g.reshape(1, D_MODEL))


def _block_diag(width, group, value):
    i = jnp.arange(width) // group
    return jnp.where(i[:, None] == i[None, :], value, 0.0).astype(F32)


def _layer(x, mod, final_g, norm_mix_g, w_in, conv_dw_w, conv_dw_b, conv_ln_w, conv_ln_b, rwkv_mu, rwkv_w0, rwkv_w2,
           rwkv_a0, rwkv_a2, rwkv_g2, rwkv_k_k, rwkv_k_a, rwkv_r_k, rwkv_gn_w, rwkv_gn_b, w_out, norm_ffn_g,
           peer_w_q, peer_sub_keys, peer_u, peer_v):
    bsz, s, _ = x.shape
    sh_mix, sc_mix, gt_mix, sh_ffn, sc_ffn, gt_ffn = (
        mod[:, i * D_MODEL:(i + 1) * D_MODEL].reshape(bsz, 1, D_MODEL) for i in range(6))

    yglu, prw = _in_proj(x, sh_mix, sc_mix, norm_mix_g, w_in.astype(BF16))
    y_conv = _conv(yglu, conv_dw_w, conv_dw_b, conv_ln_w, conv_ln_b)

    zpad = jnp.zeros((LORA_W, RWKV_CH), F32)
    r, k, v, lw, a, b, g, bonus = _rwkv_pre(
        prw, rwkv_mu, rwkv_w0, jnp.concatenate([rwkv_w2, zpad], axis=0), rwkv_a0,
        jnp.concatenate([zpad, rwkv_a2], axis=0), rwkv_g2, rwkv_k_k, rwkv_k_a, rwkv_r_k.reshape(-1),
        _block_diag(RWKV_CH, HEAD, 1.0))
    y_rwkv = _rwkv(r, k, v, lw, a, b, g, bonus, _block_diag(RWKV_CH, HEAD, 1.0 / HEAD), rwkv_gn_w, rwkv_gn_b)

    x1, u2 = _out_proj(y_conv, y_rwkv, w_out.astype(BF16), x, gt_mix, norm_ffn_g, sh_ffn, sc_ffn)

    n = bsz * s
    u2f = u2.reshape(n, D_MODEL)
    keys = peer_sub_keys.reshape(2 * PEER_HEADS, PEER_NKEYS, PEER_DQ // 2)
    idx, gate = _route(u2f, peer_w_q.astype(BF16), keys)
    wts = _peer_u(idx, u2f.reshape(n, CHUNKS, LANES), gate, _pack_table(peer_u))
    peer = _peer_v(idx, wts, _pack_table(peer_v))
    return _final(x1, peer.reshape(bsz, s, D_MODEL), gt_ffn, final_g)


def kernel(x, c, ada_w, ada_b, norm_mix_g, w_in, conv_dw_w, conv_dw_b, conv_ln_w, conv_ln_b, rwkv_mu, rwkv_w0,
           rwkv_w2, rwkv_a0, rwkv_a2, rwkv_g2, rwkv_k_k, rwkv_k_a, rwkv_r_k, rwkv_gn_w, rwkv_gn_b, w_out,
           norm_ffn_g, peer_w_q, peer_sub_keys, peer_u, peer_v, final_g):
    depth = ada_w.shape[0]
    assert depth == 1, "one layer: the final norm is fused into the last layer's residual"
    mod = _mod(c, ada_w[0], ada_b[0])
    return _layer(x, mod, final_g, norm_mix_g[0], w_in[0], conv_dw_w[0], conv_dw_b[0], conv_ln_w[0],
                        conv_ln_b[0], rwkv_mu[0], rwkv_w0[0], rwkv_w2[0], rwkv_a0[0], rwkv_a2[0], rwkv_g2[0],
                        rwkv_k_k[0], rwkv_k_a[0], rwkv_r_k[0], rwkv_gn_w[0], rwkv_gn_b[0], w_out[0],
                        norm_ffn_g[0], peer_w_q[0], peer_sub_keys[0], peer_u[0], peer_v[0])
```

```python
import functools

import jax
import jax.numpy as jnp
from jax import lax
from jax.experimental import pallas as pl
from jax.experimental.pallas import tpu as pltpu

F32 = jnp.float32
BF16 = jnp.bfloat16
I32 = jnp.int32
HI = lax.Precision.HIGHEST

D_MODEL = 1024
CONV_CH = 512
RWKV_CH = 512
HEAD = 64
CONV_WIDTH = 31
LORA_W = 64
LORA_A = 64
LORA_G = 128
RWKV_PROJ = 3 * RWKV_CH + LORA_W + LORA_A + LORA_G
IN_PROJ = 2 * CONV_CH + RWKV_PROJ
PEER_HEADS = 8
PEER_NKEYS = 128
PEER_EXPERTS = PEER_NKEYS * PEER_NKEYS
PEER_DQ = 256
PEER_TOPK = 16
NPAIR = PEER_HEADS * PEER_TOPK
RMS_EPS = 1e-6
LN_EPS = 1e-5
GN_EPS = 64e-5

LANES = 128
CHUNKS = D_MODEL // LANES
WORD_ROWS = CHUNKS // 2
CHUNK_T = 64
VMEM_LIMIT = 56 * 1024 * 1024


def _cparams(sem, vmem=None):
    return pltpu.CompilerParams(dimension_semantics=sem, vmem_limit_bytes=vmem or VMEM_LIMIT)


def _dot(a, b, precision=None):
    return jnp.dot(a, b, precision=precision, preferred_element_type=F32)


def _dot_nt(a, b, precision=None):
    return lax.dot_general(a, b, (((1,), (1,)), ((), ())), precision=precision, preferred_element_type=F32)


def _dot_tn(a, b, precision=None):
    return lax.dot_general(a, b, (((0,), (0,)), ((), ())), precision=precision, preferred_element_type=F32)


def _split(a):
    hi = a.astype(BF16)
    return hi, (a - hi.astype(F32)).astype(BF16)


def _mm(a, b, passes, dot=_dot):
    if passes == 6:
        return dot(a, b, HI)
    if passes == 1:
        return dot(a.astype(BF16), b.astype(BF16))
    ka = 0 if dot is _dot_tn else 1
    kb = 1 if dot is _dot_nt else 0
    ah, al = _split(a)
    bh, bl = _split(b)
    return dot(jnp.concatenate([ah, ah, al], axis=ka), jnp.concatenate([bh, bl, bh], axis=kb))


def _mm_bf16_rhs(a, b_bf16):
    ah, al = _split(a)
    return _dot(jnp.concatenate([ah, al], axis=1), jnp.concatenate([b_bf16, b_bf16], axis=0))


def _rms(x, g):
    return x * lax.rsqrt(jnp.mean(x * x, axis=-1, keepdims=True) + RMS_EPS) * g


def _mod_kernel(c_ref, w_ref, b_ref, o_ref):
    c = c_ref[...]
    o_ref[...] = _dot(c * jax.nn.sigmoid(c), w_ref[...], HI) + b_ref[...]


def _mod(c, w, b):
    bsz = c.shape[0]
    n = w.shape[1]
    tn = 1024
    return pl.pallas_call(
        _mod_kernel,
        grid=(n // tn,),
        in_specs=[pl.BlockSpec((bsz, D_MODEL), lambda j: (0, 0)),
                  pl.BlockSpec((D_MODEL, tn), lambda j: (0, j)),
                  pl.BlockSpec((1, tn), lambda j: (0, j))],
        out_specs=pl.BlockSpec((bsz, tn), lambda j: (0, j)),
        out_shape=jax.ShapeDtypeStruct((bsz, n), F32),
        compiler_params=_cparams(("parallel",)),
    )(c, w, b.reshape(1, n))


def _in_proj_kernel(x_ref, sh_ref, sc_ref, g_ref, w_ref, yglu_ref, prw_ref):
    u = _rms(x_ref[0], g_ref[...]) * (1.0 + sc_ref[0]) + sh_ref[0]
    p = _dot(u.astype(BF16), w_ref[...])
    yglu_ref[0] = p[:, :CONV_CH] * jax.nn.sigmoid(p[:, CONV_CH:2 * CONV_CH])
    prw_ref[0] = p[:, 2 * CONV_CH:]


def _in_proj(x, sh, sc, g, w_bf16, tm=256):
    bsz, s, _ = x.shape
    vec = pl.BlockSpec((1, 1, D_MODEL), lambda b, i: (b, 0, 0))
    return pl.pallas_call(
        _in_proj_kernel,
        grid=(bsz, s // tm),
        in_specs=[pl.BlockSpec((1, tm, D_MODEL), lambda b, i: (b, i, 0)), vec, vec,
                  pl.BlockSpec((1, D_MODEL), lambda b, i: (0, 0)),
                  pl.BlockSpec((D_MODEL, IN_PROJ), lambda b, i: (0, 0))],
        out_specs=[pl.BlockSpec((1, tm, CONV_CH), lambda b, i: (b, i, 0)),
                   pl.BlockSpec((1, tm, RWKV_PROJ), lambda b, i: (b, i, 0))],
        out_shape=[jax.ShapeDtypeStruct((bsz, s, CONV_CH), F32),
                   jax.ShapeDtypeStruct((bsz, s, RWKV_PROJ), F32)],
        compiler_params=_cparams(("parallel", "parallel")),
    )(x, sh, sc, g.reshape(1, D_MODEL), w_bf16)


CONV_HALO = 32
CONV_ROWS = 64


def _conv_kernel(cur_ref, prev_ref, w_ref, b_ref, lnw_ref, lnb_ref, o_ref, pad_ref):
    tc = cur_ref.shape[1]
    pad_ref[0:CONV_HALO, :] = jnp.where(pl.program_id(1) > 0, prev_ref[0], 0.0)
    pad_ref[CONV_HALO:CONV_HALO + tc, :] = cur_ref[0]
    off = CONV_HALO - (CONV_WIDTH - 1)
    for r0 in range(0, tc, CONV_ROWS):
        acc = jnp.zeros((CONV_ROWS, CONV_CH), F32)
        for j in range(CONV_WIDTH):
            acc = acc + w_ref[j:j + 1, :] * pad_ref[r0 + off + j:r0 + off + j + CONV_ROWS, :]
        y = acc + b_ref[...]
        mu = jnp.mean(y, axis=-1, keepdims=True)
        yc = y - mu
        var = jnp.mean(yc * yc, axis=-1, keepdims=True)
        yn = yc * lax.rsqrt(var + LN_EPS) * lnw_ref[...] + lnb_ref[...]
        o_ref[0, r0:r0 + CONV_ROWS, :] = (yn * jax.nn.sigmoid(yn)).astype(o_ref.dtype)


def _conv(yglu, w, b, lnw, lnb, tc=256):
    bsz, s, _ = yglu.shape
    hb = tc // CONV_HALO
    row = lambda a: a.reshape(1, CONV_CH)
    const = lambda shp: pl.BlockSpec(shp, lambda bb, i: (0, 0))
    return pl.pallas_call(
        _conv_kernel,
        grid=(bsz, s // tc),
        in_specs=[pl.BlockSpec((1, tc, CONV_CH), lambda bb, i: (bb, i, 0)),
                  pl.BlockSpec((1, CONV_HALO, CONV_CH), lambda bb, i: (bb, jnp.maximum(i * hb - 1, 0), 0)),
                  const((CONV_WIDTH, CONV_CH)), const((1, CONV_CH)), const((1, CONV_CH)), const((1, CONV_CH))],
        out_specs=pl.BlockSpec((1, tc, CONV_CH), lambda bb, i: (bb, i, 0)),
        out_shape=jax.ShapeDtypeStruct((bsz, s, CONV_CH), BF16),
        scratch_shapes=[pltpu.VMEM((CONV_HALO + tc, CONV_CH), F32)],
        compiler_params=_cparams(("parallel", "parallel")),
    )(yglu, yglu, w, row(b), row(lnw), row(lnb))


def _softplus(z):
    return jnp.maximum(z, 0.0) + jnp.log1p(jnp.exp(-jnp.abs(z)))


def _rwkv_pre_kernel(cur_ref, prev_ref, mu_ref, w0_ref, w2_ref, a0_ref, a2_ref, g2_ref, kk_ref, ka_ref, rk_ref,
                     bd_ref, r_o, k_o, v_o, lw_o, a_o, b_o, g_o, bonus_o):
    cur = cur_ref[0]
    prow = jnp.where(pl.program_id(1) > 0, prev_ref[0][7:8, :], 0.0)
    rows = lax.broadcasted_iota(I32, cur.shape, 0)
    prev = jnp.where(rows == 0, prow, pltpu.roll(cur, 1, axis=0))
    xs = cur + mu_ref[...] * (prev - cur)
    r = xs[:, 0:RWKV_CH]
    k = xs[:, RWKV_CH:2 * RWKV_CH]
    v = xs[:, 2 * RWKV_CH:3 * RWKV_CH]
    wa = xs[:, 3 * RWKV_CH:3 * RWKV_CH + LORA_W + LORA_A]
    gl = xs[:, 3 * RWKV_CH + LORA_W + LORA_A:]
    w = -_softplus(-(w0_ref[...] + _dot(jnp.tanh(wa), w2_ref[...], HI))) - 0.5
    a = jax.nn.sigmoid(a0_ref[...] + _dot(wa, a2_ref[...], HI))
    g = _dot(jax.nn.sigmoid(gl), g2_ref[...], HI)
    kk = k * kk_ref[...]
    kkn = kk / jnp.maximum(jnp.sqrt(_dot(kk * kk, bd_ref[...], HI)), 1e-12)
    k2 = k * (1.0 + (a - 1.0) * ka_ref[...])
    r_o[0] = r
    k_o[0] = k2
    v_o[0] = v
    lw_o[0] = -jnp.exp(w)
    a_o[0] = -kkn
    b_o[0] = kkn * a
    g_o[0] = g
    bonus_o[0] = _dot(r * k2 * rk_ref[...], bd_ref[...], HI) * v


def _rwkv_pre(prw, mu, w0, w2p, a0, a2p, g2, k_k, k_a, r_k, bd, tr=256):
    bsz, s, _ = prw.shape
    row = lambda a: a.reshape(1, -1)
    const = lambda shp: pl.BlockSpec(shp, lambda bb, i: (0, 0))
    oblk = pl.BlockSpec((1, tr, RWKV_CH), lambda bb, i: (bb, i, 0))
    return pl.pallas_call(
        _rwkv_pre_kernel,
        grid=(bsz, s // tr),
        in_specs=[pl.BlockSpec((1, tr, RWKV_PROJ), lambda bb, i: (bb, i, 0)),
                  pl.BlockSpec((1, 8, RWKV_PROJ), lambda bb, i: (bb, jnp.maximum(i * (tr // 8) - 1, 0), 0)),
                  const((1, RWKV_PROJ)), const((1, RWKV_CH)), const((LORA_W + LORA_A, RWKV_CH)),
                  const((1, RWKV_CH)), const((LORA_W + LORA_A, RWKV_CH)), const((LORA_G, RWKV_CH)),
                  const((1, RWKV_CH)), const((1, RWKV_CH)), const((1, RWKV_CH)), const((RWKV_CH, RWKV_CH))],
        out_specs=[oblk] * 8,
        out_shape=[jax.ShapeDtypeStruct((bsz, s, RWKV_CH), F32)] * 8,
        compiler_params=_cparams(("parallel", "parallel")),
    )(prw, prw, row(mu), row(w0), w2p, row(a0), a2p, g2, row(k_k), row(k_a), row(r_k), bd)


RWKV_PASSES = {"gram": 3, "lakv": 3, "solve": 3, "out": 3, "state": 3}
SOLVE_BLK = 16


def _expand(x, lane_lo):
    return jnp.concatenate([jnp.where(lane_lo, x, 0.0), jnp.where(lane_lo, 0.0, x)], axis=0)


def _rwkv_kernel(r_ref, k_ref, v_ref, lw_ref, a_ref, b_ref, g_ref, bonus_ref, bdm_ref, gnw_ref, gnb_ref,
                 y_ref, s_ref):
    C = CHUNK_T
    H2 = 2 * C

    @pl.when(pl.program_id(1) == 0)
    def _():
        s_ref[...] = jnp.zeros_like(s_ref)

    lw = lw_ref[0]
    tri = (lax.broadcasted_iota(I32, (C, C), 0) >= lax.broadcasted_iota(I32, (C, C), 1)).astype(F32)
    cum = _dot(tri, lw, HI)
    tot = cum[C - 1:C, :]
    e_pos = jnp.exp(cum)
    e_neg = jnp.exp(-cum)
    e_rem = jnp.exp(tot - cum)
    rt = r_ref[0] * e_pos
    at = a_ref[0] * jnp.exp(cum - lw)
    kt = k_ref[0] * e_neg
    bt = b_ref[0] * e_neg
    kp = k_ref[0] * e_rem
    bp = b_ref[0] * e_rem
    pc = jnp.exp(tot)
    vv = v_ref[0]

    lane_lo = lax.broadcasted_iota(I32, (C, LANES), 1) < HEAD
    tt = lax.broadcasted_iota(I32, (H2, H2), 0) % C
    ss = lax.broadcasted_iota(I32, (H2, H2), 1) % C
    strict = tt > ss
    incl = tt >= ss
    near = tt // SOLVE_BLK == ss // SOLVE_BLK
    eye = lax.broadcasted_iota(I32, (LANES, LANES), 0) == lax.broadcasted_iota(I32, (LANES, LANES), 1)

    P = RWKV_PASSES
    pairs = range(RWKV_CH // LANES)
    each = lambda fn, *lists: [fn(*args) for args in zip(*lists)]
    sls = [slice(hp * LANES, (hp + 1) * LANES) for hp in pairs]
    ax, rx, bx, kx, vx, bpx, kpx = ([_expand(t[:, sl], lane_lo) for sl in sls] for t in (at, rt, bt, kt, vv, bp, kp))
    gram = each(lambda a_, r_, b_, k_: _mm(jnp.concatenate([a_, r_], axis=0), jnp.concatenate([b_, k_], axis=0),
                                           P["gram"], _dot_nt), ax, rx, bx, kx)
    l_ab = [jnp.where(strict, g_[:H2, :H2], 0.0) for g_ in gram]
    l_ak = [jnp.where(strict, g_[:H2, H2:], 0.0) for g_ in gram]
    m_r = [jnp.concatenate([jnp.where(incl, g_[H2:, :H2], 0.0), jnp.where(incl, g_[H2:, H2:], 0.0)], axis=1)
           for g_ in gram]
    dg = [jnp.where(near, l_, 0.0) for l_ in l_ab]
    lakv = each(lambda l_, v_: _mm(l_, v_, P["lakv"]), l_ak, vx)
    xf = each(lambda a_, lv_, l_, d_: jnp.concatenate([a_, lv_, l_ - d_], axis=1), ax, lakv, l_ab, dg)
    n_sq = SOLVE_BLK.bit_length() - 1
    for it in range(n_sq):
        xf = each(lambda x_, d_: x_ + _mm(d_, x_, P["solve"]), xf, dg)
        if it + 1 < n_sq:
            dg = [_mm(d_, d_, P["solve"]) for d_ in dg]
    x = [x_[:, :2 * LANES] for x_ in xf]
    f = [x_[:, 2 * LANES:] for x_ in xf]
    n_sq = (C // SOLVE_BLK).bit_length() - 1
    for it in range(n_sq):
        x = each(lambda x_, f_: x_ + _mm(f_, x_, P["solve"]), x, f)
        if it + 1 < n_sq:
            f = [_mm(f_, f_, P["solve"]) for f_ in f]
    zero = jnp.zeros((H2, LANES), F32)
    z = each(lambda x_, v_: jnp.concatenate([x_, jnp.concatenate([zero, v_], axis=1)], axis=0), x, vx)
    w1 = each(lambda m_, z_: _mm(m_, z_, P["out"]), m_r, z)
    w2 = each(lambda b_, k_, z_: _mm(jnp.concatenate([b_, k_], axis=0), z_, P["out"], _dot_tn), bpx, kpx, z)
    ys = []
    for hp in pairs:
        ra = rx[hp] + w1[hp][:, :LANES]
        ra = ra[:C] + ra[C:]
        y0 = w1[hp][:C, LANES:] + w1[hp][C:, LANES:]
        mt = w2[hp][:, :LANES] + jnp.where(eye, pc[:, sls[hp]], 0.0)
        s0 = s_ref[hp]
        ys.append(_mm(ra, s0, P["state"]) + y0)
        s_ref[hp] = _mm(mt, s0, P["state"]) + w2[hp][:, LANES:]
    yc = [y_ - _mm_bf16_rhs(y_, bdm_ref[...]) for y_ in ys]
    ys = [c_ * lax.rsqrt(_mm_bf16_rhs(c_ * c_, bdm_ref[...]) + GN_EPS) for c_ in yc]

    yn = jnp.concatenate(ys, axis=1) * gnw_ref[...] + gnb_ref[...]
    y_ref[0] = ((yn + bonus_ref[0]) * g_ref[0]).astype(y_ref.dtype)


def _rwkv(r, k, v, lw, a, b, g, bonus, bdm, gnw, gnb):
    bsz, s, _ = r.shape
    blk = pl.BlockSpec((1, CHUNK_T, RWKV_CH), lambda bb, c: (bb, c, 0))
    const = lambda shp: pl.BlockSpec(shp, lambda bb, c: (0, 0))
    return pl.pallas_call(
        _rwkv_kernel,
        grid=(bsz, s // CHUNK_T),
        in_specs=[blk] * 8 + [const((LANES, LANES)), const((1, RWKV_CH)), const((1, RWKV_CH))],
        out_specs=blk,
        out_shape=jax.ShapeDtypeStruct((bsz, s, RWKV_CH), BF16),
        scratch_shapes=[pltpu.VMEM((RWKV_CH // LANES, LANES, LANES), F32)],
        compiler_params=_cparams(("parallel", "arbitrary")),
    )(r, k, v, lw, a, b, g, bonus, bdm, gnw.reshape(1, -1), gnb.reshape(1, -1))


def _out_proj_kernel(yc_ref, yr_ref, wo_ref, x_ref, gt_ref, g_ref, sh_ref, sc_ref, x1_o, u2_o):
    mix = _dot(yc_ref[0], wo_ref[0:CONV_CH, :]) + _dot(yr_ref[0], wo_ref[CONV_CH:, :])
    x1 = x_ref[0] + gt_ref[0] * mix
    x1_o[0] = x1
    u2_o[0] = _rms(x1, g_ref[...]) * (1.0 + sc_ref[0]) + sh_ref[0]


def _out_proj(yc, yr, wo_bf16, x, gt, g, sh, sc, tm=256):
    bsz, s, _ = x.shape
    vec = pl.BlockSpec((1, 1, D_MODEL), lambda b, i: (b, 0, 0))
    tile = lambda w: pl.BlockSpec((1, tm, w), lambda b, i: (b, i, 0))
    return pl.pallas_call(
        _out_proj_kernel,
        grid=(bsz, s // tm),
        in_specs=[tile(CONV_CH), tile(RWKV_CH), pl.BlockSpec((D_MODEL, D_MODEL), lambda b, i: (0, 0)),
                  tile(D_MODEL), vec, pl.BlockSpec((1, D_MODEL), lambda b, i: (0, 0)), vec, vec],
        out_specs=[tile(D_MODEL), tile(D_MODEL)],
        out_shape=[jax.ShapeDtypeStruct((bsz, s, D_MODEL), F32)] * 2,
        compiler_params=_cparams(("parallel", "parallel")),
    )(yc, yr, wo_bf16, x, gt, g.reshape(1, D_MODEL), sh, sc)


def _topk_rows(s, k):
    n = s.shape[0]
    rid = lax.broadcasted_iota(I32, s.shape, 0)
    vals, ids = [], []
    for _ in range(k):
        m = jnp.max(s, axis=0, keepdims=True)
        j = jnp.min(jnp.where(s == m, rid, n), axis=0, keepdims=True)
        vals.append(m)
        ids.append(j)
        s = jnp.where(rid == j, -jnp.inf, s)
    return jnp.concatenate(vals, axis=0), jnp.concatenate(ids, axis=0)


def _route_kernel(u_ref, wq_ref, keys_ref, idx_o, gate_o, q_ref, idx_s, gate_s):
    tm = u_ref.shape[0]
    q = _dot(u_ref[...].astype(BF16), wq_ref[...])
    for j in range(2 * PEER_HEADS):
        q_ref[j] = q[:, j * LANES:(j + 1) * LANES]
    K = PEER_TOPK
    row8 = lax.broadcasted_iota(I32, (8, tm), 0)

    def head(h, carry):
        av, ai = _topk_rows(_dot_nt(keys_ref[2 * h], q_ref[2 * h], HI), K)
        bv, bi = _topk_rows(_dot_nt(keys_ref[2 * h + 1], q_ref[2 * h + 1], HI), K)
        cs, cf, ce = [av[0:1] + bv], [lax.broadcasted_iota(I32, (K, tm), 0)], [ai[0:1] * PEER_NKEYS + bi]
        for x in range(1, 8):
            cs.append(jnp.where(row8 < K // (x + 1), av[x:x + 1] + bv[0:8], -jnp.inf))
            cf.append(x * K + row8)
            ce.append(ai[x:x + 1] * PEER_NKEYS + bi[0:8])
        cs.append(av[8:16] + bv[0:1])
        cf.append((row8 + 8) * K)
        ce.append(ai[8:16] * PEER_NKEYS + bi[0:1])
        s = jnp.concatenate(cs, axis=0)
        flat = jnp.concatenate(cf, axis=0)
        eid = jnp.concatenate(ce, axis=0)
        best, experts = [], []
        for _ in range(K):
            m = jnp.max(s, axis=0, keepdims=True)
            f = jnp.min(jnp.where(s == m, flat, K * K), axis=0, keepdims=True)
            sel = flat == f
            best.append(m)
            experts.append(jnp.sum(jnp.where(sel, eid, 0), axis=0, keepdims=True))
            s = jnp.where(sel, -jnp.inf, s)
        best = jnp.concatenate(best, axis=0)
        e = jnp.exp(best - best[0:1])
        gate_s[h] = e / jnp.sum(e, axis=0, keepdims=True)
        idx_s[h] = jnp.concatenate(experts, axis=0) * WORD_ROWS
        return carry

    lax.fori_loop(0, PEER_HEADS, head, 0)
    idx_o[...] = jnp.transpose(idx_s[...].reshape(NPAIR, tm))
    gate_o[...] = jnp.transpose(gate_s[...].reshape(NPAIR, tm))


def _route(u2, wq_bf16, keys, tm=256):
    n = u2.shape[0]
    oblk = pl.BlockSpec((tm, NPAIR), lambda i: (i, 0))
    return pl.pallas_call(
        _route_kernel,
        grid=(n // tm,),
        in_specs=[pl.BlockSpec((tm, D_MODEL), lambda i: (i, 0)),
                  pl.BlockSpec((D_MODEL, PEER_HEADS * PEER_DQ), lambda i: (0, 0)),
                  pl.BlockSpec((2 * PEER_HEADS, PEER_NKEYS, PEER_DQ // 2), lambda i: (0, 0, 0))],
        out_specs=[oblk, oblk],
        out_shape=[jax.ShapeDtypeStruct((n, NPAIR), I32), jax.ShapeDtypeStruct((n, NPAIR), F32)],
        scratch_shapes=[pltpu.VMEM((2 * PEER_HEADS, tm, LANES), F32),
                        pltpu.VMEM((PEER_HEADS, PEER_TOPK, tm), I32),
                        pltpu.VMEM((PEER_HEADS, PEER_TOPK, tm), F32)],
        compiler_params=_cparams(("parallel",)),
        name="route",
    )(u2, wq_bf16, keys)


TOK_UNROLL = 4


def _pack_table(t):
    e = t.shape[0]
    tb = t.astype(BF16).reshape(e, WORD_ROWS, 2, LANES)
    return lax.bitcast_convert_type(jnp.swapaxes(tb, 2, 3), I32).reshape(e * WORD_ROWS, LANES)


def _gather_rows(idx_ref, t, tab_ref, g_ref):
    for kk in range(NPAIR):
        row = pl.multiple_of(idx_ref[t, kk], WORD_ROWS)
        g_ref[kk * WORD_ROWS:(kk + 1) * WORD_ROWS, :] = tab_ref[pl.ds(row, WORD_ROWS), :]


def _peer_u_kernel(idx_ref, x_ref, gate_ref, keep_ref, sel_ref, tab_ref, w_o, g_ref, p_ref):
    tp = x_ref.shape[0]

    def tokens(i, carry):
        t0 = i * TOK_UNROLL
        for u in range(TOK_UNROLL):
            _gather_rows(idx_ref, t0 + u, tab_ref, g_ref.at[u])
        for u in range(TOK_UNROLL):
            g = pltpu.bitcast(g_ref[u], BF16)
            xc = x_ref[t0 + u]
            xt = jnp.concatenate([xc] * (LANES // CHUNKS), axis=0).astype(BF16)
            r = _dot_nt(g, xt) * keep_ref[...]
            part = jnp.sum(r.reshape(NPAIR // 16, 16 * CHUNKS, LANES), axis=1)
            p_ref[pl.ds(pl.multiple_of((t0 + u) * 8, 8), 8), :] = part
        return carry

    lax.fori_loop(0, tp // TOK_UNROLL, tokens, 0)
    z = _dot(p_ref[...], sel_ref[...], HI)
    grp = lax.broadcasted_iota(I32, z.shape, 0) % 8 == lax.broadcasted_iota(I32, z.shape, 1) // 16
    h = jnp.sum(jnp.where(grp, z, 0.0).reshape(tp, 8, LANES), axis=1)
    w_o[...] = gate_ref[...] * (0.5 * h * (1.0 + lax.erf(h * (2.0 ** -0.5))))


def _peer_u(idx, x3, gate, tab, tp=128):
    n = x3.shape[0]
    rows = jnp.arange(NPAIR * CHUNKS)[:, None]
    lanes = jnp.arange(LANES)[None, :]
    keep = ((lanes % CHUNKS == rows % CHUNKS) & (lanes // CHUNKS == (rows // CHUNKS) % 16)).astype(F32)
    sel = (jnp.arange(LANES)[:, None] // CHUNKS == jnp.arange(LANES)[None, :] % 16).astype(F32)
    return pl.pallas_call(
        _peer_u_kernel,
        grid=(n // tp,),
        in_specs=[pl.BlockSpec((tp, NPAIR), lambda i: (i, 0), memory_space=pltpu.SMEM),
                  pl.BlockSpec((tp, CHUNKS, LANES), lambda i: (i, 0, 0)),
                  pl.BlockSpec((tp, NPAIR), lambda i: (i, 0)),
                  pl.BlockSpec((NPAIR * CHUNKS, LANES), lambda i: (0, 0)),
                  pl.BlockSpec((LANES, LANES), lambda i: (0, 0)),
                  pl.BlockSpec(memory_space=pltpu.VMEM)],
        out_specs=pl.BlockSpec((tp, NPAIR), lambda i: (i, 0)),
        out_shape=jax.ShapeDtypeStruct((n, NPAIR), F32),
        scratch_shapes=[pltpu.VMEM((TOK_UNROLL, NPAIR * WORD_ROWS, LANES), I32),
                        pltpu.VMEM((tp * 8, LANES), F32)],
        compiler_params=_cparams(("arbitrary",)),
        name="peer_u",
    )(idx, x3, gate, keep, sel, tab)


def _peer_v_kernel(idx_ref, w_ref, rep_ref, diag_ref, tab_ref, o_ref, g_ref, wx_ref):
    tp = w_ref.shape[0]
    wx_ref[...] = _dot(w_ref[...], rep_ref[...], HI)

    def tokens(i, carry):
        t0 = i * TOK_UNROLL
        for u in range(TOK_UNROLL):
            _gather_rows(idx_ref, t0 + u, tab_ref, g_ref.at[u])
        for u in range(TOK_UNROLL):
            g = pltpu.bitcast(g_ref[u], BF16)
            wm = (wx_ref[pl.ds(t0 + u, 1), :] * diag_ref[...]).astype(BF16)
            o_ref[t0 + u] = _dot(wm, g)
        return carry

    lax.fori_loop(0, tp // TOK_UNROLL, tokens, 0)


def _peer_v(idx, w, tab, tp=128):
    n = w.shape[0]
    rep = jnp.repeat(jnp.eye(NPAIR, dtype=F32), CHUNKS, axis=1)
    diag = (jnp.arange(CHUNKS)[:, None] == jnp.arange(NPAIR * CHUNKS)[None, :] % CHUNKS).astype(F32)
    return pl.pallas_call(
        _peer_v_kernel,
        grid=(n // tp,),
        in_specs=[pl.BlockSpec((tp, NPAIR), lambda i: (i, 0), memory_space=pltpu.SMEM),
                  pl.BlockSpec((tp, NPAIR), lambda i: (i, 0)),
                  pl.BlockSpec((NPAIR, NPAIR * CHUNKS), lambda i: (0, 0)),
                  pl.BlockSpec((CHUNKS, NPAIR * CHUNKS), lambda i: (0, 0)),
                  pl.BlockSpec(memory_space=pltpu.VMEM)],
        out_specs=pl.BlockSpec((tp, CHUNKS, LANES), lambda i: (i, 0, 0)),
        out_shape=jax.ShapeDtypeStruct((n, CHUNKS, LANES), F32),
        scratch_shapes=[pltpu.VMEM((TOK_UNROLL, NPAIR * WORD_ROWS, LANES), I32),
                        pltpu.VMEM((tp, NPAIR * CHUNKS), F32)],
        compiler_params=_cparams(("arbitrary",)),
        name="peer_v",
    )(idx, w, rep, diag, tab)


def _final_kernel(x1_ref, p_ref, gt_ref, g_ref, o_ref):
    o_ref[0] = _rms(x1_ref[0] + gt_ref[0] * p_ref[0], g_ref[...])


def _final(x1, peer, gt, g, tm=512):
    bsz, s, _ = x1.shape
    tile = pl.BlockSpec((1, tm, D_MODEL), lambda b, i: (b, i, 0))
    return pl.pallas_call(
        _final_kernel,
        grid=(bsz, s // tm),
        in_specs=[tile, tile, pl.BlockSpec((1, 1, D_MODEL), lambda b, i: (b, 0, 0)),
                  pl.BlockSpec((1, D_MODEL), lambda b, i: (0, 0))],
        out_specs=tile,
        out_shape=jax.ShapeDtypeStruct((bsz, s, D_MODEL), F32),
        compiler_params=_cparams(("parallel", "parallel")),
    )(x1, peer, gt, g.reshape(1, D_MODEL))


def _block_diag(width, group, value):
    i = jnp.arange(width) // group
    return jnp.where(i[:, None] == i[None, :], value, 0.0).astype(F32)


def _layer(x, mod, final_g, norm_mix_g, w_in, conv_dw_w, conv_dw_b, conv_ln_w, conv_ln_b, rwkv_mu, rwkv_w0, rwkv_w2,
           rwkv_a0, rwkv_a2, rwkv_g2, rwkv_k_k, rwkv_k_a, rwkv_r_k, rwkv_gn_w, rwkv_gn_b, w_out, norm_ffn_g,
           peer_w_q, peer_sub_keys, peer_u, peer_v):
    bsz, s, _ = x.shape
    sh_mix, sc_mix, gt_mix, sh_ffn, sc_ffn, gt_ffn = (
        mod[:, i * D_MODEL:(i + 1) * D_MODEL].reshape(bsz, 1, D_MODEL) for i in range(6))

    yglu, prw = _in_proj(x, sh_mix, sc_mix, norm_mix_g, w_in.astype(BF16))
    y_conv = _conv(yglu, conv_dw_w, conv_dw_b, conv_ln_w, conv_ln_b)

    zpad = jnp.zeros((LORA_W, RWKV_CH), F32)
    r, k, v, lw, a, b, g, bonus = _rwkv_pre(
        prw, rwkv_mu, rwkv_w0, jnp.concatenate([rwkv_w2, zpad], axis=0), rwkv_a0,
        jnp.concatenate([zpad, rwkv_a2], axis=0), rwkv_g2, rwkv_k_k, rwkv_k_a, rwkv_r_k.reshape(-1),
        _block_diag(RWKV_CH, HEAD, 1.0))
    y_rwkv = _rwkv(r, k, v, lw, a, b, g, bonus, _block_diag(LANES, HEAD, 1.0 / HEAD).astype(BF16), rwkv_gn_w,
                   rwkv_gn_b)

    x1, u2 = _out_proj(y_conv, y_rwkv, w_out.astype(BF16), x, gt_mix, norm_ffn_g, sh_ffn, sc_ffn)

    n = bsz * s
    u2f = u2.reshape(n, D_MODEL)
    keys = peer_sub_keys.reshape(2 * PEER_HEADS, PEER_NKEYS, PEER_DQ // 2)
    idx, gate = _route(u2f, peer_w_q.astype(BF16), keys)
    wts = _peer_u(idx, u2f.reshape(n, CHUNKS, LANES), gate, _pack_table(peer_u))
    peer = _peer_v(idx, wts, _pack_table(peer_v))
    return _final(x1, peer.reshape(bsz, s, D_MODEL), gt_ffn, final_g)


def kernel(x, c, ada_w, ada_b, norm_mix_g, w_in, conv_dw_w, conv_dw_b, conv_ln_w, conv_ln_b, rwkv_mu, rwkv_w0,
           rwkv_w2, rwkv_a0, rwkv_a2, rwkv_g2, rwkv_k_k, rwkv_k_a, rwkv_r_k, rwkv_gn_w, rwkv_gn_b, w_out,
           norm_ffn_g, peer_w_q, peer_sub_keys, peer_u, peer_v, final_g):
    depth = ada_w.shape[0]
    assert depth == 1, "one layer: the final norm is fused into the last layer's residual"
    mod = _mod(c, ada_w[0], ada_b[0])
    return _layer(x, mod, final_g, norm_mix_g[0], w_in[0], conv_dw_w[0], conv_dw_b[0], conv_ln_w[0],
                        conv_ln_b[0], rwkv_mu[0], rwkv_w0[0], rwkv_w2[0], rwkv_a0[0], rwkv_a2[0], rwkv_g2[0],
                        rwkv_k_k[0], rwkv_k_a[0], rwkv_r_k[0], rwkv_gn_w[0], rwkv_gn_b[0], w_out[0],
                        norm_ffn_g[0], peer_w_q[0], peer_sub_keys[0], peer_u[0], peer_v[0])
```

```python
import functools

import jax
import jax.numpy as jnp
from jax import lax
from jax.experimental import pallas as pl
from jax.experimental.pallas import tpu as pltpu

F32 = jnp.float32
BF16 = jnp.bfloat16
I32 = jnp.int32
HI = lax.Precision.HIGHEST

D_MODEL = 1024
CONV_CH = 512
RWKV_CH = 512
HEAD = 64
CONV_WIDTH = 31
LORA_W = 64
LORA_A = 64
LORA_G = 128
RWKV_PROJ = 3 * RWKV_CH + LORA_W + LORA_A + LORA_G
IN_PROJ = 2 * CONV_CH + RWKV_PROJ
PEER_HEADS = 8
PEER_NKEYS = 128
PEER_EXPERTS = PEER_NKEYS * PEER_NKEYS
PEER_DQ = 256
PEER_TOPK = 16
NPAIR = PEER_HEADS * PEER_TOPK
RMS_EPS = 1e-6
LN_EPS = 1e-5
GN_EPS = 64e-5

LANES = 128
CHUNKS = D_MODEL // LANES
WORD_ROWS = CHUNKS // 2
CHUNK_T = 64
VMEM_LIMIT = 56 * 1024 * 1024


def _cparams(sem, vmem=None):
    return pltpu.CompilerParams(dimension_semantics=sem, vmem_limit_bytes=vmem or VMEM_LIMIT)


def _dot(a, b, precision=None):
    return jnp.dot(a, b, precision=precision, preferred_element_type=F32)


def _dot_nt(a, b, precision=None):
    return lax.dot_general(a, b, (((1,), (1,)), ((), ())), precision=precision, preferred_element_type=F32)


def _dot_tn(a, b, precision=None):
    return lax.dot_general(a, b, (((0,), (0,)), ((), ())), precision=precision, preferred_element_type=F32)


def _split(a):
    hi = a.astype(BF16)
    return hi, (a - hi.astype(F32)).astype(BF16)


def _mm(a, b, passes, dot=_dot):
    if passes == 6:
        return dot(a, b, HI)
    if passes == 1:
        return dot(a.astype(BF16), b.astype(BF16))
    ka = 0 if dot is _dot_tn else 1
    kb = 1 if dot is _dot_nt else 0
    ah, al = _split(a)
    bh, bl = _split(b)
    return dot(jnp.concatenate([ah, ah, al], axis=ka), jnp.concatenate([bh, bl, bh], axis=kb))


def _mm_bf16_rhs(a, b_bf16):
    ah, al = _split(a)
    return _dot(jnp.concatenate([ah, al], axis=1), jnp.concatenate([b_bf16, b_bf16], axis=0))


def _rms(x, g):
    return x * lax.rsqrt(jnp.mean(x * x, axis=-1, keepdims=True) + RMS_EPS) * g


def _mod_kernel(c_ref, w_ref, b_ref, o_ref):
    c = c_ref[...]
    o_ref[...] = _dot(c * jax.nn.sigmoid(c), w_ref[...], HI) + b_ref[...]


def _mod(c, w, b):
    bsz = c.shape[0]
    n = w.shape[1]
    tn = 1024
    return pl.pallas_call(
        _mod_kernel,
        grid=(n // tn,),
        in_specs=[pl.BlockSpec((bsz, D_MODEL), lambda j: (0, 0)),
                  pl.BlockSpec((D_MODEL, tn), lambda j: (0, j)),
                  pl.BlockSpec((1, tn), lambda j: (0, j))],
        out_specs=pl.BlockSpec((bsz, tn), lambda j: (0, j)),
        out_shape=jax.ShapeDtypeStruct((bsz, n), F32),
        compiler_params=_cparams(("parallel",)),
    )(c, w, b.reshape(1, n))


def _in_proj_kernel(x_ref, sh_ref, sc_ref, g_ref, w_ref, yglu_ref, prw_ref):
    u = _rms(x_ref[0], g_ref[...]) * (1.0 + sc_ref[0]) + sh_ref[0]
    p = _dot(u.astype(BF16), w_ref[...])
    yglu_ref[0] = p[:, :CONV_CH] * jax.nn.sigmoid(p[:, CONV_CH:2 * CONV_CH])
    prw_ref[0] = p[:, 2 * CONV_CH:]


def _in_proj(x, sh, sc, g, w_bf16, tm=256):
    bsz, s, _ = x.shape
    vec = pl.BlockSpec((1, 1, D_MODEL), lambda b, i: (b, 0, 0))
    return pl.pallas_call(
        _in_proj_kernel,
        grid=(bsz, s // tm),
        in_specs=[pl.BlockSpec((1, tm, D_MODEL), lambda b, i: (b, i, 0)), vec, vec,
                  pl.BlockSpec((1, D_MODEL), lambda b, i: (0, 0)),
                  pl.BlockSpec((D_MODEL, IN_PROJ), lambda b, i: (0, 0))],
        out_specs=[pl.BlockSpec((1, tm, CONV_CH), lambda b, i: (b, i, 0)),
                   pl.BlockSpec((1, tm, RWKV_PROJ), lambda b, i: (b, i, 0))],
        out_shape=[jax.ShapeDtypeStruct((bsz, s, CONV_CH), F32),
                   jax.ShapeDtypeStruct((bsz, s, RWKV_PROJ), F32)],
        compiler_params=_cparams(("parallel", "parallel")),
    )(x, sh, sc, g.reshape(1, D_MODEL), w_bf16)


CONV_HALO = 32
CONV_ROWS = 64


def _conv_kernel(cur_ref, prev_ref, w_ref, b_ref, lnw_ref, lnb_ref, o_ref, pad_ref):
    tc = cur_ref.shape[1]
    pad_ref[0:CONV_HALO, :] = jnp.where(pl.program_id(1) > 0, prev_ref[0], 0.0)
    pad_ref[CONV_HALO:CONV_HALO + tc, :] = cur_ref[0]
    off = CONV_HALO - (CONV_WIDTH - 1)
    for r0 in range(0, tc, CONV_ROWS):
        acc = jnp.zeros((CONV_ROWS, CONV_CH), F32)
        for j in range(CONV_WIDTH):
            acc = acc + w_ref[j:j + 1, :] * pad_ref[r0 + off + j:r0 + off + j + CONV_ROWS, :]
        y = acc + b_ref[...]
        mu = jnp.mean(y, axis=-1, keepdims=True)
        yc = y - mu
        var = jnp.mean(yc * yc, axis=-1, keepdims=True)
        yn = yc * lax.rsqrt(var + LN_EPS) * lnw_ref[...] + lnb_ref[...]
        o_ref[0, r0:r0 + CONV_ROWS, :] = (yn * jax.nn.sigmoid(yn)).astype(o_ref.dtype)


def _conv(yglu, w, b, lnw, lnb, tc=256):
    bsz, s, _ = yglu.shape
    hb = tc // CONV_HALO
    row = lambda a: a.reshape(1, CONV_CH)
    const = lambda shp: pl.BlockSpec(shp, lambda bb, i: (0, 0))
    return pl.pallas_call(
        _conv_kernel,
        grid=(bsz, s // tc),
        in_specs=[pl.BlockSpec((1, tc, CONV_CH), lambda bb, i: (bb, i, 0)),
                  pl.BlockSpec((1, CONV_HALO, CONV_CH), lambda bb, i: (bb, jnp.maximum(i * hb - 1, 0), 0)),
                  const((CONV_WIDTH, CONV_CH)), const((1, CONV_CH)), const((1, CONV_CH)), const((1, CONV_CH))],
        out_specs=pl.BlockSpec((1, tc, CONV_CH), lambda bb, i: (bb, i, 0)),
        out_shape=jax.ShapeDtypeStruct((bsz, s, CONV_CH), BF16),
        scratch_shapes=[pltpu.VMEM((CONV_HALO + tc, CONV_CH), F32)],
        compiler_params=_cparams(("parallel", "parallel")),
    )(yglu, yglu, w, row(b), row(lnw), row(lnb))


def _softplus(z):
    return jnp.maximum(z, 0.0) + jnp.log1p(jnp.exp(-jnp.abs(z)))


def _rwkv_pre_kernel(cur_ref, prev_ref, mu_ref, w0_ref, w2_ref, a0_ref, a2_ref, g2_ref, kk_ref, ka_ref, rk_ref,
                     bd_ref, r_o, k_o, v_o, lw_o, a_o, b_o, g_o, bonus_o):
    cur = cur_ref[0]
    prow = jnp.where(pl.program_id(1) > 0, prev_ref[0][7:8, :], 0.0)
    rows = lax.broadcasted_iota(I32, cur.shape, 0)
    prev = jnp.where(rows == 0, prow, pltpu.roll(cur, 1, axis=0))
    xs = cur + mu_ref[...] * (prev - cur)
    r = xs[:, 0:RWKV_CH]
    k = xs[:, RWKV_CH:2 * RWKV_CH]
    v = xs[:, 2 * RWKV_CH:3 * RWKV_CH]
    wa = xs[:, 3 * RWKV_CH:3 * RWKV_CH + LORA_W + LORA_A]
    gl = xs[:, 3 * RWKV_CH + LORA_W + LORA_A:]
    w = -_softplus(-(w0_ref[...] + _dot(jnp.tanh(wa), w2_ref[...], HI))) - 0.5
    a = jax.nn.sigmoid(a0_ref[...] + _dot(wa, a2_ref[...], HI))
    g = _dot(jax.nn.sigmoid(gl), g2_ref[...], HI)
    kk = k * kk_ref[...]
    kkn = kk / jnp.maximum(jnp.sqrt(_dot(kk * kk, bd_ref[...], HI)), 1e-12)
    k2 = k * (1.0 + (a - 1.0) * ka_ref[...])
    r_o[0] = r
    k_o[0] = k2
    v_o[0] = v
    lw_o[0] = -jnp.exp(w)
    a_o[0] = -kkn
    b_o[0] = kkn * a
    g_o[0] = g
    bonus_o[0] = _dot(r * k2 * rk_ref[...], bd_ref[...], HI) * v


def _rwkv_pre(prw, mu, w0, w2p, a0, a2p, g2, k_k, k_a, r_k, bd, tr=256):
    bsz, s, _ = prw.shape
    row = lambda a: a.reshape(1, -1)
    const = lambda shp: pl.BlockSpec(shp, lambda bb, i: (0, 0))
    oblk = pl.BlockSpec((1, tr, RWKV_CH), lambda bb, i: (bb, i, 0))
    return pl.pallas_call(
        _rwkv_pre_kernel,
        grid=(bsz, s // tr),
        in_specs=[pl.BlockSpec((1, tr, RWKV_PROJ), lambda bb, i: (bb, i, 0)),
                  pl.BlockSpec((1, 8, RWKV_PROJ), lambda bb, i: (bb, jnp.maximum(i * (tr // 8) - 1, 0), 0)),
                  const((1, RWKV_PROJ)), const((1, RWKV_CH)), const((LORA_W + LORA_A, RWKV_CH)),
                  const((1, RWKV_CH)), const((LORA_W + LORA_A, RWKV_CH)), const((LORA_G, RWKV_CH)),
                  const((1, RWKV_CH)), const((1, RWKV_CH)), const((1, RWKV_CH)), const((RWKV_CH, RWKV_CH))],
        out_specs=[oblk] * 8,
        out_shape=[jax.ShapeDtypeStruct((bsz, s, RWKV_CH), F32)] * 8,
        compiler_params=_cparams(("parallel", "parallel")),
    )(prw, prw, row(mu), row(w0), w2p, row(a0), a2p, g2, row(k_k), row(k_a), row(r_k), bd)


RWKV_PASSES = {"gram": 3, "lakv": 3, "solve": 3, "out": 3, "state": 3}
SOLVE_BLK = 16


def _expand(x, lane_lo):
    return jnp.concatenate([jnp.where(lane_lo, x, 0.0), jnp.where(lane_lo, 0.0, x)], axis=0)


def _rwkv_kernel(r_ref, k_ref, v_ref, lw_ref, a_ref, b_ref, g_ref, bonus_ref, bdm_ref, gnw_ref, gnb_ref,
                 y_ref, s_ref):
    C = CHUNK_T
    H2 = 2 * C

    @pl.when(pl.program_id(1) == 0)
    def _():
        s_ref[...] = jnp.zeros_like(s_ref)

    lw = lw_ref[0]
    tri = (lax.broadcasted_iota(I32, (C, C), 0) >= lax.broadcasted_iota(I32, (C, C), 1)).astype(F32)
    cum = _dot(tri, lw, HI)
    tot = cum[C - 1:C, :]
    e_pos = jnp.exp(cum)
    e_neg = jnp.exp(-cum)
    e_rem = jnp.exp(tot - cum)
    rt = r_ref[0] * e_pos
    at = a_ref[0] * jnp.exp(cum - lw)
    kt = k_ref[0] * e_neg
    bt = b_ref[0] * e_neg
    kp = k_ref[0] * e_rem
    bp = b_ref[0] * e_rem
    pc = jnp.exp(tot)
    vv = v_ref[0]

    lane_lo = lax.broadcasted_iota(I32, (C, LANES), 1) < HEAD
    tt = lax.broadcasted_iota(I32, (H2, H2), 0) % C
    ss = lax.broadcasted_iota(I32, (H2, H2), 1) % C
    strict = tt > ss
    incl = tt >= ss
    near = tt // SOLVE_BLK == ss // SOLVE_BLK
    eye = lax.broadcasted_iota(I32, (LANES, LANES), 0) == lax.broadcasted_iota(I32, (LANES, LANES), 1)

    P = RWKV_PASSES
    pairs = range(RWKV_CH // LANES)
    each = lambda fn, *lists: [fn(*args) for args in zip(*lists)]
    sls = [slice(hp * LANES, (hp + 1) * LANES) for hp in pairs]
    ax, rx, bx, kx, vx, bpx, kpx = ([_expand(t[:, sl], lane_lo) for sl in sls] for t in (at, rt, bt, kt, vv, bp, kp))
    gram = each(lambda a_, r_, b_, k_: _mm(jnp.concatenate([a_, r_], axis=0), jnp.concatenate([b_, k_], axis=0),
                                           P["gram"], _dot_nt), ax, rx, bx, kx)
    l_ab = [jnp.where(strict, g_[:H2, :H2], 0.0) for g_ in gram]
    l_ak = [jnp.where(strict, g_[:H2, H2:], 0.0) for g_ in gram]
    m_r = [jnp.concatenate([jnp.where(incl, g_[H2:, :H2], 0.0), jnp.where(incl, g_[H2:, H2:], 0.0)], axis=1)
           for g_ in gram]
    dg = [jnp.where(near, l_, 0.0) for l_ in l_ab]
    lakv = each(lambda l_, v_: _mm(l_, v_, P["lakv"]), l_ak, vx)
    xf = each(lambda a_, lv_, l_, d_: jnp.concatenate([a_, lv_, l_ - d_], axis=1), ax, lakv, l_ab, dg)
    n_sq = SOLVE_BLK.bit_length() - 1
    for it in range(n_sq):
        xf = each(lambda x_, d_: x_ + _mm(d_, x_, P["solve"]), xf, dg)
        if it + 1 < n_sq:
            dg = [_mm(d_, d_, P["solve"]) for d_ in dg]
    x = [x_[:, :2 * LANES] for x_ in xf]
    f = [x_[:, 2 * LANES:] for x_ in xf]
    n_sq = (C // SOLVE_BLK).bit_length() - 1
    for it in range(n_sq):
        x = each(lambda x_, f_: x_ + _mm(f_, x_, P["solve"]), x, f)
        if it + 1 < n_sq:
            f = [_mm(f_, f_, P["solve"]) for f_ in f]
    zero = jnp.zeros((H2, LANES), F32)
    z = each(lambda x_, v_: jnp.concatenate([x_, jnp.concatenate([zero, v_], axis=1)], axis=0), x, vx)
    w1 = each(lambda m_, z_: _mm(m_, z_, P["out"]), m_r, z)
    w2 = each(lambda b_, k_, z_: _mm(jnp.concatenate([b_, k_], axis=0), z_, P["out"], _dot_tn), bpx, kpx, z)
    ys = []
    for hp in pairs:
        ra = rx[hp] + w1[hp][:, :LANES]
        ra = ra[:C] + ra[C:]
        y0 = w1[hp][:C, LANES:] + w1[hp][C:, LANES:]
        mt = w2[hp][:, :LANES] + jnp.where(eye, pc[:, sls[hp]], 0.0)
        s0 = s_ref[hp]
        ys.append(_mm(ra, s0, P["state"]) + y0)
        s_ref[hp] = _mm(mt, s0, P["state"]) + w2[hp][:, LANES:]
    yc = [y_ - _mm_bf16_rhs(y_, bdm_ref[...]) for y_ in ys]
    ys = [c_ * lax.rsqrt(_mm_bf16_rhs(c_ * c_, bdm_ref[...]) + GN_EPS) for c_ in yc]

    yn = jnp.concatenate(ys, axis=1) * gnw_ref[...] + gnb_ref[...]
    y_ref[0] = ((yn + bonus_ref[0]) * g_ref[0]).astype(y_ref.dtype)


def _rwkv(r, k, v, lw, a, b, g, bonus, bdm, gnw, gnb):
    bsz, s, _ = r.shape
    blk = pl.BlockSpec((1, CHUNK_T, RWKV_CH), lambda bb, c: (bb, c, 0))
    const = lambda shp: pl.BlockSpec(shp, lambda bb, c: (0, 0))
    return pl.pallas_call(
        _rwkv_kernel,
        grid=(bsz, s // CHUNK_T),
        in_specs=[blk] * 8 + [const((LANES, LANES)), const((1, RWKV_CH)), const((1, RWKV_CH))],
        out_specs=blk,
        out_shape=jax.ShapeDtypeStruct((bsz, s, RWKV_CH), BF16),
        scratch_shapes=[pltpu.VMEM((RWKV_CH // LANES, LANES, LANES), F32)],
        compiler_params=_cparams(("parallel", "arbitrary")),
    )(r, k, v, lw, a, b, g, bonus, bdm, gnw.reshape(1, -1), gnb.reshape(1, -1))


def _out_proj_kernel(yc_ref, yr_ref, wo_ref, x_ref, gt_ref, g_ref, sh_ref, sc_ref, x1_o, u2_o):
    mix = _dot(yc_ref[0], wo_ref[0:CONV_CH, :]) + _dot(yr_ref[0], wo_ref[CONV_CH:, :])
    x1 = x_ref[0] + gt_ref[0] * mix
    x1_o[0] = x1
    u2 = _rms(x1, g_ref[...]) * (1.0 + sc_ref[0]) + sh_ref[0]
    for c in range(CHUNKS):
        u2_o[0, :, c, :] = u2[:, c * LANES:(c + 1) * LANES]


def _out_proj(yc, yr, wo_bf16, x, gt, g, sh, sc, tm=256):
    bsz, s, _ = x.shape
    vec = pl.BlockSpec((1, 1, D_MODEL), lambda b, i: (b, 0, 0))
    tile = lambda w: pl.BlockSpec((1, tm, w), lambda b, i: (b, i, 0))
    return pl.pallas_call(
        _out_proj_kernel,
        grid=(bsz, s // tm),
        in_specs=[tile(CONV_CH), tile(RWKV_CH), pl.BlockSpec((D_MODEL, D_MODEL), lambda b, i: (0, 0)),
                  tile(D_MODEL), vec, pl.BlockSpec((1, D_MODEL), lambda b, i: (0, 0)), vec, vec],
        out_specs=[tile(D_MODEL), pl.BlockSpec((1, tm, CHUNKS, LANES), lambda b, i: (b, i, 0, 0))],
        out_shape=[jax.ShapeDtypeStruct((bsz, s, D_MODEL), F32),
                   jax.ShapeDtypeStruct((bsz, s, CHUNKS, LANES), F32)],
        compiler_params=_cparams(("parallel", "parallel")),
        name="out_proj",
    )(yc, yr, wo_bf16, x, gt, g.reshape(1, D_MODEL), sh, sc)


def _topk_rows(s, k):
    n = s.shape[0]
    rid = lax.broadcasted_iota(I32, s.shape, 0)
    vals, ids = [], []
    for _ in range(k):
        m = jnp.max(s, axis=0, keepdims=True)
        j = jnp.min(jnp.where(s == m, rid, n), axis=0, keepdims=True)
        vals.append(m)
        ids.append(j)
        s = jnp.where(rid == j, -jnp.inf, s)
    return jnp.concatenate(vals, axis=0), jnp.concatenate(ids, axis=0)


def _route_kernel(u_ref, wq_ref, keys_ref, idx_o, gate_o, q_ref, idx_s, gate_s):
    tm = u_ref.shape[0]
    u = jnp.concatenate([u_ref[:, c, :] for c in range(CHUNKS)], axis=1)
    q = _dot(u.astype(BF16), wq_ref[...])
    for j in range(2 * PEER_HEADS):
        q_ref[j] = q[:, j * LANES:(j + 1) * LANES]
    K = PEER_TOPK
    row8 = lax.broadcasted_iota(I32, (8, tm), 0)

    def head(h, carry):
        av, ai = _topk_rows(_dot_nt(keys_ref[2 * h], q_ref[2 * h], HI), K)
        bv, bi = _topk_rows(_dot_nt(keys_ref[2 * h + 1], q_ref[2 * h + 1], HI), K)
        cs, cf, ce = [av[0:1] + bv], [lax.broadcasted_iota(I32, (K, tm), 0)], [ai[0:1] * PEER_NKEYS + bi]
        for x in range(1, 8):
            cs.append(jnp.where(row8 < K // (x + 1), av[x:x + 1] + bv[0:8], -jnp.inf))
            cf.append(x * K + row8)
            ce.append(ai[x:x + 1] * PEER_NKEYS + bi[0:8])
        cs.append(av[8:16] + bv[0:1])
        cf.append((row8 + 8) * K)
        ce.append(ai[8:16] * PEER_NKEYS + bi[0:1])
        s = jnp.concatenate(cs, axis=0)
        flat = jnp.concatenate(cf, axis=0)
        eid = jnp.concatenate(ce, axis=0)
        best, experts = [], []
        for _ in range(K):
            m = jnp.max(s, axis=0, keepdims=True)
            f = jnp.min(jnp.where(s == m, flat, K * K), axis=0, keepdims=True)
            sel = flat == f
            best.append(m)
            experts.append(jnp.sum(jnp.where(sel, eid, 0), axis=0, keepdims=True))
            s = jnp.where(sel, -jnp.inf, s)
        best = jnp.concatenate(best, axis=0)
        e = jnp.exp(best - best[0:1])
        gate_s[h] = e / jnp.sum(e, axis=0, keepdims=True)
        idx_s[h] = jnp.concatenate(experts, axis=0) * WORD_ROWS
        return carry

    lax.fori_loop(0, PEER_HEADS, head, 0)
    idx_o[...] = jnp.transpose(idx_s[...].reshape(NPAIR, tm))
    gate_o[...] = jnp.transpose(gate_s[...].reshape(NPAIR, tm))


def _route(u2, wq_bf16, keys, tm=256):
    n = u2.shape[0]
    oblk = pl.BlockSpec((tm, NPAIR), lambda i: (i, 0))
    return pl.pallas_call(
        _route_kernel,
        grid=(n // tm,),
        in_specs=[pl.BlockSpec((tm, CHUNKS, LANES), lambda i: (i, 0, 0)),
                  pl.BlockSpec((D_MODEL, PEER_HEADS * PEER_DQ), lambda i: (0, 0)),
                  pl.BlockSpec((2 * PEER_HEADS, PEER_NKEYS, PEER_DQ // 2), lambda i: (0, 0, 0))],
        out_specs=[oblk, oblk],
        out_shape=[jax.ShapeDtypeStruct((n, NPAIR), I32), jax.ShapeDtypeStruct((n, NPAIR), F32)],
        scratch_shapes=[pltpu.VMEM((2 * PEER_HEADS, tm, LANES), F32),
                        pltpu.VMEM((PEER_HEADS, PEER_TOPK, tm), I32),
                        pltpu.VMEM((PEER_HEADS, PEER_TOPK, tm), F32)],
        compiler_params=_cparams(("parallel",)),
        name="route",
    )(u2, wq_bf16, keys)


TOK_UNROLL = 8


def _pack_kernel(t_ref, o_ref):
    te = t_ref.shape[0]
    bits = lambda v: lax.bitcast_convert_type(v.astype(BF16).astype(F32), I32)
    for s in range(WORD_ROWS):
        lo = bits(t_ref[:, (2 * s) * LANES:(2 * s + 1) * LANES])
        hi = bits(t_ref[:, (2 * s + 1) * LANES:(2 * s + 2) * LANES])
        o_ref[pl.ds(s, te, stride=WORD_ROWS), :] = (hi & -65536) | lax.shift_right_logical(lo, 16)


def _pack_table(t, te=512):
    e = t.shape[0]
    return pl.pallas_call(
        _pack_kernel,
        grid=(e // te,),
        in_specs=[pl.BlockSpec((te, D_MODEL), lambda i: (i, 0))],
        out_specs=pl.BlockSpec((te * WORD_ROWS, LANES), lambda i: (i, 0)),
        out_shape=jax.ShapeDtypeStruct((e * WORD_ROWS, LANES), I32),
        compiler_params=_cparams(("parallel",)),
        name="pack_table",
    )(t)


def _gather_rows(idx_ref, t, tab_ref, g_ref):
    idx_row = idx_ref.at[t]
    for kk in range(NPAIR):
        row = pl.multiple_of(idx_row[kk], WORD_ROWS)
        g_ref[kk * WORD_ROWS:(kk + 1) * WORD_ROWS, :] = tab_ref[pl.ds(row, WORD_ROWS), :]


def _peer_u_kernel(idx_ref, x_ref, gate_ref, keep_ref, sel_ref, tab_ref, w_o, g_ref, p_ref):
    tp = x_ref.shape[0]

    def tokens(i, carry):
        t0 = i * TOK_UNROLL
        for u in range(TOK_UNROLL):
            _gather_rows(idx_ref, t0 + u, tab_ref, g_ref.at[u])
        for u in range(TOK_UNROLL):
            g = pltpu.bitcast(g_ref[u], BF16)
            xc = x_ref[t0 + u]
            xt = jnp.concatenate([xc] * (LANES // CHUNKS), axis=0).astype(BF16)
            r = _dot_nt(g, xt) * keep_ref[...]
            part = jnp.sum(r.reshape(NPAIR // 16, 16 * CHUNKS, LANES), axis=1)
            p_ref[pl.ds(pl.multiple_of((t0 + u) * 8, 8), 8), :] = part
        return carry

    lax.fori_loop(0, tp // TOK_UNROLL, tokens, 0)
    z = _dot(p_ref[...], sel_ref[...], HI)
    grp = lax.broadcasted_iota(I32, z.shape, 0) % 8 == lax.broadcasted_iota(I32, z.shape, 1) // 16
    h = jnp.sum(jnp.where(grp, z, 0.0).reshape(tp, 8, LANES), axis=1)
    w_o[...] = gate_ref[...] * (0.5 * h * (1.0 + lax.erf(h * (2.0 ** -0.5))))


def _peer_u(idx, x3, gate, tab, tp=128):
    n = x3.shape[0]
    rows = jnp.arange(NPAIR * CHUNKS)[:, None]
    lanes = jnp.arange(LANES)[None, :]
    keep = ((lanes % CHUNKS == rows % CHUNKS) & (lanes // CHUNKS == (rows // CHUNKS) % 16)).astype(F32)
    sel = (jnp.arange(LANES)[:, None] // CHUNKS == jnp.arange(LANES)[None, :] % 16).astype(F32)
    return pl.pallas_call(
        _peer_u_kernel,
        grid=(n // tp,),
        in_specs=[pl.BlockSpec((tp, NPAIR), lambda i: (i, 0), memory_space=pltpu.SMEM),
                  pl.BlockSpec((tp, CHUNKS, LANES), lambda i: (i, 0, 0)),
                  pl.BlockSpec((tp, NPAIR), lambda i: (i, 0)),
                  pl.BlockSpec((NPAIR * CHUNKS, LANES), lambda i: (0, 0)),
                  pl.BlockSpec((LANES, LANES), lambda i: (0, 0)),
                  pl.BlockSpec(memory_space=pltpu.VMEM)],
        out_specs=pl.BlockSpec((tp, NPAIR), lambda i: (i, 0)),
        out_shape=jax.ShapeDtypeStruct((n, NPAIR), F32),
        scratch_shapes=[pltpu.VMEM((TOK_UNROLL, NPAIR * WORD_ROWS, LANES), I32),
                        pltpu.VMEM((tp * 8, LANES), F32)],
        compiler_params=_cparams(("arbitrary",)),
        name="peer_u",
    )(idx, x3, gate, keep, sel, tab)


def _peer_v_kernel(idx_ref, w_ref, rep_ref, diag_ref, tab_ref, o_ref, g_ref, wx_ref):
    tp = w_ref.shape[0]
    wx_ref[...] = _dot(w_ref[...], rep_ref[...], HI)

    def tokens(i, carry):
        t0 = i * TOK_UNROLL
        for u in range(TOK_UNROLL):
            _gather_rows(idx_ref, t0 + u, tab_ref, g_ref.at[u])
        for u in range(TOK_UNROLL):
            g = pltpu.bitcast(g_ref[u], BF16)
            wm = (wx_ref[pl.ds(t0 + u, 1), :] * diag_ref[...]).astype(BF16)
            o_ref[t0 + u] = _dot(wm, g)
        return carry

    lax.fori_loop(0, tp // TOK_UNROLL, tokens, 0)


def _peer_v(idx, w, tab, tp=128):
    n = w.shape[0]
    rep = jnp.repeat(jnp.eye(NPAIR, dtype=F32), CHUNKS, axis=1)
    diag = (jnp.arange(CHUNKS)[:, None] == jnp.arange(NPAIR * CHUNKS)[None, :] % CHUNKS).astype(F32)
    return pl.pallas_call(
        _peer_v_kernel,
        grid=(n // tp,),
        in_specs=[pl.BlockSpec((tp, NPAIR), lambda i: (i, 0), memory_space=pltpu.SMEM),
                  pl.BlockSpec((tp, NPAIR), lambda i: (i, 0)),
                  pl.BlockSpec((NPAIR, NPAIR * CHUNKS), lambda i: (0, 0)),
                  pl.BlockSpec((CHUNKS, NPAIR * CHUNKS), lambda i: (0, 0)),
                  pl.BlockSpec(memory_space=pltpu.VMEM)],
        out_specs=pl.BlockSpec((tp, CHUNKS, LANES), lambda i: (i, 0, 0)),
        out_shape=jax.ShapeDtypeStruct((n, CHUNKS, LANES), F32),
        scratch_shapes=[pltpu.VMEM((TOK_UNROLL, NPAIR * WORD_ROWS, LANES), I32),
                        pltpu.VMEM((tp, NPAIR * CHUNKS), F32)],
        compiler_params=_cparams(("arbitrary",)),
        name="peer_v",
    )(idx, w, rep, diag, tab)


def _final_kernel(x1_ref, p_ref, gt_ref, g_ref, o_ref):
    p = jnp.concatenate([p_ref[0, :, c, :] for c in range(CHUNKS)], axis=1)
    o_ref[0] = _rms(x1_ref[0] + gt_ref[0] * p, g_ref[...])


def _final(x1, peer, gt, g, tm=512):
    bsz, s, _ = x1.shape
    tile = pl.BlockSpec((1, tm, D_MODEL), lambda b, i: (b, i, 0))
    return pl.pallas_call(
        _final_kernel,
        grid=(bsz, s // tm),
        in_specs=[tile, pl.BlockSpec((1, tm, CHUNKS, LANES), lambda b, i: (b, i, 0, 0)),
                  pl.BlockSpec((1, 1, D_MODEL), lambda b, i: (b, 0, 0)),
                  pl.BlockSpec((1, D_MODEL), lambda b, i: (0, 0))],
        out_specs=tile,
        out_shape=jax.ShapeDtypeStruct((bsz, s, D_MODEL), F32),
        compiler_params=_cparams(("parallel", "parallel")),
        name="final",
    )(x1, peer, gt, g.reshape(1, D_MODEL))


def _block_diag(width, group, value):
    i = jnp.arange(width) // group
    return jnp.where(i[:, None] == i[None, :], value, 0.0).astype(F32)


def _layer(x, mod, final_g, norm_mix_g, w_in, conv_dw_w, conv_dw_b, conv_ln_w, conv_ln_b, rwkv_mu, rwkv_w0, rwkv_w2,
           rwkv_a0, rwkv_a2, rwkv_g2, rwkv_k_k, rwkv_k_a, rwkv_r_k, rwkv_gn_w, rwkv_gn_b, w_out, norm_ffn_g,
           peer_w_q, peer_sub_keys, peer_u, peer_v):
    bsz, s, _ = x.shape
    sh_mix, sc_mix, gt_mix, sh_ffn, sc_ffn, gt_ffn = (
        mod[:, i * D_MODEL:(i + 1) * D_MODEL].reshape(bsz, 1, D_MODEL) for i in range(6))

    yglu, prw = _in_proj(x, sh_mix, sc_mix, norm_mix_g, w_in.astype(BF16))
    y_conv = _conv(yglu, conv_dw_w, conv_dw_b, conv_ln_w, conv_ln_b)

    zpad = jnp.zeros((LORA_W, RWKV_CH), F32)
    r, k, v, lw, a, b, g, bonus = _rwkv_pre(
        prw, rwkv_mu, rwkv_w0, jnp.concatenate([rwkv_w2, zpad], axis=0), rwkv_a0,
        jnp.concatenate([zpad, rwkv_a2], axis=0), rwkv_g2, rwkv_k_k, rwkv_k_a, rwkv_r_k.reshape(-1),
        _block_diag(RWKV_CH, HEAD, 1.0))
    y_rwkv = _rwkv(r, k, v, lw, a, b, g, bonus, _block_diag(LANES, HEAD, 1.0 / HEAD).astype(BF16), rwkv_gn_w,
                   rwkv_gn_b)

    x1, u2 = _out_proj(y_conv, y_rwkv, w_out.astype(BF16), x, gt_mix, norm_ffn_g, sh_ffn, sc_ffn)

    n = bsz * s
    u3 = u2.reshape(n, CHUNKS, LANES)
    keys = peer_sub_keys.reshape(2 * PEER_HEADS, PEER_NKEYS, PEER_DQ // 2)
    idx, gate = _route(u3, peer_w_q.astype(BF16), keys)
    wts = _peer_u(idx, u3, gate, _pack_table(peer_u))
    peer = _peer_v(idx, wts, _pack_table(peer_v))
    return _final(x1, peer.reshape(bsz, s, CHUNKS, LANES), gt_ffn, final_g)


def kernel(x, c, ada_w, ada_b, norm_mix_g, w_in, conv_dw_w, conv_dw_b, conv_ln_w, conv_ln_b, rwkv_mu, rwkv_w0,
           rwkv_w2, rwkv_a0, rwkv_a2, rwkv_g2, rwkv_k_k, rwkv_k_a, rwkv_r_k, rwkv_gn_w, rwkv_gn_b, w_out,
           norm_ffn_g, peer_w_q, peer_sub_keys, peer_u, peer_v, final_g):
    depth = ada_w.shape[0]
    assert depth == 1, "one layer: the final norm is fused into the last layer's residual"
    mod = _mod(c, ada_w[0], ada_b[0])
    return _layer(x, mod, final_g, norm_mix_g[0], w_in[0], conv_dw_w[0], conv_dw_b[0], conv_ln_w[0],
                        conv_ln_b[0], rwkv_mu[0], rwkv_w0[0], rwkv_w2[0], rwkv_a0[0], rwkv_a2[0], rwkv_g2[0],
                        rwkv_k_k[0], rwkv_k_a[0], rwkv_r_k[0], rwkv_gn_w[0], rwkv_gn_b[0], w_out[0],
                        norm_ffn_g[0], peer_w_q[0], peer_sub_keys[0], peer_u[0], peer_v[0])
```

```python
import functools

import jax
import jax.numpy as jnp
from jax import lax
from jax.experimental import pallas as pl
from jax.experimental.pallas import tpu as pltpu

F32 = jnp.float32
BF16 = jnp.bfloat16
I32 = jnp.int32
HI = lax.Precision.HIGHEST

D_MODEL = 1024
CONV_CH = 512
RWKV_CH = 512
HEAD = 64
CONV_WIDTH = 31
LORA_W = 64
LORA_A = 64
LORA_G = 128
RWKV_PROJ = 3 * RWKV_CH + LORA_W + LORA_A + LORA_G
IN_PROJ = 2 * CONV_CH + RWKV_PROJ
PEER_HEADS = 8
PEER_NKEYS = 128
PEER_EXPERTS = PEER_NKEYS * PEER_NKEYS
PEER_DQ = 256
PEER_TOPK = 16
NPAIR = PEER_HEADS * PEER_TOPK
RMS_EPS = 1e-6
LN_EPS = 1e-5
GN_EPS = 64e-5

LANES = 128
CHUNKS = D_MODEL // LANES
WORD_ROWS = CHUNKS // 2
CHUNK_T = 64
VMEM_LIMIT = 56 * 1024 * 1024


def _cparams(sem, vmem=None):
    return pltpu.CompilerParams(dimension_semantics=sem, vmem_limit_bytes=vmem or VMEM_LIMIT)


def _dot(a, b, precision=None):
    return jnp.dot(a, b, precision=precision, preferred_element_type=F32)


def _dot_nt(a, b, precision=None):
    return lax.dot_general(a, b, (((1,), (1,)), ((), ())), precision=precision, preferred_element_type=F32)


def _dot_tn(a, b, precision=None):
    return lax.dot_general(a, b, (((0,), (0,)), ((), ())), precision=precision, preferred_element_type=F32)


def _split(a):
    hi = a.astype(BF16)
    return hi, (a - hi.astype(F32)).astype(BF16)


def _mm(a, b, passes, dot=_dot):
    if passes == 6:
        return dot(a, b, HI)
    if passes == 1:
        return dot(a.astype(BF16), b.astype(BF16))
    ka = 0 if dot is _dot_tn else 1
    kb = 1 if dot is _dot_nt else 0
    ah, al = _split(a)
    bh, bl = _split(b)
    return dot(jnp.concatenate([ah, ah, al], axis=ka), jnp.concatenate([bh, bl, bh], axis=kb))


def _mm_bf16_rhs(a, b_bf16, terms=2):
    parts = []
    for _ in range(terms):
        p = a.astype(BF16)
        parts.append(p)
        a = a - p.astype(F32)
    return _dot(jnp.concatenate(parts, axis=1), jnp.concatenate([b_bf16] * terms, axis=0))


def _head_sums(x, bd_bf16):
    return jnp.concatenate([_mm_bf16_rhs(x[:, i * LANES:(i + 1) * LANES], bd_bf16, 3)
                            for i in range(x.shape[1] // LANES)], axis=1)


def _rms(x, g):
    return x * lax.rsqrt(jnp.mean(x * x, axis=-1, keepdims=True) + RMS_EPS) * g


def _mod_kernel(c_ref, w_ref, b_ref, o_ref):
    c = c_ref[...]
    o_ref[...] = _dot(c * jax.nn.sigmoid(c), w_ref[...], HI) + b_ref[...]


def _mod(c, w, b):
    bsz = c.shape[0]
    n = w.shape[1]
    tn = 1024
    return pl.pallas_call(
        _mod_kernel,
        grid=(n // tn,),
        in_specs=[pl.BlockSpec((bsz, D_MODEL), lambda j: (0, 0)),
                  pl.BlockSpec((D_MODEL, tn), lambda j: (0, j)),
                  pl.BlockSpec((1, tn), lambda j: (0, j))],
        out_specs=pl.BlockSpec((bsz, tn), lambda j: (0, j)),
        out_shape=jax.ShapeDtypeStruct((bsz, n), F32),
        compiler_params=_cparams(("parallel",)),
    )(c, w, b.reshape(1, n))


def _in_proj_kernel(x_ref, sh_ref, sc_ref, g_ref, w_ref, yglu_ref, prw_ref):
    u = _rms(x_ref[0], g_ref[...]) * (1.0 + sc_ref[0]) + sh_ref[0]
    p = _dot(u.astype(BF16), w_ref[...])
    yglu_ref[0] = p[:, :CONV_CH] * jax.nn.sigmoid(p[:, CONV_CH:2 * CONV_CH])
    prw_ref[0] = p[:, 2 * CONV_CH:]


def _in_proj(x, sh, sc, g, w_bf16, tm=256):
    bsz, s, _ = x.shape
    vec = pl.BlockSpec((1, 1, D_MODEL), lambda b, i: (b, 0, 0))
    return pl.pallas_call(
        _in_proj_kernel,
        grid=(bsz, s // tm),
        in_specs=[pl.BlockSpec((1, tm, D_MODEL), lambda b, i: (b, i, 0)), vec, vec,
                  pl.BlockSpec((1, D_MODEL), lambda b, i: (0, 0)),
                  pl.BlockSpec((D_MODEL, IN_PROJ), lambda b, i: (0, 0))],
        out_specs=[pl.BlockSpec((1, tm, CONV_CH), lambda b, i: (b, i, 0)),
                   pl.BlockSpec((1, tm, RWKV_PROJ), lambda b, i: (b, i, 0))],
        out_shape=[jax.ShapeDtypeStruct((bsz, s, CONV_CH), F32),
                   jax.ShapeDtypeStruct((bsz, s, RWKV_PROJ), F32)],
        compiler_params=_cparams(("parallel", "parallel")),
    )(x, sh, sc, g.reshape(1, D_MODEL), w_bf16)


CONV_HALO = 32
CONV_ROWS = 64


def _conv_kernel(cur_ref, prev_ref, w_ref, b_ref, lnw_ref, lnb_ref, o_ref, pad_ref):
    tc = cur_ref.shape[1]
    pad_ref[0:CONV_HALO, :] = jnp.where(pl.program_id(1) > 0, prev_ref[0], 0.0)
    pad_ref[CONV_HALO:CONV_HALO + tc, :] = cur_ref[0]
    off = CONV_HALO - (CONV_WIDTH - 1)
    for r0 in range(0, tc, CONV_ROWS):
        acc = jnp.zeros((CONV_ROWS, CONV_CH), F32)
        for j in range(CONV_WIDTH):
            acc = acc + w_ref[j:j + 1, :] * pad_ref[r0 + off + j:r0 + off + j + CONV_ROWS, :]
        y = acc + b_ref[...]
        mu = jnp.mean(y, axis=-1, keepdims=True)
        yc = y - mu
        var = jnp.mean(yc * yc, axis=-1, keepdims=True)
        yn = yc * lax.rsqrt(var + LN_EPS) * lnw_ref[...] + lnb_ref[...]
        o_ref[0, r0:r0 + CONV_ROWS, :] = (yn * jax.nn.sigmoid(yn)).astype(o_ref.dtype)


def _conv(yglu, w, b, lnw, lnb, tc=256):
    bsz, s, _ = yglu.shape
    hb = tc // CONV_HALO
    row = lambda a: a.reshape(1, CONV_CH)
    const = lambda shp: pl.BlockSpec(shp, lambda bb, i: (0, 0))
    return pl.pallas_call(
        _conv_kernel,
        grid=(bsz, s // tc),
        in_specs=[pl.BlockSpec((1, tc, CONV_CH), lambda bb, i: (bb, i, 0)),
                  pl.BlockSpec((1, CONV_HALO, CONV_CH), lambda bb, i: (bb, jnp.maximum(i * hb - 1, 0), 0)),
                  const((CONV_WIDTH, CONV_CH)), const((1, CONV_CH)), const((1, CONV_CH)), const((1, CONV_CH))],
        out_specs=pl.BlockSpec((1, tc, CONV_CH), lambda bb, i: (bb, i, 0)),
        out_shape=jax.ShapeDtypeStruct((bsz, s, CONV_CH), BF16),
        scratch_shapes=[pltpu.VMEM((CONV_HALO + tc, CONV_CH), F32)],
        compiler_params=_cparams(("parallel", "parallel")),
    )(yglu, yglu, w, row(b), row(lnw), row(lnb))


def _softplus(z):
    return jnp.maximum(z, 0.0) + jnp.log1p(jnp.exp(-jnp.abs(z)))


def _rwkv_pre_kernel(cur_ref, prev_ref, mu_ref, w0_ref, w2_ref, a0_ref, a2_ref, g2_ref, kk_ref, ka_ref, rk_ref,
                     bd_ref, r_o, k_o, v_o, lw_o, a_o, b_o, g_o, bonus_o):
    cur = cur_ref[0]
    prow = jnp.where(pl.program_id(1) > 0, prev_ref[0][7:8, :], 0.0)
    rows = lax.broadcasted_iota(I32, cur.shape, 0)
    prev = jnp.where(rows == 0, prow, pltpu.roll(cur, 1, axis=0))
    xs = cur + mu_ref[...] * (prev - cur)
    r = xs[:, 0:RWKV_CH]
    k = xs[:, RWKV_CH:2 * RWKV_CH]
    v = xs[:, 2 * RWKV_CH:3 * RWKV_CH]
    wa = xs[:, 3 * RWKV_CH:3 * RWKV_CH + LORA_W + LORA_A]
    gl = xs[:, 3 * RWKV_CH + LORA_W + LORA_A:]
    w = -_softplus(-(w0_ref[...] + _dot(jnp.tanh(wa), w2_ref[...], HI))) - 0.5
    a = jax.nn.sigmoid(a0_ref[...] + _dot(wa, a2_ref[...], HI))
    g = _dot(jax.nn.sigmoid(gl), g2_ref[...], HI)
    kk = k * kk_ref[...]
    kkn = kk / jnp.maximum(jnp.sqrt(_head_sums(kk * kk, bd_ref[...])), 1e-12)
    k2 = k * (1.0 + (a - 1.0) * ka_ref[...])
    r_o[0] = r
    k_o[0] = k2
    v_o[0] = v
    lw_o[0] = -jnp.exp(w)
    a_o[0] = -kkn
    b_o[0] = kkn * a
    g_o[0] = g
    bonus_o[0] = _head_sums(r * k2 * rk_ref[...], bd_ref[...]) * v


def _rwkv_pre(prw, mu, w0, w2p, a0, a2p, g2, k_k, k_a, r_k, bd, tr=256):
    bsz, s, _ = prw.shape
    row = lambda a: a.reshape(1, -1)
    const = lambda shp: pl.BlockSpec(shp, lambda bb, i: (0, 0))
    oblk = pl.BlockSpec((1, tr, RWKV_CH), lambda bb, i: (bb, i, 0))
    return pl.pallas_call(
        _rwkv_pre_kernel,
        grid=(bsz, s // tr),
        in_specs=[pl.BlockSpec((1, tr, RWKV_PROJ), lambda bb, i: (bb, i, 0)),
                  pl.BlockSpec((1, 8, RWKV_PROJ), lambda bb, i: (bb, jnp.maximum(i * (tr // 8) - 1, 0), 0)),
                  const((1, RWKV_PROJ)), const((1, RWKV_CH)), const((LORA_W + LORA_A, RWKV_CH)),
                  const((1, RWKV_CH)), const((LORA_W + LORA_A, RWKV_CH)), const((LORA_G, RWKV_CH)),
                  const((1, RWKV_CH)), const((1, RWKV_CH)), const((1, RWKV_CH)), const((LANES, LANES))],
        out_specs=[oblk] * 8,
        out_shape=[jax.ShapeDtypeStruct((bsz, s, RWKV_CH), F32)] * 8,
        compiler_params=_cparams(("parallel", "parallel")),
    )(prw, prw, row(mu), row(w0), w2p, row(a0), a2p, g2, row(k_k), row(k_a), row(r_k), bd)


RWKV_PASSES = {"gram": 3, "lakv": 3, "solve": 3, "out": 3, "state": 3}
SOLVE_BLK = 16


def _expand(x, lane_lo):
    return jnp.concatenate([jnp.where(lane_lo, x, 0.0), jnp.where(lane_lo, 0.0, x)], axis=0)


def _rwkv_kernel(r_ref, k_ref, v_ref, lw_ref, a_ref, b_ref, g_ref, bonus_ref, bdm_ref, gnw_ref, gnb_ref,
                 y_ref, s_ref):
    C = CHUNK_T
    H2 = 2 * C

    @pl.when(pl.program_id(1) == 0)
    def _():
        s_ref[...] = jnp.zeros_like(s_ref)

    lw = lw_ref[0]
    tri = (lax.broadcasted_iota(I32, (C, C), 0) >= lax.broadcasted_iota(I32, (C, C), 1)).astype(F32)
    cum = _dot(tri, lw, HI)
    tot = cum[C - 1:C, :]
    e_pos = jnp.exp(cum)
    e_neg = jnp.exp(-cum)
    e_rem = jnp.exp(tot - cum)
    rt = r_ref[0] * e_pos
    at = a_ref[0] * jnp.exp(cum - lw)
    kt = k_ref[0] * e_neg
    bt = b_ref[0] * e_neg
    kp = k_ref[0] * e_rem
    bp = b_ref[0] * e_rem
    pc = jnp.exp(tot)
    vv = v_ref[0]

    lane_lo = lax.broadcasted_iota(I32, (C, LANES), 1) < HEAD
    tt = lax.broadcasted_iota(I32, (H2, H2), 0) % C
    ss = lax.broadcasted_iota(I32, (H2, H2), 1) % C
    strict = tt > ss
    incl = tt >= ss
    near = tt // SOLVE_BLK == ss // SOLVE_BLK
    eye = lax.broadcasted_iota(I32, (LANES, LANES), 0) == lax.broadcasted_iota(I32, (LANES, LANES), 1)

    P = RWKV_PASSES
    pairs = range(RWKV_CH // LANES)
    each = lambda fn, *lists: [fn(*args) for args in zip(*lists)]
    sls = [slice(hp * LANES, (hp + 1) * LANES) for hp in pairs]
    ax, rx, bx, kx, vx, bpx, kpx = ([_expand(t[:, sl], lane_lo) for sl in sls] for t in (at, rt, bt, kt, vv, bp, kp))
    gram = each(lambda a_, r_, b_, k_: _mm(jnp.concatenate([a_, r_], axis=0), jnp.concatenate([b_, k_], axis=0),
                                           P["gram"], _dot_nt), ax, rx, bx, kx)
    l_ab = [jnp.where(strict, g_[:H2, :H2], 0.0) for g_ in gram]
    l_ak = [jnp.where(strict, g_[:H2, H2:], 0.0) for g_ in gram]
    m_r = [jnp.concatenate([jnp.where(incl, g_[H2:, :H2], 0.0), jnp.where(incl, g_[H2:, H2:], 0.0)], axis=1)
           for g_ in gram]
    dg = [jnp.where(near, l_, 0.0) for l_ in l_ab]
    lakv = each(lambda l_, v_: _mm(l_, v_, P["lakv"]), l_ak, vx)
    xf = each(lambda a_, lv_, l_, d_: jnp.concatenate([a_, lv_, l_ - d_], axis=1), ax, lakv, l_ab, dg)
    n_sq = SOLVE_BLK.bit_length() - 1
    for it in range(n_sq):
        xf = each(lambda x_, d_: x_ + _mm(d_, x_, P["solve"]), xf, dg)
        if it + 1 < n_sq:
            dg = [_mm(d_, d_, P["solve"]) for d_ in dg]
    x = [x_[:, :2 * LANES] for x_ in xf]
    f = [x_[:, 2 * LANES:] for x_ in xf]
    n_sq = (C // SOLVE_BLK).bit_length() - 1
    for it in range(n_sq):
        x = each(lambda x_, f_: x_ + _mm(f_, x_, P["solve"]), x, f)
        if it + 1 < n_sq:
            f = [_mm(f_, f_, P["solve"]) for f_ in f]
    zero = jnp.zeros((H2, LANES), F32)
    z = each(lambda x_, v_: jnp.concatenate([x_, jnp.concatenate([zero, v_], axis=1)], axis=0), x, vx)
    w1 = each(lambda m_, z_: _mm(m_, z_, P["out"]), m_r, z)
    w2 = each(lambda b_, k_, z_: _mm(jnp.concatenate([b_, k_], axis=0), z_, P["out"], _dot_tn), bpx, kpx, z)
    ys = []
    for hp in pairs:
        ra = rx[hp] + w1[hp][:, :LANES]
        ra = ra[:C] + ra[C:]
        y0 = w1[hp][:C, LANES:] + w1[hp][C:, LANES:]
        mt = w2[hp][:, :LANES] + jnp.where(eye, pc[:, sls[hp]], 0.0)
        s0 = s_ref[hp]
        ys.append(_mm(ra, s0, P["state"]) + y0)
        s_ref[hp] = _mm(mt, s0, P["state"]) + w2[hp][:, LANES:]
    yc = [y_ - _mm_bf16_rhs(y_, bdm_ref[...]) for y_ in ys]
    ys = [c_ * lax.rsqrt(_mm_bf16_rhs(c_ * c_, bdm_ref[...]) + GN_EPS) for c_ in yc]

    yn = jnp.concatenate(ys, axis=1) * gnw_ref[...] + gnb_ref[...]
    y_ref[0] = ((yn + bonus_ref[0]) * g_ref[0]).astype(y_ref.dtype)


def _rwkv(r, k, v, lw, a, b, g, bonus, bdm, gnw, gnb):
    bsz, s, _ = r.shape
    blk = pl.BlockSpec((1, CHUNK_T, RWKV_CH), lambda bb, c: (bb, c, 0))
    const = lambda shp: pl.BlockSpec(shp, lambda bb, c: (0, 0))
    return pl.pallas_call(
        _rwkv_kernel,
        grid=(bsz, s // CHUNK_T),
        in_specs=[blk] * 8 + [const((LANES, LANES)), const((1, RWKV_CH)), const((1, RWKV_CH))],
        out_specs=blk,
        out_shape=jax.ShapeDtypeStruct((bsz, s, RWKV_CH), BF16),
        scratch_shapes=[pltpu.VMEM((RWKV_CH // LANES, LANES, LANES), F32)],
        compiler_params=_cparams(("parallel", "arbitrary")),
    )(r, k, v, lw, a, b, g, bonus, bdm, gnw.reshape(1, -1), gnb.reshape(1, -1))


def _out_proj_kernel(yc_ref, yr_ref, wo_ref, x_ref, gt_ref, g_ref, sh_ref, sc_ref, x1_o, u2_o):
    mix = _dot(yc_ref[0], wo_ref[0:CONV_CH, :]) + _dot(yr_ref[0], wo_ref[CONV_CH:, :])
    x1 = x_ref[0] + gt_ref[0] * mix
    x1_o[0] = x1
    u2 = _rms(x1, g_ref[...]) * (1.0 + sc_ref[0]) + sh_ref[0]
    for c in range(CHUNKS):
        u2_o[0, :, c, :] = u2[:, c * LANES:(c + 1) * LANES]


def _out_proj(yc, yr, wo_bf16, x, gt, g, sh, sc, tm=256):
    bsz, s, _ = x.shape
    vec = pl.BlockSpec((1, 1, D_MODEL), lambda b, i: (b, 0, 0))
    tile = lambda w: pl.BlockSpec((1, tm, w), lambda b, i: (b, i, 0))
    return pl.pallas_call(
        _out_proj_kernel,
        grid=(bsz, s // tm),
        in_specs=[tile(CONV_CH), tile(RWKV_CH), pl.BlockSpec((D_MODEL, D_MODEL), lambda b, i: (0, 0)),
                  tile(D_MODEL), vec, pl.BlockSpec((1, D_MODEL), lambda b, i: (0, 0)), vec, vec],
        out_specs=[tile(D_MODEL), pl.BlockSpec((1, tm, CHUNKS, LANES), lambda b, i: (b, i, 0, 0))],
        out_shape=[jax.ShapeDtypeStruct((bsz, s, D_MODEL), F32),
                   jax.ShapeDtypeStruct((bsz, s, CHUNKS, LANES), F32)],
        compiler_params=_cparams(("parallel", "parallel")),
        name="out_proj",
    )(yc, yr, wo_bf16, x, gt, g.reshape(1, D_MODEL), sh, sc)


HEADS_STEP = 4


def _topk_rows(ss, k):
    n = ss[0].shape[0]
    rid = lax.broadcasted_iota(I32, ss[0].shape, 0)
    vals, ids = [[] for _ in ss], [[] for _ in ss]
    for _ in range(k):
        ms = [jnp.max(s, axis=0, keepdims=True) for s in ss]
        js = [jnp.min(jnp.where(s == m, rid, n), axis=0, keepdims=True) for s, m in zip(ss, ms)]
        ss = [jnp.where(rid == j, -jnp.inf, s) for s, j in zip(ss, js)]
        for v, i, m, j in zip(vals, ids, ms, js):
            v.append(m)
            i.append(j)
    return [(jnp.concatenate(v, axis=0), jnp.concatenate(i, axis=0)) for v, i in zip(vals, ids)]


def _route_kernel(u_ref, wq_ref, keys_ref, idx_o, gate_o, q_ref, idx_s, gate_s):
    tm = u_ref.shape[0]
    u = jnp.concatenate([u_ref[:, c, :] for c in range(CHUNKS)], axis=1)
    q = _dot(u.astype(BF16), wq_ref[...])
    for j in range(2 * PEER_HEADS):
        q_ref[j] = q[:, j * LANES:(j + 1) * LANES]
    K = PEER_TOPK
    tt = LANES
    tiles = range(tm // tt)
    row8 = lax.broadcasted_iota(I32, (8, tt), 0)
    rowk = lax.broadcasted_iota(I32, (K, tt), 0)

    def heads(i, carry):
        probs = [(i * HEADS_STEP + dh, lt) for dh in range(HEADS_STEP) for lt in tiles]
        scores = [_mm(keys_ref[2 * h + p], q_ref[2 * h + p, lt * tt:(lt + 1) * tt, :], 3, _dot_nt)
                  for h, lt in probs for p in range(2)]
        tops = _topk_rows(scores, K)
        ss, flats, eids = [], [], []
        for n_ in range(len(probs)):
            (av, ai), (bv, bi) = tops[2 * n_], tops[2 * n_ + 1]
            cs, cf, ce = [av[0:1] + bv], [rowk], [ai[0:1] * PEER_NKEYS + bi]
            for x in range(1, 8):
                cs.append(jnp.where(row8 < K // (x + 1), av[x:x + 1] + bv[0:8], -jnp.inf))
                cf.append(x * K + row8)
                ce.append(ai[x:x + 1] * PEER_NKEYS + bi[0:8])
            cs.append(av[8:16] + bv[0:1])
            cf.append((row8 + 8) * K)
            ce.append(ai[8:16] * PEER_NKEYS + bi[0:1])
            ss.append(jnp.concatenate(cs, axis=0))
            flats.append(jnp.concatenate(cf, axis=0))
            eids.append(jnp.concatenate(ce, axis=0))
        best, experts = [[] for _ in probs], [[] for _ in probs]
        for _ in range(K):
            ms = [jnp.max(s, axis=0, keepdims=True) for s in ss]
            sels = [fl == jnp.min(jnp.where(s == m, fl, K * K), axis=0, keepdims=True)
                    for s, m, fl in zip(ss, ms, flats)]
            for n_ in range(len(probs)):
                best[n_].append(ms[n_])
                experts[n_].append(jnp.sum(jnp.where(sels[n_], eids[n_], 0), axis=0, keepdims=True))
            ss = [jnp.where(sel, -jnp.inf, s) for s, sel in zip(ss, sels)]
        for n_, (h, lt) in enumerate(probs):
            b = jnp.concatenate(best[n_], axis=0)
            e = jnp.exp(b - b[0:1])
            gate_s[lt, h] = e / jnp.sum(e, axis=0, keepdims=True)
            idx_s[lt, h] = jnp.concatenate(experts[n_], axis=0) * WORD_ROWS
        return carry

    lax.fori_loop(0, PEER_HEADS // HEADS_STEP, heads, 0)
    for lt in tiles:
        idx_o[lt * tt:(lt + 1) * tt, :] = jnp.transpose(idx_s[lt].reshape(NPAIR, tt))
        gate_o[lt * tt:(lt + 1) * tt, :] = jnp.transpose(gate_s[lt].reshape(NPAIR, tt))


def _route(u2, wq_bf16, keys, tm=256):
    n = u2.shape[0]
    oblk = pl.BlockSpec((tm, NPAIR), lambda i: (i, 0))
    return pl.pallas_call(
        _route_kernel,
        grid=(n // tm,),
        in_specs=[pl.BlockSpec((tm, CHUNKS, LANES), lambda i: (i, 0, 0)),
                  pl.BlockSpec((D_MODEL, PEER_HEADS * PEER_DQ), lambda i: (0, 0)),
                  pl.BlockSpec((2 * PEER_HEADS, PEER_NKEYS, PEER_DQ // 2), lambda i: (0, 0, 0))],
        out_specs=[oblk, oblk],
        out_shape=[jax.ShapeDtypeStruct((n, NPAIR), I32), jax.ShapeDtypeStruct((n, NPAIR), F32)],
        scratch_shapes=[pltpu.VMEM((2 * PEER_HEADS, tm, LANES), F32),
                        pltpu.VMEM((tm // LANES, PEER_HEADS, PEER_TOPK, LANES), I32),
                        pltpu.VMEM((tm // LANES, PEER_HEADS, PEER_TOPK, LANES), F32)],
        compiler_params=_cparams(("parallel",)),
        name="route",
    )(u2, wq_bf16, keys)


TOK_UNROLL = 16


def _pack_kernel(t_ref, o_ref):
    te = t_ref.shape[0]
    bits = lambda v: lax.bitcast_convert_type(v.astype(BF16).astype(F32), I32)
    for s in range(WORD_ROWS):
        lo = bits(t_ref[:, (2 * s) * LANES:(2 * s + 1) * LANES])
        hi = bits(t_ref[:, (2 * s + 1) * LANES:(2 * s + 2) * LANES])
        o_ref[pl.ds(s, te, stride=WORD_ROWS), :] = (hi & -65536) | lax.shift_right_logical(lo, 16)


def _pack_table(t, te=512):
    e = t.shape[0]
    return pl.pallas_call(
        _pack_kernel,
        grid=(e // te,),
        in_specs=[pl.BlockSpec((te, D_MODEL), lambda i: (i, 0))],
        out_specs=pl.BlockSpec((te * WORD_ROWS, LANES), lambda i: (i, 0)),
        out_shape=jax.ShapeDtypeStruct((e * WORD_ROWS, LANES), I32),
        compiler_params=_cparams(("parallel",)),
        name="pack_table",
    )(t)


def _gather_rows(idx_ref, t, tab_ref, g_ref):
    idx_row = idx_ref.at[t]
    for kk in range(NPAIR):
        row = pl.multiple_of(idx_row[kk], WORD_ROWS)
        g_ref[kk * WORD_ROWS:(kk + 1) * WORD_ROWS, :] = tab_ref[pl.ds(row, WORD_ROWS), :]


def _peer_u_kernel(idx_ref, x_ref, gate_ref, keep_ref, sel_ref, tab_ref, w_o, g_ref, p_ref):
    tp = x_ref.shape[0]

    def tokens(i, carry):
        t0 = i * TOK_UNROLL
        for u in range(TOK_UNROLL):
            _gather_rows(idx_ref, t0 + u, tab_ref, g_ref.at[u])
        for u in range(TOK_UNROLL):
            g = pltpu.bitcast(g_ref[u], BF16)
            xc = x_ref[t0 + u]
            xt = jnp.concatenate([xc] * (LANES // CHUNKS), axis=0).astype(BF16)
            r = _dot_nt(g, xt) * keep_ref[...]
            part = jnp.sum(r.reshape(NPAIR // 16, 16 * CHUNKS, LANES), axis=1)
            p_ref[pl.ds(pl.multiple_of((t0 + u) * 8, 8), 8), :] = part
        return carry

    lax.fori_loop(0, tp // TOK_UNROLL, tokens, 0)
    z = _dot(p_ref[...], sel_ref[...], HI)
    grp = lax.broadcasted_iota(I32, z.shape, 0) % 8 == lax.broadcasted_iota(I32, z.shape, 1) // 16
    h = jnp.sum(jnp.where(grp, z, 0.0).reshape(tp, 8, LANES), axis=1)
    w_o[...] = gate_ref[...] * (0.5 * h * (1.0 + lax.erf(h * (2.0 ** -0.5))))


def _peer_u(idx, x3, gate, tab, tp=128):
    n = x3.shape[0]
    rows = jnp.arange(NPAIR * CHUNKS)[:, None]
    lanes = jnp.arange(LANES)[None, :]
    keep = ((lanes % CHUNKS == rows % CHUNKS) & (lanes // CHUNKS == (rows // CHUNKS) % 16)).astype(F32)
    sel = (jnp.arange(LANES)[:, None] // CHUNKS == jnp.arange(LANES)[None, :] % 16).astype(F32)
    return pl.pallas_call(
        _peer_u_kernel,
        grid=(n // tp,),
        in_specs=[pl.BlockSpec((tp, NPAIR), lambda i: (i, 0), memory_space=pltpu.SMEM),
                  pl.BlockSpec((tp, CHUNKS, LANES), lambda i: (i, 0, 0)),
                  pl.BlockSpec((tp, NPAIR), lambda i: (i, 0)),
                  pl.BlockSpec((NPAIR * CHUNKS, LANES), lambda i: (0, 0)),
                  pl.BlockSpec((LANES, LANES), lambda i: (0, 0)),
                  pl.BlockSpec(memory_space=pltpu.VMEM)],
        out_specs=pl.BlockSpec((tp, NPAIR), lambda i: (i, 0)),
        out_shape=jax.ShapeDtypeStruct((n, NPAIR), F32),
        scratch_shapes=[pltpu.VMEM((TOK_UNROLL, NPAIR * WORD_ROWS, LANES), I32),
                        pltpu.VMEM((tp * 8, LANES), F32)],
        compiler_params=_cparams(("arbitrary",)),
        name="peer_u",
    )(idx, x3, gate, keep, sel, tab)


def _peer_v_kernel(idx_ref, w_ref, rep_ref, diag_ref, tab_ref, o_ref, g_ref, wx_ref):
    tp = w_ref.shape[0]
    wx_ref[...] = _dot(w_ref[...], rep_ref[...], HI)

    def tokens(i, carry):
        t0 = i * TOK_UNROLL
        for u in range(TOK_UNROLL):
            _gather_rows(idx_ref, t0 + u, tab_ref, g_ref.at[u])
        for u in range(TOK_UNROLL):
            g = pltpu.bitcast(g_ref[u], BF16)
            wm = (wx_ref[pl.ds(t0 + u, 1), :] * diag_ref[...]).astype(BF16)
            o_ref[t0 + u] = _dot(wm, g)
        return carry

    lax.fori_loop(0, tp // TOK_UNROLL, tokens, 0)


def _peer_v(idx, w, tab, tp=128):
    n = w.shape[0]
    rep = jnp.repeat(jnp.eye(NPAIR, dtype=F32), CHUNKS, axis=1)
    diag = (jnp.arange(CHUNKS)[:, None] == jnp.arange(NPAIR * CHUNKS)[None, :] % CHUNKS).astype(F32)
    return pl.pallas_call(
        _peer_v_kernel,
        grid=(n // tp,),
        in_specs=[pl.BlockSpec((tp, NPAIR), lambda i: (i, 0), memory_space=pltpu.SMEM),
                  pl.BlockSpec((tp, NPAIR), lambda i: (i, 0)),
                  pl.BlockSpec((NPAIR, NPAIR * CHUNKS), lambda i: (0, 0)),
                  pl.BlockSpec((CHUNKS, NPAIR * CHUNKS), lambda i: (0, 0)),
                  pl.BlockSpec(memory_space=pltpu.VMEM)],
        out_specs=pl.BlockSpec((tp, CHUNKS, LANES), lambda i: (i, 0, 0)),
        out_shape=jax.ShapeDtypeStruct((n, CHUNKS, LANES), F32),
        scratch_shapes=[pltpu.VMEM((TOK_UNROLL, NPAIR * WORD_ROWS, LANES), I32),
                        pltpu.VMEM((tp, NPAIR * CHUNKS), F32)],
        compiler_params=_cparams(("arbitrary",)),
        name="peer_v",
    )(idx, w, rep, diag, tab)


def _final_kernel(x1_ref, p_ref, gt_ref, g_ref, o_ref):
    p = jnp.concatenate([p_ref[0, :, c, :] for c in range(CHUNKS)], axis=1)
    o_ref[0] = _rms(x1_ref[0] + gt_ref[0] * p, g_ref[...])


def _final(x1, peer, gt, g, tm=512):
    bsz, s, _ = x1.shape
    tile = pl.BlockSpec((1, tm, D_MODEL), lambda b, i: (b, i, 0))
    return pl.pallas_call(
        _final_kernel,
        grid=(bsz, s // tm),
        in_specs=[tile, pl.BlockSpec((1, tm, CHUNKS, LANES), lambda b, i: (b, i, 0, 0)),
                  pl.BlockSpec((1, 1, D_MODEL), lambda b, i: (b, 0, 0)),
                  pl.BlockSpec((1, D_MODEL), lambda b, i: (0, 0))],
        out_specs=tile,
        out_shape=jax.ShapeDtypeStruct((bsz, s, D_MODEL), F32),
        compiler_params=_cparams(("parallel", "parallel")),
        name="final",
    )(x1, peer, gt, g.reshape(1, D_MODEL))


def _block_diag(width, group, value):
    i = jnp.arange(width) // group
    return jnp.where(i[:, None] == i[None, :], value, 0.0).astype(F32)


def _layer(x, mod, final_g, norm_mix_g, w_in, conv_dw_w, conv_dw_b, conv_ln_w, conv_ln_b, rwkv_mu, rwkv_w0, rwkv_w2,
           rwkv_a0, rwkv_a2, rwkv_g2, rwkv_k_k, rwkv_k_a, rwkv_r_k, rwkv_gn_w, rwkv_gn_b, w_out, norm_ffn_g,
           peer_w_q, peer_sub_keys, peer_u, peer_v):
    bsz, s, _ = x.shape
    sh_mix, sc_mix, gt_mix, sh_ffn, sc_ffn, gt_ffn = (
        mod[:, i * D_MODEL:(i + 1) * D_MODEL].reshape(bsz, 1, D_MODEL) for i in range(6))

    yglu, prw = _in_proj(x, sh_mix, sc_mix, norm_mix_g, w_in.astype(BF16))
    y_conv = _conv(yglu, conv_dw_w, conv_dw_b, conv_ln_w, conv_ln_b)

    zpad = jnp.zeros((LORA_W, RWKV_CH), F32)
    r, k, v, lw, a, b, g, bonus = _rwkv_pre(
        prw, rwkv_mu, rwkv_w0, jnp.concatenate([rwkv_w2, zpad], axis=0), rwkv_a0,
        jnp.concatenate([zpad, rwkv_a2], axis=0), rwkv_g2, rwkv_k_k, rwkv_k_a, rwkv_r_k.reshape(-1),
        _block_diag(LANES, HEAD, 1.0).astype(BF16))
    y_rwkv = _rwkv(r, k, v, lw, a, b, g, bonus, _block_diag(LANES, HEAD, 1.0 / HEAD).astype(BF16), rwkv_gn_w,
                   rwkv_gn_b)

    x1, u2 = _out_proj(y_conv, y_rwkv, w_out.astype(BF16), x, gt_mix, norm_ffn_g, sh_ffn, sc_ffn)

    n = bsz * s
    u3 = u2.reshape(n, CHUNKS, LANES)
    keys = peer_sub_keys.reshape(2 * PEER_HEADS, PEER_NKEYS, PEER_DQ // 2)
    idx, gate = _route(u3, peer_w_q.astype(BF16), keys)
    wts = _peer_u(idx, u3, gate, _pack_table(peer_u))
    peer = _peer_v(idx, wts, _pack_table(peer_v))
    return _final(x1, peer.reshape(bsz, s, CHUNKS, LANES), gt_ffn, final_g)


def kernel(x, c, ada_w, ada_b, norm_mix_g, w_in, conv_dw_w, conv_dw_b, conv_ln_w, conv_ln_b, rwkv_mu, rwkv_w0,
           rwkv_w2, rwkv_a0, rwkv_a2, rwkv_g2, rwkv_k_k, rwkv_k_a, rwkv_r_k, rwkv_gn_w, rwkv_gn_b, w_out,
           norm_ffn_g, peer_w_q, peer_sub_keys, peer_u, peer_v, final_g):
    depth = ada_w.shape[0]
    assert depth == 1, "one layer: the final norm is fused into the last layer's residual"
    mod = _mod(c, ada_w[0], ada_b[0])
    return _layer(x, mod, final_g, norm_mix_g[0], w_in[0], conv_dw_w[0], conv_dw_b[0], conv_ln_w[0],
                        conv_ln_b[0], rwkv_mu[0], rwkv_w0[0], rwkv_w2[0], rwkv_a0[0], rwkv_a2[0], rwkv_g2[0],
                        rwkv_k_k[0], rwkv_k_a[0], rwkv_r_k[0], rwkv_gn_w[0], rwkv_gn_b[0], w_out[0],
                        norm_ffn_g[0], peer_w_q[0], peer_sub_keys[0], peer_u[0], peer_v[0])
```

```python
import functools

import jax
import jax.numpy as jnp
from jax import lax
from jax.experimental import pallas as pl
from jax.experimental.pallas import tpu as pltpu

F32 = jnp.float32
BF16 = jnp.bfloat16
I32 = jnp.int32
HI = lax.Precision.HIGHEST

D_MODEL = 1024
CONV_CH = 512
RWKV_CH = 512
HEAD = 64
CONV_WIDTH = 31
LORA_W = 64
LORA_A = 64
LORA_G = 128
RWKV_PROJ = 3 * RWKV_CH + LORA_W + LORA_A + LORA_G
IN_PROJ = 2 * CONV_CH + RWKV_PROJ
PEER_HEADS = 8
PEER_NKEYS = 128
PEER_EXPERTS = PEER_NKEYS * PEER_NKEYS
PEER_DQ = 256
PEER_TOPK = 16
NPAIR = PEER_HEADS * PEER_TOPK
RMS_EPS = 1e-6
LN_EPS = 1e-5
GN_EPS = 64e-5

LANES = 128
CHUNKS = D_MODEL // LANES
WORD_ROWS = CHUNKS // 2
CHUNK_T = 64
VMEM_LIMIT = 56 * 1024 * 1024


def _cparams(sem, vmem=None):
    return pltpu.CompilerParams(dimension_semantics=sem, vmem_limit_bytes=vmem or VMEM_LIMIT)


def _dot(a, b, precision=None):
    return jnp.dot(a, b, precision=precision, preferred_element_type=F32)


def _dot_nt(a, b, precision=None):
    return lax.dot_general(a, b, (((1,), (1,)), ((), ())), precision=precision, preferred_element_type=F32)


def _dot_tn(a, b, precision=None):
    return lax.dot_general(a, b, (((0,), (0,)), ((), ())), precision=precision, preferred_element_type=F32)


def _split(a):
    hi = a.astype(BF16)
    return hi, (a - hi.astype(F32)).astype(BF16)


def _mm(a, b, passes, dot=_dot):
    if passes == 6:
        return dot(a, b, HI)
    if passes == 1:
        return dot(a.astype(BF16), b.astype(BF16))
    ka = 0 if dot is _dot_tn else 1
    kb = 1 if dot is _dot_nt else 0
    ah, al = _split(a)
    bh, bl = _split(b)
    return dot(jnp.concatenate([ah, ah, al], axis=ka), jnp.concatenate([bh, bl, bh], axis=kb))


def _mm_bf16_rhs(a, b_bf16, terms=2):
    parts = []
    for _ in range(terms):
        p = a.astype(BF16)
        parts.append(p)
        a = a - p.astype(F32)
    return _dot(jnp.concatenate(parts, axis=1), jnp.concatenate([b_bf16] * terms, axis=0))


def _head_sums(x, bd_bf16):
    return jnp.concatenate([_mm_bf16_rhs(x[:, i * LANES:(i + 1) * LANES], bd_bf16, 3)
                            for i in range(x.shape[1] // LANES)], axis=1)


def _rms(x, g):
    return x * lax.rsqrt(jnp.mean(x * x, axis=-1, keepdims=True) + RMS_EPS) * g


def _mod_kernel(c_ref, w_ref, b_ref, o_ref):
    c = c_ref[...]
    o_ref[...] = _dot(c * jax.nn.sigmoid(c), w_ref[...], HI) + b_ref[...]


def _mod(c, w, b):
    bsz = c.shape[0]
    n = w.shape[1]
    tn = 1024
    return pl.pallas_call(
        _mod_kernel,
        grid=(n // tn,),
        in_specs=[pl.BlockSpec((bsz, D_MODEL), lambda j: (0, 0)),
                  pl.BlockSpec((D_MODEL, tn), lambda j: (0, j)),
                  pl.BlockSpec((1, tn), lambda j: (0, j))],
        out_specs=pl.BlockSpec((bsz, tn), lambda j: (0, j)),
        out_shape=jax.ShapeDtypeStruct((bsz, n), F32),
        compiler_params=_cparams(("parallel",)),
    )(c, w, b.reshape(1, n))


def _in_proj_kernel(x_ref, sh_ref, sc_ref, g_ref, w_ref, yglu_ref, prw_ref):
    u = _rms(x_ref[0], g_ref[...]) * (1.0 + sc_ref[0]) + sh_ref[0]
    p = _dot(u.astype(BF16), w_ref[...])
    yglu_ref[0] = p[:, :CONV_CH] * jax.nn.sigmoid(p[:, CONV_CH:2 * CONV_CH])
    prw_ref[0] = p[:, 2 * CONV_CH:]


def _in_proj(x, sh, sc, g, w_bf16, tm=256):
    bsz, s, _ = x.shape
    vec = pl.BlockSpec((1, 1, D_MODEL), lambda b, i: (b, 0, 0))
    return pl.pallas_call(
        _in_proj_kernel,
        grid=(bsz, s // tm),
        in_specs=[pl.BlockSpec((1, tm, D_MODEL), lambda b, i: (b, i, 0)), vec, vec,
                  pl.BlockSpec((1, D_MODEL), lambda b, i: (0, 0)),
                  pl.BlockSpec((D_MODEL, IN_PROJ), lambda b, i: (0, 0))],
        out_specs=[pl.BlockSpec((1, tm, CONV_CH), lambda b, i: (b, i, 0)),
                   pl.BlockSpec((1, tm, RWKV_PROJ), lambda b, i: (b, i, 0))],
        out_shape=[jax.ShapeDtypeStruct((bsz, s, CONV_CH), F32),
                   jax.ShapeDtypeStruct((bsz, s, RWKV_PROJ), F32)],
        compiler_params=_cparams(("parallel", "parallel")),
    )(x, sh, sc, g.reshape(1, D_MODEL), w_bf16)


CONV_HALO = 32
CONV_ROWS = 64


def _conv_kernel(cur_ref, prev_ref, w_ref, b_ref, lnw_ref, lnb_ref, o_ref, pad_ref):
    tc = cur_ref.shape[1]
    pad_ref[0, 0:CONV_HALO, :] = jnp.where(pl.program_id(1) > 0, prev_ref[0], 0.0)
    pad_ref[0, CONV_HALO:CONV_HALO + tc, :] = cur_ref[0]
    span = CONV_HALO + tc - 8
    for r in range(1, 8):
        pad_ref[r, 0:span, :] = pad_ref[0, r:r + span, :]
    off = CONV_HALO - (CONV_WIDTH - 1)
    for r0 in range(0, tc, CONV_ROWS):
        acc = jnp.zeros((CONV_ROWS, CONV_CH), F32)
        for j in range(CONV_WIDTH):
            q, r = divmod(off + j, 8)
            acc = acc + w_ref[j:j + 1, :] * pad_ref[r, r0 + 8 * q:r0 + 8 * q + CONV_ROWS, :]
        y = acc + b_ref[...]
        mu = jnp.mean(y, axis=-1, keepdims=True)
        yc = y - mu
        var = jnp.mean(yc * yc, axis=-1, keepdims=True)
        yn = yc * lax.rsqrt(var + LN_EPS) * lnw_ref[...] + lnb_ref[...]
        o_ref[0, r0:r0 + CONV_ROWS, :] = (yn * jax.nn.sigmoid(yn)).astype(o_ref.dtype)


def _conv(yglu, w, b, lnw, lnb, tc=256):
    bsz, s, _ = yglu.shape
    hb = tc // CONV_HALO
    row = lambda a: a.reshape(1, CONV_CH)
    const = lambda shp: pl.BlockSpec(shp, lambda bb, i: (0, 0))
    return pl.pallas_call(
        _conv_kernel,
        grid=(bsz, s // tc),
        in_specs=[pl.BlockSpec((1, tc, CONV_CH), lambda bb, i: (bb, i, 0)),
                  pl.BlockSpec((1, CONV_HALO, CONV_CH), lambda bb, i: (bb, jnp.maximum(i * hb - 1, 0), 0)),
                  const((CONV_WIDTH, CONV_CH)), const((1, CONV_CH)), const((1, CONV_CH)), const((1, CONV_CH))],
        out_specs=pl.BlockSpec((1, tc, CONV_CH), lambda bb, i: (bb, i, 0)),
        out_shape=jax.ShapeDtypeStruct((bsz, s, CONV_CH), BF16),
        scratch_shapes=[pltpu.VMEM((8, CONV_HALO + tc, CONV_CH), F32)],
        compiler_params=_cparams(("parallel", "parallel")),
    )(yglu, yglu, w, row(b), row(lnw), row(lnb))


def _softplus(z):
    return jnp.maximum(z, 0.0) + jnp.log1p(jnp.exp(-jnp.abs(z)))


def _rwkv_pre_kernel(cur_ref, prev_ref, mu_ref, w0_ref, w2_ref, a0_ref, a2_ref, g2_ref, kk_ref, ka_ref, rk_ref,
                     bd_ref, r_o, k_o, v_o, lw_o, a_o, b_o, g_o, bonus_o):
    cur = cur_ref[0]
    prow = jnp.where(pl.program_id(1) > 0, prev_ref[0][7:8, :], 0.0)
    rows = lax.broadcasted_iota(I32, cur.shape, 0)
    prev = jnp.where(rows == 0, prow, pltpu.roll(cur, 1, axis=0))
    xs = cur + mu_ref[...] * (prev - cur)
    r = xs[:, 0:RWKV_CH]
    k = xs[:, RWKV_CH:2 * RWKV_CH]
    v = xs[:, 2 * RWKV_CH:3 * RWKV_CH]
    wa = xs[:, 3 * RWKV_CH:3 * RWKV_CH + LORA_W + LORA_A]
    gl = xs[:, 3 * RWKV_CH + LORA_W + LORA_A:]
    w = -_softplus(-(w0_ref[...] + _dot(jnp.tanh(wa), w2_ref[...], HI))) - 0.5
    a = jax.nn.sigmoid(a0_ref[...] + _dot(wa, a2_ref[...], HI))
    g = _dot(jax.nn.sigmoid(gl), g2_ref[...], HI)
    kk = k * kk_ref[...]
    kkn = kk / jnp.maximum(jnp.sqrt(_head_sums(kk * kk, bd_ref[...])), 1e-12)
    k2 = k * (1.0 + (a - 1.0) * ka_ref[...])
    r_o[0] = r
    k_o[0] = k2
    v_o[0] = v
    lw_o[0] = -jnp.exp(w)
    a_o[0] = -kkn
    b_o[0] = kkn * a
    g_o[0] = g
    bonus_o[0] = _head_sums(r * k2 * rk_ref[...], bd_ref[...]) * v


def _rwkv_pre(prw, mu, w0, w2p, a0, a2p, g2, k_k, k_a, r_k, bd, tr=256):
    bsz, s, _ = prw.shape
    row = lambda a: a.reshape(1, -1)
    const = lambda shp: pl.BlockSpec(shp, lambda bb, i: (0, 0))
    oblk = pl.BlockSpec((1, tr, RWKV_CH), lambda bb, i: (bb, i, 0))
    return pl.pallas_call(
        _rwkv_pre_kernel,
        grid=(bsz, s // tr),
        in_specs=[pl.BlockSpec((1, tr, RWKV_PROJ), lambda bb, i: (bb, i, 0)),
                  pl.BlockSpec((1, 8, RWKV_PROJ), lambda bb, i: (bb, jnp.maximum(i * (tr // 8) - 1, 0), 0)),
                  const((1, RWKV_PROJ)), const((1, RWKV_CH)), const((LORA_W + LORA_A, RWKV_CH)),
                  const((1, RWKV_CH)), const((LORA_W + LORA_A, RWKV_CH)), const((LORA_G, RWKV_CH)),
                  const((1, RWKV_CH)), const((1, RWKV_CH)), const((1, RWKV_CH)), const((LANES, LANES))],
        out_specs=[oblk] * 8,
        out_shape=[jax.ShapeDtypeStruct((bsz, s, RWKV_CH), F32)] * 8,
        compiler_params=_cparams(("parallel", "parallel")),
    )(prw, prw, row(mu), row(w0), w2p, row(a0), a2p, g2, row(k_k), row(k_a), row(r_k), bd)


RWKV_PASSES = {"gram": 3, "lakv": 3, "solve": 3, "out": 3, "state": 3}
SOLVE_BLK = 16


def _expand(x, lane_lo):
    return jnp.concatenate([jnp.where(lane_lo, x, 0.0), jnp.where(lane_lo, 0.0, x)], axis=0)


def _rwkv_kernel(r_ref, k_ref, v_ref, lw_ref, a_ref, b_ref, g_ref, bonus_ref, bdm_ref, gnw_ref, gnb_ref,
                 y_ref, s_ref):
    C = CHUNK_T
    H2 = 2 * C

    @pl.when(pl.program_id(1) == 0)
    def _():
        s_ref[...] = jnp.zeros_like(s_ref)

    lw = lw_ref[0]
    tri = (lax.broadcasted_iota(I32, (C, C), 0) >= lax.broadcasted_iota(I32, (C, C), 1)).astype(F32)
    cum = _dot(tri, lw, HI)
    tot = cum[C - 1:C, :]
    e_pos = jnp.exp(cum)
    e_neg = jnp.exp(-cum)
    e_rem = jnp.exp(tot - cum)
    rt = r_ref[0] * e_pos
    at = a_ref[0] * jnp.exp(cum - lw)
    kt = k_ref[0] * e_neg
    bt = b_ref[0] * e_neg
    kp = k_ref[0] * e_rem
    bp = b_ref[0] * e_rem
    pc = jnp.exp(tot)
    vv = v_ref[0]

    lane_lo = lax.broadcasted_iota(I32, (C, LANES), 1) < HEAD
    tt = lax.broadcasted_iota(I32, (H2, H2), 0) % C
    ss = lax.broadcasted_iota(I32, (H2, H2), 1) % C
    strict = tt > ss
    incl = tt >= ss
    near = tt // SOLVE_BLK == ss // SOLVE_BLK
    eye = lax.broadcasted_iota(I32, (LANES, LANES), 0) == lax.broadcasted_iota(I32, (LANES, LANES), 1)

    P = RWKV_PASSES
    pairs = range(RWKV_CH // LANES)
    each = lambda fn, *lists: [fn(*args) for args in zip(*lists)]
    sls = [slice(hp * LANES, (hp + 1) * LANES) for hp in pairs]
    ax, rx, bx, kx, vx, bpx, kpx = ([_expand(t[:, sl], lane_lo) for sl in sls] for t in (at, rt, bt, kt, vv, bp, kp))
    gram = each(lambda a_, r_, b_, k_: _mm(jnp.concatenate([a_, r_], axis=0), jnp.concatenate([b_, k_], axis=0),
                                           P["gram"], _dot_nt), ax, rx, bx, kx)
    l_ab = [jnp.where(strict, g_[:H2, :H2], 0.0) for g_ in gram]
    l_ak = [jnp.where(strict, g_[:H2, H2:], 0.0) for g_ in gram]
    m_r = [jnp.concatenate([jnp.where(incl, g_[H2:, :H2], 0.0), jnp.where(incl, g_[H2:, H2:], 0.0)], axis=1)
           for g_ in gram]
    dg = [jnp.where(near, l_, 0.0) for l_ in l_ab]
    lakv = each(lambda l_, v_: _mm(l_, v_, P["lakv"]), l_ak, vx)
    xf = each(lambda a_, lv_, l_, d_: jnp.concatenate([a_, lv_, l_ - d_], axis=1), ax, lakv, l_ab, dg)
    n_sq = SOLVE_BLK.bit_length() - 1
    for it in range(n_sq):
        xf = each(lambda x_, d_: x_ + _mm(d_, x_, P["solve"]), xf, dg)
        if it + 1 < n_sq:
            dg = [_mm(d_, d_, P["solve"]) for d_ in dg]
    x = [x_[:, :2 * LANES] for x_ in xf]
    f = [x_[:, 2 * LANES:] for x_ in xf]
    n_sq = (C // SOLVE_BLK).bit_length() - 1
    for it in range(n_sq):
        x = each(lambda x_, f_: x_ + _mm(f_, x_, P["solve"]), x, f)
        if it + 1 < n_sq:
            f = [_mm(f_, f_, P["solve"]) for f_ in f]
    zero = jnp.zeros((H2, LANES), F32)
    z = each(lambda x_, v_: jnp.concatenate([x_, jnp.concatenate([zero, v_], axis=1)], axis=0), x, vx)
    w1 = each(lambda m_, z_: _mm(m_, z_, P["out"]), m_r, z)
    w2 = each(lambda b_, k_, z_: _mm(jnp.concatenate([b_, k_], axis=0), z_, P["out"], _dot_tn), bpx, kpx, z)
    ys = []
    for hp in pairs:
        ra = rx[hp] + w1[hp][:, :LANES]
        ra = ra[:C] + ra[C:]
        y0 = w1[hp][:C, LANES:] + w1[hp][C:, LANES:]
        mt = w2[hp][:, :LANES] + jnp.where(eye, pc[:, sls[hp]], 0.0)
        s0 = s_ref[hp]
        ys.append(_mm(ra, s0, P["state"]) + y0)
        s_ref[hp] = _mm(mt, s0, P["state"]) + w2[hp][:, LANES:]
    yc = [y_ - _mm_bf16_rhs(y_, bdm_ref[...]) for y_ in ys]
    ys = [c_ * lax.rsqrt(_mm_bf16_rhs(c_ * c_, bdm_ref[...]) + GN_EPS) for c_ in yc]

    yn = jnp.concatenate(ys, axis=1) * gnw_ref[...] + gnb_ref[...]
    y_ref[0] = ((yn + bonus_ref[0]) * g_ref[0]).astype(y_ref.dtype)


def _rwkv(r, k, v, lw, a, b, g, bonus, bdm, gnw, gnb):
    bsz, s, _ = r.shape
    blk = pl.BlockSpec((1, CHUNK_T, RWKV_CH), lambda bb, c: (bb, c, 0))
    const = lambda shp: pl.BlockSpec(shp, lambda bb, c: (0, 0))
    return pl.pallas_call(
        _rwkv_kernel,
        grid=(bsz, s // CHUNK_T),
        in_specs=[blk] * 8 + [const((LANES, LANES)), const((1, RWKV_CH)), const((1, RWKV_CH))],
        out_specs=blk,
        out_shape=jax.ShapeDtypeStruct((bsz, s, RWKV_CH), BF16),
        scratch_shapes=[pltpu.VMEM((RWKV_CH // LANES, LANES, LANES), F32)],
        compiler_params=_cparams(("parallel", "arbitrary")),
    )(r, k, v, lw, a, b, g, bonus, bdm, gnw.reshape(1, -1), gnb.reshape(1, -1))


def _out_proj_kernel(yc_ref, yr_ref, wo_ref, x_ref, gt_ref, g_ref, sh_ref, sc_ref, x1_o, u2_o):
    mix = _dot(yc_ref[0], wo_ref[0:CONV_CH, :]) + _dot(yr_ref[0], wo_ref[CONV_CH:, :])
    x1 = x_ref[0] + gt_ref[0] * mix
    x1_o[0] = x1
    u2 = _rms(x1, g_ref[...]) * (1.0 + sc_ref[0]) + sh_ref[0]
    for c in range(CHUNKS):
        u2_o[0, :, c, :] = u2[:, c * LANES:(c + 1) * LANES]


def _out_proj(yc, yr, wo_bf16, x, gt, g, sh, sc, tm=256):
    bsz, s, _ = x.shape
    vec = pl.BlockSpec((1, 1, D_MODEL), lambda b, i: (b, 0, 0))
    tile = lambda w: pl.BlockSpec((1, tm, w), lambda b, i: (b, i, 0))
    return pl.pallas_call(
        _out_proj_kernel,
        grid=(bsz, s // tm),
        in_specs=[tile(CONV_CH), tile(RWKV_CH), pl.BlockSpec((D_MODEL, D_MODEL), lambda b, i: (0, 0)),
                  tile(D_MODEL), vec, pl.BlockSpec((1, D_MODEL), lambda b, i: (0, 0)), vec, vec],
        out_specs=[tile(D_MODEL), pl.BlockSpec((1, tm, CHUNKS, LANES), lambda b, i: (b, i, 0, 0))],
        out_shape=[jax.ShapeDtypeStruct((bsz, s, D_MODEL), F32),
                   jax.ShapeDtypeStruct((bsz, s, CHUNKS, LANES), F32)],
        compiler_params=_cparams(("parallel", "parallel")),
        name="out_proj",
    )(yc, yr, wo_bf16, x, gt, g.reshape(1, D_MODEL), sh, sc)


HEADS_STEP = 4


def _topk_rows(ss, k):
    n, t = ss[0].shape
    rids = [lax.broadcasted_iota(I32, (8, t), 0) + r0 for r0 in range(0, n, 8)]
    ss = [[s[r0:r0 + 8] for r0 in range(0, n, 8)] for s in ss]
    vals, ids = [[] for _ in ss], [[] for _ in ss]
    for _ in range(k):
        tops = [_max_with_tag(s, [rids]) for s in ss]
        ss = [[jnp.where(r == j, -jnp.inf, c) for c, r in zip(s, rids)] for s, (_, (j,)) in zip(ss, tops)]
        for v, i, (m, (j,)) in zip(vals, ids, tops):
            v.append(m)
            i.append(j)
    return [(jnp.concatenate(v, axis=0), jnp.concatenate(i, axis=0)) for v, i in zip(vals, ids)]


def _max_with_tag(chunks, tags):
    vals, tags = list(chunks), [list(tg) for tg in tags]
    while len(vals) > 1:
        nv, nt = [], [[] for _ in tags]
        for a in range(0, len(vals) - 1, 2):
            first = vals[a] >= vals[a + 1]
            nv.append(jnp.maximum(vals[a], vals[a + 1]))
            for dst, tg in zip(nt, tags):
                dst.append(jnp.where(first, tg[a], tg[a + 1]))
        if len(vals) % 2:
            nv.append(vals[-1])
            for dst, tg in zip(nt, tags):
                dst.append(tg[-1])
        vals, tags = nv, nt
    v8 = vals[0]
    m = jnp.max(v8, axis=0, keepdims=True)
    big = jnp.iinfo(jnp.int32).max
    key = jnp.min(jnp.where(v8 == m, tags[0][0], big), axis=0, keepdims=True)
    out = [key]
    for tg in tags[1:]:
        out.append(jnp.sum(jnp.where(tags[0][0] == key, tg[0], 0), axis=0, keepdims=True))
    return m, out


def _route_kernel(u_ref, wq_ref, keys_ref, idx_o, gate_o, q_ref, idx_s, gate_s):
    tm = u_ref.shape[0]
    u = jnp.concatenate([u_ref[:, c, :] for c in range(CHUNKS)], axis=1)
    q = _dot(u.astype(BF16), wq_ref[...])
    for j in range(2 * PEER_HEADS):
        q_ref[j] = q[:, j * LANES:(j + 1) * LANES]
    K = PEER_TOPK
    tt = LANES
    tiles = range(tm // tt)
    row8 = lax.broadcasted_iota(I32, (8, tt), 0)

    def heads(i, carry):
        probs = [(i * HEADS_STEP + dh, lt) for dh in range(HEADS_STEP) for lt in tiles]
        scores = [_mm(keys_ref[2 * h + p], q_ref[2 * h + p, lt * tt:(lt + 1) * tt, :], 3, _dot_nt)
                  for h, lt in probs for p in range(2)]
        tops = _topk_rows(scores, K)
        ss, flats, eids = [], [], []
        for n_ in range(len(probs)):
            (av, ai), (bv, bi) = tops[2 * n_], tops[2 * n_ + 1]
            cs = [av[0:1] + bv[0:8], av[0:1] + bv[8:16]]
            cf = [row8, row8 + 8]
            ce = [ai[0:1] * PEER_NKEYS + bi[0:8], ai[0:1] * PEER_NKEYS + bi[8:16]]
            for x in range(1, 8):
                cs.append(jnp.where(row8 < K // (x + 1), av[x:x + 1] + bv[0:8], -jnp.inf))
                cf.append(x * K + row8)
                ce.append(ai[x:x + 1] * PEER_NKEYS + bi[0:8])
            cs.append(av[8:16] + bv[0:1])
            cf.append((row8 + 8) * K)
            ce.append(ai[8:16] * PEER_NKEYS + bi[0:1])
            ss.append(cs)
            flats.append(cf)
            eids.append(ce)
        best, experts = [[] for _ in probs], [[] for _ in probs]
        for _ in range(K):
            tops2 = [_max_with_tag(s, [fl, ei]) for s, fl, ei in zip(ss, flats, eids)]
            for n_, (m, (f, e)) in enumerate(tops2):
                best[n_].append(m)
                experts[n_].append(e)
            ss = [[jnp.where(fc == f, -jnp.inf, c) for c, fc in zip(s, fl)]
                  for s, fl, (_, (f, _e)) in zip(ss, flats, tops2)]
        for n_, (h, lt) in enumerate(probs):
            b = jnp.concatenate(best[n_], axis=0)
            e = jnp.exp(b - b[0:1])
            gate_s[lt, h] = e / jnp.sum(e, axis=0, keepdims=True)
            idx_s[lt, h] = jnp.concatenate(experts[n_], axis=0) * WORD_ROWS
        return carry

    lax.fori_loop(0, PEER_HEADS // HEADS_STEP, heads, 0)
    for lt in tiles:
        idx_o[lt * tt:(lt + 1) * tt, :] = jnp.transpose(idx_s[lt].reshape(NPAIR, tt))
        gate_o[lt * tt:(lt + 1) * tt, :] = jnp.transpose(gate_s[lt].reshape(NPAIR, tt))


def _route(u2, wq_bf16, keys, tm=256):
    n = u2.shape[0]
    oblk = pl.BlockSpec((tm, NPAIR), lambda i: (i, 0))
    return pl.pallas_call(
        _route_kernel,
        grid=(n // tm,),
        in_specs=[pl.BlockSpec((tm, CHUNKS, LANES), lambda i: (i, 0, 0)),
                  pl.BlockSpec((D_MODEL, PEER_HEADS * PEER_DQ), lambda i: (0, 0)),
                  pl.BlockSpec((2 * PEER_HEADS, PEER_NKEYS, PEER_DQ // 2), lambda i: (0, 0, 0))],
        out_specs=[oblk, oblk],
        out_shape=[jax.ShapeDtypeStruct((n, NPAIR), I32), jax.ShapeDtypeStruct((n, NPAIR), F32)],
        scratch_shapes=[pltpu.VMEM((2 * PEER_HEADS, tm, LANES), F32),
                        pltpu.VMEM((tm // LANES, PEER_HEADS, PEER_TOPK, LANES), I32),
                        pltpu.VMEM((tm // LANES, PEER_HEADS, PEER_TOPK, LANES), F32)],
        compiler_params=_cparams(("parallel",)),
        name="route",
    )(u2, wq_bf16, keys)


TOK_UNROLL = 16


def _pack_kernel(t_ref, o_ref):
    te = t_ref.shape[0]
    bits = lambda v: lax.bitcast_convert_type(v.astype(BF16).astype(F32), I32)
    for s in range(WORD_ROWS):
        lo = bits(t_ref[:, (2 * s) * LANES:(2 * s + 1) * LANES])
        hi = bits(t_ref[:, (2 * s + 1) * LANES:(2 * s + 2) * LANES])
        o_ref[pl.ds(s, te, stride=WORD_ROWS), :] = (hi & -65536) | lax.shift_right_logical(lo, 16)


def _pack_table(t, te=512):
    e = t.shape[0]
    return pl.pallas_call(
        _pack_kernel,
        grid=(e // te,),
        in_specs=[pl.BlockSpec((te, D_MODEL), lambda i: (i, 0))],
        out_specs=pl.BlockSpec((te * WORD_ROWS, LANES), lambda i: (i, 0)),
        out_shape=jax.ShapeDtypeStruct((e * WORD_ROWS, LANES), I32),
        compiler_params=_cparams(("parallel",)),
        name="pack_table",
    )(t)


def _gather_rows(idx_ref, t0, tab_ref, g_ref):
    for u in range(TOK_UNROLL):
        idx_row = idx_ref.at[t0 + u]
        for kk in range(NPAIR):
            row = pl.multiple_of(idx_row[kk], WORD_ROWS)
            g_ref[u, kk * WORD_ROWS:(kk + 1) * WORD_ROWS, :] = tab_ref[pl.ds(row, WORD_ROWS), :]


def _peer_u_kernel(idx_ref, x_ref, gate_ref, keep_ref, sel_ref, tab_ref, w_o, g_ref, p_ref):
    tp = x_ref.shape[0]

    def tokens(i, carry):
        t0 = i * TOK_UNROLL
        _gather_rows(idx_ref, t0, tab_ref, g_ref)
        for u in range(TOK_UNROLL):
            g = pltpu.bitcast(g_ref[u], BF16)
            xc = x_ref[t0 + u]
            xt = jnp.concatenate([xc] * (LANES // CHUNKS), axis=0).astype(BF16)
            r = _dot_nt(g, xt) * keep_ref[...]
            part = jnp.sum(r.reshape(NPAIR // 16, 16 * CHUNKS, LANES), axis=1)
            p_ref[pl.ds(pl.multiple_of((t0 + u) * 8, 8), 8), :] = part
        return carry

    lax.fori_loop(0, tp // TOK_UNROLL, tokens, 0)
    z = _dot(p_ref[...], sel_ref[...], HI)
    grp = lax.broadcasted_iota(I32, z.shape, 0) % 8 == lax.broadcasted_iota(I32, z.shape, 1) // 16
    h = jnp.sum(jnp.where(grp, z, 0.0).reshape(tp, 8, LANES), axis=1)
    w_o[...] = gate_ref[...] * (0.5 * h * (1.0 + lax.erf(h * (2.0 ** -0.5))))


def _peer_u(idx, x3, gate, tab, tp=128):
    n = x3.shape[0]
    rows = jnp.arange(NPAIR * CHUNKS)[:, None]
    lanes = jnp.arange(LANES)[None, :]
    keep = ((lanes % CHUNKS == rows % CHUNKS) & (lanes // CHUNKS == (rows // CHUNKS) % 16)).astype(F32)
    sel = (jnp.arange(LANES)[:, None] // CHUNKS == jnp.arange(LANES)[None, :] % 16).astype(F32)
    return pl.pallas_call(
        _peer_u_kernel,
        grid=(n // tp,),
        in_specs=[pl.BlockSpec((tp, NPAIR), lambda i: (i, 0), memory_space=pltpu.SMEM),
                  pl.BlockSpec((tp, CHUNKS, LANES), lambda i: (i, 0, 0)),
                  pl.BlockSpec((tp, NPAIR), lambda i: (i, 0)),
                  pl.BlockSpec((NPAIR * CHUNKS, LANES), lambda i: (0, 0)),
                  pl.BlockSpec((LANES, LANES), lambda i: (0, 0)),
                  pl.BlockSpec(memory_space=pltpu.VMEM)],
        out_specs=pl.BlockSpec((tp, NPAIR), lambda i: (i, 0)),
        out_shape=jax.ShapeDtypeStruct((n, NPAIR), F32),
        scratch_shapes=[pltpu.VMEM((TOK_UNROLL, NPAIR * WORD_ROWS, LANES), I32),
                        pltpu.VMEM((tp * 8, LANES), F32)],
        compiler_params=_cparams(("arbitrary",)),
        name="peer_u",
    )(idx, x3, gate, keep, sel, tab)


def _peer_v_kernel(idx_ref, w_ref, rep_ref, diag_ref, tab_ref, o_ref, g_ref, wx_ref):
    tp = w_ref.shape[0]
    wx_ref[...] = _dot(w_ref[...], rep_ref[...], HI)

    def tokens(i, carry):
        t0 = i * TOK_UNROLL
        _gather_rows(idx_ref, t0, tab_ref, g_ref)
        for u in range(TOK_UNROLL):
            g = pltpu.bitcast(g_ref[u], BF16)
            wm = (wx_ref[pl.ds(t0 + u, 1), :] * diag_ref[...]).astype(BF16)
            o_ref[t0 + u] = _dot(wm, g)
        return carry

    lax.fori_loop(0, tp // TOK_UNROLL, tokens, 0)


def _peer_v(idx, w, tab, tp=128):
    n = w.shape[0]
    rep = jnp.repeat(jnp.eye(NPAIR, dtype=F32), CHUNKS, axis=1)
    diag = (jnp.arange(CHUNKS)[:, None] == jnp.arange(NPAIR * CHUNKS)[None, :] % CHUNKS).astype(F32)
    return pl.pallas_call(
        _peer_v_kernel,
        grid=(n // tp,),
        in_specs=[pl.BlockSpec((tp, NPAIR), lambda i: (i, 0), memory_space=pltpu.SMEM),
                  pl.BlockSpec((tp, NPAIR), lambda i: (i, 0)),
                  pl.BlockSpec((NPAIR, NPAIR * CHUNKS), lambda i: (0, 0)),
                  pl.BlockSpec((CHUNKS, NPAIR * CHUNKS), lambda i: (0, 0)),
                  pl.BlockSpec(memory_space=pltpu.VMEM)],
        out_specs=pl.BlockSpec((tp, CHUNKS, LANES), lambda i: (i, 0, 0)),
        out_shape=jax.ShapeDtypeStruct((n, CHUNKS, LANES), F32),
        scratch_shapes=[pltpu.VMEM((TOK_UNROLL, NPAIR * WORD_ROWS, LANES), I32),
                        pltpu.VMEM((tp, NPAIR * CHUNKS), F32)],
        compiler_params=_cparams(("arbitrary",)),
        name="peer_v",
    )(idx, w, rep, diag, tab)


def _final_kernel(x1_ref, p_ref, gt_ref, g_ref, o_ref):
    p = jnp.concatenate([p_ref[0, :, c, :] for c in range(CHUNKS)], axis=1)
    o_ref[0] = _rms(x1_ref[0] + gt_ref[0] * p, g_ref[...])


def _final(x1, peer, gt, g, tm=512):
    bsz, s, _ = x1.shape
    tile = pl.BlockSpec((1, tm, D_MODEL), lambda b, i: (b, i, 0))
    return pl.pallas_call(
        _final_kernel,
        grid=(bsz, s // tm),
        in_specs=[tile, pl.BlockSpec((1, tm, CHUNKS, LANES), lambda b, i: (b, i, 0, 0)),
                  pl.BlockSpec((1, 1, D_MODEL), lambda b, i: (b, 0, 0)),
                  pl.BlockSpec((1, D_MODEL), lambda b, i: (0, 0))],
        out_specs=tile,
        out_shape=jax.ShapeDtypeStruct((bsz, s, D_MODEL), F32),
        compiler_params=_cparams(("parallel", "parallel")),
        name="final",
    )(x1, peer, gt, g.reshape(1, D_MODEL))


def _block_diag(width, group, value):
    i = jnp.arange(width) // group
    return jnp.where(i[:, None] == i[None, :], value, 0.0).astype(F32)


def _layer(x, mod, final_g, norm_mix_g, w_in, conv_dw_w, conv_dw_b, conv_ln_w, conv_ln_b, rwkv_mu, rwkv_w0, rwkv_w2,
           rwkv_a0, rwkv_a2, rwkv_g2, rwkv_k_k, rwkv_k_a, rwkv_r_k, rwkv_gn_w, rwkv_gn_b, w_out, norm_ffn_g,
           peer_w_q, peer_sub_keys, peer_u, peer_v):
    bsz, s, _ = x.shape
    sh_mix, sc_mix, gt_mix, sh_ffn, sc_ffn, gt_ffn = (
        mod[:, i * D_MODEL:(i + 1) * D_MODEL].reshape(bsz, 1, D_MODEL) for i in range(6))

    yglu, prw = _in_proj(x, sh_mix, sc_mix, norm_mix_g, w_in.astype(BF16))
    y_conv = _conv(yglu, conv_dw_w, conv_dw_b, conv_ln_w, conv_ln_b)

    zpad = jnp.zeros((LORA_W, RWKV_CH), F32)
    r, k, v, lw, a, b, g, bonus = _rwkv_pre(
        prw, rwkv_mu, rwkv_w0, jnp.concatenate([rwkv_w2, zpad], axis=0), rwkv_a0,
        jnp.concatenate([zpad, rwkv_a2], axis=0), rwkv_g2, rwkv_k_k, rwkv_k_a, rwkv_r_k.reshape(-1),
        _block_diag(LANES, HEAD, 1.0).astype(BF16))
    y_rwkv = _rwkv(r, k, v, lw, a, b, g, bonus, _block_diag(LANES, HEAD, 1.0 / HEAD).astype(BF16), rwkv_gn_w,
                   rwkv_gn_b)

    x1, u2 = _out_proj(y_conv, y_rwkv, w_out.astype(BF16), x, gt_mix, norm_ffn_g, sh_ffn, sc_ffn)

    n = bsz * s
    u3 = u2.reshape(n, CHUNKS, LANES)
    keys = peer_sub_keys.reshape(2 * PEER_HEADS, PEER_NKEYS, PEER_DQ // 2)
    idx, gate = _route(u3, peer_w_q.astype(BF16), keys)
    wts = _peer_u(idx, u3, gate, _pack_table(peer_u))
    peer = _peer_v(idx, wts, _pack_table(peer_v))
    return _final(x1, peer.reshape(bsz, s, CHUNKS, LANES), gt_ffn, final_g)


def kernel(x, c, ada_w, ada_b, norm_mix_g, w_in, conv_dw_w, conv_dw_b, conv_ln_w, conv_ln_b, rwkv_mu, rwkv_w0,
           rwkv_w2, rwkv_a0, rwkv_a2, rwkv_g2, rwkv_k_k, rwkv_k_a, rwkv_r_k, rwkv_gn_w, rwkv_gn_b, w_out,
           norm_ffn_g, peer_w_q, peer_sub_keys, peer_u, peer_v, final_g):
    depth = ada_w.shape[0]
    assert depth == 1, "one layer: the final norm is fused into the last layer's residual"
    mod = _mod(c, ada_w[0], ada_b[0])
    return _layer(x, mod, final_g, norm_mix_g[0], w_in[0], conv_dw_w[0], conv_dw_b[0], conv_ln_w[0],
                        conv_ln_b[0], rwkv_mu[0], rwkv_w0[0], rwkv_w2[0], rwkv_a0[0], rwkv_a2[0], rwkv_g2[0],
                        rwkv_k_k[0], rwkv_k_a[0], rwkv_r_k[0], rwkv_gn_w[0], rwkv_gn_b[0], w_out[0],
                        norm_ffn_g[0], peer_w_q[0], peer_sub_keys[0], peer_u[0], peer_v[0])
```

```python
import functools

import jax
import jax.numpy as jnp
from jax import lax
from jax.experimental import pallas as pl
from jax.experimental.pallas import tpu as pltpu

F32 = jnp.float32
BF16 = jnp.bfloat16
I32 = jnp.int32
HI = lax.Precision.HIGHEST

D_MODEL = 1024
CONV_CH = 512
RWKV_CH = 512
HEAD = 64
CONV_WIDTH = 31
LORA_W = 64
LORA_A = 64
LORA_G = 128
RWKV_PROJ = 3 * RWKV_CH + LORA_W + LORA_A + LORA_G
IN_PROJ = 2 * CONV_CH + RWKV_PROJ
PEER_HEADS = 8
PEER_NKEYS = 128
PEER_EXPERTS = PEER_NKEYS * PEER_NKEYS
PEER_DQ = 256
PEER_TOPK = 16
NPAIR = PEER_HEADS * PEER_TOPK
RMS_EPS = 1e-6
LN_EPS = 1e-5
GN_EPS = 64e-5

LANES = 128
CHUNKS = D_MODEL // LANES
WORD_ROWS = CHUNKS // 2
CHUNK_T = 64
VMEM_LIMIT = 56 * 1024 * 1024


def _cparams(sem, vmem=None):
    return pltpu.CompilerParams(dimension_semantics=sem, vmem_limit_bytes=vmem or VMEM_LIMIT)


def _dot(a, b, precision=None):
    return jnp.dot(a, b, precision=precision, preferred_element_type=F32)


def _dot_nt(a, b, precision=None):
    return lax.dot_general(a, b, (((1,), (1,)), ((), ())), precision=precision, preferred_element_type=F32)


def _dot_tn(a, b, precision=None):
    return lax.dot_general(a, b, (((0,), (0,)), ((), ())), precision=precision, preferred_element_type=F32)


def _split(a):
    hi = a.astype(BF16)
    return hi, (a - hi.astype(F32)).astype(BF16)


def _mm(a, b, passes, dot=_dot):
    if passes == 6:
        return dot(a, b, HI)
    if passes == 1:
        return dot(a.astype(BF16), b.astype(BF16))
    ka = 0 if dot is _dot_tn else 1
    kb = 1 if dot is _dot_nt else 0
    ah, al = _split(a)
    bh, bl = _split(b)
    return dot(jnp.concatenate([ah, ah, al], axis=ka), jnp.concatenate([bh, bl, bh], axis=kb))


def _mm_bf16_rhs(a, b_bf16, terms=2):
    parts = []
    for _ in range(terms):
        p = a.astype(BF16)
        parts.append(p)
        a = a - p.astype(F32)
    return _dot(jnp.concatenate(parts, axis=1), jnp.concatenate([b_bf16] * terms, axis=0))


def _head_sums(x, bd_bf16):
    return jnp.concatenate([_mm_bf16_rhs(x[:, i * LANES:(i + 1) * LANES], bd_bf16, 3)
                            for i in range(x.shape[1] // LANES)], axis=1)


def _rms(x, g):
    return x * lax.rsqrt(jnp.mean(x * x, axis=-1, keepdims=True) + RMS_EPS) * g


def _mod_kernel(c_ref, w_ref, b_ref, o_ref):
    c = c_ref[...]
    o_ref[...] = _dot(c * jax.nn.sigmoid(c), w_ref[...], HI) + b_ref[...]


def _mod(c, w, b):
    bsz = c.shape[0]
    n = w.shape[1]
    tn = 1024
    return pl.pallas_call(
        _mod_kernel,
        grid=(n // tn,),
        in_specs=[pl.BlockSpec((bsz, D_MODEL), lambda j: (0, 0)),
                  pl.BlockSpec((D_MODEL, tn), lambda j: (0, j)),
                  pl.BlockSpec((1, tn), lambda j: (0, j))],
        out_specs=pl.BlockSpec((bsz, tn), lambda j: (0, j)),
        out_shape=jax.ShapeDtypeStruct((bsz, n), F32),
        compiler_params=_cparams(("parallel",)),
    )(c, w, b.reshape(1, n))


def _in_proj_kernel(x_ref, sh_ref, sc_ref, g_ref, w_ref, yglu_ref, prw_ref):
    u = _rms(x_ref[0], g_ref[...]) * (1.0 + sc_ref[0]) + sh_ref[0]
    p = _dot(u.astype(BF16), w_ref[...])
    yglu_ref[0] = p[:, :CONV_CH] * jax.nn.sigmoid(p[:, CONV_CH:2 * CONV_CH])
    prw_ref[0] = p[:, 2 * CONV_CH:]


def _in_proj(x, sh, sc, g, w_bf16, tm=256):
    bsz, s, _ = x.shape
    vec = pl.BlockSpec((1, 1, D_MODEL), lambda b, i: (b, 0, 0))
    return pl.pallas_call(
        _in_proj_kernel,
        grid=(bsz, s // tm),
        in_specs=[pl.BlockSpec((1, tm, D_MODEL), lambda b, i: (b, i, 0)), vec, vec,
                  pl.BlockSpec((1, D_MODEL), lambda b, i: (0, 0)),
                  pl.BlockSpec((D_MODEL, IN_PROJ), lambda b, i: (0, 0))],
        out_specs=[pl.BlockSpec((1, tm, CONV_CH), lambda b, i: (b, i, 0)),
                   pl.BlockSpec((1, tm, RWKV_PROJ), lambda b, i: (b, i, 0))],
        out_shape=[jax.ShapeDtypeStruct((bsz, s, CONV_CH), F32),
                   jax.ShapeDtypeStruct((bsz, s, RWKV_PROJ), F32)],
        compiler_params=_cparams(("parallel", "parallel")),
    )(x, sh, sc, g.reshape(1, D_MODEL), w_bf16)


CONV_HALO = 32
CONV_ROWS = 64


def _conv_kernel(cur_ref, prev_ref, w_ref, b_ref, lnw_ref, lnb_ref, o_ref, pad_ref):
    tc = cur_ref.shape[1]
    pad_ref[0, 0:CONV_HALO, :] = jnp.where(pl.program_id(1) > 0, prev_ref[0], 0.0)
    pad_ref[0, CONV_HALO:CONV_HALO + tc, :] = cur_ref[0]
    span = CONV_HALO + tc - 8
    for r in range(1, 8):
        pad_ref[r, 0:span, :] = pad_ref[0, r:r + span, :]
    off = CONV_HALO - (CONV_WIDTH - 1)
    for r0 in range(0, tc, CONV_ROWS):
        acc = jnp.zeros((CONV_ROWS, CONV_CH), F32)
        for j in range(CONV_WIDTH):
            q, r = divmod(off + j, 8)
            acc = acc + w_ref[j:j + 1, :] * pad_ref[r, r0 + 8 * q:r0 + 8 * q + CONV_ROWS, :]
        y = acc + b_ref[...]
        mu = jnp.mean(y, axis=-1, keepdims=True)
        yc = y - mu
        var = jnp.mean(yc * yc, axis=-1, keepdims=True)
        yn = yc * lax.rsqrt(var + LN_EPS) * lnw_ref[...] + lnb_ref[...]
        o_ref[0, r0:r0 + CONV_ROWS, :] = (yn * jax.nn.sigmoid(yn)).astype(o_ref.dtype)


def _conv(yglu, w, b, lnw, lnb, tc=256):
    bsz, s, _ = yglu.shape
    hb = tc // CONV_HALO
    row = lambda a: a.reshape(1, CONV_CH)
    const = lambda shp: pl.BlockSpec(shp, lambda bb, i: (0, 0))
    return pl.pallas_call(
        _conv_kernel,
        grid=(bsz, s // tc),
        in_specs=[pl.BlockSpec((1, tc, CONV_CH), lambda bb, i: (bb, i, 0)),
                  pl.BlockSpec((1, CONV_HALO, CONV_CH), lambda bb, i: (bb, jnp.maximum(i * hb - 1, 0), 0)),
                  const((CONV_WIDTH, CONV_CH)), const((1, CONV_CH)), const((1, CONV_CH)), const((1, CONV_CH))],
        out_specs=pl.BlockSpec((1, tc, CONV_CH), lambda bb, i: (bb, i, 0)),
        out_shape=jax.ShapeDtypeStruct((bsz, s, CONV_CH), BF16),
        scratch_shapes=[pltpu.VMEM((8, CONV_HALO + tc, CONV_CH), F32)],
        compiler_params=_cparams(("parallel", "parallel")),
    )(yglu, yglu, w, row(b), row(lnw), row(lnb))


def _softplus(z):
    return jnp.maximum(z, 0.0) + jnp.log1p(jnp.exp(-jnp.abs(z)))


def _rwkv_features(cur, prow, mu, w0, w2p, a0, a2p, g2, k_k, k_a, r_k, bd):
    rows = lax.broadcasted_iota(I32, cur.shape, 0)
    prev = jnp.where(rows == 0, prow, pltpu.roll(cur, 1, axis=0))
    xs = cur + mu * (prev - cur)
    r = xs[:, 0:RWKV_CH]
    k = xs[:, RWKV_CH:2 * RWKV_CH]
    v = xs[:, 2 * RWKV_CH:3 * RWKV_CH]
    wa = xs[:, 3 * RWKV_CH:3 * RWKV_CH + LORA_W + LORA_A]
    gl = xs[:, 3 * RWKV_CH + LORA_W + LORA_A:]
    w = -_softplus(-(w0 + _mm(jnp.tanh(wa), w2p, 3))) - 0.5
    a = jax.nn.sigmoid(a0 + _mm(wa, a2p, 3))
    g = _mm(jax.nn.sigmoid(gl), g2, 3)
    kk = k * k_k
    kkn = kk / jnp.maximum(jnp.sqrt(_head_sums(kk * kk, bd)), 1e-12)
    k2 = k * (1.0 + (a - 1.0) * k_a)
    bonus = _head_sums(r * k2 * r_k, bd) * v
    return r, k2, v, -jnp.exp(w), -kkn, kkn * a, g, bonus


RWKV_PASSES = {"gram": 3, "lakv": 3, "solve": 3, "out": 3, "state": 3}
SOLVE_BLK = 16


def _expand(x, lane_lo):
    return jnp.concatenate([jnp.where(lane_lo, x, 0.0), jnp.where(lane_lo, 0.0, x)], axis=0)


def _rwkv_kernel(p_ref, mu_ref, w0_ref, w2_ref, a0_ref, a2_ref, g2_ref, kk_ref, ka_ref, rk_ref, bd_ref,
                 bdm_ref, gnw_ref, gnb_ref, y_ref, s_ref, last_ref):
    C = CHUNK_T
    H2 = 2 * C

    @pl.when(pl.program_id(1) == 0)
    def _():
        s_ref[...] = jnp.zeros_like(s_ref)
        last_ref[...] = jnp.zeros_like(last_ref)

    cur = p_ref[0]
    r, k, vv, lw, a, b, gate, bonus = _rwkv_features(
        cur, last_ref[...], mu_ref[...], w0_ref[...], w2_ref[...], a0_ref[...], a2_ref[...], g2_ref[...],
        kk_ref[...], ka_ref[...], rk_ref[...], bd_ref[...])
    last_ref[...] = cur[C - 1:C, :]

    tri = (lax.broadcasted_iota(I32, (C, C), 0) >= lax.broadcasted_iota(I32, (C, C), 1)).astype(F32)
    cum = _dot(tri, lw, HI)
    tot = cum[C - 1:C, :]
    e_pos = jnp.exp(cum)
    e_neg = jnp.exp(-cum)
    e_rem = jnp.exp(tot - cum)
    rt = r * e_pos
    at = a * jnp.exp(cum - lw)
    kt = k * e_neg
    bt = b * e_neg
    kp = k * e_rem
    bp = b * e_rem
    pc = jnp.exp(tot)

    lane_lo = lax.broadcasted_iota(I32, (C, LANES), 1) < HEAD
    tt = lax.broadcasted_iota(I32, (H2, H2), 0) % C
    ss = lax.broadcasted_iota(I32, (H2, H2), 1) % C
    strict = tt > ss
    incl = tt >= ss
    near = tt // SOLVE_BLK == ss // SOLVE_BLK
    eye = lax.broadcasted_iota(I32, (LANES, LANES), 0) == lax.broadcasted_iota(I32, (LANES, LANES), 1)

    P = RWKV_PASSES
    pairs = range(RWKV_CH // LANES)
    each = lambda fn, *lists: [fn(*args) for args in zip(*lists)]
    sls = [slice(hp * LANES, (hp + 1) * LANES) for hp in pairs]
    ax, rx, bx, kx, vx, bpx, kpx = ([_expand(t[:, sl], lane_lo) for sl in sls] for t in (at, rt, bt, kt, vv, bp, kp))
    gram = each(lambda a_, r_, b_, k_: _mm(jnp.concatenate([a_, r_], axis=0), jnp.concatenate([b_, k_], axis=0),
                                           P["gram"], _dot_nt), ax, rx, bx, kx)
    l_ab = [jnp.where(strict, g_[:H2, :H2], 0.0) for g_ in gram]
    l_ak = [jnp.where(strict, g_[:H2, H2:], 0.0) for g_ in gram]
    m_r = [jnp.concatenate([jnp.where(incl, g_[H2:, :H2], 0.0), jnp.where(incl, g_[H2:, H2:], 0.0)], axis=1)
           for g_ in gram]
    dg = [jnp.where(near, l_, 0.0) for l_ in l_ab]
    lakv = each(lambda l_, v_: _mm(l_, v_, P["lakv"]), l_ak, vx)
    xf = each(lambda a_, lv_, l_, d_: jnp.concatenate([a_, lv_, l_ - d_], axis=1), ax, lakv, l_ab, dg)
    n_sq = SOLVE_BLK.bit_length() - 1
    for it in range(n_sq):
        xf = each(lambda x_, d_: x_ + _mm(d_, x_, P["solve"]), xf, dg)
        if it + 1 < n_sq:
            dg = [_mm(d_, d_, P["solve"]) for d_ in dg]
    x = [x_[:, :2 * LANES] for x_ in xf]
    f = [x_[:, 2 * LANES:] for x_ in xf]
    n_sq = (C // SOLVE_BLK).bit_length() - 1
    for it in range(n_sq):
        x = each(lambda x_, f_: x_ + _mm(f_, x_, P["solve"]), x, f)
        if it + 1 < n_sq:
            f = [_mm(f_, f_, P["solve"]) for f_ in f]
    zero = jnp.zeros((H2, LANES), F32)
    z = each(lambda x_, v_: jnp.concatenate([x_, jnp.concatenate([zero, v_], axis=1)], axis=0), x, vx)
    w1 = each(lambda m_, z_: _mm(m_, z_, P["out"]), m_r, z)
    w2 = each(lambda b_, k_, z_: _mm(jnp.concatenate([b_, k_], axis=0), z_, P["out"], _dot_tn), bpx, kpx, z)
    ys = []
    for hp in pairs:
        ra = rx[hp] + w1[hp][:, :LANES]
        ra = ra[:C] + ra[C:]
        y0 = w1[hp][:C, LANES:] + w1[hp][C:, LANES:]
        mt = w2[hp][:, :LANES] + jnp.where(eye, pc[:, sls[hp]], 0.0)
        s0 = s_ref[hp]
        ys.append(_mm(ra, s0, P["state"]) + y0)
        s_ref[hp] = _mm(mt, s0, P["state"]) + w2[hp][:, LANES:]
    yc = [y_ - _mm_bf16_rhs(y_, bdm_ref[...]) for y_ in ys]
    ys = [c_ * lax.rsqrt(_mm_bf16_rhs(c_ * c_, bdm_ref[...]) + GN_EPS) for c_ in yc]

    yn = jnp.concatenate(ys, axis=1) * gnw_ref[...] + gnb_ref[...]
    y_ref[0] = ((yn + bonus) * gate).astype(y_ref.dtype)


def _rwkv(prw, mu, w0, w2p, a0, a2p, g2, k_k, k_a, r_k, gnw, gnb):
    bsz, s, _ = prw.shape
    row = lambda t: t.reshape(1, -1)
    const = lambda shp: pl.BlockSpec(shp, lambda bb, c: (0, 0))
    vec = const((1, RWKV_CH))
    lora = const((LORA_W + LORA_A, RWKV_CH))
    return pl.pallas_call(
        _rwkv_kernel,
        grid=(bsz, s // CHUNK_T),
        in_specs=[pl.BlockSpec((1, CHUNK_T, RWKV_PROJ), lambda bb, c: (bb, c, 0)), const((1, RWKV_PROJ)),
                  vec, lora, vec, lora, const((LORA_G, RWKV_CH)), vec, vec, vec,
                  const((LANES, LANES)), const((LANES, LANES)), vec, vec],
        out_specs=pl.BlockSpec((1, CHUNK_T, RWKV_CH), lambda bb, c: (bb, c, 0)),
        out_shape=jax.ShapeDtypeStruct((bsz, s, RWKV_CH), BF16),
        scratch_shapes=[pltpu.VMEM((RWKV_CH // LANES, LANES, LANES), F32), pltpu.VMEM((1, RWKV_PROJ), F32)],
        compiler_params=_cparams(("parallel", "arbitrary")),
        name="rwkv",
    )(prw, row(mu), row(w0), w2p, row(a0), a2p, g2, row(k_k), row(k_a), row(r_k),
      _block_diag(LANES, HEAD, 1.0).astype(BF16), _block_diag(LANES, HEAD, 1.0 / HEAD).astype(BF16),
      row(gnw), row(gnb))


def _out_proj_kernel(yc_ref, yr_ref, wo_ref, x_ref, gt_ref, g_ref, sh_ref, sc_ref, x1_o, u2_o):
    mix = _dot(yc_ref[0], wo_ref[0:CONV_CH, :]) + _dot(yr_ref[0], wo_ref[CONV_CH:, :])
    x1 = x_ref[0] + gt_ref[0] * mix
    x1_o[0] = x1
    u2 = _rms(x1, g_ref[...]) * (1.0 + sc_ref[0]) + sh_ref[0]
    for c in range(CHUNKS):
        u2_o[0, :, c, :] = u2[:, c * LANES:(c + 1) * LANES]


def _out_proj(yc, yr, wo_bf16, x, gt, g, sh, sc, tm=256):
    bsz, s, _ = x.shape
    vec = pl.BlockSpec((1, 1, D_MODEL), lambda b, i: (b, 0, 0))
    tile = lambda w: pl.BlockSpec((1, tm, w), lambda b, i: (b, i, 0))
    return pl.pallas_call(
        _out_proj_kernel,
        grid=(bsz, s // tm),
        in_specs=[tile(CONV_CH), tile(RWKV_CH), pl.BlockSpec((D_MODEL, D_MODEL), lambda b, i: (0, 0)),
                  tile(D_MODEL), vec, pl.BlockSpec((1, D_MODEL), lambda b, i: (0, 0)), vec, vec],
        out_specs=[tile(D_MODEL), pl.BlockSpec((1, tm, CHUNKS, LANES), lambda b, i: (b, i, 0, 0))],
        out_shape=[jax.ShapeDtypeStruct((bsz, s, D_MODEL), F32),
                   jax.ShapeDtypeStruct((bsz, s, CHUNKS, LANES), F32)],
        compiler_params=_cparams(("parallel", "parallel")),
        name="out_proj",
    )(yc, yr, wo_bf16, x, gt, g.reshape(1, D_MODEL), sh, sc)


HEADS_STEP = 4


def _topk_rows(ss, k):
    n, t = ss[0].shape
    rids = [lax.broadcasted_iota(I32, (8, t), 0) + r0 for r0 in range(0, n, 8)]
    ss = [[s[r0:r0 + 8] for r0 in range(0, n, 8)] for s in ss]
    vals, ids = [[] for _ in ss], [[] for _ in ss]
    for _ in range(k):
        tops = [_max_with_tag(s, [rids]) for s in ss]
        ss = [[jnp.where(r == j, -jnp.inf, c) for c, r in zip(s, rids)] for s, (_, (j,)) in zip(ss, tops)]
        for v, i, (m, (j,)) in zip(vals, ids, tops):
            v.append(m)
            i.append(j)
    return [(jnp.concatenate(v, axis=0), jnp.concatenate(i, axis=0)) for v, i in zip(vals, ids)]


def _max_with_tag(chunks, tags):
    vals, tags = list(chunks), [list(tg) for tg in tags]
    while len(vals) > 1:
        nv, nt = [], [[] for _ in tags]
        for a in range(0, len(vals) - 1, 2):
            first = vals[a] >= vals[a + 1]
            nv.append(jnp.maximum(vals[a], vals[a + 1]))
            for dst, tg in zip(nt, tags):
                dst.append(jnp.where(first, tg[a], tg[a + 1]))
        if len(vals) % 2:
            nv.append(vals[-1])
            for dst, tg in zip(nt, tags):
                dst.append(tg[-1])
        vals, tags = nv, nt
    v8 = vals[0]
    m = jnp.max(v8, axis=0, keepdims=True)
    big = jnp.iinfo(jnp.int32).max
    key = jnp.min(jnp.where(v8 == m, tags[0][0], big), axis=0, keepdims=True)
    out = [key]
    for tg in tags[1:]:
        out.append(jnp.sum(jnp.where(tags[0][0] == key, tg[0], 0), axis=0, keepdims=True))
    return m, out


def _route_kernel(u_ref, wq_ref, keys_ref, idx_o, gate_o, q_ref, idx_s, gate_s):
    tm = u_ref.shape[0]
    u = jnp.concatenate([u_ref[:, c, :] for c in range(CHUNKS)], axis=1)
    q = _dot(u.astype(BF16), wq_ref[...])
    for j in range(2 * PEER_HEADS):
        q_ref[j] = q[:, j * LANES:(j + 1) * LANES]
    K = PEER_TOPK
    tt = LANES
    tiles = range(tm // tt)
    row8 = lax.broadcasted_iota(I32, (8, tt), 0)

    def heads(i, carry):
        probs = [(i * HEADS_STEP + dh, lt) for dh in range(HEADS_STEP) for lt in tiles]
        scores = [_mm(keys_ref[2 * h + p], q_ref[2 * h + p, lt * tt:(lt + 1) * tt, :], 3, _dot_nt)
                  for h, lt in probs for p in range(2)]
        tops = _topk_rows(scores, K)
        ss, flats, eids = [], [], []
        for n_ in range(len(probs)):
            (av, ai), (bv, bi) = tops[2 * n_], tops[2 * n_ + 1]
            cs = [av[0:1] + bv[0:8], av[0:1] + bv[8:16]]
            cf = [row8, row8 + 8]
            ce = [ai[0:1] * PEER_NKEYS + bi[0:8], ai[0:1] * PEER_NKEYS + bi[8:16]]
            for x in range(1, 8):
                cs.append(jnp.where(row8 < K // (x + 1), av[x:x + 1] + bv[0:8], -jnp.inf))
                cf.append(x * K + row8)
                ce.append(ai[x:x + 1] * PEER_NKEYS + bi[0:8])
            cs.append(av[8:16] + bv[0:1])
            cf.append((row8 + 8) * K)
            ce.append(ai[8:16] * PEER_NKEYS + bi[0:1])
            ss.append(cs)
            flats.append(cf)
            eids.append(ce)
        best, experts = [[] for _ in probs], [[] for _ in probs]
        for _ in range(K):
            tops2 = [_max_with_tag(s, [fl, ei]) for s, fl, ei in zip(ss, flats, eids)]
            for n_, (m, (f, e)) in enumerate(tops2):
                best[n_].append(m)
                experts[n_].append(e)
            ss = [[jnp.where(fc == f, -jnp.inf, c) for c, fc in zip(s, fl)]
                  for s, fl, (_, (f, _e)) in zip(ss, flats, tops2)]
        for n_, (h, lt) in enumerate(probs):
            b = jnp.concatenate(best[n_], axis=0)
            e = jnp.exp(b - b[0:1])
            gate_s[lt, h] = e / jnp.sum(e, axis=0, keepdims=True)
            idx_s[lt, h] = jnp.concatenate(experts[n_], axis=0) * WORD_ROWS
        return carry

    lax.fori_loop(0, PEER_HEADS // HEADS_STEP, heads, 0)
    for lt in tiles:
        idx_o[lt * tt:(lt + 1) * tt, :] = jnp.transpose(idx_s[lt].reshape(NPAIR, tt))
        gate_o[lt * tt:(lt + 1) * tt, :] = jnp.transpose(gate_s[lt].reshape(NPAIR, tt))


def _route(u2, wq_bf16, keys, tm=256):
    n = u2.shape[0]
    oblk = pl.BlockSpec((tm, NPAIR), lambda i: (i, 0))
    return pl.pallas_call(
        _route_kernel,
        grid=(n // tm,),
        in_specs=[pl.BlockSpec((tm, CHUNKS, LANES), lambda i: (i, 0, 0)),
                  pl.BlockSpec((D_MODEL, PEER_HEADS * PEER_DQ), lambda i: (0, 0)),
                  pl.BlockSpec((2 * PEER_HEADS, PEER_NKEYS, PEER_DQ // 2), lambda i: (0, 0, 0))],
        out_specs=[oblk, oblk],
        out_shape=[jax.ShapeDtypeStruct((n, NPAIR), I32), jax.ShapeDtypeStruct((n, NPAIR), F32)],
        scratch_shapes=[pltpu.VMEM((2 * PEER_HEADS, tm, LANES), F32),
                        pltpu.VMEM((tm // LANES, PEER_HEADS, PEER_TOPK, LANES), I32),
                        pltpu.VMEM((tm // LANES, PEER_HEADS, PEER_TOPK, LANES), F32)],
        compiler_params=_cparams(("parallel",)),
        name="route",
    )(u2, wq_bf16, keys)


TOK_UNROLL = 16


def _pack_kernel(t_ref, o_ref):
    te = t_ref.shape[0]
    bits = lambda v: lax.bitcast_convert_type(v.astype(BF16).astype(F32), I32)
    for s in range(WORD_ROWS):
        lo = bits(t_ref[:, (2 * s) * LANES:(2 * s + 1) * LANES])
        hi = bits(t_ref[:, (2 * s + 1) * LANES:(2 * s + 2) * LANES])
        o_ref[pl.ds(s, te, stride=WORD_ROWS), :] = (hi & -65536) | lax.shift_right_logical(lo, 16)


def _pack_table(t, te=512):
    e = t.shape[0]
    return pl.pallas_call(
        _pack_kernel,
        grid=(e // te,),
        in_specs=[pl.BlockSpec((te, D_MODEL), lambda i: (i, 0))],
        out_specs=pl.BlockSpec((te * WORD_ROWS, LANES), lambda i: (i, 0)),
        out_shape=jax.ShapeDtypeStruct((e * WORD_ROWS, LANES), I32),
        compiler_params=_cparams(("parallel",)),
        name="pack_table",
    )(t)


def _gather_rows(idx_ref, t0, tab_ref, g_ref):
    for u in range(TOK_UNROLL):
        idx_row = idx_ref.at[t0 + u]
        for kk in range(NPAIR):
            row = pl.multiple_of(idx_row[kk], WORD_ROWS)
            g_ref[u, kk * WORD_ROWS:(kk + 1) * WORD_ROWS, :] = tab_ref[pl.ds(row, WORD_ROWS), :]


def _peer_u_kernel(idx_ref, x_ref, gate_ref, keep_ref, sel_ref, tab_ref, w_o, g_ref, p_ref):
    tp = x_ref.shape[0]

    def tokens(i, carry):
        t0 = pl.multiple_of(i * TOK_UNROLL, TOK_UNROLL)
        _gather_rows(idx_ref, t0, tab_ref, g_ref)
        xs = x_ref[pl.ds(t0, TOK_UNROLL)]
        parts = []
        for u in range(TOK_UNROLL):
            g = pltpu.bitcast(g_ref[u], BF16)
            xc = xs[u]
            xt = jnp.concatenate([xc] * (LANES // CHUNKS), axis=0).astype(BF16)
            r = _dot_nt(g, xt) * keep_ref[...]
            parts.append(jnp.sum(r.reshape(NPAIR // 16, 16 * CHUNKS, LANES), axis=1))
        p_ref[pl.ds(pl.multiple_of(t0 * 8, 8 * TOK_UNROLL), 8 * TOK_UNROLL), :] = jnp.concatenate(parts, axis=0)
        return carry

    lax.fori_loop(0, tp // TOK_UNROLL, tokens, 0)
    z = _mm_bf16_rhs(p_ref[...], sel_ref[...], 3)
    grp = lax.broadcasted_iota(I32, z.shape, 0) % 8 == lax.broadcasted_iota(I32, z.shape, 1) // 16
    h = jnp.sum(jnp.where(grp, z, 0.0).reshape(tp, 8, LANES), axis=1)
    w_o[...] = gate_ref[...] * (0.5 * h * (1.0 + lax.erf(h * (2.0 ** -0.5))))


def _peer_u(idx, x3, gate, tab, tp=128):
    n = x3.shape[0]
    rows = jnp.arange(NPAIR * CHUNKS)[:, None]
    lanes = jnp.arange(LANES)[None, :]
    keep = ((lanes % CHUNKS == rows % CHUNKS) & (lanes // CHUNKS == (rows // CHUNKS) % 16)).astype(F32)
    sel = (jnp.arange(LANES)[:, None] // CHUNKS == jnp.arange(LANES)[None, :] % 16).astype(BF16)
    return pl.pallas_call(
        _peer_u_kernel,
        grid=(n // tp,),
        in_specs=[pl.BlockSpec((tp, NPAIR), lambda i: (i, 0), memory_space=pltpu.SMEM),
                  pl.BlockSpec((tp, CHUNKS, LANES), lambda i: (i, 0, 0)),
                  pl.BlockSpec((tp, NPAIR), lambda i: (i, 0)),
                  pl.BlockSpec((NPAIR * CHUNKS, LANES), lambda i: (0, 0)),
                  pl.BlockSpec((LANES, LANES), lambda i: (0, 0)),
                  pl.BlockSpec(memory_space=pltpu.VMEM)],
        out_specs=pl.BlockSpec((tp, NPAIR), lambda i: (i, 0)),
        out_shape=jax.ShapeDtypeStruct((n, NPAIR), F32),
        scratch_shapes=[pltpu.VMEM((TOK_UNROLL, NPAIR * WORD_ROWS, LANES), I32),
                        pltpu.VMEM((tp * 8, LANES), F32)],
        compiler_params=_cparams(("arbitrary",)),
        name="peer_u",
    )(idx, x3, gate, keep, sel, tab)


def _peer_v_kernel(idx_ref, w_ref, rep_ref, diag_ref, tab_ref, o_ref, g_ref, wx_ref):
    tp = w_ref.shape[0]
    wx_ref[...] = _mm_bf16_rhs(w_ref[...], rep_ref[...], 3)

    def tokens(i, carry):
        t0 = pl.multiple_of(i * TOK_UNROLL, TOK_UNROLL)
        _gather_rows(idx_ref, t0, tab_ref, g_ref)
        wx = wx_ref[pl.ds(t0, TOK_UNROLL), :]
        outs = []
        for u in range(TOK_UNROLL):
            g = pltpu.bitcast(g_ref[u], BF16)
            wm = (wx[u:u + 1, :] * diag_ref[...]).astype(BF16)
            outs.append(_dot(wm, g))
        o_ref[pl.ds(t0, TOK_UNROLL)] = jnp.stack(outs, axis=0)
        return carry

    lax.fori_loop(0, tp // TOK_UNROLL, tokens, 0)


def _peer_v(idx, w, tab, tp=128):
    n = w.shape[0]
    rep = jnp.repeat(jnp.eye(NPAIR, dtype=BF16), CHUNKS, axis=1)
    diag = (jnp.arange(CHUNKS)[:, None] == jnp.arange(NPAIR * CHUNKS)[None, :] % CHUNKS).astype(F32)
    return pl.pallas_call(
        _peer_v_kernel,
        grid=(n // tp,),
        in_specs=[pl.BlockSpec((tp, NPAIR), lambda i: (i, 0), memory_space=pltpu.SMEM),
                  pl.BlockSpec((tp, NPAIR), lambda i: (i, 0)),
                  pl.BlockSpec((NPAIR, NPAIR * CHUNKS), lambda i: (0, 0)),
                  pl.BlockSpec((CHUNKS, NPAIR * CHUNKS), lambda i: (0, 0)),
                  pl.BlockSpec(memory_space=pltpu.VMEM)],
        out_specs=pl.BlockSpec((tp, CHUNKS, LANES), lambda i: (i, 0, 0)),
        out_shape=jax.ShapeDtypeStruct((n, CHUNKS, LANES), F32),
        scratch_shapes=[pltpu.VMEM((TOK_UNROLL, NPAIR * WORD_ROWS, LANES), I32),
                        pltpu.VMEM((tp, NPAIR * CHUNKS), F32)],
        compiler_params=_cparams(("arbitrary",)),
        name="peer_v",
    )(idx, w, rep, diag, tab)


def _final_kernel(x1_ref, p_ref, gt_ref, g_ref, o_ref):
    p = jnp.concatenate([p_ref[0, :, c, :] for c in range(CHUNKS)], axis=1)
    o_ref[0] = _rms(x1_ref[0] + gt_ref[0] * p, g_ref[...])


def _final(x1, peer, gt, g, tm=512):
    bsz, s, _ = x1.shape
    tile = pl.BlockSpec((1, tm, D_MODEL), lambda b, i: (b, i, 0))
    return pl.pallas_call(
        _final_kernel,
        grid=(bsz, s // tm),
        in_specs=[tile, pl.BlockSpec((1, tm, CHUNKS, LANES), lambda b, i: (b, i, 0, 0)),
                  pl.BlockSpec((1, 1, D_MODEL), lambda b, i: (b, 0, 0)),
                  pl.BlockSpec((1, D_MODEL), lambda b, i: (0, 0))],
        out_specs=tile,
        out_shape=jax.ShapeDtypeStruct((bsz, s, D_MODEL), F32),
        compiler_params=_cparams(("parallel", "parallel")),
        name="final",
    )(x1, peer, gt, g.reshape(1, D_MODEL))


def _block_diag(width, group, value):
    i = jnp.arange(width) // group
    return jnp.where(i[:, None] == i[None, :], value, 0.0).astype(F32)


def _layer(x, mod, final_g, norm_mix_g, w_in, conv_dw_w, conv_dw_b, conv_ln_w, conv_ln_b, rwkv_mu, rwkv_w0, rwkv_w2,
           rwkv_a0, rwkv_a2, rwkv_g2, rwkv_k_k, rwkv_k_a, rwkv_r_k, rwkv_gn_w, rwkv_gn_b, w_out, norm_ffn_g,
           peer_w_q, peer_sub_keys, peer_u, peer_v):
    bsz, s, _ = x.shape
    sh_mix, sc_mix, gt_mix, sh_ffn, sc_ffn, gt_ffn = (
        mod[:, i * D_MODEL:(i + 1) * D_MODEL].reshape(bsz, 1, D_MODEL) for i in range(6))

    yglu, prw = _in_proj(x, sh_mix, sc_mix, norm_mix_g, w_in.astype(BF16))
    y_conv = _conv(yglu, conv_dw_w, conv_dw_b, conv_ln_w, conv_ln_b)

    zpad = jnp.zeros((LORA_W, RWKV_CH), F32)
    y_rwkv = _rwkv(prw, rwkv_mu, rwkv_w0, jnp.concatenate([rwkv_w2, zpad], axis=0), rwkv_a0,
                   jnp.concatenate([zpad, rwkv_a2], axis=0), rwkv_g2, rwkv_k_k, rwkv_k_a, rwkv_r_k.reshape(-1),
                   rwkv_gn_w, rwkv_gn_b)

    x1, u2 = _out_proj(y_conv, y_rwkv, w_out.astype(BF16), x, gt_mix, norm_ffn_g, sh_ffn, sc_ffn)

    n = bsz * s
    u3 = u2.reshape(n, CHUNKS, LANES)
    keys = peer_sub_keys.reshape(2 * PEER_HEADS, PEER_NKEYS, PEER_DQ // 2)
    idx, gate = _route(u3, peer_w_q.astype(BF16), keys)
    wts = _peer_u(idx, u3, gate, _pack_table(peer_u))
    peer = _peer_v(idx, wts, _pack_table(peer_v))
    return _final(x1, peer.reshape(bsz, s, CHUNKS, LANES), gt_ffn, final_g)


def kernel(x, c, ada_w, ada_b, norm_mix_g, w_in, conv_dw_w, conv_dw_b, conv_ln_w, conv_ln_b, rwkv_mu, rwkv_w0,
           rwkv_w2, rwkv_a0, rwkv_a2, rwkv_g2, rwkv_k_k, rwkv_k_a, rwkv_r_k, rwkv_gn_w, rwkv_gn_b, w_out,
           norm_ffn_g, peer_w_q, peer_sub_keys, peer_u, peer_v, final_g):
    depth = ada_w.shape[0]
    assert depth == 1, "one layer: the final norm is fused into the last layer's residual"
    mod = _mod(c, ada_w[0], ada_b[0])
    return _layer(x, mod, final_g, norm_mix_g[0], w_in[0], conv_dw_w[0], conv_dw_b[0], conv_ln_w[0],
                        conv_ln_b[0], rwkv_mu[0], rwkv_w0[0], rwkv_w2[0], rwkv_a0[0], rwkv_a2[0], rwkv_g2[0],
                        rwkv_k_k[0], rwkv_k_a[0], rwkv_r_k[0], rwkv_gn_w[0], rwkv_gn_b[0], w_out[0],
                        norm_ffn_g[0], peer_w_q[0], peer_sub_keys[0], peer_u[0], peer_v[0])
```

```python
import functools

import jax
import jax.numpy as jnp
from jax import lax
from jax.experimental import pallas as pl
from jax.experimental.pallas import tpu as pltpu
from jax.experimental.pallas import tpu_sc as plsc

F32 = jnp.float32
BF16 = jnp.bfloat16
I32 = jnp.int32
HI = lax.Precision.HIGHEST

D_MODEL = 1024
CONV_CH = 512
RWKV_CH = 512
HEAD = 64
CONV_WIDTH = 31
LORA_W = 64
LORA_A = 64
LORA_G = 128
RWKV_PROJ = 3 * RWKV_CH + LORA_W + LORA_A + LORA_G
IN_PROJ = 2 * CONV_CH + RWKV_PROJ
PEER_HEADS = 8
PEER_NKEYS = 128
PEER_EXPERTS = PEER_NKEYS * PEER_NKEYS
PEER_DQ = 256
PEER_TOPK = 16
NPAIR = PEER_HEADS * PEER_TOPK
RMS_EPS = 1e-6
LN_EPS = 1e-5
GN_EPS = 64e-5

LANES = 128
CHUNKS = D_MODEL // LANES
WORD_ROWS = CHUNKS // 2
CHUNK_T = 64
VMEM_LIMIT = 56 * 1024 * 1024


def _cparams(sem, vmem=None):
    return pltpu.CompilerParams(dimension_semantics=sem, vmem_limit_bytes=vmem or VMEM_LIMIT)


def _dot(a, b, precision=None):
    return jnp.dot(a, b, precision=precision, preferred_element_type=F32)


def _dot_nt(a, b, precision=None):
    return lax.dot_general(a, b, (((1,), (1,)), ((), ())), precision=precision, preferred_element_type=F32)


def _dot_tn(a, b, precision=None):
    return lax.dot_general(a, b, (((0,), (0,)), ((), ())), precision=precision, preferred_element_type=F32)


def _split(a):
    hi = a.astype(BF16)
    return hi, (a - hi.astype(F32)).astype(BF16)


def _mm(a, b, passes, dot=_dot):
    if passes == 6:
        return dot(a, b, HI)
    if passes == 1:
        return dot(a.astype(BF16), b.astype(BF16))
    ka = 0 if dot is _dot_tn else 1
    kb = 1 if dot is _dot_nt else 0
    ah, al = _split(a)
    bh, bl = _split(b)
    return dot(jnp.concatenate([ah, ah, al], axis=ka), jnp.concatenate([bh, bl, bh], axis=kb))


def _mm_bf16_rhs(a, b_bf16, terms=2):
    parts = []
    for _ in range(terms):
        p = a.astype(BF16)
        parts.append(p)
        a = a - p.astype(F32)
    return _dot(jnp.concatenate(parts, axis=1), jnp.concatenate([b_bf16] * terms, axis=0))


def _head_sums(x, bd_bf16):
    return jnp.concatenate([_mm_bf16_rhs(x[:, i * LANES:(i + 1) * LANES], bd_bf16, 3)
                            for i in range(x.shape[1] // LANES)], axis=1)


def _rms(x, g):
    return x * lax.rsqrt(jnp.mean(x * x, axis=-1, keepdims=True) + RMS_EPS) * g


def _mod_kernel(c_ref, w_ref, b_ref, o_ref):
    c = c_ref[...]
    o_ref[...] = _dot(c * jax.nn.sigmoid(c), w_ref[...], HI) + b_ref[...]


def _mod(c, w, b):
    bsz = c.shape[0]
    n = w.shape[1]
    tn = 1024
    return pl.pallas_call(
        _mod_kernel,
        grid=(n // tn,),
        in_specs=[pl.BlockSpec((bsz, D_MODEL), lambda j: (0, 0)),
                  pl.BlockSpec((D_MODEL, tn), lambda j: (0, j)),
                  pl.BlockSpec((1, tn), lambda j: (0, j))],
        out_specs=pl.BlockSpec((bsz, tn), lambda j: (0, j)),
        out_shape=jax.ShapeDtypeStruct((bsz, n), F32),
        compiler_params=_cparams(("parallel",)),
    )(c, w, b.reshape(1, n))


def _in_proj_kernel(x_ref, sh_ref, sc_ref, g_ref, w_ref, yglu_ref, prw_ref):
    u = _rms(x_ref[0], g_ref[...]) * (1.0 + sc_ref[0]) + sh_ref[0]
    p = _dot(u.astype(BF16), w_ref[...])
    yglu_ref[0] = p[:, :CONV_CH] * jax.nn.sigmoid(p[:, CONV_CH:2 * CONV_CH])
    prw_ref[0] = p[:, 2 * CONV_CH:]


def _in_proj(x, sh, sc, g, w_bf16, tm=256):
    bsz, s, _ = x.shape
    vec = pl.BlockSpec((1, 1, D_MODEL), lambda b, i: (b, 0, 0))
    return pl.pallas_call(
        _in_proj_kernel,
        grid=(bsz, s // tm),
        in_specs=[pl.BlockSpec((1, tm, D_MODEL), lambda b, i: (b, i, 0)), vec, vec,
                  pl.BlockSpec((1, D_MODEL), lambda b, i: (0, 0)),
                  pl.BlockSpec((D_MODEL, IN_PROJ), lambda b, i: (0, 0))],
        out_specs=[pl.BlockSpec((1, tm, CONV_CH), lambda b, i: (b, i, 0)),
                   pl.BlockSpec((1, tm, RWKV_PROJ), lambda b, i: (b, i, 0))],
        out_shape=[jax.ShapeDtypeStruct((bsz, s, CONV_CH), F32),
                   jax.ShapeDtypeStruct((bsz, s, RWKV_PROJ), F32)],
        compiler_params=_cparams(("parallel", "parallel")),
    )(x, sh, sc, g.reshape(1, D_MODEL), w_bf16)


CONV_HALO = 32
CONV_ROWS = 64


def _conv_kernel(cur_ref, prev_ref, w_ref, b_ref, lnw_ref, lnb_ref, o_ref, pad_ref):
    tc = cur_ref.shape[1]
    pad_ref[0, 0:CONV_HALO, :] = jnp.where(pl.program_id(1) > 0, prev_ref[0], 0.0)
    pad_ref[0, CONV_HALO:CONV_HALO + tc, :] = cur_ref[0]
    span = CONV_HALO + tc - 8
    for r in range(1, 8):
        pad_ref[r, 0:span, :] = pad_ref[0, r:r + span, :]
    off = CONV_HALO - (CONV_WIDTH - 1)
    for r0 in range(0, tc, CONV_ROWS):
        acc = jnp.zeros((CONV_ROWS, CONV_CH), F32)
        for j in range(CONV_WIDTH):
            q, r = divmod(off + j, 8)
            acc = acc + w_ref[j:j + 1, :] * pad_ref[r, r0 + 8 * q:r0 + 8 * q + CONV_ROWS, :]
        y = acc + b_ref[...]
        mu = jnp.mean(y, axis=-1, keepdims=True)
        yc = y - mu
        var = jnp.mean(yc * yc, axis=-1, keepdims=True)
        yn = yc * lax.rsqrt(var + LN_EPS) * lnw_ref[...] + lnb_ref[...]
        o_ref[0, r0:r0 + CONV_ROWS, :] = (yn * jax.nn.sigmoid(yn)).astype(o_ref.dtype)


def _conv(yglu, w, b, lnw, lnb, tc=256):
    bsz, s, _ = yglu.shape
    hb = tc // CONV_HALO
    row = lambda a: a.reshape(1, CONV_CH)
    const = lambda shp: pl.BlockSpec(shp, lambda bb, i: (0, 0))
    return pl.pallas_call(
        _conv_kernel,
        grid=(bsz, s // tc),
        in_specs=[pl.BlockSpec((1, tc, CONV_CH), lambda bb, i: (bb, i, 0)),
                  pl.BlockSpec((1, CONV_HALO, CONV_CH), lambda bb, i: (bb, jnp.maximum(i * hb - 1, 0), 0)),
                  const((CONV_WIDTH, CONV_CH)), const((1, CONV_CH)), const((1, CONV_CH)), const((1, CONV_CH))],
        out_specs=pl.BlockSpec((1, tc, CONV_CH), lambda bb, i: (bb, i, 0)),
        out_shape=jax.ShapeDtypeStruct((bsz, s, CONV_CH), BF16),
        scratch_shapes=[pltpu.VMEM((8, CONV_HALO + tc, CONV_CH), F32)],
        compiler_params=_cparams(("parallel", "parallel")),
    )(yglu, yglu, w, row(b), row(lnw), row(lnb))


def _softplus(z):
    return jnp.maximum(z, 0.0) + jnp.log1p(jnp.exp(-jnp.abs(z)))


def _rwkv_features(cur, prow, mu, w0, w2p, a0, a2p, g2, k_k, k_a, r_k, bd):
    rows = lax.broadcasted_iota(I32, cur.shape, 0)
    prev = jnp.where(rows == 0, prow, pltpu.roll(cur, 1, axis=0))
    xs = cur + mu * (prev - cur)
    r = xs[:, 0:RWKV_CH]
    k = xs[:, RWKV_CH:2 * RWKV_CH]
    v = xs[:, 2 * RWKV_CH:3 * RWKV_CH]
    wa = xs[:, 3 * RWKV_CH:3 * RWKV_CH + LORA_W + LORA_A]
    gl = xs[:, 3 * RWKV_CH + LORA_W + LORA_A:]
    w = -_softplus(-(w0 + _mm(jnp.tanh(wa), w2p, 3))) - 0.5
    a = jax.nn.sigmoid(a0 + _mm(wa, a2p, 3))
    g = _mm(jax.nn.sigmoid(gl), g2, 3)
    kk = k * k_k
    kkn = kk / jnp.maximum(jnp.sqrt(_head_sums(kk * kk, bd)), 1e-12)
    k2 = k * (1.0 + (a - 1.0) * k_a)
    bonus = _head_sums(r * k2 * r_k, bd) * v
    return r, k2, v, -jnp.exp(w), -kkn, kkn * a, g, bonus


RWKV_PASSES = {"gram": 3, "lakv": 3, "solve": 3, "out": 3, "state": 3}
SOLVE_BLK = 16


def _expand(x, lane_lo):
    return jnp.concatenate([jnp.where(lane_lo, x, 0.0), jnp.where(lane_lo, 0.0, x)], axis=0)


def _rwkv_kernel(p_ref, mu_ref, w0_ref, w2_ref, a0_ref, a2_ref, g2_ref, kk_ref, ka_ref, rk_ref, bd_ref,
                 bdm_ref, gnw_ref, gnb_ref, y_ref, s_ref, last_ref):
    C = CHUNK_T
    H2 = 2 * C

    @pl.when(pl.program_id(1) == 0)
    def _():
        s_ref[...] = jnp.zeros_like(s_ref)
        last_ref[...] = jnp.zeros_like(last_ref)

    cur = p_ref[0]
    r, k, vv, lw, a, b, gate, bonus = _rwkv_features(
        cur, last_ref[...], mu_ref[...], w0_ref[...], w2_ref[...], a0_ref[...], a2_ref[...], g2_ref[...],
        kk_ref[...], ka_ref[...], rk_ref[...], bd_ref[...])
    last_ref[...] = cur[C - 1:C, :]

    tri = (lax.broadcasted_iota(I32, (C, C), 0) >= lax.broadcasted_iota(I32, (C, C), 1)).astype(F32)
    cum = _dot(tri, lw, HI)
    tot = cum[C - 1:C, :]
    e_pos = jnp.exp(cum)
    e_neg = jnp.exp(-cum)
    e_rem = jnp.exp(tot - cum)
    rt = r * e_pos
    at = a * jnp.exp(cum - lw)
    kt = k * e_neg
    bt = b * e_neg
    kp = k * e_rem
    bp = b * e_rem
    pc = jnp.exp(tot)

    lane_lo = lax.broadcasted_iota(I32, (C, LANES), 1) < HEAD
    tt = lax.broadcasted_iota(I32, (H2, H2), 0) % C
    ss = lax.broadcasted_iota(I32, (H2, H2), 1) % C
    strict = tt > ss
    incl = tt >= ss
    near = tt // SOLVE_BLK == ss // SOLVE_BLK
    eye = lax.broadcasted_iota(I32, (LANES, LANES), 0) == lax.broadcasted_iota(I32, (LANES, LANES), 1)

    P = RWKV_PASSES
    pairs = range(RWKV_CH // LANES)
    each = lambda fn, *lists: [fn(*args) for args in zip(*lists)]
    sls = [slice(hp * LANES, (hp + 1) * LANES) for hp in pairs]
    ax, rx, bx, kx, vx, bpx, kpx = ([_expand(t[:, sl], lane_lo) for sl in sls] for t in (at, rt, bt, kt, vv, bp, kp))
    gram = each(lambda a_, r_, b_, k_: _mm(jnp.concatenate([a_, r_], axis=0), jnp.concatenate([b_, k_], axis=0),
                                           P["gram"], _dot_nt), ax, rx, bx, kx)
    l_ab = [jnp.where(strict, g_[:H2, :H2], 0.0) for g_ in gram]
    l_ak = [jnp.where(strict, g_[:H2, H2:], 0.0) for g_ in gram]
    m_r = [jnp.concatenate([jnp.where(incl, g_[H2:, :H2], 0.0), jnp.where(incl, g_[H2:, H2:], 0.0)], axis=1)
           for g_ in gram]
    dg = [jnp.where(near, l_, 0.0) for l_ in l_ab]
    lakv = each(lambda l_, v_: _mm(l_, v_, P["lakv"]), l_ak, vx)
    xf = each(lambda a_, lv_, l_, d_: jnp.concatenate([a_, lv_, l_ - d_], axis=1), ax, lakv, l_ab, dg)
    n_sq = SOLVE_BLK.bit_length() - 1
    for it in range(n_sq):
        xf = each(lambda x_, d_: x_ + _mm(d_, x_, P["solve"]), xf, dg)
        if it + 1 < n_sq:
            dg = [_mm(d_, d_, P["solve"]) for d_ in dg]
    x = [x_[:, :2 * LANES] for x_ in xf]
    f = [x_[:, 2 * LANES:] for x_ in xf]
    n_sq = (C // SOLVE_BLK).bit_length() - 1
    for it in range(n_sq):
        x = each(lambda x_, f_: x_ + _mm(f_, x_, P["solve"]), x, f)
        if it + 1 < n_sq:
            f = [_mm(f_, f_, P["solve"]) for f_ in f]
    zero = jnp.zeros((H2, LANES), F32)
    z = each(lambda x_, v_: jnp.concatenate([x_, jnp.concatenate([zero, v_], axis=1)], axis=0), x, vx)
    w1 = each(lambda m_, z_: _mm(m_, z_, P["out"]), m_r, z)
    w2 = each(lambda b_, k_, z_: _mm(jnp.concatenate([b_, k_], axis=0), z_, P["out"], _dot_tn), bpx, kpx, z)
    ys = []
    for hp in pairs:
        ra = rx[hp] + w1[hp][:, :LANES]
        ra = ra[:C] + ra[C:]
        y0 = w1[hp][:C, LANES:] + w1[hp][C:, LANES:]
        mt = w2[hp][:, :LANES] + jnp.where(eye, pc[:, sls[hp]], 0.0)
        s0 = s_ref[hp]
        ys.append(_mm(ra, s0, P["state"]) + y0)
        s_ref[hp] = _mm(mt, s0, P["state"]) + w2[hp][:, LANES:]
    yc = [y_ - _mm_bf16_rhs(y_, bdm_ref[...]) for y_ in ys]
    ys = [c_ * lax.rsqrt(_mm_bf16_rhs(c_ * c_, bdm_ref[...]) + GN_EPS) for c_ in yc]

    yn = jnp.concatenate(ys, axis=1) * gnw_ref[...] + gnb_ref[...]
    y_ref[0] = ((yn + bonus) * gate).astype(y_ref.dtype)


def _rwkv(prw, mu, w0, w2p, a0, a2p, g2, k_k, k_a, r_k, gnw, gnb):
    bsz, s, _ = prw.shape
    row = lambda t: t.reshape(1, -1)
    const = lambda shp: pl.BlockSpec(shp, lambda bb, c: (0, 0))
    vec = const((1, RWKV_CH))
    lora = const((LORA_W + LORA_A, RWKV_CH))
    return pl.pallas_call(
        _rwkv_kernel,
        grid=(bsz, s // CHUNK_T),
        in_specs=[pl.BlockSpec((1, CHUNK_T, RWKV_PROJ), lambda bb, c: (bb, c, 0)), const((1, RWKV_PROJ)),
                  vec, lora, vec, lora, const((LORA_G, RWKV_CH)), vec, vec, vec,
                  const((LANES, LANES)), const((LANES, LANES)), vec, vec],
        out_specs=pl.BlockSpec((1, CHUNK_T, RWKV_CH), lambda bb, c: (bb, c, 0)),
        out_shape=jax.ShapeDtypeStruct((bsz, s, RWKV_CH), BF16),
        scratch_shapes=[pltpu.VMEM((RWKV_CH // LANES, LANES, LANES), F32), pltpu.VMEM((1, RWKV_PROJ), F32)],
        compiler_params=_cparams(("parallel", "arbitrary")),
        name="rwkv",
    )(prw, row(mu), row(w0), w2p, row(a0), a2p, g2, row(k_k), row(k_a), row(r_k),
      _block_diag(LANES, HEAD, 1.0).astype(BF16), _block_diag(LANES, HEAD, 1.0 / HEAD).astype(BF16),
      row(gnw), row(gnb))


def _out_proj_kernel(yc_ref, yr_ref, wo_ref, x_ref, gt_ref, g_ref, sh_ref, sc_ref, x1_o, u2_o):
    mix = _dot(yc_ref[0], wo_ref[0:CONV_CH, :]) + _dot(yr_ref[0], wo_ref[CONV_CH:, :])
    x1 = x_ref[0] + gt_ref[0] * mix
    x1_o[0] = x1
    u2 = _rms(x1, g_ref[...]) * (1.0 + sc_ref[0]) + sh_ref[0]
    for c in range(CHUNKS):
        u2_o[0, :, c, :] = u2[:, c * LANES:(c + 1) * LANES]


def _out_proj(yc, yr, wo_bf16, x, gt, g, sh, sc, tm=256):
    bsz, s, _ = x.shape
    vec = pl.BlockSpec((1, 1, D_MODEL), lambda b, i: (b, 0, 0))
    tile = lambda w: pl.BlockSpec((1, tm, w), lambda b, i: (b, i, 0))
    return pl.pallas_call(
        _out_proj_kernel,
        grid=(bsz, s // tm),
        in_specs=[tile(CONV_CH), tile(RWKV_CH), pl.BlockSpec((D_MODEL, D_MODEL), lambda b, i: (0, 0)),
                  tile(D_MODEL), vec, pl.BlockSpec((1, D_MODEL), lambda b, i: (0, 0)), vec, vec],
        out_specs=[tile(D_MODEL), pl.BlockSpec((1, tm, CHUNKS, LANES), lambda b, i: (b, i, 0, 0))],
        out_shape=[jax.ShapeDtypeStruct((bsz, s, D_MODEL), F32),
                   jax.ShapeDtypeStruct((bsz, s, CHUNKS, LANES), F32)],
        compiler_params=_cparams(("parallel", "parallel")),
        name="out_proj",
    )(yc, yr, wo_bf16, x, gt, g.reshape(1, D_MODEL), sh, sc)


HEADS_STEP = 4


def _topk_rows(ss, k):
    n, t = ss[0].shape
    rids = [lax.broadcasted_iota(I32, (8, t), 0) + r0 for r0 in range(0, n, 8)]
    ss = [[s[r0:r0 + 8] for r0 in range(0, n, 8)] for s in ss]
    vals, ids = [[] for _ in ss], [[] for _ in ss]
    for _ in range(k):
        tops = [_max_with_tag(s, [rids]) for s in ss]
        ss = [[jnp.where(r == j, -jnp.inf, c) for c, r in zip(s, rids)] for s, (_, (j,)) in zip(ss, tops)]
        for v, i, (m, (j,)) in zip(vals, ids, tops):
            v.append(m)
            i.append(j)
    return [(jnp.concatenate(v, axis=0), jnp.concatenate(i, axis=0)) for v, i in zip(vals, ids)]


def _max_with_tag(chunks, tags):
    vals, tags = list(chunks), [list(tg) for tg in tags]
    while len(vals) > 1:
        nv, nt = [], [[] for _ in tags]
        for a in range(0, len(vals) - 1, 2):
            first = vals[a] >= vals[a + 1]
            nv.append(jnp.maximum(vals[a], vals[a + 1]))
            for dst, tg in zip(nt, tags):
                dst.append(jnp.where(first, tg[a], tg[a + 1]))
        if len(vals) % 2:
            nv.append(vals[-1])
            for dst, tg in zip(nt, tags):
                dst.append(tg[-1])
        vals, tags = nv, nt
    v8 = vals[0]
    m = jnp.max(v8, axis=0, keepdims=True)
    big = jnp.iinfo(jnp.int32).max
    key = jnp.min(jnp.where(v8 == m, tags[0][0], big), axis=0, keepdims=True)
    out = [key]
    for tg in tags[1:]:
        out.append(jnp.sum(jnp.where(tags[0][0] == key, tg[0], 0), axis=0, keepdims=True))
    return m, out


def _route_kernel(u_ref, wq_ref, keys_ref, idx_o, gate_o, q_ref, idx_s, gate_s):
    tm = u_ref.shape[0]
    u = jnp.concatenate([u_ref[:, c, :] for c in range(CHUNKS)], axis=1)
    q = _dot(u.astype(BF16), wq_ref[...])
    for j in range(2 * PEER_HEADS):
        q_ref[j] = q[:, j * LANES:(j + 1) * LANES]
    K = PEER_TOPK
    tt = LANES
    tiles = range(tm // tt)
    row8 = lax.broadcasted_iota(I32, (8, tt), 0)

    def heads(i, carry):
        probs = [(i * HEADS_STEP + dh, lt) for dh in range(HEADS_STEP) for lt in tiles]
        scores = [_mm(keys_ref[2 * h + p], q_ref[2 * h + p, lt * tt:(lt + 1) * tt, :], 3, _dot_nt)
                  for h, lt in probs for p in range(2)]
        tops = _topk_rows(scores, K)
        ss, flats, eids = [], [], []
        for n_ in range(len(probs)):
            (av, ai), (bv, bi) = tops[2 * n_], tops[2 * n_ + 1]
            cs = [av[0:1] + bv[0:8], av[0:1] + bv[8:16]]
            cf = [row8, row8 + 8]
            ce = [ai[0:1] * PEER_NKEYS + bi[0:8], ai[0:1] * PEER_NKEYS + bi[8:16]]
            for x in range(1, 8):
                cs.append(jnp.where(row8 < K // (x + 1), av[x:x + 1] + bv[0:8], -jnp.inf))
                cf.append(x * K + row8)
                ce.append(ai[x:x + 1] * PEER_NKEYS + bi[0:8])
            cs.append(av[8:16] + bv[0:1])
            cf.append((row8 + 8) * K)
            ce.append(ai[8:16] * PEER_NKEYS + bi[0:1])
            ss.append(cs)
            flats.append(cf)
            eids.append(ce)
        best, experts = [[] for _ in probs], [[] for _ in probs]
        for _ in range(K):
            tops2 = [_max_with_tag(s, [fl, ei]) for s, fl, ei in zip(ss, flats, eids)]
            for n_, (m, (f, e)) in enumerate(tops2):
                best[n_].append(m)
                experts[n_].append(e)
            ss = [[jnp.where(fc == f, -jnp.inf, c) for c, fc in zip(s, fl)]
                  for s, fl, (_, (f, _e)) in zip(ss, flats, tops2)]
        for n_, (h, lt) in enumerate(probs):
            b = jnp.concatenate(best[n_], axis=0)
            e = jnp.exp(b - b[0:1])
            gate_s[lt, h] = e / jnp.sum(e, axis=0, keepdims=True)
            idx_s[lt, h] = jnp.concatenate(experts[n_], axis=0) * WORD_ROWS
        return carry

    lax.fori_loop(0, PEER_HEADS // HEADS_STEP, heads, 0)
    for lt in tiles:
        idx_o[lt * tt:(lt + 1) * tt, :] = jnp.transpose(idx_s[lt].reshape(NPAIR, tt))
        gate_o[lt * tt:(lt + 1) * tt, :] = jnp.transpose(gate_s[lt].reshape(NPAIR, tt))


def _route(u2, wq_bf16, keys, tm=256):
    n = u2.shape[0]
    oblk = pl.BlockSpec((tm, NPAIR), lambda i: (i, 0))
    return pl.pallas_call(
        _route_kernel,
        grid=(n // tm,),
        in_specs=[pl.BlockSpec((tm, CHUNKS, LANES), lambda i: (i, 0, 0)),
                  pl.BlockSpec((D_MODEL, PEER_HEADS * PEER_DQ), lambda i: (0, 0)),
                  pl.BlockSpec((2 * PEER_HEADS, PEER_NKEYS, PEER_DQ // 2), lambda i: (0, 0, 0))],
        out_specs=[oblk, oblk],
        out_shape=[jax.ShapeDtypeStruct((n, NPAIR), I32), jax.ShapeDtypeStruct((n, NPAIR), F32)],
        scratch_shapes=[pltpu.VMEM((2 * PEER_HEADS, tm, LANES), F32),
                        pltpu.VMEM((tm // LANES, PEER_HEADS, PEER_TOPK, LANES), I32),
                        pltpu.VMEM((tm // LANES, PEER_HEADS, PEER_TOPK, LANES), F32)],
        compiler_params=_cparams(("parallel",)),
        name="route",
    )(u2, wq_bf16, keys)


TOK_UNROLL = 16


def _pack_kernel(t_ref, o_ref):
    te = t_ref.shape[0]
    bits = lambda v: lax.bitcast_convert_type(v.astype(BF16).astype(F32), I32)
    for s in range(WORD_ROWS):
        lo = bits(t_ref[:, (2 * s) * LANES:(2 * s + 1) * LANES])
        hi = bits(t_ref[:, (2 * s + 1) * LANES:(2 * s + 2) * LANES])
        o_ref[pl.ds(s, te, stride=WORD_ROWS), :] = (hi & -65536) | lax.shift_right_logical(lo, 16)


def _pack_table(t, te=512):
    e = t.shape[0]
    return pl.pallas_call(
        _pack_kernel,
        grid=(e // te,),
        in_specs=[pl.BlockSpec((te, D_MODEL), lambda i: (i, 0))],
        out_specs=pl.BlockSpec((te * WORD_ROWS, LANES), lambda i: (i, 0)),
        out_shape=jax.ShapeDtypeStruct((e * WORD_ROWS, LANES), I32),
        compiler_params=_cparams(("parallel",)),
        name="pack_table",
    )(t)


def _gather_rows(idx_ref, t0, tab_ref, g_ref):
    for u in range(TOK_UNROLL):
        idx_row = idx_ref.at[t0 + u]
        for kk in range(NPAIR):
            row = pl.multiple_of(idx_row[kk], WORD_ROWS)
            g_ref[u, kk * WORD_ROWS:(kk + 1) * WORD_ROWS, :] = tab_ref[pl.ds(row, WORD_ROWS), :]


def _peer_u_kernel(idx_ref, x_ref, gate_ref, keep_ref, sel_ref, tab_ref, w_o, g_ref, p_ref):
    tp = x_ref.shape[0]

    def tokens(i, carry):
        t0 = pl.multiple_of(i * TOK_UNROLL, TOK_UNROLL)
        _gather_rows(idx_ref, t0, tab_ref, g_ref)
        xs = x_ref[pl.ds(t0, TOK_UNROLL)]
        parts = []
        for u in range(TOK_UNROLL):
            g = pltpu.bitcast(g_ref[u], BF16)
            xc = xs[u]
            xt = jnp.concatenate([xc] * (LANES // CHUNKS), axis=0).astype(BF16)
            r = _dot_nt(g, xt) * keep_ref[...]
            parts.append(jnp.sum(r.reshape(NPAIR // 16, 16 * CHUNKS, LANES), axis=1))
        p_ref[pl.ds(pl.multiple_of(t0 * 8, 8 * TOK_UNROLL), 8 * TOK_UNROLL), :] = jnp.concatenate(parts, axis=0)
        return carry

    lax.fori_loop(0, tp // TOK_UNROLL, tokens, 0)
    z = _mm_bf16_rhs(p_ref[...], sel_ref[...], 3)
    grp = lax.broadcasted_iota(I32, z.shape, 0) % 8 == lax.broadcasted_iota(I32, z.shape, 1) // 16
    h = jnp.sum(jnp.where(grp, z, 0.0).reshape(tp, 8, LANES), axis=1)
    w_o[...] = gate_ref[...] * (0.5 * h * (1.0 + lax.erf(h * (2.0 ** -0.5))))


def _peer_u(idx, x3, gate, tab, tp=128):
    n = x3.shape[0]
    rows = jnp.arange(NPAIR * CHUNKS)[:, None]
    lanes = jnp.arange(LANES)[None, :]
    keep = ((lanes % CHUNKS == rows % CHUNKS) & (lanes // CHUNKS == (rows // CHUNKS) % 16)).astype(F32)
    sel = (jnp.arange(LANES)[:, None] // CHUNKS == jnp.arange(LANES)[None, :] % 16).astype(BF16)
    return pl.pallas_call(
        _peer_u_kernel,
        grid=(n // tp,),
        in_specs=[pl.BlockSpec((tp, NPAIR), lambda i: (i, 0), memory_space=pltpu.SMEM),
                  pl.BlockSpec((tp, CHUNKS, LANES), lambda i: (i, 0, 0)),
                  pl.BlockSpec((tp, NPAIR), lambda i: (i, 0)),
                  pl.BlockSpec((NPAIR * CHUNKS, LANES), lambda i: (0, 0)),
                  pl.BlockSpec((LANES, LANES), lambda i: (0, 0)),
                  pl.BlockSpec(memory_space=pltpu.VMEM)],
        out_specs=pl.BlockSpec((tp, NPAIR), lambda i: (i, 0)),
        out_shape=jax.ShapeDtypeStruct((n, NPAIR), F32),
        scratch_shapes=[pltpu.VMEM((TOK_UNROLL, NPAIR * WORD_ROWS, LANES), I32),
                        pltpu.VMEM((tp * 8, LANES), F32)],
        compiler_params=_cparams(("arbitrary",)),
        name="peer_u",
    )(idx, x3, gate, keep, sel, tab)


def _peer_v_kernel(idx_ref, w_ref, rep_ref, diag_ref, tab_ref, o_ref, g_ref, wx_ref):
    tp = w_ref.shape[0]
    wx_ref[...] = _mm_bf16_rhs(w_ref[...], rep_ref[...], 3)

    def tokens(i, carry):
        t0 = pl.multiple_of(i * TOK_UNROLL, TOK_UNROLL)
        _gather_rows(idx_ref, t0, tab_ref, g_ref)
        wx = wx_ref[pl.ds(t0, TOK_UNROLL), :]
        outs = []
        for u in range(TOK_UNROLL):
            g = pltpu.bitcast(g_ref[u], BF16)
            wm = (wx[u:u + 1, :] * diag_ref[...]).astype(BF16)
            outs.append(_dot(wm, g))
        o_ref[pl.ds(t0, TOK_UNROLL)] = jnp.stack(outs, axis=0)
        return carry

    lax.fori_loop(0, tp // TOK_UNROLL, tokens, 0)


def _peer_v(idx, w, tab, tp=128):
    n = w.shape[0]
    rep = jnp.repeat(jnp.eye(NPAIR, dtype=BF16), CHUNKS, axis=1)
    diag = (jnp.arange(CHUNKS)[:, None] == jnp.arange(NPAIR * CHUNKS)[None, :] % CHUNKS).astype(F32)
    return pl.pallas_call(
        _peer_v_kernel,
        grid=(n // tp,),
        in_specs=[pl.BlockSpec((tp, NPAIR), lambda i: (i, 0), memory_space=pltpu.SMEM),
                  pl.BlockSpec((tp, NPAIR), lambda i: (i, 0)),
                  pl.BlockSpec((NPAIR, NPAIR * CHUNKS), lambda i: (0, 0)),
                  pl.BlockSpec((CHUNKS, NPAIR * CHUNKS), lambda i: (0, 0)),
                  pl.BlockSpec(memory_space=pltpu.VMEM)],
        out_specs=pl.BlockSpec((tp, CHUNKS, LANES), lambda i: (i, 0, 0)),
        out_shape=jax.ShapeDtypeStruct((n, CHUNKS, LANES), F32),
        scratch_shapes=[pltpu.VMEM((TOK_UNROLL, NPAIR * WORD_ROWS, LANES), I32),
                        pltpu.VMEM((tp, NPAIR * CHUNKS), F32)],
        compiler_params=_cparams(("arbitrary",)),
        name="peer_v",
    )(idx, w, rep, diag, tab)


SC_LANES = 16
SC_WORKERS = 32
SC_ROWS = 32
SC_TOKENS = 16
SC_BATCHES = 2
SC_COLS = 16


def _peer_v_sc(ids, wts, table):
    m = ids.shape[0]
    per_worker = m // SC_WORKERS
    blocks = NPAIR // SC_ROWS
    ids_b = ids.reshape(m * blocks, SC_ROWS)
    wts_f = wts.reshape(m * NPAIR)
    mesh = plsc.VectorSubcoreMesh(core_axis_name="c", subcore_axis_name="s")

    @functools.partial(
        pl.kernel, mesh=mesh, out_type=jax.ShapeDtypeStruct((m, D_MODEL), F32),
        scratch_types=[pltpu.VMEM((SC_TOKENS * blocks, SC_ROWS), I32), pltpu.VMEM((SC_TOKENS * NPAIR,), F32),
                       pltpu.VMEM((2, SC_ROWS, D_MODEL), F32), pltpu.VMEM((D_MODEL,), F32),
                       pltpu.SemaphoreType.DMA((2,))],
        compiler_params=pltpu.CompilerParams(needs_layout_passes=False),
        name="peer_v_sc")
    def run(tab_hbm, ids_hbm, w_hbm, out_hbm, ids_v, w_v, rows_v, acc_v, sems):
        wid = lax.axis_index("s") * 2 + lax.axis_index("c")
        base = wid * per_worker
        zero = jnp.zeros((SC_LANES,), F32)

        def gather(u, b):
            return pltpu.make_async_copy(tab_hbm.at[ids_v.at[u * blocks + b]], rows_v.at[b % 2], sems.at[b % 2])

        @pl.loop(0, per_worker // SC_TOKENS)
        def _(tb):
            t0 = base + tb * SC_TOKENS
            pltpu.sync_copy(ids_hbm.at[pl.ds(t0 * blocks, SC_TOKENS * blocks)], ids_v)
            pltpu.sync_copy(w_hbm.at[pl.ds(t0 * NPAIR, SC_TOKENS * NPAIR)], w_v)

            @pl.loop(0, SC_TOKENS)
            def _(u):
                for j in range(D_MODEL // SC_LANES):
                    acc_v[pl.ds(j * SC_LANES, SC_LANES)] = zero
                gather(u, 0).start()
                for b in range(blocks):
                    if b + 1 < blocks:
                        gather(u, b + 1).start()
                    gather(u, b).wait()

                    for c0 in range(0, D_MODEL // SC_LANES, SC_COLS):
                        def row(r, accs, b=b, c0=c0):
                            wk = plsc.load_gather(w_v, [jnp.full((SC_LANES,), u * NPAIR + b * SC_ROWS + r, I32)])
                            return tuple(a + wk * rows_v[b % 2, r, pl.ds((c0 + j) * SC_LANES, SC_LANES)]
                                         for j, a in enumerate(accs))
                        accs = lax.fori_loop(0, SC_ROWS, row, (zero,) * SC_COLS)
                        for j, a in enumerate(accs):
                            plsc.addupdate(acc_v.at[pl.ds((c0 + j) * SC_LANES, SC_LANES)], a)
                pltpu.sync_copy(acc_v, out_hbm.at[t0 + u])

    return run(table, ids_b, wts_f)


def _final_kernel(x1_ref, psc_ref, ptc_ref, gt_ref, g_ref, o_ref, *, sc_batches):
    p_tc = jnp.concatenate([ptc_ref[0, :, c, :] for c in range(CHUNKS)], axis=1)
    p = jnp.where(pl.program_id(0) < sc_batches, psc_ref[0], p_tc)
    o_ref[0] = _rms(x1_ref[0] + gt_ref[0] * p, g_ref[...])


def _final(x1, peer_sc, peer_tc, gt, g, tm=512):
    bsz, s, _ = x1.shape
    k = peer_sc.shape[0]
    tile = pl.BlockSpec((1, tm, D_MODEL), lambda b, i: (b, i, 0))
    return pl.pallas_call(
        functools.partial(_final_kernel, sc_batches=k),
        grid=(bsz, s // tm),
        in_specs=[tile, pl.BlockSpec((1, tm, D_MODEL), lambda b, i: (jnp.minimum(b, k - 1), i, 0)),
                  pl.BlockSpec((1, tm, CHUNKS, LANES), lambda b, i: (jnp.maximum(b - k, 0), i, 0, 0)),
                  pl.BlockSpec((1, 1, D_MODEL), lambda b, i: (b, 0, 0)),
                  pl.BlockSpec((1, D_MODEL), lambda b, i: (0, 0))],
        out_specs=tile,
        out_shape=jax.ShapeDtypeStruct((bsz, s, D_MODEL), F32),
        compiler_params=_cparams(("parallel", "parallel")),
        name="final",
    )(x1, peer_sc, peer_tc, gt, g.reshape(1, D_MODEL))


def _block_diag(width, group, value):
    i = jnp.arange(width) // group
    return jnp.where(i[:, None] == i[None, :], value, 0.0).astype(F32)


def _layer(x, mod, final_g, norm_mix_g, w_in, conv_dw_w, conv_dw_b, conv_ln_w, conv_ln_b, rwkv_mu, rwkv_w0, rwkv_w2,
           rwkv_a0, rwkv_a2, rwkv_g2, rwkv_k_k, rwkv_k_a, rwkv_r_k, rwkv_gn_w, rwkv_gn_b, w_out, norm_ffn_g,
           peer_w_q, peer_sub_keys, peer_u, peer_v):
    bsz, s, _ = x.shape
    sh_mix, sc_mix, gt_mix, sh_ffn, sc_ffn, gt_ffn = (
        mod[:, i * D_MODEL:(i + 1) * D_MODEL].reshape(bsz, 1, D_MODEL) for i in range(6))

    yglu, prw = _in_proj(x, sh_mix, sc_mix, norm_mix_g, w_in.astype(BF16))
    y_conv = _conv(yglu, conv_dw_w, conv_dw_b, conv_ln_w, conv_ln_b)

    zpad = jnp.zeros((LORA_W, RWKV_CH), F32)
    y_rwkv = _rwkv(prw, rwkv_mu, rwkv_w0, jnp.concatenate([rwkv_w2, zpad], axis=0), rwkv_a0,
                   jnp.concatenate([zpad, rwkv_a2], axis=0), rwkv_g2, rwkv_k_k, rwkv_k_a, rwkv_r_k.reshape(-1),
                   rwkv_gn_w, rwkv_gn_b)

    x1, u2 = _out_proj(y_conv, y_rwkv, w_out.astype(BF16), x, gt_mix, norm_ffn_g, sh_ffn, sc_ffn)

    n = bsz * s
    u3 = u2.reshape(n, CHUNKS, LANES)
    keys = peer_sub_keys.reshape(2 * PEER_HEADS, PEER_NKEYS, PEER_DQ // 2)
    idx, gate = _route(u3, peer_w_q.astype(BF16), keys)
    wts = _peer_u(idx, u3, gate, _pack_table(peer_u))
    n_sc = SC_BATCHES * s
    peer_sc = _peer_v_sc(lax.shift_right_logical(idx[:n_sc], 2), wts[:n_sc], peer_v)
    peer_tc = _peer_v(idx[n_sc:], wts[n_sc:], _pack_table(peer_v))
    return _final(x1, peer_sc.reshape(SC_BATCHES, s, D_MODEL), peer_tc.reshape(bsz - SC_BATCHES, s, CHUNKS, LANES),
                  gt_ffn, final_g)


def kernel(x, c, ada_w, ada_b, norm_mix_g, w_in, conv_dw_w, conv_dw_b, conv_ln_w, conv_ln_b, rwkv_mu, rwkv_w0,
           rwkv_w2, rwkv_a0, rwkv_a2, rwkv_g2, rwkv_k_k, rwkv_k_a, rwkv_r_k, rwkv_gn_w, rwkv_gn_b, w_out,
           norm_ffn_g, peer_w_q, peer_sub_keys, peer_u, peer_v, final_g):
    depth = ada_w.shape[0]
    assert depth == 1, "one layer: the final norm is fused into the last layer's residual"
    mod = _mod(c, ada_w[0], ada_b[0])
    return _layer(x, mod, final_g, norm_mix_g[0], w_in[0], conv_dw_w[0], conv_dw_b[0], conv_ln_w[0],
                        conv_ln_b[0], rwkv_mu[0], rwkv_w0[0], rwkv_w2[0], rwkv_a0[0], rwkv_a2[0], rwkv_g2[0],
                        rwkv_k_k[0], rwkv_k_a[0], rwkv_r_k[0], rwkv_gn_w[0], rwkv_gn_b[0], w_out[0],
                        norm_ffn_g[0], peer_w_q[0], peer_sub_keys[0], peer_u[0], peer_v[0])
```

```python
import functools

import jax
import jax.numpy as jnp
from jax import lax
from jax.experimental import pallas as pl
from jax.experimental.pallas import tpu as pltpu
from jax.experimental.pallas import tpu_sc as plsc

F32 = jnp.float32
BF16 = jnp.bfloat16
I32 = jnp.int32
HI = lax.Precision.HIGHEST

D_MODEL = 1024
CONV_CH = 512
RWKV_CH = 512
HEAD = 64
CONV_WIDTH = 31
LORA_W = 64
LORA_A = 64
LORA_G = 128
RWKV_PROJ = 3 * RWKV_CH + LORA_W + LORA_A + LORA_G
IN_PROJ = 2 * CONV_CH + RWKV_PROJ
PEER_HEADS = 8
PEER_NKEYS = 128
PEER_EXPERTS = PEER_NKEYS * PEER_NKEYS
PEER_DQ = 256
PEER_TOPK = 16
NPAIR = PEER_HEADS * PEER_TOPK
RMS_EPS = 1e-6
LN_EPS = 1e-5
GN_EPS = 64e-5

LANES = 128
CHUNKS = D_MODEL // LANES
WORD_ROWS = CHUNKS // 2
CHUNK_T = 64
VMEM_LIMIT = 56 * 1024 * 1024


def _cparams(sem, vmem=None):
    return pltpu.CompilerParams(dimension_semantics=sem, vmem_limit_bytes=vmem or VMEM_LIMIT)


def _dot(a, b, precision=None):
    return jnp.dot(a, b, precision=precision, preferred_element_type=F32)


def _dot_nt(a, b, precision=None):
    return lax.dot_general(a, b, (((1,), (1,)), ((), ())), precision=precision, preferred_element_type=F32)


def _dot_tn(a, b, precision=None):
    return lax.dot_general(a, b, (((0,), (0,)), ((), ())), precision=precision, preferred_element_type=F32)


def _split(a):
    hi = a.astype(BF16)
    return hi, (a - hi.astype(F32)).astype(BF16)


def _mm(a, b, passes, dot=_dot):
    if passes == 6:
        return dot(a, b, HI)
    if passes == 1:
        return dot(a.astype(BF16), b.astype(BF16))
    ka = 0 if dot is _dot_tn else 1
    kb = 1 if dot is _dot_nt else 0
    ah, al = _split(a)
    bh, bl = _split(b)
    return dot(jnp.concatenate([ah, ah, al], axis=ka), jnp.concatenate([bh, bl, bh], axis=kb))


def _mm_bf16_rhs(a, b_bf16, terms=2):
    parts = []
    for _ in range(terms):
        p = a.astype(BF16)
        parts.append(p)
        a = a - p.astype(F32)
    return _dot(jnp.concatenate(parts, axis=1), jnp.concatenate([b_bf16] * terms, axis=0))


def _head_sums(x, bd_bf16):
    return jnp.concatenate([_mm_bf16_rhs(x[:, i * LANES:(i + 1) * LANES], bd_bf16, 3)
                            for i in range(x.shape[1] // LANES)], axis=1)


def _rms(x, g):
    return x * lax.rsqrt(jnp.mean(x * x, axis=-1, keepdims=True) + RMS_EPS) * g


def _mod_kernel(c_ref, w_ref, b_ref, o_ref):
    c = c_ref[...]
    o_ref[...] = _dot(c * jax.nn.sigmoid(c), w_ref[...], HI) + b_ref[...]


def _mod(c, w, b):
    bsz = c.shape[0]
    n = w.shape[1]
    tn = 1024
    return pl.pallas_call(
        _mod_kernel,
        grid=(n // tn,),
        in_specs=[pl.BlockSpec((bsz, D_MODEL), lambda j: (0, 0)),
                  pl.BlockSpec((D_MODEL, tn), lambda j: (0, j)),
                  pl.BlockSpec((1, tn), lambda j: (0, j))],
        out_specs=pl.BlockSpec((bsz, tn), lambda j: (0, j)),
        out_shape=jax.ShapeDtypeStruct((bsz, n), F32),
        compiler_params=_cparams(("parallel",)),
    )(c, w, b.reshape(1, n))


def _in_proj_kernel(x_ref, sh_ref, sc_ref, g_ref, w_ref, yglu_ref, prw_ref):
    u = _rms(x_ref[0], g_ref[...]) * (1.0 + sc_ref[0]) + sh_ref[0]
    p = _dot(u.astype(BF16), w_ref[...])
    yglu_ref[0] = p[:, :CONV_CH] * jax.nn.sigmoid(p[:, CONV_CH:2 * CONV_CH])
    prw_ref[0] = p[:, 2 * CONV_CH:]


def _in_proj(x, sh, sc, g, w_bf16, tm=256):
    bsz, s, _ = x.shape
    vec = pl.BlockSpec((1, 1, D_MODEL), lambda b, i: (b, 0, 0))
    return pl.pallas_call(
        _in_proj_kernel,
        grid=(bsz, s // tm),
        in_specs=[pl.BlockSpec((1, tm, D_MODEL), lambda b, i: (b, i, 0)), vec, vec,
                  pl.BlockSpec((1, D_MODEL), lambda b, i: (0, 0)),
                  pl.BlockSpec((D_MODEL, IN_PROJ), lambda b, i: (0, 0))],
        out_specs=[pl.BlockSpec((1, tm, CONV_CH), lambda b, i: (b, i, 0)),
                   pl.BlockSpec((1, tm, RWKV_PROJ), lambda b, i: (b, i, 0))],
        out_shape=[jax.ShapeDtypeStruct((bsz, s, CONV_CH), F32),
                   jax.ShapeDtypeStruct((bsz, s, RWKV_PROJ), F32)],
        compiler_params=_cparams(("parallel", "parallel")),
    )(x, sh, sc, g.reshape(1, D_MODEL), w_bf16)


CONV_HALO = 32
CONV_ROWS = 64


def _conv_kernel(cur_ref, prev_ref, w_ref, b_ref, lnw_ref, lnb_ref, o_ref, pad_ref):
    tc = cur_ref.shape[1]
    pad_ref[0, 0:CONV_HALO, :] = jnp.where(pl.program_id(1) > 0, prev_ref[0], 0.0)
    pad_ref[0, CONV_HALO:CONV_HALO + tc, :] = cur_ref[0]
    span = CONV_HALO + tc - 8
    for r in range(1, 8):
        pad_ref[r, 0:span, :] = pad_ref[0, r:r + span, :]
    off = CONV_HALO - (CONV_WIDTH - 1)
    for r0 in range(0, tc, CONV_ROWS):
        acc = jnp.zeros((CONV_ROWS, CONV_CH), F32)
        for j in range(CONV_WIDTH):
            q, r = divmod(off + j, 8)
            acc = acc + w_ref[j:j + 1, :] * pad_ref[r, r0 + 8 * q:r0 + 8 * q + CONV_ROWS, :]
        y = acc + b_ref[...]
        mu = jnp.mean(y, axis=-1, keepdims=True)
        yc = y - mu
        var = jnp.mean(yc * yc, axis=-1, keepdims=True)
        yn = yc * lax.rsqrt(var + LN_EPS) * lnw_ref[...] + lnb_ref[...]
        o_ref[0, r0:r0 + CONV_ROWS, :] = (yn * jax.nn.sigmoid(yn)).astype(o_ref.dtype)


def _conv(yglu, w, b, lnw, lnb, tc=256):
    bsz, s, _ = yglu.shape
    hb = tc // CONV_HALO
    row = lambda a: a.reshape(1, CONV_CH)
    const = lambda shp: pl.BlockSpec(shp, lambda bb, i: (0, 0))
    return pl.pallas_call(
        _conv_kernel,
        grid=(bsz, s // tc),
        in_specs=[pl.BlockSpec((1, tc, CONV_CH), lambda bb, i: (bb, i, 0)),
                  pl.BlockSpec((1, CONV_HALO, CONV_CH), lambda bb, i: (bb, jnp.maximum(i * hb - 1, 0), 0)),
                  const((CONV_WIDTH, CONV_CH)), const((1, CONV_CH)), const((1, CONV_CH)), const((1, CONV_CH))],
        out_specs=pl.BlockSpec((1, tc, CONV_CH), lambda bb, i: (bb, i, 0)),
        out_shape=jax.ShapeDtypeStruct((bsz, s, CONV_CH), BF16),
        scratch_shapes=[pltpu.VMEM((8, CONV_HALO + tc, CONV_CH), F32)],
        compiler_params=_cparams(("parallel", "parallel")),
    )(yglu, yglu, w, row(b), row(lnw), row(lnb))


def _softplus(z):
    return jnp.maximum(z, 0.0) + jnp.log1p(jnp.exp(-jnp.abs(z)))


def _rwkv_features(cur, prow, mu, w0, w2p, a0, a2p, g2, k_k, k_a, r_k, bd):
    rows = lax.broadcasted_iota(I32, cur.shape, 0)
    prev = jnp.where(rows == 0, prow, pltpu.roll(cur, 1, axis=0))
    xs = cur + mu * (prev - cur)
    r = xs[:, 0:RWKV_CH]
    k = xs[:, RWKV_CH:2 * RWKV_CH]
    v = xs[:, 2 * RWKV_CH:3 * RWKV_CH]
    wa = xs[:, 3 * RWKV_CH:3 * RWKV_CH + LORA_W + LORA_A]
    gl = xs[:, 3 * RWKV_CH + LORA_W + LORA_A:]
    w = -_softplus(-(w0 + _mm(jnp.tanh(wa), w2p, 3))) - 0.5
    a = jax.nn.sigmoid(a0 + _mm(wa, a2p, 3))
    g = _mm(jax.nn.sigmoid(gl), g2, 3)
    kk = k * k_k
    kkn = kk / jnp.maximum(jnp.sqrt(_head_sums(kk * kk, bd)), 1e-12)
    k2 = k * (1.0 + (a - 1.0) * k_a)
    bonus = _head_sums(r * k2 * r_k, bd) * v
    return r, k2, v, -jnp.exp(w), -kkn, kkn * a, g, bonus


RWKV_PASSES = {"gram": 3, "lakv": 3, "solve": 3, "out": 3, "state": 3}
SOLVE_BLK = 16


def _expand(x, lane_lo):
    return jnp.concatenate([jnp.where(lane_lo, x, 0.0), jnp.where(lane_lo, 0.0, x)], axis=0)


def _rwkv_kernel(p_ref, mu_ref, w0_ref, w2_ref, a0_ref, a2_ref, g2_ref, kk_ref, ka_ref, rk_ref, bd_ref,
                 bdm_ref, gnw_ref, gnb_ref, y_ref, s_ref, last_ref):
    C = CHUNK_T
    H2 = 2 * C

    @pl.when(pl.program_id(1) == 0)
    def _():
        s_ref[...] = jnp.zeros_like(s_ref)
        last_ref[...] = jnp.zeros_like(last_ref)

    cur = p_ref[0]
    r, k, vv, lw, a, b, gate, bonus = _rwkv_features(
        cur, last_ref[...], mu_ref[...], w0_ref[...], w2_ref[...], a0_ref[...], a2_ref[...], g2_ref[...],
        kk_ref[...], ka_ref[...], rk_ref[...], bd_ref[...])
    last_ref[...] = cur[C - 1:C, :]

    tri = (lax.broadcasted_iota(I32, (C, C), 0) >= lax.broadcasted_iota(I32, (C, C), 1)).astype(F32)
    cum = _dot(tri, lw, HI)
    tot = cum[C - 1:C, :]
    e_pos = jnp.exp(cum)
    e_neg = jnp.exp(-cum)
    e_rem = jnp.exp(tot - cum)
    rt = r * e_pos
    at = a * jnp.exp(cum - lw)
    kt = k * e_neg
    bt = b * e_neg
    kp = k * e_rem
    bp = b * e_rem
    pc = jnp.exp(tot)

    lane_lo = lax.broadcasted_iota(I32, (C, LANES), 1) < HEAD
    tt = lax.broadcasted_iota(I32, (H2, H2), 0) % C
    ss = lax.broadcasted_iota(I32, (H2, H2), 1) % C
    strict = tt > ss
    incl = tt >= ss
    near = tt // SOLVE_BLK == ss // SOLVE_BLK
    eye = lax.broadcasted_iota(I32, (LANES, LANES), 0) == lax.broadcasted_iota(I32, (LANES, LANES), 1)

    P = RWKV_PASSES
    pairs = range(RWKV_CH // LANES)
    each = lambda fn, *lists: [fn(*args) for args in zip(*lists)]
    sls = [slice(hp * LANES, (hp + 1) * LANES) for hp in pairs]
    ax, rx, bx, kx, vx, bpx, kpx = ([_expand(t[:, sl], lane_lo) for sl in sls] for t in (at, rt, bt, kt, vv, bp, kp))
    gram = each(lambda a_, r_, b_, k_: _mm(jnp.concatenate([a_, r_], axis=0), jnp.concatenate([b_, k_], axis=0),
                                           P["gram"], _dot_nt), ax, rx, bx, kx)
    l_ab = [jnp.where(strict, g_[:H2, :H2], 0.0) for g_ in gram]
    l_ak = [jnp.where(strict, g_[:H2, H2:], 0.0) for g_ in gram]
    m_r = [jnp.concatenate([jnp.where(incl, g_[H2:, :H2], 0.0), jnp.where(incl, g_[H2:, H2:], 0.0)], axis=1)
           for g_ in gram]
    dg = [jnp.where(near, l_, 0.0) for l_ in l_ab]
    lakv = each(lambda l_, v_: _mm(l_, v_, P["lakv"]), l_ak, vx)
    xf = each(lambda a_, lv_, l_, d_: jnp.concatenate([a_, lv_, l_ - d_], axis=1), ax, lakv, l_ab, dg)
    n_sq = SOLVE_BLK.bit_length() - 1
    for it in range(n_sq):
        xf = each(lambda x_, d_: x_ + _mm(d_, x_, P["solve"]), xf, dg)
        if it + 1 < n_sq:
            dg = [_mm(d_, d_, P["solve"]) for d_ in dg]
    x = [x_[:, :2 * LANES] for x_ in xf]
    f = [x_[:, 2 * LANES:] for x_ in xf]
    n_sq = (C // SOLVE_BLK).bit_length() - 1
    for it in range(n_sq):
        x = each(lambda x_, f_: x_ + _mm(f_, x_, P["solve"]), x, f)
        if it + 1 < n_sq:
            f = [_mm(f_, f_, P["solve"]) for f_ in f]
    zero = jnp.zeros((H2, LANES), F32)
    z = each(lambda x_, v_: jnp.concatenate([x_, jnp.concatenate([zero, v_], axis=1)], axis=0), x, vx)
    w1 = each(lambda m_, z_: _mm(m_, z_, P["out"]), m_r, z)
    w2 = each(lambda b_, k_, z_: _mm(jnp.concatenate([b_, k_], axis=0), z_, P["out"], _dot_tn), bpx, kpx, z)
    ys = []
    for hp in pairs:
        ra = rx[hp] + w1[hp][:, :LANES]
        ra = ra[:C] + ra[C:]
        y0 = w1[hp][:C, LANES:] + w1[hp][C:, LANES:]
        mt = w2[hp][:, :LANES] + jnp.where(eye, pc[:, sls[hp]], 0.0)
        s0 = s_ref[hp]
        ys.append(_mm(ra, s0, P["state"]) + y0)
        s_ref[hp] = _mm(mt, s0, P["state"]) + w2[hp][:, LANES:]
    yc = [y_ - _mm_bf16_rhs(y_, bdm_ref[...]) for y_ in ys]
    ys = [c_ * lax.rsqrt(_mm_bf16_rhs(c_ * c_, bdm_ref[...]) + GN_EPS) for c_ in yc]

    yn = jnp.concatenate(ys, axis=1) * gnw_ref[...] + gnb_ref[...]
    y_ref[0] = ((yn + bonus) * gate).astype(y_ref.dtype)


def _rwkv(prw, mu, w0, w2p, a0, a2p, g2, k_k, k_a, r_k, gnw, gnb):
    bsz, s, _ = prw.shape
    row = lambda t: t.reshape(1, -1)
    const = lambda shp: pl.BlockSpec(shp, lambda bb, c: (0, 0))
    vec = const((1, RWKV_CH))
    lora = const((LORA_W + LORA_A, RWKV_CH))
    return pl.pallas_call(
        _rwkv_kernel,
        grid=(bsz, s // CHUNK_T),
        in_specs=[pl.BlockSpec((1, CHUNK_T, RWKV_PROJ), lambda bb, c: (bb, c, 0)), const((1, RWKV_PROJ)),
                  vec, lora, vec, lora, const((LORA_G, RWKV_CH)), vec, vec, vec,
                  const((LANES, LANES)), const((LANES, LANES)), vec, vec],
        out_specs=pl.BlockSpec((1, CHUNK_T, RWKV_CH), lambda bb, c: (bb, c, 0)),
        out_shape=jax.ShapeDtypeStruct((bsz, s, RWKV_CH), BF16),
        scratch_shapes=[pltpu.VMEM((RWKV_CH // LANES, LANES, LANES), F32), pltpu.VMEM((1, RWKV_PROJ), F32)],
        compiler_params=_cparams(("parallel", "arbitrary")),
        name="rwkv",
    )(prw, row(mu), row(w0), w2p, row(a0), a2p, g2, row(k_k), row(k_a), row(r_k),
      _block_diag(LANES, HEAD, 1.0).astype(BF16), _block_diag(LANES, HEAD, 1.0 / HEAD).astype(BF16),
      row(gnw), row(gnb))


def _out_proj_kernel(yc_ref, yr_ref, wo_ref, x_ref, gt_ref, g_ref, sh_ref, sc_ref, x1_o, u2_o):
    mix = _dot(yc_ref[0], wo_ref[0:CONV_CH, :]) + _dot(yr_ref[0], wo_ref[CONV_CH:, :])
    x1 = x_ref[0] + gt_ref[0] * mix
    x1_o[0] = x1
    u2 = _rms(x1, g_ref[...]) * (1.0 + sc_ref[0]) + sh_ref[0]
    for c in range(CHUNKS):
        u2_o[0, :, c, :] = u2[:, c * LANES:(c + 1) * LANES]


def _out_proj(yc, yr, wo_bf16, x, gt, g, sh, sc, tm=256):
    bsz, s, _ = x.shape
    vec = pl.BlockSpec((1, 1, D_MODEL), lambda b, i: (b, 0, 0))
    tile = lambda w: pl.BlockSpec((1, tm, w), lambda b, i: (b, i, 0))
    return pl.pallas_call(
        _out_proj_kernel,
        grid=(bsz, s // tm),
        in_specs=[tile(CONV_CH), tile(RWKV_CH), pl.BlockSpec((D_MODEL, D_MODEL), lambda b, i: (0, 0)),
                  tile(D_MODEL), vec, pl.BlockSpec((1, D_MODEL), lambda b, i: (0, 0)), vec, vec],
        out_specs=[tile(D_MODEL), pl.BlockSpec((1, tm, CHUNKS, LANES), lambda b, i: (b, i, 0, 0))],
        out_shape=[jax.ShapeDtypeStruct((bsz, s, D_MODEL), F32),
                   jax.ShapeDtypeStruct((bsz, s, CHUNKS, LANES), F32)],
        compiler_params=_cparams(("parallel", "parallel")),
        name="out_proj",
    )(yc, yr, wo_bf16, x, gt, g.reshape(1, D_MODEL), sh, sc)


HEADS_STEP = 4


def _topk_rows(ss, k):
    n, t = ss[0].shape
    rids = [lax.broadcasted_iota(I32, (8, t), 0) + r0 for r0 in range(0, n, 8)]
    ss = [[s[r0:r0 + 8] for r0 in range(0, n, 8)] for s in ss]
    vals, ids = [[] for _ in ss], [[] for _ in ss]
    for _ in range(k):
        tops = [_max_with_tag(s, [rids]) for s in ss]
        ss = [[jnp.where(r == j, -jnp.inf, c) for c, r in zip(s, rids)] for s, (_, (j,)) in zip(ss, tops)]
        for v, i, (m, (j,)) in zip(vals, ids, tops):
            v.append(m)
            i.append(j)
    return [(jnp.concatenate(v, axis=0), jnp.concatenate(i, axis=0)) for v, i in zip(vals, ids)]


def _max_with_tag(chunks, tags):
    vals, tags = list(chunks), [list(tg) for tg in tags]
    while len(vals) > 1:
        nv, nt = [], [[] for _ in tags]
        for a in range(0, len(vals) - 1, 2):
            first = vals[a] >= vals[a + 1]
            nv.append(jnp.maximum(vals[a], vals[a + 1]))
            for dst, tg in zip(nt, tags):
                dst.append(jnp.where(first, tg[a], tg[a + 1]))
        if len(vals) % 2:
            nv.append(vals[-1])
            for dst, tg in zip(nt, tags):
                dst.append(tg[-1])
        vals, tags = nv, nt
    v8 = vals[0]
    m = jnp.max(v8, axis=0, keepdims=True)
    big = jnp.iinfo(jnp.int32).max
    key = jnp.min(jnp.where(v8 == m, tags[0][0], big), axis=0, keepdims=True)
    out = [key]
    for tg in tags[1:]:
        out.append(jnp.sum(jnp.where(tags[0][0] == key, tg[0], 0), axis=0, keepdims=True))
    return m, out


def _route_kernel(u_ref, wq_ref, keys_ref, idx_o, gate_o, q_ref, idx_s, gate_s):
    tm = u_ref.shape[0]
    u = jnp.concatenate([u_ref[:, c, :] for c in range(CHUNKS)], axis=1)
    q = _dot(u.astype(BF16), wq_ref[...])
    for j in range(2 * PEER_HEADS):
        q_ref[j] = q[:, j * LANES:(j + 1) * LANES]
    K = PEER_TOPK
    tt = LANES
    tiles = range(tm // tt)
    row8 = lax.broadcasted_iota(I32, (8, tt), 0)

    def heads(i, carry):
        probs = [(i * HEADS_STEP + dh, lt) for dh in range(HEADS_STEP) for lt in tiles]
        scores = [_mm(keys_ref[2 * h + p], q_ref[2 * h + p, lt * tt:(lt + 1) * tt, :], 3, _dot_nt)
                  for h, lt in probs for p in range(2)]
        tops = _topk_rows(scores, K)
        ss, flats, eids = [], [], []
        for n_ in range(len(probs)):
            (av, ai), (bv, bi) = tops[2 * n_], tops[2 * n_ + 1]
            cs = [av[0:1] + bv[0:8], av[0:1] + bv[8:16]]
            cf = [row8, row8 + 8]
            ce = [ai[0:1] * PEER_NKEYS + bi[0:8], ai[0:1] * PEER_NKEYS + bi[8:16]]
            for x in range(1, 8):
                cs.append(jnp.where(row8 < K // (x + 1), av[x:x + 1] + bv[0:8], -jnp.inf))
                cf.append(x * K + row8)
                ce.append(ai[x:x + 1] * PEER_NKEYS + bi[0:8])
            cs.append(av[8:16] + bv[0:1])
            cf.append((row8 + 8) * K)
            ce.append(ai[8:16] * PEER_NKEYS + bi[0:1])
            ss.append(cs)
            flats.append(cf)
            eids.append(ce)
        best, experts = [[] for _ in probs], [[] for _ in probs]
        for _ in range(K):
            tops2 = [_max_with_tag(s, [fl, ei]) for s, fl, ei in zip(ss, flats, eids)]
            for n_, (m, (f, e)) in enumerate(tops2):
                best[n_].append(m)
                experts[n_].append(e)
            ss = [[jnp.where(fc == f, -jnp.inf, c) for c, fc in zip(s, fl)]
                  for s, fl, (_, (f, _e)) in zip(ss, flats, tops2)]
        for n_, (h, lt) in enumerate(probs):
            b = jnp.concatenate(best[n_], axis=0)
            e = jnp.exp(b - b[0:1])
            gate_s[lt, h] = e / jnp.sum(e, axis=0, keepdims=True)
            idx_s[lt, h] = jnp.concatenate(experts[n_], axis=0) * WORD_ROWS
        return carry

    lax.fori_loop(0, PEER_HEADS // HEADS_STEP, heads, 0)
    for lt in tiles:
        idx_o[lt * tt:(lt + 1) * tt, :] = jnp.transpose(idx_s[lt].reshape(NPAIR, tt))
        gate_o[lt * tt:(lt + 1) * tt, :] = jnp.transpose(gate_s[lt].reshape(NPAIR, tt))


def _route(u2, wq_bf16, keys, tm=256):
    n = u2.shape[0]
    oblk = pl.BlockSpec((tm, NPAIR), lambda i: (i, 0))
    return pl.pallas_call(
        _route_kernel,
        grid=(n // tm,),
        in_specs=[pl.BlockSpec((tm, CHUNKS, LANES), lambda i: (i, 0, 0)),
                  pl.BlockSpec((D_MODEL, PEER_HEADS * PEER_DQ), lambda i: (0, 0)),
                  pl.BlockSpec((2 * PEER_HEADS, PEER_NKEYS, PEER_DQ // 2), lambda i: (0, 0, 0))],
        out_specs=[oblk, oblk],
        out_shape=[jax.ShapeDtypeStruct((n, NPAIR), I32), jax.ShapeDtypeStruct((n, NPAIR), F32)],
        scratch_shapes=[pltpu.VMEM((2 * PEER_HEADS, tm, LANES), F32),
                        pltpu.VMEM((tm // LANES, PEER_HEADS, PEER_TOPK, LANES), I32),
                        pltpu.VMEM((tm // LANES, PEER_HEADS, PEER_TOPK, LANES), F32)],
        compiler_params=_cparams(("parallel",)),
        name="route",
    )(u2, wq_bf16, keys)


TOK_UNROLL = 16


def _pack_kernel(t_ref, o_ref):
    te = t_ref.shape[0]
    bits = lambda v: lax.bitcast_convert_type(v.astype(BF16).astype(F32), I32)
    for s in range(WORD_ROWS):
        lo = bits(t_ref[:, (2 * s) * LANES:(2 * s + 1) * LANES])
        hi = bits(t_ref[:, (2 * s + 1) * LANES:(2 * s + 2) * LANES])
        o_ref[pl.ds(s, te, stride=WORD_ROWS), :] = (hi & -65536) | lax.shift_right_logical(lo, 16)


def _pack_table(t, te=512):
    e = t.shape[0]
    return pl.pallas_call(
        _pack_kernel,
        grid=(e // te,),
        in_specs=[pl.BlockSpec((te, D_MODEL), lambda i: (i, 0))],
        out_specs=pl.BlockSpec((te * WORD_ROWS, LANES), lambda i: (i, 0)),
        out_shape=jax.ShapeDtypeStruct((e * WORD_ROWS, LANES), I32),
        compiler_params=_cparams(("parallel",)),
        name="pack_table",
    )(t)


def _gather_rows(idx_ref, t0, tab_ref, g_ref):
    for u in range(TOK_UNROLL):
        idx_row = idx_ref.at[t0 + u]
        for kk in range(NPAIR):
            row = pl.multiple_of(idx_row[kk], WORD_ROWS)
            g_ref[u, kk * WORD_ROWS:(kk + 1) * WORD_ROWS, :] = tab_ref[pl.ds(row, WORD_ROWS), :]


def _peer_u_kernel(idx_ref, x_ref, gate_ref, keep_ref, sel_ref, tab_ref, w_o, g_ref, p_ref):
    tp = x_ref.shape[0]

    def tokens(i, carry):
        t0 = pl.multiple_of(i * TOK_UNROLL, TOK_UNROLL)
        _gather_rows(idx_ref, t0, tab_ref, g_ref)
        xs = x_ref[pl.ds(t0, TOK_UNROLL)]
        parts = []
        for u in range(TOK_UNROLL):
            g = pltpu.bitcast(g_ref[u], BF16)
            xc = xs[u]
            xt = jnp.concatenate([xc] * (LANES // CHUNKS), axis=0).astype(BF16)
            r = _dot_nt(g, xt) * keep_ref[...]
            parts.append(jnp.sum(r.reshape(NPAIR // 16, 16 * CHUNKS, LANES), axis=1))
        p_ref[pl.ds(pl.multiple_of(t0 * 8, 8 * TOK_UNROLL), 8 * TOK_UNROLL), :] = jnp.concatenate(parts, axis=0)
        return carry

    lax.fori_loop(0, tp // TOK_UNROLL, tokens, 0)
    z = _mm_bf16_rhs(p_ref[...], sel_ref[...], 3)
    grp = lax.broadcasted_iota(I32, z.shape, 0) % 8 == lax.broadcasted_iota(I32, z.shape, 1) // 16
    h = jnp.sum(jnp.where(grp, z, 0.0).reshape(tp, 8, LANES), axis=1)
    w_o[...] = gate_ref[...] * (0.5 * h * (1.0 + lax.erf(h * (2.0 ** -0.5))))


def _peer_u(idx, x3, gate, tab, start, n, tp=128):
    o = start // tp
    rows = jnp.arange(NPAIR * CHUNKS)[:, None]
    lanes = jnp.arange(LANES)[None, :]
    keep = ((lanes % CHUNKS == rows % CHUNKS) & (lanes // CHUNKS == (rows // CHUNKS) % 16)).astype(F32)
    sel = (jnp.arange(LANES)[:, None] // CHUNKS == jnp.arange(LANES)[None, :] % 16).astype(BF16)
    return pl.pallas_call(
        _peer_u_kernel,
        grid=(n // tp,),
        in_specs=[pl.BlockSpec((tp, NPAIR), lambda i: (i + o, 0), memory_space=pltpu.SMEM),
                  pl.BlockSpec((tp, CHUNKS, LANES), lambda i: (i + o, 0, 0)),
                  pl.BlockSpec((tp, NPAIR), lambda i: (i + o, 0)),
                  pl.BlockSpec((NPAIR * CHUNKS, LANES), lambda i: (0, 0)),
                  pl.BlockSpec((LANES, LANES), lambda i: (0, 0)),
                  pl.BlockSpec(memory_space=pltpu.VMEM)],
        out_specs=pl.BlockSpec((tp, NPAIR), lambda i: (i, 0)),
        out_shape=jax.ShapeDtypeStruct((n, NPAIR), F32),
        scratch_shapes=[pltpu.VMEM((TOK_UNROLL, NPAIR * WORD_ROWS, LANES), I32),
                        pltpu.VMEM((tp * 8, LANES), F32)],
        compiler_params=_cparams(("arbitrary",)),
        name="peer_u",
    )(idx, x3, gate, keep, sel, tab)


def _peer_v_kernel(idx_ref, w_ref, rep_ref, diag_ref, tab_ref, o_ref, g_ref, wx_ref):
    tp = w_ref.shape[0]
    wx_ref[...] = _mm_bf16_rhs(w_ref[...], rep_ref[...], 3)

    def tokens(i, carry):
        t0 = pl.multiple_of(i * TOK_UNROLL, TOK_UNROLL)
        _gather_rows(idx_ref, t0, tab_ref, g_ref)
        wx = wx_ref[pl.ds(t0, TOK_UNROLL), :]
        outs = []
        for u in range(TOK_UNROLL):
            g = pltpu.bitcast(g_ref[u], BF16)
            wm = (wx[u:u + 1, :] * diag_ref[...]).astype(BF16)
            outs.append(_dot(wm, g))
        o_ref[pl.ds(t0, TOK_UNROLL)] = jnp.stack(outs, axis=0)
        return carry

    lax.fori_loop(0, tp // TOK_UNROLL, tokens, 0)


def _peer_v(idx, w, tab, start, tp=128):
    n = w.shape[0]
    o = start // tp
    rep = jnp.repeat(jnp.eye(NPAIR, dtype=BF16), CHUNKS, axis=1)
    diag = (jnp.arange(CHUNKS)[:, None] == jnp.arange(NPAIR * CHUNKS)[None, :] % CHUNKS).astype(F32)
    return pl.pallas_call(
        _peer_v_kernel,
        grid=(n // tp,),
        in_specs=[pl.BlockSpec((tp, NPAIR), lambda i: (i + o, 0), memory_space=pltpu.SMEM),
                  pl.BlockSpec((tp, NPAIR), lambda i: (i, 0)),
                  pl.BlockSpec((NPAIR, NPAIR * CHUNKS), lambda i: (0, 0)),
                  pl.BlockSpec((CHUNKS, NPAIR * CHUNKS), lambda i: (0, 0)),
                  pl.BlockSpec(memory_space=pltpu.VMEM)],
        out_specs=pl.BlockSpec((tp, CHUNKS, LANES), lambda i: (i, 0, 0)),
        out_shape=jax.ShapeDtypeStruct((n, CHUNKS, LANES), F32),
        scratch_shapes=[pltpu.VMEM((TOK_UNROLL, NPAIR * WORD_ROWS, LANES), I32),
                        pltpu.VMEM((tp, NPAIR * CHUNKS), F32)],
        compiler_params=_cparams(("arbitrary",)),
        name="peer_v",
    )(idx, w, rep, diag, tab)


SC_LANES = 16
SC_WORKERS = 32
SC_ROWS = 32
SC_TOKENS = 16
SC_SHARE_NUM, SC_SHARE_DEN = 31, 64
SC_COLS = 16


def _peer_v_sc(ids, wts, table):
    m = wts.shape[0]
    per_worker = m // SC_WORKERS
    blocks = NPAIR // SC_ROWS
    ids_b = ids.reshape(ids.shape[0] * blocks, SC_ROWS)
    wts_f = wts.reshape(m * NPAIR)
    mesh = plsc.VectorSubcoreMesh(core_axis_name="c", subcore_axis_name="s")

    @functools.partial(
        pl.kernel, mesh=mesh, out_type=jax.ShapeDtypeStruct((m, D_MODEL), F32),
        scratch_types=[pltpu.VMEM((SC_TOKENS * blocks, SC_ROWS), I32), pltpu.VMEM((SC_TOKENS * NPAIR,), F32),
                       pltpu.VMEM((2, SC_ROWS, D_MODEL), F32), pltpu.VMEM((D_MODEL,), F32),
                       pltpu.SemaphoreType.DMA((2,))],
        compiler_params=pltpu.CompilerParams(needs_layout_passes=False),
        name="peer_v_sc")
    def run(tab_hbm, ids_hbm, w_hbm, out_hbm, ids_v, w_v, rows_v, acc_v, sems):
        wid = lax.axis_index("s") * 2 + lax.axis_index("c")
        base = wid * per_worker
        zero = jnp.zeros((SC_LANES,), F32)

        def gather(u, b):
            return pltpu.make_async_copy(tab_hbm.at[ids_v.at[u * blocks + b]], rows_v.at[b % 2], sems.at[b % 2])

        @pl.loop(0, per_worker // SC_TOKENS)
        def _(tb):
            t0 = base + tb * SC_TOKENS
            pltpu.sync_copy(ids_hbm.at[pl.ds(t0 * blocks, SC_TOKENS * blocks)], ids_v)
            pltpu.sync_copy(w_hbm.at[pl.ds(t0 * NPAIR, SC_TOKENS * NPAIR)], w_v)

            @pl.loop(0, SC_TOKENS)
            def _(u):
                for j in range(D_MODEL // SC_LANES):
                    acc_v[pl.ds(j * SC_LANES, SC_LANES)] = zero
                gather(u, 0).start()
                for b in range(blocks):
                    if b + 1 < blocks:
                        gather(u, b + 1).start()
                    gather(u, b).wait()

                    for c0 in range(0, D_MODEL // SC_LANES, SC_COLS):
                        def row(r, accs, b=b, c0=c0):
                            wk = plsc.load_gather(w_v, [jnp.full((SC_LANES,), u * NPAIR + b * SC_ROWS + r, I32)])
                            return tuple(a + wk * rows_v[b % 2, r, pl.ds((c0 + j) * SC_LANES, SC_LANES)]
                                         for j, a in enumerate(accs))
                        accs = lax.fori_loop(0, SC_ROWS, row, (zero,) * SC_COLS)
                        for j, a in enumerate(accs):
                            plsc.addupdate(acc_v.at[pl.ds((c0 + j) * SC_LANES, SC_LANES)], a)
                pltpu.sync_copy(acc_v, out_hbm.at[t0 + u])

    return run(table, ids_b, wts_f)


def _final_kernel(x1_ref, psc_ref, ptc_ref, gt_ref, g_ref, o_ref, *, sc_tiles):
    p_tc = jnp.concatenate([ptc_ref[:, c, :] for c in range(CHUNKS)], axis=1)
    p = jnp.where(pl.program_id(0) < sc_tiles, psc_ref[...], p_tc)
    o_ref[...] = _rms(x1_ref[...] + gt_ref[0] * p, g_ref[...])


def _final(x1, peer_sc, peer_tc, gt, g, tm=512):
    n = x1.shape[0]
    per_seq = n // gt.shape[0] // tm
    k = peer_sc.shape[0] // tm
    tile = pl.BlockSpec((tm, D_MODEL), lambda i: (i, 0))
    return pl.pallas_call(
        functools.partial(_final_kernel, sc_tiles=k),
        grid=(n // tm,),
        in_specs=[tile, pl.BlockSpec((tm, D_MODEL), lambda i: (jnp.minimum(i, k - 1), 0)),
                  pl.BlockSpec((tm, CHUNKS, LANES), lambda i: (jnp.maximum(i - k, 0), 0, 0)),
                  pl.BlockSpec((1, 1, D_MODEL), lambda i: (i // per_seq, 0, 0)),
                  pl.BlockSpec((1, D_MODEL), lambda i: (0, 0))],
        out_specs=tile,
        out_shape=jax.ShapeDtypeStruct((n, D_MODEL), F32),
        compiler_params=_cparams(("parallel",)),
        name="final",
    )(x1, peer_sc, peer_tc, gt, g.reshape(1, D_MODEL))


def _block_diag(width, group, value):
    i = jnp.arange(width) // group
    return jnp.where(i[:, None] == i[None, :], value, 0.0).astype(F32)


def _layer(x, mod, final_g, norm_mix_g, w_in, conv_dw_w, conv_dw_b, conv_ln_w, conv_ln_b, rwkv_mu, rwkv_w0, rwkv_w2,
           rwkv_a0, rwkv_a2, rwkv_g2, rwkv_k_k, rwkv_k_a, rwkv_r_k, rwkv_gn_w, rwkv_gn_b, w_out, norm_ffn_g,
           peer_w_q, peer_sub_keys, peer_u, peer_v):
    bsz, s, _ = x.shape
    sh_mix, sc_mix, gt_mix, sh_ffn, sc_ffn, gt_ffn = (
        mod[:, i * D_MODEL:(i + 1) * D_MODEL].reshape(bsz, 1, D_MODEL) for i in range(6))

    yglu, prw = _in_proj(x, sh_mix, sc_mix, norm_mix_g, w_in.astype(BF16))
    y_conv = _conv(yglu, conv_dw_w, conv_dw_b, conv_ln_w, conv_ln_b)

    zpad = jnp.zeros((LORA_W, RWKV_CH), F32)
    y_rwkv = _rwkv(prw, rwkv_mu, rwkv_w0, jnp.concatenate([rwkv_w2, zpad], axis=0), rwkv_a0,
                   jnp.concatenate([zpad, rwkv_a2], axis=0), rwkv_g2, rwkv_k_k, rwkv_k_a, rwkv_r_k.reshape(-1),
                   rwkv_gn_w, rwkv_gn_b)

    x1, u2 = _out_proj(y_conv, y_rwkv, w_out.astype(BF16), x, gt_mix, norm_ffn_g, sh_ffn, sc_ffn)

    n = bsz * s
    u3 = u2.reshape(n, CHUNKS, LANES)
    keys = peer_sub_keys.reshape(2 * PEER_HEADS, PEER_NKEYS, PEER_DQ // 2)
    idx, gate = _route(u3, peer_w_q.astype(BF16), keys)
    n_sc = n * SC_SHARE_NUM // SC_SHARE_DEN
    tab_u = _pack_table(peer_u)
    wts_a = _peer_u(idx, u3, gate, tab_u, 0, n_sc)
    peer_sc = _peer_v_sc(lax.shift_right_logical(idx, 2), wts_a, peer_v)
    wts_b = _peer_u(idx, u3, gate, tab_u, n_sc, n - n_sc)
    peer_tc = _peer_v(idx, wts_b, _pack_table(peer_v), n_sc)
    return _final(x1.reshape(n, D_MODEL), peer_sc, peer_tc, gt_ffn, final_g).reshape(bsz, s, D_MODEL)


def kernel(x, c, ada_w, ada_b, norm_mix_g, w_in, conv_dw_w, conv_dw_b, conv_ln_w, conv_ln_b, rwkv_mu, rwkv_w0,
           rwkv_w2, rwkv_a0, rwkv_a2, rwkv_g2, rwkv_k_k, rwkv_k_a, rwkv_r_k, rwkv_gn_w, rwkv_gn_b, w_out,
           norm_ffn_g, peer_w_q, peer_sub_keys, peer_u, peer_v, final_g):
    depth = ada_w.shape[0]
    assert depth == 1, "one layer: the final norm is fused into the last layer's residual"
    mod = _mod(c, ada_w[0], ada_b[0])
    return _layer(x, mod, final_g, norm_mix_g[0], w_in[0], conv_dw_w[0], conv_dw_b[0], conv_ln_w[0],
                        conv_ln_b[0], rwkv_mu[0], rwkv_w0[0], rwkv_w2[0], rwkv_a0[0], rwkv_a2[0], rwkv_g2[0],
                        rwkv_k_k[0], rwkv_k_a[0], rwkv_r_k[0], rwkv_gn_w[0], rwkv_gn_b[0], w_out[0],
                        norm_ffn_g[0], peer_w_q[0], peer_sub_keys[0], peer_u[0], peer_v[0])
```

```python
import functools

import jax
import jax.numpy as jnp
from jax import lax
from jax.experimental import pallas as pl
from jax.experimental.pallas import tpu as pltpu
from jax.experimental.pallas import tpu_sc as plsc

F32 = jnp.float32
BF16 = jnp.bfloat16
I32 = jnp.int32
HI = lax.Precision.HIGHEST

D_MODEL = 1024
CONV_CH = 512
RWKV_CH = 512
HEAD = 64
CONV_WIDTH = 31
LORA_W = 64
LORA_A = 64
LORA_G = 128
RWKV_PROJ = 3 * RWKV_CH + LORA_W + LORA_A + LORA_G
IN_PROJ = 2 * CONV_CH + RWKV_PROJ
PEER_HEADS = 8
PEER_NKEYS = 128
PEER_EXPERTS = PEER_NKEYS * PEER_NKEYS
PEER_DQ = 256
PEER_TOPK = 16
NPAIR = PEER_HEADS * PEER_TOPK
RMS_EPS = 1e-6
LN_EPS = 1e-5
GN_EPS = 64e-5

LANES = 128
CHUNKS = D_MODEL // LANES
WORD_ROWS = CHUNKS // 2
CHUNK_T = 64
VMEM_LIMIT = 56 * 1024 * 1024


def _cparams(sem, vmem=None):
    return pltpu.CompilerParams(dimension_semantics=sem, vmem_limit_bytes=vmem or VMEM_LIMIT)


def _dot(a, b, precision=None):
    return jnp.dot(a, b, precision=precision, preferred_element_type=F32)


def _dot_nt(a, b, precision=None):
    return lax.dot_general(a, b, (((1,), (1,)), ((), ())), precision=precision, preferred_element_type=F32)


def _dot_tn(a, b, precision=None):
    return lax.dot_general(a, b, (((0,), (0,)), ((), ())), precision=precision, preferred_element_type=F32)


def _split(a):
    hi = a.astype(BF16)
    return hi, (a - hi.astype(F32)).astype(BF16)


def _mm(a, b, passes, dot=_dot):
    if passes == 6:
        return dot(a, b, HI)
    if passes == 1:
        return dot(a.astype(BF16), b.astype(BF16))
    ka = 0 if dot is _dot_tn else 1
    kb = 1 if dot is _dot_nt else 0
    ah, al = _split(a)
    bh, bl = _split(b)
    return dot(jnp.concatenate([ah, ah, al], axis=ka), jnp.concatenate([bh, bl, bh], axis=kb))


def _mm_bf16_rhs(a, b_bf16, terms=2):
    parts = []
    for _ in range(terms):
        p = a.astype(BF16)
        parts.append(p)
        a = a - p.astype(F32)
    return _dot(jnp.concatenate(parts, axis=1), jnp.concatenate([b_bf16] * terms, axis=0))


def _head_sums(x, bd_bf16):
    return jnp.concatenate([_mm_bf16_rhs(x[:, i * LANES:(i + 1) * LANES], bd_bf16, 3)
                            for i in range(x.shape[1] // LANES)], axis=1)


def _rms(x, g):
    return x * lax.rsqrt(jnp.mean(x * x, axis=-1, keepdims=True) + RMS_EPS) * g


def _mod_kernel(c_ref, w_ref, b_ref, o_ref):
    c = c_ref[...]
    o_ref[...] = _dot(c * jax.nn.sigmoid(c), w_ref[...], HI) + b_ref[...]


def _mod(c, w, b):
    bsz = c.shape[0]
    n = w.shape[1]
    tn = 1024
    return pl.pallas_call(
        _mod_kernel,
        grid=(n // tn,),
        in_specs=[pl.BlockSpec((bsz, D_MODEL), lambda j: (0, 0)),
                  pl.BlockSpec((D_MODEL, tn), lambda j: (0, j)),
                  pl.BlockSpec((1, tn), lambda j: (0, j))],
        out_specs=pl.BlockSpec((bsz, tn), lambda j: (0, j)),
        out_shape=jax.ShapeDtypeStruct((bsz, n), F32),
        compiler_params=_cparams(("parallel",)),
    )(c, w, b.reshape(1, n))


def _in_proj_kernel(x_ref, sh_ref, sc_ref, g_ref, w_ref, yglu_ref, prw_ref):
    u = _rms(x_ref[0], g_ref[...]) * (1.0 + sc_ref[0]) + sh_ref[0]
    p = _dot(u.astype(BF16), w_ref[...])
    yglu_ref[0] = p[:, :CONV_CH] * jax.nn.sigmoid(p[:, CONV_CH:2 * CONV_CH])
    prw_ref[0] = p[:, 2 * CONV_CH:]


def _in_proj(x, sh, sc, g, w_bf16, tm=256):
    bsz, s, _ = x.shape
    vec = pl.BlockSpec((1, 1, D_MODEL), lambda b, i: (b, 0, 0))
    return pl.pallas_call(
        _in_proj_kernel,
        grid=(bsz, s // tm),
        in_specs=[pl.BlockSpec((1, tm, D_MODEL), lambda b, i: (b, i, 0)), vec, vec,
                  pl.BlockSpec((1, D_MODEL), lambda b, i: (0, 0)),
                  pl.BlockSpec((D_MODEL, IN_PROJ), lambda b, i: (0, 0))],
        out_specs=[pl.BlockSpec((1, tm, CONV_CH), lambda b, i: (b, i, 0)),
                   pl.BlockSpec((1, tm, RWKV_PROJ), lambda b, i: (b, i, 0))],
        out_shape=[jax.ShapeDtypeStruct((bsz, s, CONV_CH), F32),
                   jax.ShapeDtypeStruct((bsz, s, RWKV_PROJ), F32)],
        compiler_params=_cparams(("parallel", "parallel")),
    )(x, sh, sc, g.reshape(1, D_MODEL), w_bf16)


CONV_HALO = 32
CONV_ROWS = 64


def _conv_kernel(cur_ref, prev_ref, w_ref, b_ref, lnw_ref, lnb_ref, o_ref, pad_ref):
    tc = cur_ref.shape[1]
    pad_ref[0, 0:CONV_HALO, :] = jnp.where(pl.program_id(1) > 0, prev_ref[0], 0.0)
    pad_ref[0, CONV_HALO:CONV_HALO + tc, :] = cur_ref[0]
    span = CONV_HALO + tc - 8
    for r in range(1, 8):
        pad_ref[r, 0:span, :] = pad_ref[0, r:r + span, :]
    off = CONV_HALO - (CONV_WIDTH - 1)
    for r0 in range(0, tc, CONV_ROWS):
        acc = jnp.zeros((CONV_ROWS, CONV_CH), F32)
        for j in range(CONV_WIDTH):
            q, r = divmod(off + j, 8)
            acc = acc + w_ref[j:j + 1, :] * pad_ref[r, r0 + 8 * q:r0 + 8 * q + CONV_ROWS, :]
        y = acc + b_ref[...]
        mu = jnp.mean(y, axis=-1, keepdims=True)
        yc = y - mu
        var = jnp.mean(yc * yc, axis=-1, keepdims=True)
        yn = yc * lax.rsqrt(var + LN_EPS) * lnw_ref[...] + lnb_ref[...]
        o_ref[0, r0:r0 + CONV_ROWS, :] = (yn * jax.nn.sigmoid(yn)).astype(o_ref.dtype)


def _conv(yglu, w, b, lnw, lnb, tc=256):
    bsz, s, _ = yglu.shape
    hb = tc // CONV_HALO
    row = lambda a: a.reshape(1, CONV_CH)
    const = lambda shp: pl.BlockSpec(shp, lambda bb, i: (0, 0))
    return pl.pallas_call(
        _conv_kernel,
        grid=(bsz, s // tc),
        in_specs=[pl.BlockSpec((1, tc, CONV_CH), lambda bb, i: (bb, i, 0)),
                  pl.BlockSpec((1, CONV_HALO, CONV_CH), lambda bb, i: (bb, jnp.maximum(i * hb - 1, 0), 0)),
                  const((CONV_WIDTH, CONV_CH)), const((1, CONV_CH)), const((1, CONV_CH)), const((1, CONV_CH))],
        out_specs=pl.BlockSpec((1, tc, CONV_CH), lambda bb, i: (bb, i, 0)),
        out_shape=jax.ShapeDtypeStruct((bsz, s, CONV_CH), BF16),
        scratch_shapes=[pltpu.VMEM((8, CONV_HALO + tc, CONV_CH), F32)],
        compiler_params=_cparams(("parallel", "parallel")),
    )(yglu, yglu, w, row(b), row(lnw), row(lnb))


def _softplus(z):
    return jnp.maximum(z, 0.0) + jnp.log1p(jnp.exp(-jnp.abs(z)))


def _rwkv_features(cur, prow, mu, w0, w2p, a0, a2p, g2, k_k, k_a, r_k, bd):
    rows = lax.broadcasted_iota(I32, cur.shape, 0)
    prev = jnp.where(rows == 0, prow, pltpu.roll(cur, 1, axis=0))
    xs = cur + mu * (prev - cur)
    r = xs[:, 0:RWKV_CH]
    k = xs[:, RWKV_CH:2 * RWKV_CH]
    v = xs[:, 2 * RWKV_CH:3 * RWKV_CH]
    wa = xs[:, 3 * RWKV_CH:3 * RWKV_CH + LORA_W + LORA_A]
    gl = xs[:, 3 * RWKV_CH + LORA_W + LORA_A:]
    w = -_softplus(-(w0 + _mm(jnp.tanh(wa), w2p, 3))) - 0.5
    a = jax.nn.sigmoid(a0 + _mm(wa, a2p, 3))
    g = _mm(jax.nn.sigmoid(gl), g2, 3)
    kk = k * k_k
    kkn = kk / jnp.maximum(jnp.sqrt(_head_sums(kk * kk, bd)), 1e-12)
    k2 = k * (1.0 + (a - 1.0) * k_a)
    bonus = _head_sums(r * k2 * r_k, bd) * v
    return r, k2, v, -jnp.exp(w), -kkn, kkn * a, g, bonus


RWKV_PASSES = {"gram": 3, "lakv": 3, "solve": 3, "out": 3, "state": 3}
SOLVE_BLK = 16


def _expand(x, lane_lo):
    return jnp.concatenate([jnp.where(lane_lo, x, 0.0), jnp.where(lane_lo, 0.0, x)], axis=0)


def _rwkv_kernel(p_ref, mu_ref, w0_ref, w2_ref, a0_ref, a2_ref, g2_ref, kk_ref, ka_ref, rk_ref, bd_ref,
                 bdm_ref, gnw_ref, gnb_ref, y_ref, s_ref, last_ref):
    C = CHUNK_T
    H2 = 2 * C

    @pl.when(pl.program_id(1) == 0)
    def _():
        s_ref[...] = jnp.zeros_like(s_ref)
        last_ref[...] = jnp.zeros_like(last_ref)

    cur = p_ref[0]
    r, k, vv, lw, a, b, gate, bonus = _rwkv_features(
        cur, last_ref[...], mu_ref[...], w0_ref[...], w2_ref[...], a0_ref[...], a2_ref[...], g2_ref[...],
        kk_ref[...], ka_ref[...], rk_ref[...], bd_ref[...])
    last_ref[...] = cur[C - 1:C, :]

    tri = (lax.broadcasted_iota(I32, (C, C), 0) >= lax.broadcasted_iota(I32, (C, C), 1)).astype(F32)
    cum = _dot(tri, lw, HI)
    tot = cum[C - 1:C, :]
    e_pos = jnp.exp(cum)
    e_neg = jnp.exp(-cum)
    e_rem = jnp.exp(tot - cum)
    rt = r * e_pos
    at = a * jnp.exp(cum - lw)
    kt = k * e_neg
    bt = b * e_neg
    kp = k * e_rem
    bp = b * e_rem
    pc = jnp.exp(tot)

    lane_lo = lax.broadcasted_iota(I32, (C, LANES), 1) < HEAD
    tt = lax.broadcasted_iota(I32, (H2, H2), 0) % C
    ss = lax.broadcasted_iota(I32, (H2, H2), 1) % C
    strict = tt > ss
    incl = tt >= ss
    near = tt // SOLVE_BLK == ss // SOLVE_BLK
    eye = lax.broadcasted_iota(I32, (LANES, LANES), 0) == lax.broadcasted_iota(I32, (LANES, LANES), 1)

    P = RWKV_PASSES
    pairs = range(RWKV_CH // LANES)
    each = lambda fn, *lists: [fn(*args) for args in zip(*lists)]
    sls = [slice(hp * LANES, (hp + 1) * LANES) for hp in pairs]
    ax, rx, bx, kx, vx, bpx, kpx = ([_expand(t[:, sl], lane_lo) for sl in sls] for t in (at, rt, bt, kt, vv, bp, kp))
    gram = each(lambda a_, r_, b_, k_: _mm(jnp.concatenate([a_, r_], axis=0), jnp.concatenate([b_, k_], axis=0),
                                           P["gram"], _dot_nt), ax, rx, bx, kx)
    l_ab = [jnp.where(strict, g_[:H2, :H2], 0.0) for g_ in gram]
    l_ak = [jnp.where(strict, g_[:H2, H2:], 0.0) for g_ in gram]
    m_r = [jnp.concatenate([jnp.where(incl, g_[H2:, :H2], 0.0), jnp.where(incl, g_[H2:, H2:], 0.0)], axis=1)
           for g_ in gram]
    dg = [jnp.where(near, l_, 0.0) for l_ in l_ab]
    lakv = each(lambda l_, v_: _mm(l_, v_, P["lakv"]), l_ak, vx)
    xf = each(lambda a_, lv_, l_, d_: jnp.concatenate([a_, lv_, l_ - d_], axis=1), ax, lakv, l_ab, dg)
    n_sq = SOLVE_BLK.bit_length() - 1
    for it in range(n_sq):
        xf = each(lambda x_, d_: x_ + _mm(d_, x_, P["solve"]), xf, dg)
        if it + 1 < n_sq:
            dg = [_mm(d_, d_, P["solve"]) for d_ in dg]
    x = [x_[:, :2 * LANES] for x_ in xf]
    f = [x_[:, 2 * LANES:] for x_ in xf]
    n_sq = (C // SOLVE_BLK).bit_length() - 1
    for it in range(n_sq):
        x = each(lambda x_, f_: x_ + _mm(f_, x_, P["solve"]), x, f)
        if it + 1 < n_sq:
            f = [_mm(f_, f_, P["solve"]) for f_ in f]
    zero = jnp.zeros((H2, LANES), F32)
    z = each(lambda x_, v_: jnp.concatenate([x_, jnp.concatenate([zero, v_], axis=1)], axis=0), x, vx)
    w1 = each(lambda m_, z_: _mm(m_, z_, P["out"]), m_r, z)
    w2 = each(lambda b_, k_, z_: _mm(jnp.concatenate([b_, k_], axis=0), z_, P["out"], _dot_tn), bpx, kpx, z)
    ys = []
    for hp in pairs:
        ra = rx[hp] + w1[hp][:, :LANES]
        ra = ra[:C] + ra[C:]
        y0 = w1[hp][:C, LANES:] + w1[hp][C:, LANES:]
        mt = w2[hp][:, :LANES] + jnp.where(eye, pc[:, sls[hp]], 0.0)
        s0 = s_ref[hp]
        ys.append(_mm(ra, s0, P["state"]) + y0)
        s_ref[hp] = _mm(mt, s0, P["state"]) + w2[hp][:, LANES:]
    yc = [y_ - _mm_bf16_rhs(y_, bdm_ref[...]) for y_ in ys]
    ys = [c_ * lax.rsqrt(_mm_bf16_rhs(c_ * c_, bdm_ref[...]) + GN_EPS) for c_ in yc]

    yn = jnp.concatenate(ys, axis=1) * gnw_ref[...] + gnb_ref[...]
    y_ref[0] = ((yn + bonus) * gate).astype(y_ref.dtype)


def _rwkv(prw, mu, w0, w2p, a0, a2p, g2, k_k, k_a, r_k, gnw, gnb):
    bsz, s, _ = prw.shape
    row = lambda t: t.reshape(1, -1)
    const = lambda shp: pl.BlockSpec(shp, lambda bb, c: (0, 0))
    vec = const((1, RWKV_CH))
    lora = const((LORA_W + LORA_A, RWKV_CH))
    return pl.pallas_call(
        _rwkv_kernel,
        grid=(bsz, s // CHUNK_T),
        in_specs=[pl.BlockSpec((1, CHUNK_T, RWKV_PROJ), lambda bb, c: (bb, c, 0)), const((1, RWKV_PROJ)),
                  vec, lora, vec, lora, const((LORA_G, RWKV_CH)), vec, vec, vec,
                  const((LANES, LANES)), const((LANES, LANES)), vec, vec],
        out_specs=pl.BlockSpec((1, CHUNK_T, RWKV_CH), lambda bb, c: (bb, c, 0)),
        out_shape=jax.ShapeDtypeStruct((bsz, s, RWKV_CH), BF16),
        scratch_shapes=[pltpu.VMEM((RWKV_CH // LANES, LANES, LANES), F32), pltpu.VMEM((1, RWKV_PROJ), F32)],
        compiler_params=_cparams(("parallel", "arbitrary")),
        name="rwkv",
    )(prw, row(mu), row(w0), w2p, row(a0), a2p, g2, row(k_k), row(k_a), row(r_k),
      _block_diag(LANES, HEAD, 1.0).astype(BF16), _block_diag(LANES, HEAD, 1.0 / HEAD).astype(BF16),
      row(gnw), row(gnb))


def _out_proj_kernel(yc_ref, yr_ref, wo_ref, x_ref, gt_ref, g_ref, sh_ref, sc_ref, x1_o, u2_o):
    mix = _dot(yc_ref[0], wo_ref[0:CONV_CH, :]) + _dot(yr_ref[0], wo_ref[CONV_CH:, :])
    x1 = x_ref[0] + gt_ref[0] * mix
    x1_o[0] = x1
    u2 = _rms(x1, g_ref[...]) * (1.0 + sc_ref[0]) + sh_ref[0]
    for c in range(CHUNKS):
        u2_o[0, :, c, :] = u2[:, c * LANES:(c + 1) * LANES]


def _out_proj(yc, yr, wo_bf16, x, gt, g, sh, sc, tm=256):
    bsz, s, _ = x.shape
    vec = pl.BlockSpec((1, 1, D_MODEL), lambda b, i: (b, 0, 0))
    tile = lambda w: pl.BlockSpec((1, tm, w), lambda b, i: (b, i, 0))
    return pl.pallas_call(
        _out_proj_kernel,
        grid=(bsz, s // tm),
        in_specs=[tile(CONV_CH), tile(RWKV_CH), pl.BlockSpec((D_MODEL, D_MODEL), lambda b, i: (0, 0)),
                  tile(D_MODEL), vec, pl.BlockSpec((1, D_MODEL), lambda b, i: (0, 0)), vec, vec],
        out_specs=[tile(D_MODEL), pl.BlockSpec((1, tm, CHUNKS, LANES), lambda b, i: (b, i, 0, 0))],
        out_shape=[jax.ShapeDtypeStruct((bsz, s, D_MODEL), F32),
                   jax.ShapeDtypeStruct((bsz, s, CHUNKS, LANES), F32)],
        compiler_params=_cparams(("parallel", "parallel")),
        name="out_proj",
    )(yc, yr, wo_bf16, x, gt, g.reshape(1, D_MODEL), sh, sc)


HEADS_STEP = 4


def _topk_rows(ss, k):
    n, t = ss[0].shape
    rids = [lax.broadcasted_iota(I32, (8, t), 0) + r0 for r0 in range(0, n, 8)]
    ss = [[s[r0:r0 + 8] for r0 in range(0, n, 8)] for s in ss]
    vals, ids = [[] for _ in ss], [[] for _ in ss]
    for _ in range(k):
        tops = [_max_with_tag(s, [rids]) for s in ss]
        ss = [[jnp.where(r == j, -jnp.inf, c) for c, r in zip(s, rids)] for s, (_, (j,)) in zip(ss, tops)]
        for v, i, (m, (j,)) in zip(vals, ids, tops):
            v.append(m)
            i.append(j)
    return [(jnp.concatenate(v, axis=0), jnp.concatenate(i, axis=0)) for v, i in zip(vals, ids)]


def _max_with_tag(chunks, tags):
    vals, tags = list(chunks), [list(tg) for tg in tags]
    while len(vals) > 1:
        nv, nt = [], [[] for _ in tags]
        for a in range(0, len(vals) - 1, 2):
            first = vals[a] >= vals[a + 1]
            nv.append(jnp.maximum(vals[a], vals[a + 1]))
            for dst, tg in zip(nt, tags):
                dst.append(jnp.where(first, tg[a], tg[a + 1]))
        if len(vals) % 2:
            nv.append(vals[-1])
            for dst, tg in zip(nt, tags):
                dst.append(tg[-1])
        vals, tags = nv, nt
    v8 = vals[0]
    m = jnp.max(v8, axis=0, keepdims=True)
    big = jnp.iinfo(jnp.int32).max
    key = jnp.min(jnp.where(v8 == m, tags[0][0], big), axis=0, keepdims=True)
    out = [key]
    for tg in tags[1:]:
        out.append(jnp.sum(jnp.where(tags[0][0] == key, tg[0], 0), axis=0, keepdims=True))
    return m, out


def _route_kernel(u_ref, wq_ref, keys_ref, idx_o, gate_o, q_ref, idx_s, gate_s):
    tm = u_ref.shape[0]
    u = jnp.concatenate([u_ref[:, c, :] for c in range(CHUNKS)], axis=1)
    q = _dot(u.astype(BF16), wq_ref[...])
    for j in range(2 * PEER_HEADS):
        q_ref[j] = q[:, j * LANES:(j + 1) * LANES]
    K = PEER_TOPK
    tt = LANES
    tiles = range(tm // tt)
    row8 = lax.broadcasted_iota(I32, (8, tt), 0)

    def heads(i, carry):
        probs = [(i * HEADS_STEP + dh, lt) for dh in range(HEADS_STEP) for lt in tiles]
        scores = [_mm(keys_ref[2 * h + p], q_ref[2 * h + p, lt * tt:(lt + 1) * tt, :], 3, _dot_nt)
                  for h, lt in probs for p in range(2)]
        tops = _topk_rows(scores, K)
        ss, flats, eids = [], [], []
        for n_ in range(len(probs)):
            (av, ai), (bv, bi) = tops[2 * n_], tops[2 * n_ + 1]
            cs = [av[0:1] + bv[0:8], av[0:1] + bv[8:16]]
            cf = [row8, row8 + 8]
            ce = [ai[0:1] * PEER_NKEYS + bi[0:8], ai[0:1] * PEER_NKEYS + bi[8:16]]
            for x in range(1, 8):
                cs.append(jnp.where(row8 < K // (x + 1), av[x:x + 1] + bv[0:8], -jnp.inf))
                cf.append(x * K + row8)
                ce.append(ai[x:x + 1] * PEER_NKEYS + bi[0:8])
            cs.append(av[8:16] + bv[0:1])
            cf.append((row8 + 8) * K)
            ce.append(ai[8:16] * PEER_NKEYS + bi[0:1])
            ss.append(cs)
            flats.append(cf)
            eids.append(ce)
        best, experts = [[] for _ in probs], [[] for _ in probs]
        for _ in range(K):
            tops2 = [_max_with_tag(s, [fl, ei]) for s, fl, ei in zip(ss, flats, eids)]
            for n_, (m, (f, e)) in enumerate(tops2):
                best[n_].append(m)
                experts[n_].append(e)
            ss = [[jnp.where(fc == f, -jnp.inf, c) for c, fc in zip(s, fl)]
                  for s, fl, (_, (f, _e)) in zip(ss, flats, tops2)]
        for n_, (h, lt) in enumerate(probs):
            b = jnp.concatenate(best[n_], axis=0)
            e = jnp.exp(b - b[0:1])
            gate_s[lt, h] = e / jnp.sum(e, axis=0, keepdims=True)
            idx_s[lt, h] = jnp.concatenate(experts[n_], axis=0) * WORD_ROWS
        return carry

    lax.fori_loop(0, PEER_HEADS // HEADS_STEP, heads, 0)
    for lt in tiles:
        idx_o[lt * tt:(lt + 1) * tt, :] = jnp.transpose(idx_s[lt].reshape(NPAIR, tt))
        gate_o[lt * tt:(lt + 1) * tt, :] = jnp.transpose(gate_s[lt].reshape(NPAIR, tt))


def _route(u2, wq_bf16, keys, tm=256):
    n = u2.shape[0]
    oblk = pl.BlockSpec((tm, NPAIR), lambda i: (i, 0))
    return pl.pallas_call(
        _route_kernel,
        grid=(n // tm,),
        in_specs=[pl.BlockSpec((tm, CHUNKS, LANES), lambda i: (i, 0, 0)),
                  pl.BlockSpec((D_MODEL, PEER_HEADS * PEER_DQ), lambda i: (0, 0)),
                  pl.BlockSpec((2 * PEER_HEADS, PEER_NKEYS, PEER_DQ // 2), lambda i: (0, 0, 0))],
        out_specs=[oblk, oblk],
        out_shape=[jax.ShapeDtypeStruct((n, NPAIR), I32), jax.ShapeDtypeStruct((n, NPAIR), F32)],
        scratch_shapes=[pltpu.VMEM((2 * PEER_HEADS, tm, LANES), F32),
                        pltpu.VMEM((tm // LANES, PEER_HEADS, PEER_TOPK, LANES), I32),
                        pltpu.VMEM((tm // LANES, PEER_HEADS, PEER_TOPK, LANES), F32)],
        compiler_params=_cparams(("parallel",)),
        name="route",
    )(u2, wq_bf16, keys)


TOK_UNROLL = 16


def _pack_kernel(t_ref, o_ref):
    te = t_ref.shape[0]
    bits = lambda v: lax.bitcast_convert_type(v.astype(BF16).astype(F32), I32)
    for s in range(WORD_ROWS):
        lo = bits(t_ref[:, (2 * s) * LANES:(2 * s + 1) * LANES])
        hi = bits(t_ref[:, (2 * s + 1) * LANES:(2 * s + 2) * LANES])
        o_ref[pl.ds(s, te, stride=WORD_ROWS), :] = (hi & -65536) | lax.shift_right_logical(lo, 16)


def _pack_table(t, te=512):
    e = t.shape[0]
    return pl.pallas_call(
        _pack_kernel,
        grid=(e // te,),
        in_specs=[pl.BlockSpec((te, D_MODEL), lambda i: (i, 0))],
        out_specs=pl.BlockSpec((te * WORD_ROWS, LANES), lambda i: (i, 0)),
        out_shape=jax.ShapeDtypeStruct((e * WORD_ROWS, LANES), I32),
        compiler_params=_cparams(("parallel",)),
        name="pack_table",
    )(t)


def _gather_rows(idx_ref, t0, tab_ref, g_ref):
    for u in range(TOK_UNROLL):
        idx_row = idx_ref.at[t0 + u]
        for kk in range(NPAIR):
            row = pl.multiple_of(idx_row[kk], WORD_ROWS)
            g_ref[u, kk * WORD_ROWS:(kk + 1) * WORD_ROWS, :] = tab_ref[pl.ds(row, WORD_ROWS), :]


def _peer_u_kernel(idx_ref, x_ref, gate_ref, keep_ref, sel_ref, tab_ref, w_o, g_ref, p_ref):
    tp = x_ref.shape[0]

    def tokens(i, carry):
        t0 = pl.multiple_of(i * TOK_UNROLL, TOK_UNROLL)
        _gather_rows(idx_ref, t0, tab_ref, g_ref)
        xs = x_ref[pl.ds(t0, TOK_UNROLL)]
        parts = []
        for u in range(TOK_UNROLL):
            g = pltpu.bitcast(g_ref[u], BF16)
            xc = xs[u]
            xt = jnp.concatenate([xc] * (LANES // CHUNKS), axis=0).astype(BF16)
            r = _dot_nt(g, xt) * keep_ref[...]
            parts.append(jnp.sum(r.reshape(NPAIR // 16, 16 * CHUNKS, LANES), axis=1))
        p_ref[pl.ds(pl.multiple_of(t0 * 8, 8 * TOK_UNROLL), 8 * TOK_UNROLL), :] = jnp.concatenate(parts, axis=0)
        return carry

    lax.fori_loop(0, tp // TOK_UNROLL, tokens, 0)
    z = _mm_bf16_rhs(p_ref[...], sel_ref[...], 3)
    grp = lax.broadcasted_iota(I32, z.shape, 0) % 8 == lax.broadcasted_iota(I32, z.shape, 1) // 16
    h = jnp.sum(jnp.where(grp, z, 0.0).reshape(tp, 8, LANES), axis=1)
    w_o[...] = gate_ref[...] * (0.5 * h * (1.0 + lax.erf(h * (2.0 ** -0.5))))


def _peer_u(idx, x3, gate, tab, start, n, tp=128):
    o = start // tp
    rows = jnp.arange(NPAIR * CHUNKS)[:, None]
    lanes = jnp.arange(LANES)[None, :]
    keep = ((lanes % CHUNKS == rows % CHUNKS) & (lanes // CHUNKS == (rows // CHUNKS) % 16)).astype(F32)
    sel = (jnp.arange(LANES)[:, None] // CHUNKS == jnp.arange(LANES)[None, :] % 16).astype(BF16)
    return pl.pallas_call(
        _peer_u_kernel,
        grid=(n // tp,),
        in_specs=[pl.BlockSpec((tp, NPAIR), lambda i: (i + o, 0), memory_space=pltpu.SMEM),
                  pl.BlockSpec((tp, CHUNKS, LANES), lambda i: (i + o, 0, 0)),
                  pl.BlockSpec((tp, NPAIR), lambda i: (i + o, 0)),
                  pl.BlockSpec((NPAIR * CHUNKS, LANES), lambda i: (0, 0)),
                  pl.BlockSpec((LANES, LANES), lambda i: (0, 0)),
                  pl.BlockSpec(memory_space=pltpu.VMEM)],
        out_specs=pl.BlockSpec((tp, NPAIR), lambda i: (i, 0)),
        out_shape=jax.ShapeDtypeStruct((n, NPAIR), F32),
        scratch_shapes=[pltpu.VMEM((TOK_UNROLL, NPAIR * WORD_ROWS, LANES), I32),
                        pltpu.VMEM((tp * 8, LANES), F32)],
        compiler_params=_cparams(("arbitrary",)),
        name="peer_u",
    )(idx, x3, gate, keep, sel, tab)


def _peer_v_kernel(idx_ref, w_ref, rep_ref, diag_ref, tab_ref, o_ref, g_ref, wx_ref):
    tp = w_ref.shape[0]
    wx_ref[...] = _mm_bf16_rhs(w_ref[...], rep_ref[...], 3)

    def tokens(i, carry):
        t0 = pl.multiple_of(i * TOK_UNROLL, TOK_UNROLL)
        _gather_rows(idx_ref, t0, tab_ref, g_ref)
        wx = wx_ref[pl.ds(t0, TOK_UNROLL), :]
        outs = []
        for u in range(TOK_UNROLL):
            g = pltpu.bitcast(g_ref[u], BF16)
            wm = (wx[u:u + 1, :] * diag_ref[...]).astype(BF16)
            outs.append(_dot(wm, g))
        o_ref[pl.ds(t0, TOK_UNROLL)] = jnp.stack(outs, axis=0)
        return carry

    lax.fori_loop(0, tp // TOK_UNROLL, tokens, 0)


def _peer_v(idx, w, tab, start, tp=128):
    n = w.shape[0]
    o = start // tp
    rep = jnp.repeat(jnp.eye(NPAIR, dtype=BF16), CHUNKS, axis=1)
    diag = (jnp.arange(CHUNKS)[:, None] == jnp.arange(NPAIR * CHUNKS)[None, :] % CHUNKS).astype(F32)
    return pl.pallas_call(
        _peer_v_kernel,
        grid=(n // tp,),
        in_specs=[pl.BlockSpec((tp, NPAIR), lambda i: (i + o, 0), memory_space=pltpu.SMEM),
                  pl.BlockSpec((tp, NPAIR), lambda i: (i, 0)),
                  pl.BlockSpec((NPAIR, NPAIR * CHUNKS), lambda i: (0, 0)),
                  pl.BlockSpec((CHUNKS, NPAIR * CHUNKS), lambda i: (0, 0)),
                  pl.BlockSpec(memory_space=pltpu.VMEM)],
        out_specs=pl.BlockSpec((tp, CHUNKS, LANES), lambda i: (i, 0, 0)),
        out_shape=jax.ShapeDtypeStruct((n, CHUNKS, LANES), F32),
        scratch_shapes=[pltpu.VMEM((TOK_UNROLL, NPAIR * WORD_ROWS, LANES), I32),
                        pltpu.VMEM((tp, NPAIR * CHUNKS), F32)],
        compiler_params=_cparams(("arbitrary",)),
        name="peer_v",
    )(idx, w, rep, diag, tab)


SC_LANES = 16
SC_WORKERS = 32
SC_ROWS = 64
SC_TOKENS = 16
SC_SHARE_NUM, SC_SHARE_DEN = 31, 64
SC_COLS = 8


def _peer_v_sc(ids, wts, packed):
    m = wts.shape[0]
    per_worker = m // SC_WORKERS
    blocks = NPAIR // SC_ROWS
    words = D_MODEL // 2
    table = packed.reshape(packed.shape[0] // WORD_ROWS, words)
    ids_b = ids.reshape(ids.shape[0] * blocks, SC_ROWS)
    wts_f = wts.reshape(m * NPAIR)
    mesh = plsc.VectorSubcoreMesh(core_axis_name="c", subcore_axis_name="s")

    @functools.partial(
        pl.kernel, mesh=mesh, out_type=jax.ShapeDtypeStruct((m, D_MODEL), F32),
        scratch_types=[pltpu.VMEM((SC_TOKENS * blocks, SC_ROWS), I32), pltpu.VMEM((SC_TOKENS * NPAIR,), F32),
                       pltpu.VMEM((2, SC_ROWS, words), I32), pltpu.VMEM((D_MODEL,), F32),
                       pltpu.SemaphoreType.DMA((2,))],
        compiler_params=pltpu.CompilerParams(needs_layout_passes=False),
        name="peer_v_sc")
    def run(tab_hbm, ids_hbm, w_hbm, out_hbm, ids_v, w_v, rows_v, acc_v, sems):
        wid = lax.axis_index("s") * 2 + lax.axis_index("c")
        base = wid * per_worker
        zero = jnp.zeros((SC_LANES,), F32)

        def gather(u, b):
            return pltpu.make_async_copy(tab_hbm.at[ids_v.at[u * blocks + b]], rows_v.at[b % 2], sems.at[b % 2])

        @pl.loop(0, per_worker // SC_TOKENS)
        def _(tb):
            t0 = base + tb * SC_TOKENS
            pltpu.sync_copy(ids_hbm.at[pl.ds(t0 * blocks, SC_TOKENS * blocks)], ids_v)
            pltpu.sync_copy(w_hbm.at[pl.ds(t0 * NPAIR, SC_TOKENS * NPAIR)], w_v)

            @pl.loop(0, SC_TOKENS)
            def _(u):
                for j in range(D_MODEL // SC_LANES):
                    acc_v[pl.ds(j * SC_LANES, SC_LANES)] = zero
                gather(u, 0).start()
                for b in range(blocks):
                    if b + 1 < blocks:
                        gather(u, b + 1).start()
                    gather(u, b).wait()

                    for c0 in range(0, words // SC_LANES, SC_COLS):
                        def row(r, accs, b=b, c0=c0):
                            wk = plsc.load_gather(w_v, [jnp.full((SC_LANES,), u * NPAIR + b * SC_ROWS + r, I32)])
                            out = []
                            for j in range(SC_COLS):
                                w32 = rows_v[b % 2, r, pl.ds((c0 + j) * SC_LANES, SC_LANES)]
                                lo = lax.bitcast_convert_type(lax.shift_left(w32, 16), F32)
                                hi = lax.bitcast_convert_type(w32 & -65536, F32)
                                out += [accs[2 * j] + wk * lo, accs[2 * j + 1] + wk * hi]
                            return tuple(out)
                        accs = lax.fori_loop(0, SC_ROWS, row, (zero,) * (2 * SC_COLS))
                        for j in range(SC_COLS):
                            s_, l0 = divmod((c0 + j) * SC_LANES, LANES)
                            plsc.addupdate(acc_v.at[pl.ds(2 * s_ * LANES + l0, SC_LANES)], accs[2 * j])
                            plsc.addupdate(acc_v.at[pl.ds((2 * s_ + 1) * LANES + l0, SC_LANES)], accs[2 * j + 1])
                pltpu.sync_copy(acc_v, out_hbm.at[t0 + u])

    return run(table, ids_b, wts_f)


def _final_kernel(x1_ref, psc_ref, ptc_ref, gt_ref, g_ref, o_ref, *, sc_tiles):
    p_tc = jnp.concatenate([ptc_ref[:, c, :] for c in range(CHUNKS)], axis=1)
    p = jnp.where(pl.program_id(0) < sc_tiles, psc_ref[...], p_tc)
    o_ref[...] = _rms(x1_ref[...] + gt_ref[0] * p, g_ref[...])


def _final(x1, peer_sc, peer_tc, gt, g, tm=512):
    n = x1.shape[0]
    per_seq = n // gt.shape[0] // tm
    k = peer_sc.shape[0] // tm
    tile = pl.BlockSpec((tm, D_MODEL), lambda i: (i, 0))
    return pl.pallas_call(
        functools.partial(_final_kernel, sc_tiles=k),
        grid=(n // tm,),
        in_specs=[tile, pl.BlockSpec((tm, D_MODEL), lambda i: (jnp.minimum(i, k - 1), 0)),
                  pl.BlockSpec((tm, CHUNKS, LANES), lambda i: (jnp.maximum(i - k, 0), 0, 0)),
                  pl.BlockSpec((1, 1, D_MODEL), lambda i: (i // per_seq, 0, 0)),
                  pl.BlockSpec((1, D_MODEL), lambda i: (0, 0))],
        out_specs=tile,
        out_shape=jax.ShapeDtypeStruct((n, D_MODEL), F32),
        compiler_params=_cparams(("parallel",)),
        name="final",
    )(x1, peer_sc, peer_tc, gt, g.reshape(1, D_MODEL))


def _block_diag(width, group, value):
    i = jnp.arange(width) // group
    return jnp.where(i[:, None] == i[None, :], value, 0.0).astype(F32)


def _layer(x, mod, final_g, norm_mix_g, w_in, conv_dw_w, conv_dw_b, conv_ln_w, conv_ln_b, rwkv_mu, rwkv_w0, rwkv_w2,
           rwkv_a0, rwkv_a2, rwkv_g2, rwkv_k_k, rwkv_k_a, rwkv_r_k, rwkv_gn_w, rwkv_gn_b, w_out, norm_ffn_g,
           peer_w_q, peer_sub_keys, peer_u, peer_v):
    bsz, s, _ = x.shape
    sh_mix, sc_mix, gt_mix, sh_ffn, sc_ffn, gt_ffn = (
        mod[:, i * D_MODEL:(i + 1) * D_MODEL].reshape(bsz, 1, D_MODEL) for i in range(6))

    yglu, prw = _in_proj(x, sh_mix, sc_mix, norm_mix_g, w_in.astype(BF16))
    y_conv = _conv(yglu, conv_dw_w, conv_dw_b, conv_ln_w, conv_ln_b)

    zpad = jnp.zeros((LORA_W, RWKV_CH), F32)
    y_rwkv = _rwkv(prw, rwkv_mu, rwkv_w0, jnp.concatenate([rwkv_w2, zpad], axis=0), rwkv_a0,
                   jnp.concatenate([zpad, rwkv_a2], axis=0), rwkv_g2, rwkv_k_k, rwkv_k_a, rwkv_r_k.reshape(-1),
                   rwkv_gn_w, rwkv_gn_b)

    x1, u2 = _out_proj(y_conv, y_rwkv, w_out.astype(BF16), x, gt_mix, norm_ffn_g, sh_ffn, sc_ffn)

    n = bsz * s
    u3 = u2.reshape(n, CHUNKS, LANES)
    keys = peer_sub_keys.reshape(2 * PEER_HEADS, PEER_NKEYS, PEER_DQ // 2)
    idx, gate = _route(u3, peer_w_q.astype(BF16), keys)
    n_sc = n * SC_SHARE_NUM // SC_SHARE_DEN
    tab_u, tab_v = _pack_table(peer_u), _pack_table(peer_v)
    wts_a = _peer_u(idx, u3, gate, tab_u, 0, n_sc)
    peer_sc = _peer_v_sc(lax.shift_right_logical(idx, 2), wts_a, tab_v)
    wts_b = _peer_u(idx, u3, gate, tab_u, n_sc, n - n_sc)
    peer_tc = _peer_v(idx, wts_b, tab_v, n_sc)
    return _final(x1.reshape(n, D_MODEL), peer_sc, peer_tc, gt_ffn, final_g).reshape(bsz, s, D_MODEL)


def kernel(x, c, ada_w, ada_b, norm_mix_g, w_in, conv_dw_w, conv_dw_b, conv_ln_w, conv_ln_b, rwkv_mu, rwkv_w0,
           rwkv_w2, rwkv_a0, rwkv_a2, rwkv_g2, rwkv_k_k, rwkv_k_a, rwkv_r_k, rwkv_gn_w, rwkv_gn_b, w_out,
           norm_ffn_g, peer_w_q, peer_sub_keys, peer_u, peer_v, final_g):
    depth = ada_w.shape[0]
    assert depth == 1, "one layer: the final norm is fused into the last layer's residual"
    mod = _mod(c, ada_w[0], ada_b[0])
    return _layer(x, mod, final_g, norm_mix_g[0], w_in[0], conv_dw_w[0], conv_dw_b[0], conv_ln_w[0],
                        conv_ln_b[0], rwkv_mu[0], rwkv_w0[0], rwkv_w2[0], rwkv_a0[0], rwkv_a2[0], rwkv_g2[0],
                        rwkv_k_k[0], rwkv_k_a[0], rwkv_r_k[0], rwkv_gn_w[0], rwkv_gn_b[0], w_out[0],
                        norm_ffn_g[0], peer_w_q[0], peer_sub_keys[0], peer_u[0], peer_v[0])
```

```python
import functools

import jax
import jax.numpy as jnp
from jax import lax
from jax.experimental import pallas as pl
from jax.experimental.pallas import tpu as pltpu
from jax.experimental.pallas import tpu_sc as plsc

F32 = jnp.float32
BF16 = jnp.bfloat16
I32 = jnp.int32
HI = lax.Precision.HIGHEST

D_MODEL = 1024
CONV_CH = 512
RWKV_CH = 512
HEAD = 64
CONV_WIDTH = 31
LORA_W = 64
LORA_A = 64
LORA_G = 128
RWKV_PROJ = 3 * RWKV_CH + LORA_W + LORA_A + LORA_G
IN_PROJ = 2 * CONV_CH + RWKV_PROJ
PEER_HEADS = 8
PEER_NKEYS = 128
PEER_EXPERTS = PEER_NKEYS * PEER_NKEYS
PEER_DQ = 256
PEER_TOPK = 16
NPAIR = PEER_HEADS * PEER_TOPK
RMS_EPS = 1e-6
LN_EPS = 1e-5
GN_EPS = 64e-5

LANES = 128
CHUNKS = D_MODEL // LANES
WORD_ROWS = CHUNKS // 2
CHUNK_T = 64
VMEM_LIMIT = 56 * 1024 * 1024


def _cparams(sem, vmem=None):
    return pltpu.CompilerParams(dimension_semantics=sem, vmem_limit_bytes=vmem or VMEM_LIMIT)


def _dot(a, b, precision=None):
    return jnp.dot(a, b, precision=precision, preferred_element_type=F32)


def _dot_nt(a, b, precision=None):
    return lax.dot_general(a, b, (((1,), (1,)), ((), ())), precision=precision, preferred_element_type=F32)


def _dot_tn(a, b, precision=None):
    return lax.dot_general(a, b, (((0,), (0,)), ((), ())), precision=precision, preferred_element_type=F32)


def _split(a):
    hi = a.astype(BF16)
    return hi, (a - hi.astype(F32)).astype(BF16)


def _mm(a, b, passes, dot=_dot):
    if passes == 6:
        return dot(a, b, HI)
    if passes == 1:
        return dot(a.astype(BF16), b.astype(BF16))
    ka = 0 if dot is _dot_tn else 1
    kb = 1 if dot is _dot_nt else 0
    ah, al = _split(a)
    bh, bl = _split(b)
    return dot(jnp.concatenate([ah, ah, al], axis=ka), jnp.concatenate([bh, bl, bh], axis=kb))


def _mm_bf16_rhs(a, b_bf16, terms=2):
    parts = []
    for _ in range(terms):
        p = a.astype(BF16)
        parts.append(p)
        a = a - p.astype(F32)
    return _dot(jnp.concatenate(parts, axis=1), jnp.concatenate([b_bf16] * terms, axis=0))


def _head_sums(x, bd_bf16):
    return jnp.concatenate([_mm_bf16_rhs(x[:, i * LANES:(i + 1) * LANES], bd_bf16, 3)
                            for i in range(x.shape[1] // LANES)], axis=1)


def _rms(x, g):
    return x * lax.rsqrt(jnp.mean(x * x, axis=-1, keepdims=True) + RMS_EPS) * g


def _mod_kernel(c_ref, w_ref, b_ref, o_ref):
    c = c_ref[...]
    o_ref[...] = _dot(c * jax.nn.sigmoid(c), w_ref[...], HI) + b_ref[...]


def _mod(c, w, b):
    bsz = c.shape[0]
    n = w.shape[1]
    tn = 1024
    return pl.pallas_call(
        _mod_kernel,
        grid=(n // tn,),
        in_specs=[pl.BlockSpec((bsz, D_MODEL), lambda j: (0, 0)),
                  pl.BlockSpec((D_MODEL, tn), lambda j: (0, j)),
                  pl.BlockSpec((1, tn), lambda j: (0, j))],
        out_specs=pl.BlockSpec((bsz, tn), lambda j: (0, j)),
        out_shape=jax.ShapeDtypeStruct((bsz, n), F32),
        compiler_params=_cparams(("parallel",)),
    )(c, w, b.reshape(1, n))


def _in_proj_kernel(x_ref, sh_ref, sc_ref, g_ref, w_ref, yglu_ref, prw_ref):
    u = _rms(x_ref[0], g_ref[...]) * (1.0 + sc_ref[0]) + sh_ref[0]
    p = _dot(u.astype(BF16), w_ref[...])
    yglu_ref[0] = p[:, :CONV_CH] * jax.nn.sigmoid(p[:, CONV_CH:2 * CONV_CH])
    prw_ref[0] = p[:, 2 * CONV_CH:]


def _in_proj(x, sh, sc, g, w_bf16, tm=256):
    bsz, s, _ = x.shape
    vec = pl.BlockSpec((1, 1, D_MODEL), lambda b, i: (b, 0, 0))
    return pl.pallas_call(
        _in_proj_kernel,
        grid=(bsz, s // tm),
        in_specs=[pl.BlockSpec((1, tm, D_MODEL), lambda b, i: (b, i, 0)), vec, vec,
                  pl.BlockSpec((1, D_MODEL), lambda b, i: (0, 0)),
                  pl.BlockSpec((D_MODEL, IN_PROJ), lambda b, i: (0, 0))],
        out_specs=[pl.BlockSpec((1, tm, CONV_CH), lambda b, i: (b, i, 0)),
                   pl.BlockSpec((1, tm, RWKV_PROJ), lambda b, i: (b, i, 0))],
        out_shape=[jax.ShapeDtypeStruct((bsz, s, CONV_CH), F32),
                   jax.ShapeDtypeStruct((bsz, s, RWKV_PROJ), F32)],
        compiler_params=_cparams(("parallel", "parallel")),
    )(x, sh, sc, g.reshape(1, D_MODEL), w_bf16)


CONV_HALO = 32
CONV_ROWS = 64


def _conv_kernel(cur_ref, prev_ref, w_ref, b_ref, lnw_ref, lnb_ref, o_ref, pad_ref):
    tc = cur_ref.shape[1]
    pad_ref[0, 0:CONV_HALO, :] = jnp.where(pl.program_id(1) > 0, prev_ref[0], 0.0)
    pad_ref[0, CONV_HALO:CONV_HALO + tc, :] = cur_ref[0]
    span = CONV_HALO + tc - 8
    for r in range(1, 8):
        pad_ref[r, 0:span, :] = pad_ref[0, r:r + span, :]
    off = CONV_HALO - (CONV_WIDTH - 1)
    for r0 in range(0, tc, CONV_ROWS):
        acc = jnp.zeros((CONV_ROWS, CONV_CH), F32)
        for j in range(CONV_WIDTH):
            q, r = divmod(off + j, 8)
            acc = acc + w_ref[j:j + 1, :] * pad_ref[r, r0 + 8 * q:r0 + 8 * q + CONV_ROWS, :]
        y = acc + b_ref[...]
        mu = jnp.mean(y, axis=-1, keepdims=True)
        yc = y - mu
        var = jnp.mean(yc * yc, axis=-1, keepdims=True)
        yn = yc * lax.rsqrt(var + LN_EPS) * lnw_ref[...] + lnb_ref[...]
        o_ref[0, r0:r0 + CONV_ROWS, :] = (yn * jax.nn.sigmoid(yn)).astype(o_ref.dtype)


def _conv(yglu, w, b, lnw, lnb, tc=256):
    bsz, s, _ = yglu.shape
    hb = tc // CONV_HALO
    row = lambda a: a.reshape(1, CONV_CH)
    const = lambda shp: pl.BlockSpec(shp, lambda bb, i: (0, 0))
    return pl.pallas_call(
        _conv_kernel,
        grid=(bsz, s // tc),
        in_specs=[pl.BlockSpec((1, tc, CONV_CH), lambda bb, i: (bb, i, 0)),
                  pl.BlockSpec((1, CONV_HALO, CONV_CH), lambda bb, i: (bb, jnp.maximum(i * hb - 1, 0), 0)),
                  const((CONV_WIDTH, CONV_CH)), const((1, CONV_CH)), const((1, CONV_CH)), const((1, CONV_CH))],
        out_specs=pl.BlockSpec((1, tc, CONV_CH), lambda bb, i: (bb, i, 0)),
        out_shape=jax.ShapeDtypeStruct((bsz, s, CONV_CH), BF16),
        scratch_shapes=[pltpu.VMEM((8, CONV_HALO + tc, CONV_CH), F32)],
        compiler_params=_cparams(("parallel", "parallel")),
    )(yglu, yglu, w, row(b), row(lnw), row(lnb))


def _softplus(z):
    return jnp.maximum(z, 0.0) + jnp.log1p(jnp.exp(-jnp.abs(z)))


def _rwkv_features(cur, prow, mu, w0, w2p, a0, a2p, g2, k_k, k_a, r_k, bd):
    rows = lax.broadcasted_iota(I32, cur.shape, 0)
    prev = jnp.where(rows == 0, prow, pltpu.roll(cur, 1, axis=0))
    xs = cur + mu * (prev - cur)
    r = xs[:, 0:RWKV_CH]
    k = xs[:, RWKV_CH:2 * RWKV_CH]
    v = xs[:, 2 * RWKV_CH:3 * RWKV_CH]
    wa = xs[:, 3 * RWKV_CH:3 * RWKV_CH + LORA_W + LORA_A]
    gl = xs[:, 3 * RWKV_CH + LORA_W + LORA_A:]
    w = -_softplus(-(w0 + _mm(jnp.tanh(wa), w2p, 3))) - 0.5
    a = jax.nn.sigmoid(a0 + _mm(wa, a2p, 3))
    g = _mm(jax.nn.sigmoid(gl), g2, 3)
    kk = k * k_k
    kkn = kk / jnp.maximum(jnp.sqrt(_head_sums(kk * kk, bd)), 1e-12)
    k2 = k * (1.0 + (a - 1.0) * k_a)
    bonus = _head_sums(r * k2 * r_k, bd) * v
    return r, k2, v, -jnp.exp(w), -kkn, kkn * a, g, bonus


RWKV_PASSES = {"gram": 3, "lakv": 3, "solve": 3, "out": 3, "state": 3}
SOLVE_BLK = 16


def _expand(x, lane_lo):
    return jnp.concatenate([jnp.where(lane_lo, x, 0.0), jnp.where(lane_lo, 0.0, x)], axis=0)


def _rwkv_kernel(p_ref, mu_ref, w0_ref, w2_ref, a0_ref, a2_ref, g2_ref, kk_ref, ka_ref, rk_ref, bd_ref,
                 bdm_ref, gnw_ref, gnb_ref, y_ref, s_ref, last_ref):
    C = CHUNK_T
    H2 = 2 * C

    @pl.when(pl.program_id(1) == 0)
    def _():
        s_ref[...] = jnp.zeros_like(s_ref)
        last_ref[...] = jnp.zeros_like(last_ref)

    cur = p_ref[0]
    r, k, vv, lw, a, b, gate, bonus = _rwkv_features(
        cur, last_ref[...], mu_ref[...], w0_ref[...], w2_ref[...], a0_ref[...], a2_ref[...], g2_ref[...],
        kk_ref[...], ka_ref[...], rk_ref[...], bd_ref[...])
    last_ref[...] = cur[C - 1:C, :]

    tri = (lax.broadcasted_iota(I32, (C, C), 0) >= lax.broadcasted_iota(I32, (C, C), 1)).astype(F32)
    cum = _dot(tri, lw, HI)
    tot = cum[C - 1:C, :]
    e_pos = jnp.exp(cum)
    e_neg = jnp.exp(-cum)
    e_rem = jnp.exp(tot - cum)
    rt = r * e_pos
    at = a * jnp.exp(cum - lw)
    kt = k * e_neg
    bt = b * e_neg
    kp = k * e_rem
    bp = b * e_rem
    pc = jnp.exp(tot)

    lane_lo = lax.broadcasted_iota(I32, (C, LANES), 1) < HEAD
    tt = lax.broadcasted_iota(I32, (H2, H2), 0) % C
    ss = lax.broadcasted_iota(I32, (H2, H2), 1) % C
    strict = tt > ss
    incl = tt >= ss
    near = tt // SOLVE_BLK == ss // SOLVE_BLK
    eye = lax.broadcasted_iota(I32, (LANES, LANES), 0) == lax.broadcasted_iota(I32, (LANES, LANES), 1)

    P = RWKV_PASSES
    pairs = range(RWKV_CH // LANES)
    each = lambda fn, *lists: [fn(*args) for args in zip(*lists)]
    sls = [slice(hp * LANES, (hp + 1) * LANES) for hp in pairs]
    ax, rx, bx, kx, vx, bpx, kpx = ([_expand(t[:, sl], lane_lo) for sl in sls] for t in (at, rt, bt, kt, vv, bp, kp))
    gram = each(lambda a_, r_, b_, k_: _mm(jnp.concatenate([a_, r_], axis=0), jnp.concatenate([b_, k_], axis=0),
                                           P["gram"], _dot_nt), ax, rx, bx, kx)
    l_ab = [jnp.where(strict, g_[:H2, :H2], 0.0) for g_ in gram]
    l_ak = [jnp.where(strict, g_[:H2, H2:], 0.0) for g_ in gram]
    m_r = [jnp.concatenate([jnp.where(incl, g_[H2:, :H2], 0.0), jnp.where(incl, g_[H2:, H2:], 0.0)], axis=1)
           for g_ in gram]
    dg = [jnp.where(near, l_, 0.0) for l_ in l_ab]
    lakv = each(lambda l_, v_: _mm(l_, v_, P["lakv"]), l_ak, vx)
    xf = each(lambda a_, lv_, l_, d_: jnp.concatenate([a_, lv_, l_ - d_], axis=1), ax, lakv, l_ab, dg)
    n_sq = SOLVE_BLK.bit_length() - 1
    for it in range(n_sq):
        xf = each(lambda x_, d_: x_ + _mm(d_, x_, P["solve"]), xf, dg)
        if it + 1 < n_sq:
            dg = [_mm(d_, d_, P["solve"]) for d_ in dg]
    x = [x_[:, :2 * LANES] for x_ in xf]
    f = [x_[:, 2 * LANES:] for x_ in xf]
    n_sq = (C // SOLVE_BLK).bit_length() - 1
    for it in range(n_sq):
        x = each(lambda x_, f_: x_ + _mm(f_, x_, P["solve"]), x, f)
        if it + 1 < n_sq:
            f = [_mm(f_, f_, P["solve"]) for f_ in f]
    zero = jnp.zeros((H2, LANES), F32)
    z = each(lambda x_, v_: jnp.concatenate([x_, jnp.concatenate([zero, v_], axis=1)], axis=0), x, vx)
    w1 = each(lambda m_, z_: _mm(m_, z_, P["out"]), m_r, z)
    w2 = each(lambda b_, k_, z_: _mm(jnp.concatenate([b_, k_], axis=0), z_, P["out"], _dot_tn), bpx, kpx, z)
    ys = []
    for hp in pairs:
        ra = rx[hp] + w1[hp][:, :LANES]
        ra = ra[:C] + ra[C:]
        y0 = w1[hp][:C, LANES:] + w1[hp][C:, LANES:]
        mt = w2[hp][:, :LANES] + jnp.where(eye, pc[:, sls[hp]], 0.0)
        s0 = s_ref[hp]
        ys.append(_mm(ra, s0, P["state"]) + y0)
        s_ref[hp] = _mm(mt, s0, P["state"]) + w2[hp][:, LANES:]
    yc = [y_ - _mm_bf16_rhs(y_, bdm_ref[...]) for y_ in ys]
    ys = [c_ * lax.rsqrt(_mm_bf16_rhs(c_ * c_, bdm_ref[...]) + GN_EPS) for c_ in yc]

    yn = jnp.concatenate(ys, axis=1) * gnw_ref[...] + gnb_ref[...]
    y_ref[0] = ((yn + bonus) * gate).astype(y_ref.dtype)


def _rwkv(prw, mu, w0, w2p, a0, a2p, g2, k_k, k_a, r_k, gnw, gnb):
    bsz, s, _ = prw.shape
    row = lambda t: t.reshape(1, -1)
    const = lambda shp: pl.BlockSpec(shp, lambda bb, c: (0, 0))
    vec = const((1, RWKV_CH))
    lora = const((LORA_W + LORA_A, RWKV_CH))
    return pl.pallas_call(
        _rwkv_kernel,
        grid=(bsz, s // CHUNK_T),
        in_specs=[pl.BlockSpec((1, CHUNK_T, RWKV_PROJ), lambda bb, c: (bb, c, 0)), const((1, RWKV_PROJ)),
                  vec, lora, vec, lora, const((LORA_G, RWKV_CH)), vec, vec, vec,
                  const((LANES, LANES)), const((LANES, LANES)), vec, vec],
        out_specs=pl.BlockSpec((1, CHUNK_T, RWKV_CH), lambda bb, c: (bb, c, 0)),
        out_shape=jax.ShapeDtypeStruct((bsz, s, RWKV_CH), BF16),
        scratch_shapes=[pltpu.VMEM((RWKV_CH // LANES, LANES, LANES), F32), pltpu.VMEM((1, RWKV_PROJ), F32)],
        compiler_params=_cparams(("parallel", "arbitrary")),
        name="rwkv",
    )(prw, row(mu), row(w0), w2p, row(a0), a2p, g2, row(k_k), row(k_a), row(r_k),
      _block_diag(LANES, HEAD, 1.0).astype(BF16), _block_diag(LANES, HEAD, 1.0 / HEAD).astype(BF16),
      row(gnw), row(gnb))


def _out_proj_kernel(yc_ref, yr_ref, wo_ref, x_ref, gt_ref, g_ref, sh_ref, sc_ref, x1_o, u2_o):
    mix = _dot(yc_ref[0], wo_ref[0:CONV_CH, :]) + _dot(yr_ref[0], wo_ref[CONV_CH:, :])
    x1 = x_ref[0] + gt_ref[0] * mix
    x1_o[0] = x1
    u2 = _rms(x1, g_ref[...]) * (1.0 + sc_ref[0]) + sh_ref[0]
    for c in range(CHUNKS):
        u2_o[0, :, c, :] = u2[:, c * LANES:(c + 1) * LANES]


def _out_proj(yc, yr, wo_bf16, x, gt, g, sh, sc, tm=256):
    bsz, s, _ = x.shape
    vec = pl.BlockSpec((1, 1, D_MODEL), lambda b, i: (b, 0, 0))
    tile = lambda w: pl.BlockSpec((1, tm, w), lambda b, i: (b, i, 0))
    return pl.pallas_call(
        _out_proj_kernel,
        grid=(bsz, s // tm),
        in_specs=[tile(CONV_CH), tile(RWKV_CH), pl.BlockSpec((D_MODEL, D_MODEL), lambda b, i: (0, 0)),
                  tile(D_MODEL), vec, pl.BlockSpec((1, D_MODEL), lambda b, i: (0, 0)), vec, vec],
        out_specs=[tile(D_MODEL), pl.BlockSpec((1, tm, CHUNKS, LANES), lambda b, i: (b, i, 0, 0))],
        out_shape=[jax.ShapeDtypeStruct((bsz, s, D_MODEL), F32),
                   jax.ShapeDtypeStruct((bsz, s, CHUNKS, LANES), F32)],
        compiler_params=_cparams(("parallel", "parallel")),
        name="out_proj",
    )(yc, yr, wo_bf16, x, gt, g.reshape(1, D_MODEL), sh, sc)


HEADS_STEP = 4


def _topk_rows(ss, k):
    n, t = ss[0].shape
    rids = [lax.broadcasted_iota(I32, (8, t), 0) + r0 for r0 in range(0, n, 8)]
    ss = [[s[r0:r0 + 8] for r0 in range(0, n, 8)] for s in ss]
    vals, ids = [[] for _ in ss], [[] for _ in ss]
    for _ in range(k):
        tops = [_max_with_tag(s, [rids]) for s in ss]
        ss = [[jnp.where(r == j, -jnp.inf, c) for c, r in zip(s, rids)] for s, (_, (j,)) in zip(ss, tops)]
        for v, i, (m, (j,)) in zip(vals, ids, tops):
            v.append(m)
            i.append(j)
    return [(jnp.concatenate(v, axis=0), jnp.concatenate(i, axis=0)) for v, i in zip(vals, ids)]


def _max_with_tag(chunks, tags):
    vals, tags = list(chunks), [list(tg) for tg in tags]
    while len(vals) > 1:
        nv, nt = [], [[] for _ in tags]
        for a in range(0, len(vals) - 1, 2):
            first = vals[a] >= vals[a + 1]
            nv.append(jnp.maximum(vals[a], vals[a + 1]))
            for dst, tg in zip(nt, tags):
                dst.append(jnp.where(first, tg[a], tg[a + 1]))
        if len(vals) % 2:
            nv.append(vals[-1])
            for dst, tg in zip(nt, tags):
                dst.append(tg[-1])
        vals, tags = nv, nt
    v8 = vals[0]
    m = jnp.max(v8, axis=0, keepdims=True)
    big = jnp.iinfo(jnp.int32).max
    key = jnp.min(jnp.where(v8 == m, tags[0][0], big), axis=0, keepdims=True)
    out = [key]
    for tg in tags[1:]:
        out.append(jnp.sum(jnp.where(tags[0][0] == key, tg[0], 0), axis=0, keepdims=True))
    return m, out


def _route_kernel(u_ref, wq_ref, keys_ref, idx_o, gate_o, q_ref, idx_s, gate_s):
    tm = u_ref.shape[0]
    u = jnp.concatenate([u_ref[:, c, :] for c in range(CHUNKS)], axis=1)
    q = _dot(u.astype(BF16), wq_ref[...])
    for j in range(2 * PEER_HEADS):
        q_ref[j] = q[:, j * LANES:(j + 1) * LANES]
    K = PEER_TOPK
    tt = LANES
    tiles = range(tm // tt)
    row8 = lax.broadcasted_iota(I32, (8, tt), 0)

    def heads(i, carry):
        probs = [(i * HEADS_STEP + dh, lt) for dh in range(HEADS_STEP) for lt in tiles]
        scores = [_mm(keys_ref[2 * h + p], q_ref[2 * h + p, lt * tt:(lt + 1) * tt, :], 3, _dot_nt)
                  for h, lt in probs for p in range(2)]
        tops = _topk_rows(scores, K)
        ss, flats, eids = [], [], []
        for n_ in range(len(probs)):
            (av, ai), (bv, bi) = tops[2 * n_], tops[2 * n_ + 1]
            cs = [av[0:1] + bv[0:8], av[0:1] + bv[8:16]]
            cf = [row8, row8 + 8]
            ce = [ai[0:1] * PEER_NKEYS + bi[0:8], ai[0:1] * PEER_NKEYS + bi[8:16]]
            for x in range(1, 8):
                cs.append(jnp.where(row8 < K // (x + 1), av[x:x + 1] + bv[0:8], -jnp.inf))
                cf.append(x * K + row8)
                ce.append(ai[x:x + 1] * PEER_NKEYS + bi[0:8])
            cs.append(av[8:16] + bv[0:1])
            cf.append((row8 + 8) * K)
            ce.append(ai[8:16] * PEER_NKEYS + bi[0:1])
            ss.append(cs)
            flats.append(cf)
            eids.append(ce)
        best, experts = [[] for _ in probs], [[] for _ in probs]
        for _ in range(K):
            tops2 = [_max_with_tag(s, [fl, ei]) for s, fl, ei in zip(ss, flats, eids)]
            for n_, (m, (f, e)) in enumerate(tops2):
                best[n_].append(m)
                experts[n_].append(e)
            ss = [[jnp.where(fc == f, -jnp.inf, c) for c, fc in zip(s, fl)]
                  for s, fl, (_, (f, _e)) in zip(ss, flats, tops2)]
        for n_, (h, lt) in enumerate(probs):
            b = jnp.concatenate(best[n_], axis=0)
            e = jnp.exp(b - b[0:1])
            gate_s[lt, h] = e / jnp.sum(e, axis=0, keepdims=True)
            idx_s[lt, h] = jnp.concatenate(experts[n_], axis=0) * WORD_ROWS
        return carry

    lax.fori_loop(0, PEER_HEADS // HEADS_STEP, heads, 0)
    for lt in tiles:
        idx_o[lt * tt:(lt + 1) * tt, :] = jnp.transpose(idx_s[lt].reshape(NPAIR, tt))
        gate_o[lt * tt:(lt + 1) * tt, :] = jnp.transpose(gate_s[lt].reshape(NPAIR, tt))


def _route(u2, wq_bf16, keys, tm=256):
    n = u2.shape[0]
    oblk = pl.BlockSpec((tm, NPAIR), lambda i: (i, 0))
    return pl.pallas_call(
        _route_kernel,
        grid=(n // tm,),
        in_specs=[pl.BlockSpec((tm, CHUNKS, LANES), lambda i: (i, 0, 0)),
                  pl.BlockSpec((D_MODEL, PEER_HEADS * PEER_DQ), lambda i: (0, 0)),
                  pl.BlockSpec((2 * PEER_HEADS, PEER_NKEYS, PEER_DQ // 2), lambda i: (0, 0, 0))],
        out_specs=[oblk, oblk],
        out_shape=[jax.ShapeDtypeStruct((n, NPAIR), I32), jax.ShapeDtypeStruct((n, NPAIR), F32)],
        scratch_shapes=[pltpu.VMEM((2 * PEER_HEADS, tm, LANES), F32),
                        pltpu.VMEM((tm // LANES, PEER_HEADS, PEER_TOPK, LANES), I32),
                        pltpu.VMEM((tm // LANES, PEER_HEADS, PEER_TOPK, LANES), F32)],
        compiler_params=_cparams(("parallel",)),
        name="route",
    )(u2, wq_bf16, keys)


TOK_UNROLL = 16


def _pack_kernel(t_ref, o_ref):
    te = t_ref.shape[0]
    bits = lambda v: lax.bitcast_convert_type(v.astype(BF16).astype(F32), I32)
    for s in range(WORD_ROWS):
        lo = bits(t_ref[:, (2 * s) * LANES:(2 * s + 1) * LANES])
        hi = bits(t_ref[:, (2 * s + 1) * LANES:(2 * s + 2) * LANES])
        o_ref[pl.ds(s, te, stride=WORD_ROWS), :] = (hi & -65536) | lax.shift_right_logical(lo, 16)


def _pack_table(t, te=512):
    e = t.shape[0]
    return pl.pallas_call(
        _pack_kernel,
        grid=(e // te,),
        in_specs=[pl.BlockSpec((te, D_MODEL), lambda i: (i, 0))],
        out_specs=pl.BlockSpec((te * WORD_ROWS, LANES), lambda i: (i, 0)),
        out_shape=jax.ShapeDtypeStruct((e * WORD_ROWS, LANES), I32),
        compiler_params=_cparams(("parallel",)),
        name="pack_table",
    )(t)


def _gather_rows(idx_ref, t0, tab_ref, g_ref):
    for u in range(TOK_UNROLL):
        idx_row = idx_ref.at[t0 + u]
        for kk in range(NPAIR):
            row = pl.multiple_of(idx_row[kk], WORD_ROWS)
            g_ref[u, kk * WORD_ROWS:(kk + 1) * WORD_ROWS, :] = tab_ref[pl.ds(row, WORD_ROWS), :]


def _peer_u_kernel(idx_ref, x_ref, gate_ref, keep_ref, sel_ref, tab_ref, w_o, g_ref, p_ref):
    tp = x_ref.shape[0]

    def tokens(i, carry):
        t0 = pl.multiple_of(i * TOK_UNROLL, TOK_UNROLL)
        _gather_rows(idx_ref, t0, tab_ref, g_ref)
        xs = x_ref[pl.ds(t0, TOK_UNROLL)]
        parts = []
        for u in range(TOK_UNROLL):
            g = pltpu.bitcast(g_ref[u], BF16)
            xc = xs[u]
            xt = jnp.concatenate([xc] * (LANES // CHUNKS), axis=0).astype(BF16)
            r = _dot_nt(g, xt) * keep_ref[...]
            parts.append(jnp.sum(r.reshape(NPAIR // 16, 16 * CHUNKS, LANES), axis=1))
        p_ref[pl.ds(pl.multiple_of(t0 * 8, 8 * TOK_UNROLL), 8 * TOK_UNROLL), :] = jnp.concatenate(parts, axis=0)
        return carry

    lax.fori_loop(0, tp // TOK_UNROLL, tokens, 0)
    z = _mm_bf16_rhs(p_ref[...], sel_ref[...], 3)
    grp = lax.broadcasted_iota(I32, z.shape, 0) % 8 == lax.broadcasted_iota(I32, z.shape, 1) // 16
    h = jnp.sum(jnp.where(grp, z, 0.0).reshape(tp, 8, LANES), axis=1)
    w_o[...] = gate_ref[...] * (0.5 * h * (1.0 + lax.erf(h * (2.0 ** -0.5))))


def _peer_u(idx, x3, gate, tab, start, n, tp=128):
    o = start // tp
    rows = jnp.arange(NPAIR * CHUNKS)[:, None]
    lanes = jnp.arange(LANES)[None, :]
    keep = ((lanes % CHUNKS == rows % CHUNKS) & (lanes // CHUNKS == (rows // CHUNKS) % 16)).astype(F32)
    sel = (jnp.arange(LANES)[:, None] // CHUNKS == jnp.arange(LANES)[None, :] % 16).astype(BF16)
    return pl.pallas_call(
        _peer_u_kernel,
        grid=(n // tp,),
        in_specs=[pl.BlockSpec((tp, NPAIR), lambda i: (i + o, 0), memory_space=pltpu.SMEM),
                  pl.BlockSpec((tp, CHUNKS, LANES), lambda i: (i + o, 0, 0)),
                  pl.BlockSpec((tp, NPAIR), lambda i: (i + o, 0)),
                  pl.BlockSpec((NPAIR * CHUNKS, LANES), lambda i: (0, 0)),
                  pl.BlockSpec((LANES, LANES), lambda i: (0, 0)),
                  pl.BlockSpec(memory_space=pltpu.VMEM)],
        out_specs=pl.BlockSpec((tp, NPAIR), lambda i: (i, 0)),
        out_shape=jax.ShapeDtypeStruct((n, NPAIR), F32),
        scratch_shapes=[pltpu.VMEM((TOK_UNROLL, NPAIR * WORD_ROWS, LANES), I32),
                        pltpu.VMEM((tp * 8, LANES), F32)],
        compiler_params=_cparams(("arbitrary",)),
        name="peer_u",
    )(idx, x3, gate, keep, sel, tab)


def _peer_v_kernel(idx_ref, w_ref, rep_ref, diag_ref, tab_ref, o_ref, g_ref, wx_ref):
    tp = w_ref.shape[0]
    wx_ref[...] = _mm_bf16_rhs(w_ref[...], rep_ref[...], 3)

    def tokens(i, carry):
        t0 = pl.multiple_of(i * TOK_UNROLL, TOK_UNROLL)
        _gather_rows(idx_ref, t0, tab_ref, g_ref)
        wx = wx_ref[pl.ds(t0, TOK_UNROLL), :]
        outs = []
        for u in range(TOK_UNROLL):
            g = pltpu.bitcast(g_ref[u], BF16)
            wm = (wx[u:u + 1, :] * diag_ref[...]).astype(BF16)
            outs.append(_dot(wm, g))
        o_ref[pl.ds(t0, TOK_UNROLL)] = jnp.stack(outs, axis=0)
        return carry

    lax.fori_loop(0, tp // TOK_UNROLL, tokens, 0)


def _peer_v(idx, w, tab, start, tp=128):
    n = w.shape[0]
    o = start // tp
    rep = jnp.repeat(jnp.eye(NPAIR, dtype=BF16), CHUNKS, axis=1)
    diag = (jnp.arange(CHUNKS)[:, None] == jnp.arange(NPAIR * CHUNKS)[None, :] % CHUNKS).astype(F32)
    return pl.pallas_call(
        _peer_v_kernel,
        grid=(n // tp,),
        in_specs=[pl.BlockSpec((tp, NPAIR), lambda i: (i + o, 0), memory_space=pltpu.SMEM),
                  pl.BlockSpec((tp, NPAIR), lambda i: (i, 0)),
                  pl.BlockSpec((NPAIR, NPAIR * CHUNKS), lambda i: (0, 0)),
                  pl.BlockSpec((CHUNKS, NPAIR * CHUNKS), lambda i: (0, 0)),
                  pl.BlockSpec(memory_space=pltpu.VMEM)],
        out_specs=pl.BlockSpec((tp, CHUNKS, LANES), lambda i: (i, 0, 0)),
        out_shape=jax.ShapeDtypeStruct((n, CHUNKS, LANES), F32),
        scratch_shapes=[pltpu.VMEM((TOK_UNROLL, NPAIR * WORD_ROWS, LANES), I32),
                        pltpu.VMEM((tp, NPAIR * CHUNKS), F32)],
        compiler_params=_cparams(("arbitrary",)),
        name="peer_v",
    )(idx, w, rep, diag, tab)


SC_LANES = 16
SC_WORKERS = 32
SC_ROWS = 64
SC_TOKENS = 16
SC_SHARE_A, SC_SHARE_B, SC_SHARE_DEN = 14, 26, 64
SC_COLS = 8


def _peer_v_sc(ids, wts, packed, start):
    m = wts.shape[0]
    per_worker = m // SC_WORKERS
    blocks = NPAIR // SC_ROWS
    words = D_MODEL // 2
    table = packed.reshape(packed.shape[0] // WORD_ROWS, words)
    ids_b = ids.reshape(ids.shape[0] * blocks, SC_ROWS)
    wts_f = wts.reshape(m * NPAIR)
    mesh = plsc.VectorSubcoreMesh(core_axis_name="c", subcore_axis_name="s")

    @functools.partial(
        pl.kernel, mesh=mesh, out_type=jax.ShapeDtypeStruct((m, D_MODEL), F32),
        scratch_types=[pltpu.VMEM((SC_TOKENS * blocks, SC_ROWS), I32), pltpu.VMEM((SC_TOKENS * NPAIR,), F32),
                       pltpu.VMEM((2, SC_ROWS, words), I32), pltpu.VMEM((D_MODEL,), F32),
                       pltpu.SemaphoreType.DMA((2,))],
        compiler_params=pltpu.CompilerParams(needs_layout_passes=False),
        name="peer_v_sc")
    def run(tab_hbm, ids_hbm, w_hbm, out_hbm, ids_v, w_v, rows_v, acc_v, sems):
        wid = lax.axis_index("s") * 2 + lax.axis_index("c")
        base = wid * per_worker
        zero = jnp.zeros((SC_LANES,), F32)

        def gather(u, b):
            return pltpu.make_async_copy(tab_hbm.at[ids_v.at[u * blocks + b]], rows_v.at[b % 2], sems.at[b % 2])

        @pl.loop(0, per_worker // SC_TOKENS)
        def _(tb):
            t0 = base + tb * SC_TOKENS
            pltpu.sync_copy(ids_hbm.at[pl.ds((start + t0) * blocks, SC_TOKENS * blocks)], ids_v)
            pltpu.sync_copy(w_hbm.at[pl.ds(t0 * NPAIR, SC_TOKENS * NPAIR)], w_v)

            @pl.loop(0, SC_TOKENS)
            def _(u):
                for j in range(D_MODEL // SC_LANES):
                    acc_v[pl.ds(j * SC_LANES, SC_LANES)] = zero
                gather(u, 0).start()
                for b in range(blocks):
                    if b + 1 < blocks:
                        gather(u, b + 1).start()
                    gather(u, b).wait()

                    for c0 in range(0, words // SC_LANES, SC_COLS):
                        def row(r, accs, b=b, c0=c0):
                            wk = plsc.load_gather(w_v, [jnp.full((SC_LANES,), u * NPAIR + b * SC_ROWS + r, I32)])
                            out = []
                            for j in range(SC_COLS):
                                w32 = rows_v[b % 2, r, pl.ds((c0 + j) * SC_LANES, SC_LANES)]
                                lo = lax.bitcast_convert_type(lax.shift_left(w32, 16), F32)
                                hi = lax.bitcast_convert_type(w32 & -65536, F32)
                                out += [accs[2 * j] + wk * lo, accs[2 * j + 1] + wk * hi]
                            return tuple(out)
                        accs = lax.fori_loop(0, SC_ROWS, row, (zero,) * (2 * SC_COLS))
                        for j in range(SC_COLS):
                            s_, l0 = divmod((c0 + j) * SC_LANES, LANES)
                            plsc.addupdate(acc_v.at[pl.ds(2 * s_ * LANES + l0, SC_LANES)], accs[2 * j])
                            plsc.addupdate(acc_v.at[pl.ds((2 * s_ + 1) * LANES + l0, SC_LANES)], accs[2 * j + 1])
                pltpu.sync_copy(acc_v, out_hbm.at[t0 + u])

    return run(table, ids_b, wts_f)


def _final_kernel(x1_ref, pa_ref, pb_ref, ptc_ref, gt_ref, g_ref, o_ref, *, a_tiles, b_tiles):
    i = pl.program_id(0)
    p_tc = jnp.concatenate([ptc_ref[:, c, :] for c in range(CHUNKS)], axis=1)
    p = jnp.where(i < a_tiles, pa_ref[...], jnp.where(i < a_tiles + b_tiles, pb_ref[...], p_tc))
    o_ref[...] = _rms(x1_ref[...] + gt_ref[0] * p, g_ref[...])


def _final(x1, peer_a, peer_b, peer_tc, gt, g, tm=512):
    n = x1.shape[0]
    per_seq = n // gt.shape[0] // tm
    ka, kb = peer_a.shape[0] // tm, peer_b.shape[0] // tm
    tile = pl.BlockSpec((tm, D_MODEL), lambda i: (i, 0))
    return pl.pallas_call(
        functools.partial(_final_kernel, a_tiles=ka, b_tiles=kb),
        grid=(n // tm,),
        in_specs=[tile, pl.BlockSpec((tm, D_MODEL), lambda i: (jnp.minimum(i, ka - 1), 0)),
                  pl.BlockSpec((tm, D_MODEL), lambda i: (jnp.clip(i - ka, 0, kb - 1), 0)),
                  pl.BlockSpec((tm, CHUNKS, LANES), lambda i: (jnp.maximum(i - ka - kb, 0), 0, 0)),
                  pl.BlockSpec((1, 1, D_MODEL), lambda i: (i // per_seq, 0, 0)),
                  pl.BlockSpec((1, D_MODEL), lambda i: (0, 0))],
        out_specs=tile,
        out_shape=jax.ShapeDtypeStruct((n, D_MODEL), F32),
        compiler_params=_cparams(("parallel",)),
        name="final",
    )(x1, peer_a, peer_b, peer_tc, gt, g.reshape(1, D_MODEL))


def _block_diag(width, group, value):
    i = jnp.arange(width) // group
    return jnp.where(i[:, None] == i[None, :], value, 0.0).astype(F32)


def _layer(x, mod, final_g, norm_mix_g, w_in, conv_dw_w, conv_dw_b, conv_ln_w, conv_ln_b, rwkv_mu, rwkv_w0, rwkv_w2,
           rwkv_a0, rwkv_a2, rwkv_g2, rwkv_k_k, rwkv_k_a, rwkv_r_k, rwkv_gn_w, rwkv_gn_b, w_out, norm_ffn_g,
           peer_w_q, peer_sub_keys, peer_u, peer_v):
    bsz, s, _ = x.shape
    sh_mix, sc_mix, gt_mix, sh_ffn, sc_ffn, gt_ffn = (
        mod[:, i * D_MODEL:(i + 1) * D_MODEL].reshape(bsz, 1, D_MODEL) for i in range(6))

    yglu, prw = _in_proj(x, sh_mix, sc_mix, norm_mix_g, w_in.astype(BF16))
    y_conv = _conv(yglu, conv_dw_w, conv_dw_b, conv_ln_w, conv_ln_b)

    zpad = jnp.zeros((LORA_W, RWKV_CH), F32)
    y_rwkv = _rwkv(prw, rwkv_mu, rwkv_w0, jnp.concatenate([rwkv_w2, zpad], axis=0), rwkv_a0,
                   jnp.concatenate([zpad, rwkv_a2], axis=0), rwkv_g2, rwkv_k_k, rwkv_k_a, rwkv_r_k.reshape(-1),
                   rwkv_gn_w, rwkv_gn_b)

    x1, u2 = _out_proj(y_conv, y_rwkv, w_out.astype(BF16), x, gt_mix, norm_ffn_g, sh_ffn, sc_ffn)

    n = bsz * s
    u3 = u2.reshape(n, CHUNKS, LANES)
    keys = peer_sub_keys.reshape(2 * PEER_HEADS, PEER_NKEYS, PEER_DQ // 2)
    idx, gate = _route(u3, peer_w_q.astype(BF16), keys)
    na, nb = n * SC_SHARE_A // SC_SHARE_DEN, n * SC_SHARE_B // SC_SHARE_DEN
    tab_u, tab_v = _pack_table(peer_u), _pack_table(peer_v)
    ids = lax.shift_right_logical(idx, 2)
    wts_a = _peer_u(idx, u3, gate, tab_u, 0, na)
    peer_a = _peer_v_sc(ids, wts_a, tab_v, 0)
    wts_b = _peer_u(idx, u3, gate, tab_u, na, nb)
    peer_b = _peer_v_sc(ids, wts_b, tab_v, na)
    wts_c = _peer_u(idx, u3, gate, tab_u, na + nb, n - na - nb)
    peer_tc = _peer_v(idx, wts_c, tab_v, na + nb)
    return _final(x1.reshape(n, D_MODEL), peer_a, peer_b, peer_tc, gt_ffn, final_g).reshape(bsz, s, D_MODEL)


def kernel(x, c, ada_w, ada_b, norm_mix_g, w_in, conv_dw_w, conv_dw_b, conv_ln_w, conv_ln_b, rwkv_mu, rwkv_w0,
           rwkv_w2, rwkv_a0, rwkv_a2, rwkv_g2, rwkv_k_k, rwkv_k_a, rwkv_r_k, rwkv_gn_w, rwkv_gn_b, w_out,
           norm_ffn_g, peer_w_q, peer_sub_keys, peer_u, peer_v, final_g):
    depth = ada_w.shape[0]
    assert depth == 1, "one layer: the final norm is fused into the last layer's residual"
    mod = _mod(c, ada_w[0], ada_b[0])
    return _layer(x, mod, final_g, norm_mix_g[0], w_in[0], conv_dw_w[0], conv_dw_b[0], conv_ln_w[0],
                        conv_ln_b[0], rwkv_mu[0], rwkv_w0[0], rwkv_w2[0], rwkv_a0[0], rwkv_a2[0], rwkv_g2[0],
                        rwkv_k_k[0], rwkv_k_a[0], rwkv_r_k[0], rwkv_gn_w[0], rwkv_gn_b[0], w_out[0],
                        norm_ffn_g[0], peer_w_q[0], peer_sub_keys[0], peer_u[0], peer_v[0])
```

```python
import functools

import jax
import jax.numpy as jnp
from jax import lax
from jax.experimental import pallas as pl
from jax.experimental.pallas import tpu as pltpu
from jax.experimental.pallas import tpu_sc as plsc

F32 = jnp.float32
BF16 = jnp.bfloat16
I32 = jnp.int32
HI = lax.Precision.HIGHEST

D_MODEL = 1024
CONV_CH = 512
RWKV_CH = 512
HEAD = 64
CONV_WIDTH = 31
LORA_W = 64
LORA_A = 64
LORA_G = 128
RWKV_PROJ = 3 * RWKV_CH + LORA_W + LORA_A + LORA_G
IN_PROJ = 2 * CONV_CH + RWKV_PROJ
PEER_HEADS = 8
PEER_NKEYS = 128
PEER_EXPERTS = PEER_NKEYS * PEER_NKEYS
PEER_DQ = 256
PEER_TOPK = 16
NPAIR = PEER_HEADS * PEER_TOPK
RMS_EPS = 1e-6
LN_EPS = 1e-5
GN_EPS = 64e-5

LANES = 128
CHUNKS = D_MODEL // LANES
WORD_ROWS = CHUNKS // 2
CHUNK_T = 64
VMEM_LIMIT = 56 * 1024 * 1024


def _cparams(sem, vmem=None):
    return pltpu.CompilerParams(dimension_semantics=sem, vmem_limit_bytes=vmem or VMEM_LIMIT)


def _dot(a, b, precision=None):
    return jnp.dot(a, b, precision=precision, preferred_element_type=F32)


def _dot_nt(a, b, precision=None):
    return lax.dot_general(a, b, (((1,), (1,)), ((), ())), precision=precision, preferred_element_type=F32)


def _dot_tn(a, b, precision=None):
    return lax.dot_general(a, b, (((0,), (0,)), ((), ())), precision=precision, preferred_element_type=F32)


def _split(a):
    hi = a.astype(BF16)
    return hi, (a - hi.astype(F32)).astype(BF16)


def _mm(a, b, passes, dot=_dot):
    if passes == 6:
        return dot(a, b, HI)
    if passes == 1:
        return dot(a.astype(BF16), b.astype(BF16))
    ka = 0 if dot is _dot_tn else 1
    kb = 1 if dot is _dot_nt else 0
    ah, al = _split(a)
    bh, bl = _split(b)
    return dot(jnp.concatenate([ah, ah, al], axis=ka), jnp.concatenate([bh, bl, bh], axis=kb))


def _mm_bf16_rhs(a, b_bf16, terms=2):
    parts = []
    for _ in range(terms):
        p = a.astype(BF16)
        parts.append(p)
        a = a - p.astype(F32)
    return _dot(jnp.concatenate(parts, axis=1), jnp.concatenate([b_bf16] * terms, axis=0))


def _head_sums(x, bd_bf16):
    return jnp.concatenate([_mm_bf16_rhs(x[:, i * LANES:(i + 1) * LANES], bd_bf16, 3)
                            for i in range(x.shape[1] // LANES)], axis=1)


def _rms(x, g):
    return x * lax.rsqrt(jnp.mean(x * x, axis=-1, keepdims=True) + RMS_EPS) * g


def _mod_kernel(c_ref, w_ref, b_ref, o_ref):
    c = c_ref[...]
    o_ref[...] = _dot(c * jax.nn.sigmoid(c), w_ref[...], HI) + b_ref[...]


def _mod(c, w, b):
    bsz = c.shape[0]
    n = w.shape[1]
    tn = 1024
    return pl.pallas_call(
        _mod_kernel,
        grid=(n // tn,),
        in_specs=[pl.BlockSpec((bsz, D_MODEL), lambda j: (0, 0)),
                  pl.BlockSpec((D_MODEL, tn), lambda j: (0, j)),
                  pl.BlockSpec((1, tn), lambda j: (0, j))],
        out_specs=pl.BlockSpec((bsz, tn), lambda j: (0, j)),
        out_shape=jax.ShapeDtypeStruct((bsz, n), F32),
        compiler_params=_cparams(("parallel",)),
    )(c, w, b.reshape(1, n))


def _in_proj_kernel(x_ref, sh_ref, sc_ref, g_ref, w_ref, yglu_ref, prw_ref):
    u = _rms(x_ref[0], g_ref[...]) * (1.0 + sc_ref[0]) + sh_ref[0]
    p = _dot(u.astype(BF16), w_ref[...])
    yglu_ref[0] = p[:, :CONV_CH] * jax.nn.sigmoid(p[:, CONV_CH:2 * CONV_CH])
    prw_ref[0] = p[:, 2 * CONV_CH:]


def _in_proj(x, sh, sc, g, w_bf16, tm=256):
    bsz, s, _ = x.shape
    vec = pl.BlockSpec((1, 1, D_MODEL), lambda b, i: (b, 0, 0))
    return pl.pallas_call(
        _in_proj_kernel,
        grid=(bsz, s // tm),
        in_specs=[pl.BlockSpec((1, tm, D_MODEL), lambda b, i: (b, i, 0)), vec, vec,
                  pl.BlockSpec((1, D_MODEL), lambda b, i: (0, 0)),
                  pl.BlockSpec((D_MODEL, IN_PROJ), lambda b, i: (0, 0))],
        out_specs=[pl.BlockSpec((1, tm, CONV_CH), lambda b, i: (b, i, 0)),
                   pl.BlockSpec((1, tm, RWKV_PROJ), lambda b, i: (b, i, 0))],
        out_shape=[jax.ShapeDtypeStruct((bsz, s, CONV_CH), F32),
                   jax.ShapeDtypeStruct((bsz, s, RWKV_PROJ), F32)],
        compiler_params=_cparams(("parallel", "parallel")),
    )(x, sh, sc, g.reshape(1, D_MODEL), w_bf16)


CONV_HALO = 32
CONV_ROWS = 64


def _conv_kernel(cur_ref, prev_ref, w_ref, b_ref, lnw_ref, lnb_ref, o_ref, pad_ref):
    tc = cur_ref.shape[1]
    pad_ref[0, 0:CONV_HALO, :] = jnp.where(pl.program_id(1) > 0, prev_ref[0], 0.0)
    pad_ref[0, CONV_HALO:CONV_HALO + tc, :] = cur_ref[0]
    span = CONV_HALO + tc - 8
    for r in range(1, 8):
        pad_ref[r, 0:span, :] = pad_ref[0, r:r + span, :]
    off = CONV_HALO - (CONV_WIDTH - 1)
    for r0 in range(0, tc, CONV_ROWS):
        acc = jnp.zeros((CONV_ROWS, CONV_CH), F32)
        for j in range(CONV_WIDTH):
            q, r = divmod(off + j, 8)
            acc = acc + w_ref[j:j + 1, :] * pad_ref[r, r0 + 8 * q:r0 + 8 * q + CONV_ROWS, :]
        y = acc + b_ref[...]
        mu = jnp.mean(y, axis=-1, keepdims=True)
        yc = y - mu
        var = jnp.mean(yc * yc, axis=-1, keepdims=True)
        yn = yc * lax.rsqrt(var + LN_EPS) * lnw_ref[...] + lnb_ref[...]
        o_ref[0, r0:r0 + CONV_ROWS, :] = (yn * jax.nn.sigmoid(yn)).astype(o_ref.dtype)


def _conv(yglu, w, b, lnw, lnb, tc=256):
    bsz, s, _ = yglu.shape
    hb = tc // CONV_HALO
    row = lambda a: a.reshape(1, CONV_CH)
    const = lambda shp: pl.BlockSpec(shp, lambda bb, i: (0, 0))
    return pl.pallas_call(
        _conv_kernel,
        grid=(bsz, s // tc),
        in_specs=[pl.BlockSpec((1, tc, CONV_CH), lambda bb, i: (bb, i, 0)),
                  pl.BlockSpec((1, CONV_HALO, CONV_CH), lambda bb, i: (bb, jnp.maximum(i * hb - 1, 0), 0)),
                  const((CONV_WIDTH, CONV_CH)), const((1, CONV_CH)), const((1, CONV_CH)), const((1, CONV_CH))],
        out_specs=pl.BlockSpec((1, tc, CONV_CH), lambda bb, i: (bb, i, 0)),
        out_shape=jax.ShapeDtypeStruct((bsz, s, CONV_CH), BF16),
        scratch_shapes=[pltpu.VMEM((8, CONV_HALO + tc, CONV_CH), F32)],
        compiler_params=_cparams(("parallel", "parallel")),
    )(yglu, yglu, w, row(b), row(lnw), row(lnb))


def _softplus(z):
    return jnp.maximum(z, 0.0) + jnp.log1p(jnp.exp(-jnp.abs(z)))


def _rwkv_features(cur, prow, mu, w0, w2p, a0, a2p, g2, k_k, k_a, r_k, bd):
    rows = lax.broadcasted_iota(I32, cur.shape, 0)
    prev = jnp.where(rows == 0, prow, pltpu.roll(cur, 1, axis=0))
    xs = cur + mu * (prev - cur)
    r = xs[:, 0:RWKV_CH]
    k = xs[:, RWKV_CH:2 * RWKV_CH]
    v = xs[:, 2 * RWKV_CH:3 * RWKV_CH]
    wa = xs[:, 3 * RWKV_CH:3 * RWKV_CH + LORA_W + LORA_A]
    gl = xs[:, 3 * RWKV_CH + LORA_W + LORA_A:]
    w = -_softplus(-(w0 + _mm(jnp.tanh(wa), w2p, 3))) - 0.5
    a = jax.nn.sigmoid(a0 + _mm(wa, a2p, 3))
    g = _mm(jax.nn.sigmoid(gl), g2, 3)
    kk = k * k_k
    kkn = kk / jnp.maximum(jnp.sqrt(_head_sums(kk * kk, bd)), 1e-12)
    k2 = k * (1.0 + (a - 1.0) * k_a)
    bonus = _head_sums(r * k2 * r_k, bd) * v
    return r, k2, v, -jnp.exp(w), -kkn, kkn * a, g, bonus


RWKV_PASSES = {"gram": 3, "lakv": 3, "solve": 3, "out": 3, "state": 3}
SOLVE_BLK = 16


def _expand(x, lane_lo):
    return jnp.concatenate([jnp.where(lane_lo, x, 0.0), jnp.where(lane_lo, 0.0, x)], axis=0)


def _rwkv_kernel(p_ref, mu_ref, w0_ref, w2_ref, a0_ref, a2_ref, g2_ref, kk_ref, ka_ref, rk_ref, bd_ref,
                 bdm_ref, gnw_ref, gnb_ref, y_ref, s_ref, last_ref):
    C = CHUNK_T
    H2 = 2 * C

    @pl.when(pl.program_id(1) == 0)
    def _():
        s_ref[...] = jnp.zeros_like(s_ref)
        last_ref[...] = jnp.zeros_like(last_ref)

    cur = p_ref[0]
    r, k, vv, lw, a, b, gate, bonus = _rwkv_features(
        cur, last_ref[...], mu_ref[...], w0_ref[...], w2_ref[...], a0_ref[...], a2_ref[...], g2_ref[...],
        kk_ref[...], ka_ref[...], rk_ref[...], bd_ref[...])
    last_ref[...] = cur[C - 1:C, :]

    tri = (lax.broadcasted_iota(I32, (C, C), 0) >= lax.broadcasted_iota(I32, (C, C), 1)).astype(F32)
    cum = _dot(tri, lw, HI)
    tot = cum[C - 1:C, :]
    e_pos = jnp.exp(cum)
    e_neg = jnp.exp(-cum)
    e_rem = jnp.exp(tot - cum)
    rt = r * e_pos
    at = a * jnp.exp(cum - lw)
    kt = k * e_neg
    bt = b * e_neg
    kp = k * e_rem
    bp = b * e_rem
    pc = jnp.exp(tot)

    lane_lo = lax.broadcasted_iota(I32, (C, LANES), 1) < HEAD
    tt = lax.broadcasted_iota(I32, (H2, H2), 0) % C
    ss = lax.broadcasted_iota(I32, (H2, H2), 1) % C
    strict = tt > ss
    incl = tt >= ss
    near = tt // SOLVE_BLK == ss // SOLVE_BLK
    eye = lax.broadcasted_iota(I32, (LANES, LANES), 0) == lax.broadcasted_iota(I32, (LANES, LANES), 1)

    P = RWKV_PASSES
    pairs = range(RWKV_CH // LANES)
    each = lambda fn, *lists: [fn(*args) for args in zip(*lists)]
    sls = [slice(hp * LANES, (hp + 1) * LANES) for hp in pairs]
    ax, rx, bx, kx, vx, bpx, kpx = ([_expand(t[:, sl], lane_lo) for sl in sls] for t in (at, rt, bt, kt, vv, bp, kp))
    gram = each(lambda a_, r_, b_, k_: _mm(jnp.concatenate([a_, r_], axis=0), jnp.concatenate([b_, k_], axis=0),
                                           P["gram"], _dot_nt), ax, rx, bx, kx)
    l_ab = [jnp.where(strict, g_[:H2, :H2], 0.0) for g_ in gram]
    l_ak = [jnp.where(strict, g_[:H2, H2:], 0.0) for g_ in gram]
    m_r = [jnp.concatenate([jnp.where(incl, g_[H2:, :H2], 0.0), jnp.where(incl, g_[H2:, H2:], 0.0)], axis=1)
           for g_ in gram]
    dg = [jnp.where(near, l_, 0.0) for l_ in l_ab]
    lakv = each(lambda l_, v_: _mm(l_, v_, P["lakv"]), l_ak, vx)
    xf = each(lambda a_, lv_, l_, d_: jnp.concatenate([a_, lv_, l_ - d_], axis=1), ax, lakv, l_ab, dg)
    n_sq = SOLVE_BLK.bit_length() - 1
    for it in range(n_sq):
        xf = each(lambda x_, d_: x_ + _mm(d_, x_, P["solve"]), xf, dg)
        if it + 1 < n_sq:
            dg = [_mm(d_, d_, P["solve"]) for d_ in dg]
    x = [x_[:, :2 * LANES] for x_ in xf]
    f = [x_[:, 2 * LANES:] for x_ in xf]
    n_sq = (C // SOLVE_BLK).bit_length() - 1
    for it in range(n_sq):
        x = each(lambda x_, f_: x_ + _mm(f_, x_, P["solve"]), x, f)
        if it + 1 < n_sq:
            f = [_mm(f_, f_, P["solve"]) for f_ in f]
    zero = jnp.zeros((H2, LANES), F32)
    z = each(lambda x_, v_: jnp.concatenate([x_, jnp.concatenate([zero, v_], axis=1)], axis=0), x, vx)
    w1 = each(lambda m_, z_: _mm(m_, z_, P["out"]), m_r, z)
    w2 = each(lambda b_, k_, z_: _mm(jnp.concatenate([b_, k_], axis=0), z_, P["out"], _dot_tn), bpx, kpx, z)
    ys = []
    for hp in pairs:
        ra = rx[hp] + w1[hp][:, :LANES]
        ra = ra[:C] + ra[C:]
        y0 = w1[hp][:C, LANES:] + w1[hp][C:, LANES:]
        mt = w2[hp][:, :LANES] + jnp.where(eye, pc[:, sls[hp]], 0.0)
        s0 = s_ref[hp]
        ys.append(_mm(ra, s0, P["state"]) + y0)
        s_ref[hp] = _mm(mt, s0, P["state"]) + w2[hp][:, LANES:]
    yc = [y_ - _mm_bf16_rhs(y_, bdm_ref[...]) for y_ in ys]
    ys = [c_ * lax.rsqrt(_mm_bf16_rhs(c_ * c_, bdm_ref[...]) + GN_EPS) for c_ in yc]

    yn = jnp.concatenate(ys, axis=1) * gnw_ref[...] + gnb_ref[...]
    y_ref[0] = ((yn + bonus) * gate).astype(y_ref.dtype)


def _rwkv(prw, mu, w0, w2p, a0, a2p, g2, k_k, k_a, r_k, gnw, gnb):
    bsz, s, _ = prw.shape
    row = lambda t: t.reshape(1, -1)
    const = lambda shp: pl.BlockSpec(shp, lambda bb, c: (0, 0))
    vec = const((1, RWKV_CH))
    lora = const((LORA_W + LORA_A, RWKV_CH))
    return pl.pallas_call(
        _rwkv_kernel,
        grid=(bsz, s // CHUNK_T),
        in_specs=[pl.BlockSpec((1, CHUNK_T, RWKV_PROJ), lambda bb, c: (bb, c, 0)), const((1, RWKV_PROJ)),
                  vec, lora, vec, lora, const((LORA_G, RWKV_CH)), vec, vec, vec,
                  const((LANES, LANES)), const((LANES, LANES)), vec, vec],
        out_specs=pl.BlockSpec((1, CHUNK_T, RWKV_CH), lambda bb, c: (bb, c, 0)),
        out_shape=jax.ShapeDtypeStruct((bsz, s, RWKV_CH), BF16),
        scratch_shapes=[pltpu.VMEM((RWKV_CH // LANES, LANES, LANES), F32), pltpu.VMEM((1, RWKV_PROJ), F32)],
        compiler_params=_cparams(("parallel", "arbitrary")),
        name="rwkv",
    )(prw, row(mu), row(w0), w2p, row(a0), a2p, g2, row(k_k), row(k_a), row(r_k),
      _block_diag(LANES, HEAD, 1.0).astype(BF16), _block_diag(LANES, HEAD, 1.0 / HEAD).astype(BF16),
      row(gnw), row(gnb))


def _out_proj_kernel(yc_ref, yr_ref, wo_ref, x_ref, gt_ref, g_ref, sh_ref, sc_ref, x1_o, u2_o):
    mix = _dot(yc_ref[0], wo_ref[0:CONV_CH, :]) + _dot(yr_ref[0], wo_ref[CONV_CH:, :])
    x1 = x_ref[0] + gt_ref[0] * mix
    x1_o[0] = x1
    u2 = _rms(x1, g_ref[...]) * (1.0 + sc_ref[0]) + sh_ref[0]
    for c in range(CHUNKS):
        u2_o[0, :, c, :] = u2[:, c * LANES:(c + 1) * LANES]


def _out_proj(yc, yr, wo_bf16, x, gt, g, sh, sc, tm=256):
    bsz, s, _ = x.shape
    vec = pl.BlockSpec((1, 1, D_MODEL), lambda b, i: (b, 0, 0))
    tile = lambda w: pl.BlockSpec((1, tm, w), lambda b, i: (b, i, 0))
    return pl.pallas_call(
        _out_proj_kernel,
        grid=(bsz, s // tm),
        in_specs=[tile(CONV_CH), tile(RWKV_CH), pl.BlockSpec((D_MODEL, D_MODEL), lambda b, i: (0, 0)),
                  tile(D_MODEL), vec, pl.BlockSpec((1, D_MODEL), lambda b, i: (0, 0)), vec, vec],
        out_specs=[tile(D_MODEL), pl.BlockSpec((1, tm, CHUNKS, LANES), lambda b, i: (b, i, 0, 0))],
        out_shape=[jax.ShapeDtypeStruct((bsz, s, D_MODEL), F32),
                   jax.ShapeDtypeStruct((bsz, s, CHUNKS, LANES), F32)],
        compiler_params=_cparams(("parallel", "parallel")),
        name="out_proj",
    )(yc, yr, wo_bf16, x, gt, g.reshape(1, D_MODEL), sh, sc)


HEADS_STEP = 4


def _topk_rows(ss, k):
    n, t = ss[0].shape
    rids = [lax.broadcasted_iota(I32, (8, t), 0) + r0 for r0 in range(0, n, 8)]
    ss = [[s[r0:r0 + 8] for r0 in range(0, n, 8)] for s in ss]
    vals, ids = [[] for _ in ss], [[] for _ in ss]
    for _ in range(k):
        tops = [_max_with_tag(s, [rids]) for s in ss]
        ss = [[jnp.where(r == j, -jnp.inf, c) for c, r in zip(s, rids)] for s, (_, (j,)) in zip(ss, tops)]
        for v, i, (m, (j,)) in zip(vals, ids, tops):
            v.append(m)
            i.append(j)
    return [(jnp.concatenate(v, axis=0), jnp.concatenate(i, axis=0)) for v, i in zip(vals, ids)]


def _max_with_tag(chunks, tags):
    vals, tags = list(chunks), [list(tg) for tg in tags]
    while len(vals) > 1:
        nv, nt = [], [[] for _ in tags]
        for a in range(0, len(vals) - 1, 2):
            first = vals[a] >= vals[a + 1]
            nv.append(jnp.maximum(vals[a], vals[a + 1]))
            for dst, tg in zip(nt, tags):
                dst.append(jnp.where(first, tg[a], tg[a + 1]))
        if len(vals) % 2:
            nv.append(vals[-1])
            for dst, tg in zip(nt, tags):
                dst.append(tg[-1])
        vals, tags = nv, nt
    v8 = vals[0]
    m = jnp.max(v8, axis=0, keepdims=True)
    big = jnp.iinfo(jnp.int32).max
    key = jnp.min(jnp.where(v8 == m, tags[0][0], big), axis=0, keepdims=True)
    out = [key]
    for tg in tags[1:]:
        out.append(jnp.sum(jnp.where(tags[0][0] == key, tg[0], 0), axis=0, keepdims=True))
    return m, out


def _route_kernel(u_ref, wq_ref, keys_ref, idx_o, gate_o, q_ref, idx_s, gate_s):
    tm = u_ref.shape[0]
    u = jnp.concatenate([u_ref[:, c, :] for c in range(CHUNKS)], axis=1)
    q = _dot(u.astype(BF16), wq_ref[...])
    for j in range(2 * PEER_HEADS):
        q_ref[j] = q[:, j * LANES:(j + 1) * LANES]
    K = PEER_TOPK
    tt = LANES
    tiles = range(tm // tt)
    row8 = lax.broadcasted_iota(I32, (8, tt), 0)

    def heads(i, carry):
        probs = [(i * HEADS_STEP + dh, lt) for dh in range(HEADS_STEP) for lt in tiles]
        scores = [_mm(keys_ref[2 * h + p], q_ref[2 * h + p, lt * tt:(lt + 1) * tt, :], 3, _dot_nt)
                  for h, lt in probs for p in range(2)]
        tops = _topk_rows(scores, K)
        ss, flats, eids = [], [], []
        for n_ in range(len(probs)):
            (av, ai), (bv, bi) = tops[2 * n_], tops[2 * n_ + 1]
            cs = [av[0:1] + bv[0:8], av[0:1] + bv[8:16]]
            cf = [row8, row8 + 8]
            ce = [ai[0:1] * PEER_NKEYS + bi[0:8], ai[0:1] * PEER_NKEYS + bi[8:16]]
            for x in range(1, 8):
                cs.append(jnp.where(row8 < K // (x + 1), av[x:x + 1] + bv[0:8], -jnp.inf))
                cf.append(x * K + row8)
                ce.append(ai[x:x + 1] * PEER_NKEYS + bi[0:8])
            cs.append(av[8:16] + bv[0:1])
            cf.append((row8 + 8) * K)
            ce.append(ai[8:16] * PEER_NKEYS + bi[0:1])
            ss.append(cs)
            flats.append(cf)
            eids.append(ce)
        best, experts = [[] for _ in probs], [[] for _ in probs]
        for _ in range(K):
            tops2 = [_max_with_tag(s, [fl, ei]) for s, fl, ei in zip(ss, flats, eids)]
            for n_, (m, (f, e)) in enumerate(tops2):
                best[n_].append(m)
                experts[n_].append(e)
            ss = [[jnp.where(fc == f, -jnp.inf, c) for c, fc in zip(s, fl)]
                  for s, fl, (_, (f, _e)) in zip(ss, flats, tops2)]
        for n_, (h, lt) in enumerate(probs):
            b = jnp.concatenate(best[n_], axis=0)
            e = jnp.exp(b - b[0:1])
            gate_s[lt, h] = e / jnp.sum(e, axis=0, keepdims=True)
            idx_s[lt, h] = jnp.concatenate(experts[n_], axis=0) * WORD_ROWS
        return carry

    lax.fori_loop(0, PEER_HEADS // HEADS_STEP, heads, 0)
    for lt in tiles:
        idx_o[lt * tt:(lt + 1) * tt, :] = jnp.transpose(idx_s[lt].reshape(NPAIR, tt))
        gate_o[lt * tt:(lt + 1) * tt, :] = jnp.transpose(gate_s[lt].reshape(NPAIR, tt))


def _route(u2, wq_bf16, keys, tm=256):
    n = u2.shape[0]
    oblk = pl.BlockSpec((tm, NPAIR), lambda i: (i, 0))
    return pl.pallas_call(
        _route_kernel,
        grid=(n // tm,),
        in_specs=[pl.BlockSpec((tm, CHUNKS, LANES), lambda i: (i, 0, 0)),
                  pl.BlockSpec((D_MODEL, PEER_HEADS * PEER_DQ), lambda i: (0, 0)),
                  pl.BlockSpec((2 * PEER_HEADS, PEER_NKEYS, PEER_DQ // 2), lambda i: (0, 0, 0))],
        out_specs=[oblk, oblk],
        out_shape=[jax.ShapeDtypeStruct((n, NPAIR), I32), jax.ShapeDtypeStruct((n, NPAIR), F32)],
        scratch_shapes=[pltpu.VMEM((2 * PEER_HEADS, tm, LANES), F32),
                        pltpu.VMEM((tm // LANES, PEER_HEADS, PEER_TOPK, LANES), I32),
                        pltpu.VMEM((tm // LANES, PEER_HEADS, PEER_TOPK, LANES), F32)],
        compiler_params=_cparams(("parallel",)),
        name="route",
    )(u2, wq_bf16, keys)


TOK_UNROLL = 16


def _pack_kernel(t_ref, o_ref):
    te = t_ref.shape[0]
    bits = lambda v: lax.bitcast_convert_type(v.astype(BF16).astype(F32), I32)
    for s in range(WORD_ROWS):
        lo = bits(t_ref[:, (2 * s) * LANES:(2 * s + 1) * LANES])
        hi = bits(t_ref[:, (2 * s + 1) * LANES:(2 * s + 2) * LANES])
        o_ref[pl.ds(s, te, stride=WORD_ROWS), :] = (hi & -65536) | lax.shift_right_logical(lo, 16)


def _pack_table(t, te=512):
    e = t.shape[0]
    return pl.pallas_call(
        _pack_kernel,
        grid=(e // te,),
        in_specs=[pl.BlockSpec((te, D_MODEL), lambda i: (i, 0))],
        out_specs=pl.BlockSpec((te * WORD_ROWS, LANES), lambda i: (i, 0)),
        out_shape=jax.ShapeDtypeStruct((e * WORD_ROWS, LANES), I32),
        compiler_params=_cparams(("parallel",)),
        name="pack_table",
    )(t)


def _gather_rows(idx_ref, t0, tab_ref, g_ref):
    for u in range(TOK_UNROLL):
        idx_row = idx_ref.at[t0 + u]
        for kk in range(NPAIR):
            row = pl.multiple_of(idx_row[kk], WORD_ROWS)
            g_ref[u, kk * WORD_ROWS:(kk + 1) * WORD_ROWS, :] = tab_ref[pl.ds(row, WORD_ROWS), :]


def _peer_u_kernel(idx_ref, x_ref, gate_ref, keep_ref, sel_ref, tab_ref, w_o, g_ref, p_ref):
    tp = x_ref.shape[0]

    def tokens(i, carry):
        t0 = pl.multiple_of(i * TOK_UNROLL, TOK_UNROLL)
        _gather_rows(idx_ref, t0, tab_ref, g_ref)
        xs = x_ref[pl.ds(t0, TOK_UNROLL)]
        parts = []
        for u in range(TOK_UNROLL):
            g = pltpu.bitcast(g_ref[u], BF16)
            xc = xs[u]
            xt = jnp.concatenate([xc] * (LANES // CHUNKS), axis=0).astype(BF16)
            r = _dot_nt(g, xt) * keep_ref[...]
            parts.append(jnp.sum(r.reshape(NPAIR // 16, 16 * CHUNKS, LANES), axis=1))
        p_ref[pl.ds(pl.multiple_of(t0 * 8, 8 * TOK_UNROLL), 8 * TOK_UNROLL), :] = jnp.concatenate(parts, axis=0)
        return carry

    lax.fori_loop(0, tp // TOK_UNROLL, tokens, 0)
    z = _mm_bf16_rhs(p_ref[...], sel_ref[...], 3)
    grp = lax.broadcasted_iota(I32, z.shape, 0) % 8 == lax.broadcasted_iota(I32, z.shape, 1) // 16
    h = jnp.sum(jnp.where(grp, z, 0.0).reshape(tp, 8, LANES), axis=1)
    w_o[...] = gate_ref[...] * (0.5 * h * (1.0 + lax.erf(h * (2.0 ** -0.5))))


def _peer_u(idx, x3, gate, tab, start, n, tp=128):
    o = start // tp
    rows = jnp.arange(NPAIR * CHUNKS)[:, None]
    lanes = jnp.arange(LANES)[None, :]
    keep = ((lanes % CHUNKS == rows % CHUNKS) & (lanes // CHUNKS == (rows // CHUNKS) % 16)).astype(F32)
    sel = (jnp.arange(LANES)[:, None] // CHUNKS == jnp.arange(LANES)[None, :] % 16).astype(BF16)
    return pl.pallas_call(
        _peer_u_kernel,
        grid=(n // tp,),
        in_specs=[pl.BlockSpec((tp, NPAIR), lambda i: (i + o, 0), memory_space=pltpu.SMEM),
                  pl.BlockSpec((tp, CHUNKS, LANES), lambda i: (i + o, 0, 0)),
                  pl.BlockSpec((tp, NPAIR), lambda i: (i + o, 0)),
                  pl.BlockSpec((NPAIR * CHUNKS, LANES), lambda i: (0, 0)),
                  pl.BlockSpec((LANES, LANES), lambda i: (0, 0)),
                  pl.BlockSpec(memory_space=pltpu.VMEM)],
        out_specs=pl.BlockSpec((tp, NPAIR), lambda i: (i, 0)),
        out_shape=jax.ShapeDtypeStruct((n, NPAIR), F32),
        scratch_shapes=[pltpu.VMEM((TOK_UNROLL, NPAIR * WORD_ROWS, LANES), I32),
                        pltpu.VMEM((tp * 8, LANES), F32)],
        compiler_params=_cparams(("arbitrary",)),
        name="peer_u",
    )(idx, x3, gate, keep, sel, tab)


def _peer_v_kernel(idx_ref, w_ref, rep_ref, diag_ref, tab_ref, o_ref, g_ref, wx_ref):
    tp = w_ref.shape[0]
    wx_ref[...] = _mm_bf16_rhs(w_ref[...], rep_ref[...], 3)

    def tokens(i, carry):
        t0 = pl.multiple_of(i * TOK_UNROLL, TOK_UNROLL)
        _gather_rows(idx_ref, t0, tab_ref, g_ref)
        wx = wx_ref[pl.ds(t0, TOK_UNROLL), :]
        outs = []
        for u in range(TOK_UNROLL):
            g = pltpu.bitcast(g_ref[u], BF16)
            wm = (wx[u:u + 1, :] * diag_ref[...]).astype(BF16)
            outs.append(_dot(wm, g))
        o_ref[pl.ds(t0, TOK_UNROLL)] = jnp.stack(outs, axis=0)
        return carry

    lax.fori_loop(0, tp // TOK_UNROLL, tokens, 0)


def _peer_v(idx, w, tab, start, tp=128):
    n = w.shape[0]
    o = start // tp
    rep = jnp.repeat(jnp.eye(NPAIR, dtype=BF16), CHUNKS, axis=1)
    diag = (jnp.arange(CHUNKS)[:, None] == jnp.arange(NPAIR * CHUNKS)[None, :] % CHUNKS).astype(F32)
    return pl.pallas_call(
        _peer_v_kernel,
        grid=(n // tp,),
        in_specs=[pl.BlockSpec((tp, NPAIR), lambda i: (i + o, 0), memory_space=pltpu.SMEM),
                  pl.BlockSpec((tp, NPAIR), lambda i: (i, 0)),
                  pl.BlockSpec((NPAIR, NPAIR * CHUNKS), lambda i: (0, 0)),
                  pl.BlockSpec((CHUNKS, NPAIR * CHUNKS), lambda i: (0, 0)),
                  pl.BlockSpec(memory_space=pltpu.VMEM)],
        out_specs=pl.BlockSpec((tp, CHUNKS, LANES), lambda i: (i, 0, 0)),
        out_shape=jax.ShapeDtypeStruct((n, CHUNKS, LANES), F32),
        scratch_shapes=[pltpu.VMEM((TOK_UNROLL, NPAIR * WORD_ROWS, LANES), I32),
                        pltpu.VMEM((tp, NPAIR * CHUNKS), F32)],
        compiler_params=_cparams(("arbitrary",)),
        name="peer_v",
    )(idx, w, rep, diag, tab)


SC_LANES = 16
SC_WORKERS = 32
SC_ROWS = 64
SC_TOKENS = 16
SC_SHARES, SC_SHARE_DEN = (4, 7, 13, 19), 64
SC_COLS = 8


def _peer_v_sc(ids, wts, packed, start):
    m = wts.shape[0]
    per_worker = m // SC_WORKERS
    blocks = NPAIR // SC_ROWS
    words = D_MODEL // 2
    table = packed.reshape(packed.shape[0] // WORD_ROWS, words)
    ids_b = ids.reshape(ids.shape[0] * blocks, SC_ROWS)
    wts_f = wts.reshape(m * NPAIR)
    mesh = plsc.VectorSubcoreMesh(core_axis_name="c", subcore_axis_name="s")

    @functools.partial(
        pl.kernel, mesh=mesh, out_type=jax.ShapeDtypeStruct((m, D_MODEL), F32),
        scratch_types=[pltpu.VMEM((SC_TOKENS * blocks, SC_ROWS), I32), pltpu.VMEM((SC_TOKENS * NPAIR,), F32),
                       pltpu.VMEM((2, SC_ROWS, words), I32), pltpu.VMEM((D_MODEL,), F32),
                       pltpu.SemaphoreType.DMA((2,))],
        compiler_params=pltpu.CompilerParams(needs_layout_passes=False),
        name="peer_v_sc")
    def run(tab_hbm, ids_hbm, w_hbm, out_hbm, ids_v, w_v, rows_v, acc_v, sems):
        wid = lax.axis_index("s") * 2 + lax.axis_index("c")
        base = wid * per_worker
        zero = jnp.zeros((SC_LANES,), F32)

        def gather(u, b):
            return pltpu.make_async_copy(tab_hbm.at[ids_v.at[u * blocks + b]], rows_v.at[b % 2], sems.at[b % 2])

        @pl.loop(0, per_worker // SC_TOKENS)
        def _(tb):
            t0 = base + tb * SC_TOKENS
            pltpu.sync_copy(ids_hbm.at[pl.ds((start + t0) * blocks, SC_TOKENS * blocks)], ids_v)
            pltpu.sync_copy(w_hbm.at[pl.ds(t0 * NPAIR, SC_TOKENS * NPAIR)], w_v)

            @pl.loop(0, SC_TOKENS)
            def _(u):
                for j in range(D_MODEL // SC_LANES):
                    acc_v[pl.ds(j * SC_LANES, SC_LANES)] = zero
                gather(u, 0).start()
                for b in range(blocks):
                    if b + 1 < blocks:
                        gather(u, b + 1).start()
                    gather(u, b).wait()

                    for c0 in range(0, words // SC_LANES, SC_COLS):
                        def row(r, accs, b=b, c0=c0):
                            wk = plsc.load_gather(w_v, [jnp.full((SC_LANES,), u * NPAIR + b * SC_ROWS + r, I32)])
                            out = []
                            for j in range(SC_COLS):
                                w32 = rows_v[b % 2, r, pl.ds((c0 + j) * SC_LANES, SC_LANES)]
                                lo = lax.bitcast_convert_type(lax.shift_left(w32, 16), F32)
                                hi = lax.bitcast_convert_type(w32 & -65536, F32)
                                out += [accs[2 * j] + wk * lo, accs[2 * j + 1] + wk * hi]
                            return tuple(out)
                        accs = lax.fori_loop(0, SC_ROWS, row, (zero,) * (2 * SC_COLS))
                        for j in range(SC_COLS):
                            s_, l0 = divmod((c0 + j) * SC_LANES, LANES)
                            plsc.addupdate(acc_v.at[pl.ds(2 * s_ * LANES + l0, SC_LANES)], accs[2 * j])
                            plsc.addupdate(acc_v.at[pl.ds((2 * s_ + 1) * LANES + l0, SC_LANES)], accs[2 * j + 1])
                pltpu.sync_copy(acc_v, out_hbm.at[t0 + u])

    return run(table, ids_b, wts_f)


def _final_kernel(x1_ref, *refs, ends):
    *sc_refs, ptc_ref, gt_ref, g_ref, o_ref = refs
    i = pl.program_id(0)
    p = jnp.concatenate([ptc_ref[:, c, :] for c in range(CHUNKS)], axis=1)
    for ref, end in reversed(list(zip(sc_refs, ends))):
        p = jnp.where(i < end, ref[...], p)
    o_ref[...] = _rms(x1_ref[...] + gt_ref[0] * p, g_ref[...])


def _final(x1, sc_parts, peer_tc, gt, g, tm=512):
    n = x1.shape[0]
    per_seq = n // gt.shape[0] // tm
    sizes = [p.shape[0] // tm for p in sc_parts]
    ends = [sum(sizes[:j + 1]) for j in range(len(sizes))]
    starts = [e - k for e, k in zip(ends, sizes)]
    tile = pl.BlockSpec((tm, D_MODEL), lambda i: (i, 0))
    part = lambda s0, k: pl.BlockSpec((tm, D_MODEL), lambda i: (jnp.clip(i - s0, 0, k - 1), 0))
    return pl.pallas_call(
        functools.partial(_final_kernel, ends=tuple(ends)),
        grid=(n // tm,),
        in_specs=[tile] + [part(s0, k) for s0, k in zip(starts, sizes)] + [
            pl.BlockSpec((tm, CHUNKS, LANES), lambda i: (jnp.maximum(i - ends[-1], 0), 0, 0)),
            pl.BlockSpec((1, 1, D_MODEL), lambda i: (i // per_seq, 0, 0)),
            pl.BlockSpec((1, D_MODEL), lambda i: (0, 0))],
        out_specs=tile,
        out_shape=jax.ShapeDtypeStruct((n, D_MODEL), F32),
        compiler_params=_cparams(("parallel",)),
        name="final",
    )(x1, *sc_parts, peer_tc, gt, g.reshape(1, D_MODEL))


def _block_diag(width, group, value):
    i = jnp.arange(width) // group
    return jnp.where(i[:, None] == i[None, :], value, 0.0).astype(F32)


def _layer(x, mod, final_g, norm_mix_g, w_in, conv_dw_w, conv_dw_b, conv_ln_w, conv_ln_b, rwkv_mu, rwkv_w0, rwkv_w2,
           rwkv_a0, rwkv_a2, rwkv_g2, rwkv_k_k, rwkv_k_a, rwkv_r_k, rwkv_gn_w, rwkv_gn_b, w_out, norm_ffn_g,
           peer_w_q, peer_sub_keys, peer_u, peer_v):
    bsz, s, _ = x.shape
    sh_mix, sc_mix, gt_mix, sh_ffn, sc_ffn, gt_ffn = (
        mod[:, i * D_MODEL:(i + 1) * D_MODEL].reshape(bsz, 1, D_MODEL) for i in range(6))

    yglu, prw = _in_proj(x, sh_mix, sc_mix, norm_mix_g, w_in.astype(BF16))
    y_conv = _conv(yglu, conv_dw_w, conv_dw_b, conv_ln_w, conv_ln_b)

    zpad = jnp.zeros((LORA_W, RWKV_CH), F32)
    y_rwkv = _rwkv(prw, rwkv_mu, rwkv_w0, jnp.concatenate([rwkv_w2, zpad], axis=0), rwkv_a0,
                   jnp.concatenate([zpad, rwkv_a2], axis=0), rwkv_g2, rwkv_k_k, rwkv_k_a, rwkv_r_k.reshape(-1),
                   rwkv_gn_w, rwkv_gn_b)

    x1, u2 = _out_proj(y_conv, y_rwkv, w_out.astype(BF16), x, gt_mix, norm_ffn_g, sh_ffn, sc_ffn)

    n = bsz * s
    u3 = u2.reshape(n, CHUNKS, LANES)
    keys = peer_sub_keys.reshape(2 * PEER_HEADS, PEER_NKEYS, PEER_DQ // 2)
    idx, gate = _route(u3, peer_w_q.astype(BF16), keys)
    tab_u, tab_v = _pack_table(peer_u), _pack_table(peer_v)
    ids = lax.shift_right_logical(idx, 2)
    start, sc_parts = 0, []
    for share in SC_SHARES:
        m = n * share // SC_SHARE_DEN
        sc_parts.append(_peer_v_sc(ids, _peer_u(idx, u3, gate, tab_u, start, m), tab_v, start))
        start += m
    peer_tc = _peer_v(idx, _peer_u(idx, u3, gate, tab_u, start, n - start), tab_v, start)
    return _final(x1.reshape(n, D_MODEL), sc_parts, peer_tc, gt_ffn, final_g).reshape(bsz, s, D_MODEL)


def kernel(x, c, ada_w, ada_b, norm_mix_g, w_in, conv_dw_w, conv_dw_b, conv_ln_w, conv_ln_b, rwkv_mu, rwkv_w0,
           rwkv_w2, rwkv_a0, rwkv_a2, rwkv_g2, rwkv_k_k, rwkv_k_a, rwkv_r_k, rwkv_gn_w, rwkv_gn_b, w_out,
           norm_ffn_g, peer_w_q, peer_sub_keys, peer_u, peer_v, final_g):
    depth = ada_w.shape[0]
    assert depth == 1, "one layer: the final norm is fused into the last layer's residual"
    mod = _mod(c, ada_w[0], ada_b[0])
    return _layer(x, mod, final_g, norm_mix_g[0], w_in[0], conv_dw_w[0], conv_dw_b[0], conv_ln_w[0],
                        conv_ln_b[0], rwkv_mu[0], rwkv_w0[0], rwkv_w2[0], rwkv_a0[0], rwkv_a2[0], rwkv_g2[0],
                        rwkv_k_k[0], rwkv_k_a[0], rwkv_r_k[0], rwkv_gn_w[0], rwkv_gn_b[0], w_out[0],
                        norm_ffn_g[0], peer_w_q[0], peer_sub_keys[0], peer_u[0], peer_v[0])
```

```python
import functools

import jax
import jax.numpy as jnp
from jax import lax
from jax.experimental import pallas as pl
from jax.experimental.pallas import tpu as pltpu
from jax.experimental.pallas import tpu_sc as plsc

F32 = jnp.float32
BF16 = jnp.bfloat16
I32 = jnp.int32
HI = lax.Precision.HIGHEST

D_MODEL = 1024
CONV_CH = 512
RWKV_CH = 512
HEAD = 64
CONV_WIDTH = 31
LORA_W = 64
LORA_A = 64
LORA_G = 128
RWKV_PROJ = 3 * RWKV_CH + LORA_W + LORA_A + LORA_G
IN_PROJ = 2 * CONV_CH + RWKV_PROJ
PEER_HEADS = 8
PEER_NKEYS = 128
PEER_EXPERTS = PEER_NKEYS * PEER_NKEYS
PEER_DQ = 256
PEER_TOPK = 16
NPAIR = PEER_HEADS * PEER_TOPK
RMS_EPS = 1e-6
LN_EPS = 1e-5
GN_EPS = 64e-5

LANES = 128
CHUNKS = D_MODEL // LANES
WORD_ROWS = CHUNKS // 2
CHUNK_T = 64
VMEM_LIMIT = 56 * 1024 * 1024


def _cparams(sem, vmem=None):
    return pltpu.CompilerParams(dimension_semantics=sem, vmem_limit_bytes=vmem or VMEM_LIMIT)


def _dot(a, b, precision=None):
    return jnp.dot(a, b, precision=precision, preferred_element_type=F32)


def _dot_nt(a, b, precision=None):
    return lax.dot_general(a, b, (((1,), (1,)), ((), ())), precision=precision, preferred_element_type=F32)


def _dot_tn(a, b, precision=None):
    return lax.dot_general(a, b, (((0,), (0,)), ((), ())), precision=precision, preferred_element_type=F32)


def _split(a):
    hi = a.astype(BF16)
    return hi, (a - hi.astype(F32)).astype(BF16)


def _mm(a, b, passes, dot=_dot):
    if passes == 6:
        return dot(a, b, HI)
    if passes == 1:
        return dot(a.astype(BF16), b.astype(BF16))
    ka = 0 if dot is _dot_tn else 1
    kb = 1 if dot is _dot_nt else 0
    ah, al = _split(a)
    bh, bl = _split(b)
    return dot(jnp.concatenate([ah, ah, al], axis=ka), jnp.concatenate([bh, bl, bh], axis=kb))


def _mm_bf16_rhs(a, b_bf16, terms=2):
    parts = []
    for _ in range(terms):
        p = a.astype(BF16)
        parts.append(p)
        a = a - p.astype(F32)
    return _dot(jnp.concatenate(parts, axis=1), jnp.concatenate([b_bf16] * terms, axis=0))


def _head_sums(x, bd_bf16):
    return jnp.concatenate([_mm_bf16_rhs(x[:, i * LANES:(i + 1) * LANES], bd_bf16, 3)
                            for i in range(x.shape[1] // LANES)], axis=1)


def _rms(x, g):
    return x * lax.rsqrt(jnp.mean(x * x, axis=-1, keepdims=True) + RMS_EPS) * g


def _mod_kernel(c_ref, w_ref, b_ref, o_ref):
    c = c_ref[...]
    o_ref[...] = _dot(c * jax.nn.sigmoid(c), w_ref[...], HI) + b_ref[...]


def _mod(c, w, b):
    bsz = c.shape[0]
    n = w.shape[1]
    tn = 1024
    return pl.pallas_call(
        _mod_kernel,
        grid=(n // tn,),
        in_specs=[pl.BlockSpec((bsz, D_MODEL), lambda j: (0, 0)),
                  pl.BlockSpec((D_MODEL, tn), lambda j: (0, j)),
                  pl.BlockSpec((1, tn), lambda j: (0, j))],
        out_specs=pl.BlockSpec((bsz, tn), lambda j: (0, j)),
        out_shape=jax.ShapeDtypeStruct((bsz, n), F32),
        compiler_params=_cparams(("parallel",)),
    )(c, w, b.reshape(1, n))


def _in_proj_kernel(x_ref, sh_ref, sc_ref, g_ref, w_ref, yglu_ref, prw_ref):
    u = _rms(x_ref[0], g_ref[...]) * (1.0 + sc_ref[0]) + sh_ref[0]
    p = _dot(u.astype(BF16), w_ref[...])
    yglu_ref[0] = p[:, :CONV_CH] * jax.nn.sigmoid(p[:, CONV_CH:2 * CONV_CH])
    prw_ref[0] = p[:, 2 * CONV_CH:]


def _in_proj(x, sh, sc, g, w_bf16, b0, tm=256):
    bsz, s = sh.shape[0], x.shape[1]
    vec = pl.BlockSpec((1, 1, D_MODEL), lambda b, i: (b, 0, 0))
    return pl.pallas_call(
        _in_proj_kernel,
        grid=(bsz, s // tm),
        in_specs=[pl.BlockSpec((1, tm, D_MODEL), lambda b, i: (b + b0, i, 0)), vec, vec,
                  pl.BlockSpec((1, D_MODEL), lambda b, i: (0, 0)),
                  pl.BlockSpec((D_MODEL, IN_PROJ), lambda b, i: (0, 0))],
        out_specs=[pl.BlockSpec((1, tm, CONV_CH), lambda b, i: (b, i, 0)),
                   pl.BlockSpec((1, tm, RWKV_PROJ), lambda b, i: (b, i, 0))],
        out_shape=[jax.ShapeDtypeStruct((bsz, s, CONV_CH), F32),
                   jax.ShapeDtypeStruct((bsz, s, RWKV_PROJ), F32)],
        compiler_params=_cparams(("parallel", "parallel")),
    )(x, sh, sc, g.reshape(1, D_MODEL), w_bf16)


CONV_HALO = 32
CONV_ROWS = 64


def _conv_kernel(cur_ref, prev_ref, w_ref, b_ref, lnw_ref, lnb_ref, o_ref, pad_ref):
    tc = cur_ref.shape[1]
    pad_ref[0, 0:CONV_HALO, :] = jnp.where(pl.program_id(1) > 0, prev_ref[0], 0.0)
    pad_ref[0, CONV_HALO:CONV_HALO + tc, :] = cur_ref[0]
    span = CONV_HALO + tc - 8
    for r in range(1, 8):
        pad_ref[r, 0:span, :] = pad_ref[0, r:r + span, :]
    off = CONV_HALO - (CONV_WIDTH - 1)
    for r0 in range(0, tc, CONV_ROWS):
        acc = jnp.zeros((CONV_ROWS, CONV_CH), F32)
        for j in range(CONV_WIDTH):
            q, r = divmod(off + j, 8)
            acc = acc + w_ref[j:j + 1, :] * pad_ref[r, r0 + 8 * q:r0 + 8 * q + CONV_ROWS, :]
        y = acc + b_ref[...]
        mu = jnp.mean(y, axis=-1, keepdims=True)
        yc = y - mu
        var = jnp.mean(yc * yc, axis=-1, keepdims=True)
        yn = yc * lax.rsqrt(var + LN_EPS) * lnw_ref[...] + lnb_ref[...]
        o_ref[0, r0:r0 + CONV_ROWS, :] = (yn * jax.nn.sigmoid(yn)).astype(o_ref.dtype)


def _conv(yglu, w, b, lnw, lnb, tc=256):
    bsz, s, _ = yglu.shape
    hb = tc // CONV_HALO
    row = lambda a: a.reshape(1, CONV_CH)
    const = lambda shp: pl.BlockSpec(shp, lambda bb, i: (0, 0))
    return pl.pallas_call(
        _conv_kernel,
        grid=(bsz, s // tc),
        in_specs=[pl.BlockSpec((1, tc, CONV_CH), lambda bb, i: (bb, i, 0)),
                  pl.BlockSpec((1, CONV_HALO, CONV_CH), lambda bb, i: (bb, jnp.maximum(i * hb - 1, 0), 0)),
                  const((CONV_WIDTH, CONV_CH)), const((1, CONV_CH)), const((1, CONV_CH)), const((1, CONV_CH))],
        out_specs=pl.BlockSpec((1, tc, CONV_CH), lambda bb, i: (bb, i, 0)),
        out_shape=jax.ShapeDtypeStruct((bsz, s, CONV_CH), BF16),
        scratch_shapes=[pltpu.VMEM((8, CONV_HALO + tc, CONV_CH), F32)],
        compiler_params=_cparams(("parallel", "parallel")),
    )(yglu, yglu, w, row(b), row(lnw), row(lnb))


def _softplus(z):
    return jnp.maximum(z, 0.0) + jnp.log1p(jnp.exp(-jnp.abs(z)))


def _rwkv_features(cur, prow, mu, w0, w2p, a0, a2p, g2, k_k, k_a, r_k, bd):
    rows = lax.broadcasted_iota(I32, cur.shape, 0)
    prev = jnp.where(rows == 0, prow, pltpu.roll(cur, 1, axis=0))
    xs = cur + mu * (prev - cur)
    r = xs[:, 0:RWKV_CH]
    k = xs[:, RWKV_CH:2 * RWKV_CH]
    v = xs[:, 2 * RWKV_CH:3 * RWKV_CH]
    wa = xs[:, 3 * RWKV_CH:3 * RWKV_CH + LORA_W + LORA_A]
    gl = xs[:, 3 * RWKV_CH + LORA_W + LORA_A:]
    w = -_softplus(-(w0 + _mm(jnp.tanh(wa), w2p, 3))) - 0.5
    a = jax.nn.sigmoid(a0 + _mm(wa, a2p, 3))
    g = _mm(jax.nn.sigmoid(gl), g2, 3)
    kk = k * k_k
    kkn = kk / jnp.maximum(jnp.sqrt(_head_sums(kk * kk, bd)), 1e-12)
    k2 = k * (1.0 + (a - 1.0) * k_a)
    bonus = _head_sums(r * k2 * r_k, bd) * v
    return r, k2, v, -jnp.exp(w), -kkn, kkn * a, g, bonus


RWKV_PASSES = {"gram": 3, "lakv": 3, "solve": 3, "out": 3, "state": 3}
SOLVE_BLK = 16


def _expand(x, lane_lo):
    return jnp.concatenate([jnp.where(lane_lo, x, 0.0), jnp.where(lane_lo, 0.0, x)], axis=0)


def _rwkv_kernel(p_ref, mu_ref, w0_ref, w2_ref, a0_ref, a2_ref, g2_ref, kk_ref, ka_ref, rk_ref, bd_ref,
                 bdm_ref, gnw_ref, gnb_ref, y_ref, s_ref, last_ref):
    C = CHUNK_T
    H2 = 2 * C

    @pl.when(pl.program_id(1) == 0)
    def _():
        s_ref[...] = jnp.zeros_like(s_ref)
        last_ref[...] = jnp.zeros_like(last_ref)

    cur = p_ref[0]
    r, k, vv, lw, a, b, gate, bonus = _rwkv_features(
        cur, last_ref[...], mu_ref[...], w0_ref[...], w2_ref[...], a0_ref[...], a2_ref[...], g2_ref[...],
        kk_ref[...], ka_ref[...], rk_ref[...], bd_ref[...])
    last_ref[...] = cur[C - 1:C, :]

    tri = (lax.broadcasted_iota(I32, (C, C), 0) >= lax.broadcasted_iota(I32, (C, C), 1)).astype(F32)
    cum = _dot(tri, lw, HI)
    tot = cum[C - 1:C, :]
    e_pos = jnp.exp(cum)
    e_neg = jnp.exp(-cum)
    e_rem = jnp.exp(tot - cum)
    rt = r * e_pos
    at = a * jnp.exp(cum - lw)
    kt = k * e_neg
    bt = b * e_neg
    kp = k * e_rem
    bp = b * e_rem
    pc = jnp.exp(tot)

    lane_lo = lax.broadcasted_iota(I32, (C, LANES), 1) < HEAD
    tt = lax.broadcasted_iota(I32, (H2, H2), 0) % C
    ss = lax.broadcasted_iota(I32, (H2, H2), 1) % C
    strict = tt > ss
    incl = tt >= ss
    near = tt // SOLVE_BLK == ss // SOLVE_BLK
    eye = lax.broadcasted_iota(I32, (LANES, LANES), 0) == lax.broadcasted_iota(I32, (LANES, LANES), 1)

    P = RWKV_PASSES
    pairs = range(RWKV_CH // LANES)
    each = lambda fn, *lists: [fn(*args) for args in zip(*lists)]
    sls = [slice(hp * LANES, (hp + 1) * LANES) for hp in pairs]
    ax, rx, bx, kx, vx, bpx, kpx = ([_expand(t[:, sl], lane_lo) for sl in sls] for t in (at, rt, bt, kt, vv, bp, kp))
    gram = each(lambda a_, r_, b_, k_: _mm(jnp.concatenate([a_, r_], axis=0), jnp.concatenate([b_, k_], axis=0),
                                           P["gram"], _dot_nt), ax, rx, bx, kx)
    l_ab = [jnp.where(strict, g_[:H2, :H2], 0.0) for g_ in gram]
    l_ak = [jnp.where(strict, g_[:H2, H2:], 0.0) for g_ in gram]
    m_r = [jnp.concatenate([jnp.where(incl, g_[H2:, :H2], 0.0), jnp.where(incl, g_[H2:, H2:], 0.0)], axis=1)
           for g_ in gram]
    dg = [jnp.where(near, l_, 0.0) for l_ in l_ab]
    lakv = each(lambda l_, v_: _mm(l_, v_, P["lakv"]), l_ak, vx)
    xf = each(lambda a_, lv_, l_, d_: jnp.concatenate([a_, lv_, l_ - d_], axis=1), ax, lakv, l_ab, dg)
    n_sq = SOLVE_BLK.bit_length() - 1
    for it in range(n_sq):
        xf = each(lambda x_, d_: x_ + _mm(d_, x_, P["solve"]), xf, dg)
        if it + 1 < n_sq:
            dg = [_mm(d_, d_, P["solve"]) for d_ in dg]
    x = [x_[:, :2 * LANES] for x_ in xf]
    f = [x_[:, 2 * LANES:] for x_ in xf]
    n_sq = (C // SOLVE_BLK).bit_length() - 1
    for it in range(n_sq):
        x = each(lambda x_, f_: x_ + _mm(f_, x_, P["solve"]), x, f)
        if it + 1 < n_sq:
            f = [_mm(f_, f_, P["solve"]) for f_ in f]
    zero = jnp.zeros((H2, LANES), F32)
    z = each(lambda x_, v_: jnp.concatenate([x_, jnp.concatenate([zero, v_], axis=1)], axis=0), x, vx)
    w1 = each(lambda m_, z_: _mm(m_, z_, P["out"]), m_r, z)
    w2 = each(lambda b_, k_, z_: _mm(jnp.concatenate([b_, k_], axis=0), z_, P["out"], _dot_tn), bpx, kpx, z)
    ys = []
    for hp in pairs:
        ra = rx[hp] + w1[hp][:, :LANES]
        ra = ra[:C] + ra[C:]
        y0 = w1[hp][:C, LANES:] + w1[hp][C:, LANES:]
        mt = w2[hp][:, :LANES] + jnp.where(eye, pc[:, sls[hp]], 0.0)
        s0 = s_ref[hp]
        ys.append(_mm(ra, s0, P["state"]) + y0)
        s_ref[hp] = _mm(mt, s0, P["state"]) + w2[hp][:, LANES:]
    yc = [y_ - _mm_bf16_rhs(y_, bdm_ref[...]) for y_ in ys]
    ys = [c_ * lax.rsqrt(_mm_bf16_rhs(c_ * c_, bdm_ref[...]) + GN_EPS) for c_ in yc]

    yn = jnp.concatenate(ys, axis=1) * gnw_ref[...] + gnb_ref[...]
    y_ref[0] = ((yn + bonus) * gate).astype(y_ref.dtype)


def _rwkv(prw, mu, w0, w2p, a0, a2p, g2, k_k, k_a, r_k, gnw, gnb):
    bsz, s, _ = prw.shape
    row = lambda t: t.reshape(1, -1)
    const = lambda shp: pl.BlockSpec(shp, lambda bb, c: (0, 0))
    vec = const((1, RWKV_CH))
    lora = const((LORA_W + LORA_A, RWKV_CH))
    return pl.pallas_call(
        _rwkv_kernel,
        grid=(bsz, s // CHUNK_T),
        in_specs=[pl.BlockSpec((1, CHUNK_T, RWKV_PROJ), lambda bb, c: (bb, c, 0)), const((1, RWKV_PROJ)),
                  vec, lora, vec, lora, const((LORA_G, RWKV_CH)), vec, vec, vec,
                  const((LANES, LANES)), const((LANES, LANES)), vec, vec],
        out_specs=pl.BlockSpec((1, CHUNK_T, RWKV_CH), lambda bb, c: (bb, c, 0)),
        out_shape=jax.ShapeDtypeStruct((bsz, s, RWKV_CH), BF16),
        scratch_shapes=[pltpu.VMEM((RWKV_CH // LANES, LANES, LANES), F32), pltpu.VMEM((1, RWKV_PROJ), F32)],
        compiler_params=_cparams(("parallel", "arbitrary")),
        name="rwkv",
    )(prw, row(mu), row(w0), w2p, row(a0), a2p, g2, row(k_k), row(k_a), row(r_k),
      _block_diag(LANES, HEAD, 1.0).astype(BF16), _block_diag(LANES, HEAD, 1.0 / HEAD).astype(BF16),
      row(gnw), row(gnb))


def _out_proj_kernel(yc_ref, yr_ref, wo_ref, x_ref, gt_ref, g_ref, sh_ref, sc_ref, x1_o, u2_o):
    mix = _dot(yc_ref[0], wo_ref[0:CONV_CH, :]) + _dot(yr_ref[0], wo_ref[CONV_CH:, :])
    x1 = x_ref[0] + gt_ref[0] * mix
    x1_o[0] = x1
    u2 = _rms(x1, g_ref[...]) * (1.0 + sc_ref[0]) + sh_ref[0]
    for c in range(CHUNKS):
        u2_o[0, :, c, :] = u2[:, c * LANES:(c + 1) * LANES]


def _out_proj(yc, yr, wo_bf16, x, gt, g, sh, sc, b0, tm=256):
    bsz, s = yc.shape[0], x.shape[1]
    vec = pl.BlockSpec((1, 1, D_MODEL), lambda b, i: (b, 0, 0))
    tile = lambda w: pl.BlockSpec((1, tm, w), lambda b, i: (b, i, 0))
    return pl.pallas_call(
        _out_proj_kernel,
        grid=(bsz, s // tm),
        in_specs=[tile(CONV_CH), tile(RWKV_CH), pl.BlockSpec((D_MODEL, D_MODEL), lambda b, i: (0, 0)),
                  pl.BlockSpec((1, tm, D_MODEL), lambda b, i: (b + b0, i, 0)), vec,
                  pl.BlockSpec((1, D_MODEL), lambda b, i: (0, 0)), vec, vec],
        out_specs=[tile(D_MODEL), pl.BlockSpec((1, tm, CHUNKS, LANES), lambda b, i: (b, i, 0, 0))],
        out_shape=[jax.ShapeDtypeStruct((bsz, s, D_MODEL), F32),
                   jax.ShapeDtypeStruct((bsz, s, CHUNKS, LANES), F32)],
        compiler_params=_cparams(("parallel", "parallel")),
        name="out_proj",
    )(yc, yr, wo_bf16, x, gt, g.reshape(1, D_MODEL), sh, sc)


HEADS_STEP = 4


def _topk_rows(ss, k):
    n, t = ss[0].shape
    rids = [lax.broadcasted_iota(I32, (8, t), 0) + r0 for r0 in range(0, n, 8)]
    ss = [[s[r0:r0 + 8] for r0 in range(0, n, 8)] for s in ss]
    vals, ids = [[] for _ in ss], [[] for _ in ss]
    for _ in range(k):
        tops = [_max_with_tag(s, [rids]) for s in ss]
        ss = [[jnp.where(r == j, -jnp.inf, c) for c, r in zip(s, rids)] for s, (_, (j,)) in zip(ss, tops)]
        for v, i, (m, (j,)) in zip(vals, ids, tops):
            v.append(m)
            i.append(j)
    return [(jnp.concatenate(v, axis=0), jnp.concatenate(i, axis=0)) for v, i in zip(vals, ids)]


def _max_with_tag(chunks, tags):
    vals, tags = list(chunks), [list(tg) for tg in tags]
    while len(vals) > 1:
        nv, nt = [], [[] for _ in tags]
        for a in range(0, len(vals) - 1, 2):
            first = vals[a] >= vals[a + 1]
            nv.append(jnp.maximum(vals[a], vals[a + 1]))
            for dst, tg in zip(nt, tags):
                dst.append(jnp.where(first, tg[a], tg[a + 1]))
        if len(vals) % 2:
            nv.append(vals[-1])
            for dst, tg in zip(nt, tags):
                dst.append(tg[-1])
        vals, tags = nv, nt
    v8 = vals[0]
    m = jnp.max(v8, axis=0, keepdims=True)
    big = jnp.iinfo(jnp.int32).max
    key = jnp.min(jnp.where(v8 == m, tags[0][0], big), axis=0, keepdims=True)
    out = [key]
    for tg in tags[1:]:
        out.append(jnp.sum(jnp.where(tags[0][0] == key, tg[0], 0), axis=0, keepdims=True))
    return m, out


def _route_kernel(u_ref, wq_ref, keys_ref, idx_o, gate_o, q_ref, idx_s, gate_s):
    tm = u_ref.shape[0]
    u = jnp.concatenate([u_ref[:, c, :] for c in range(CHUNKS)], axis=1)
    q = _dot(u.astype(BF16), wq_ref[...])
    for j in range(2 * PEER_HEADS):
        q_ref[j] = q[:, j * LANES:(j + 1) * LANES]
    K = PEER_TOPK
    tt = LANES
    tiles = range(tm // tt)
    row8 = lax.broadcasted_iota(I32, (8, tt), 0)

    def heads(i, carry):
        probs = [(i * HEADS_STEP + dh, lt) for dh in range(HEADS_STEP) for lt in tiles]
        scores = [_mm(keys_ref[2 * h + p], q_ref[2 * h + p, lt * tt:(lt + 1) * tt, :], 3, _dot_nt)
                  for h, lt in probs for p in range(2)]
        tops = _topk_rows(scores, K)
        ss, flats, eids = [], [], []
        for n_ in range(len(probs)):
            (av, ai), (bv, bi) = tops[2 * n_], tops[2 * n_ + 1]
            cs = [av[0:1] + bv[0:8], av[0:1] + bv[8:16]]
            cf = [row8, row8 + 8]
            ce = [ai[0:1] * PEER_NKEYS + bi[0:8], ai[0:1] * PEER_NKEYS + bi[8:16]]
            for x in range(1, 8):
                cs.append(jnp.where(row8 < K // (x + 1), av[x:x + 1] + bv[0:8], -jnp.inf))
                cf.append(x * K + row8)
                ce.append(ai[x:x + 1] * PEER_NKEYS + bi[0:8])
            cs.append(av[8:16] + bv[0:1])
            cf.append((row8 + 8) * K)
            ce.append(ai[8:16] * PEER_NKEYS + bi[0:1])
            ss.append(cs)
            flats.append(cf)
            eids.append(ce)
        best, experts = [[] for _ in probs], [[] for _ in probs]
        for _ in range(K):
            tops2 = [_max_with_tag(s, [fl, ei]) for s, fl, ei in zip(ss, flats, eids)]
            for n_, (m, (f, e)) in enumerate(tops2):
                best[n_].append(m)
                experts[n_].append(e)
            ss = [[jnp.where(fc == f, -jnp.inf, c) for c, fc in zip(s, fl)]
                  for s, fl, (_, (f, _e)) in zip(ss, flats, tops2)]
        for n_, (h, lt) in enumerate(probs):
            b = jnp.concatenate(best[n_], axis=0)
            e = jnp.exp(b - b[0:1])
            gate_s[lt, h] = e / jnp.sum(e, axis=0, keepdims=True)
            idx_s[lt, h] = jnp.concatenate(experts[n_], axis=0) * WORD_ROWS
        return carry

    lax.fori_loop(0, PEER_HEADS // HEADS_STEP, heads, 0)
    for lt in tiles:
        idx_o[lt * tt:(lt + 1) * tt, :] = jnp.transpose(idx_s[lt].reshape(NPAIR, tt))
        gate_o[lt * tt:(lt + 1) * tt, :] = jnp.transpose(gate_s[lt].reshape(NPAIR, tt))


def _route(u2, wq_bf16, keys, tm=256):
    n = u2.shape[0]
    oblk = pl.BlockSpec((tm, NPAIR), lambda i: (i, 0))
    return pl.pallas_call(
        _route_kernel,
        grid=(n // tm,),
        in_specs=[pl.BlockSpec((tm, CHUNKS, LANES), lambda i: (i, 0, 0)),
                  pl.BlockSpec((D_MODEL, PEER_HEADS * PEER_DQ), lambda i: (0, 0)),
                  pl.BlockSpec((2 * PEER_HEADS, PEER_NKEYS, PEER_DQ // 2), lambda i: (0, 0, 0))],
        out_specs=[oblk, oblk],
        out_shape=[jax.ShapeDtypeStruct((n, NPAIR), I32), jax.ShapeDtypeStruct((n, NPAIR), F32)],
        scratch_shapes=[pltpu.VMEM((2 * PEER_HEADS, tm, LANES), F32),
                        pltpu.VMEM((tm // LANES, PEER_HEADS, PEER_TOPK, LANES), I32),
                        pltpu.VMEM((tm // LANES, PEER_HEADS, PEER_TOPK, LANES), F32)],
        compiler_params=_cparams(("parallel",)),
        name="route",
    )(u2, wq_bf16, keys)


TOK_UNROLL = 16


def _pack_kernel(t_ref, o_ref):
    te = t_ref.shape[0]
    bits = lambda v: lax.bitcast_convert_type(v.astype(BF16).astype(F32), I32)
    for s in range(WORD_ROWS):
        lo = bits(t_ref[:, (2 * s) * LANES:(2 * s + 1) * LANES])
        hi = bits(t_ref[:, (2 * s + 1) * LANES:(2 * s + 2) * LANES])
        o_ref[pl.ds(s, te, stride=WORD_ROWS), :] = (hi & -65536) | lax.shift_right_logical(lo, 16)


def _pack_table(t, te=512):
    e = t.shape[0]
    return pl.pallas_call(
        _pack_kernel,
        grid=(e // te,),
        in_specs=[pl.BlockSpec((te, D_MODEL), lambda i: (i, 0))],
        out_specs=pl.BlockSpec((te * WORD_ROWS, LANES), lambda i: (i, 0)),
        out_shape=jax.ShapeDtypeStruct((e * WORD_ROWS, LANES), I32),
        compiler_params=_cparams(("parallel",)),
        name="pack_table",
    )(t)


def _gather_rows(idx_ref, t0, tab_ref, g_ref):
    for u in range(TOK_UNROLL):
        idx_row = idx_ref.at[t0 + u]
        for kk in range(NPAIR):
            row = pl.multiple_of(idx_row[kk], WORD_ROWS)
            g_ref[u, kk * WORD_ROWS:(kk + 1) * WORD_ROWS, :] = tab_ref[pl.ds(row, WORD_ROWS), :]


def _peer_u_kernel(idx_ref, x_ref, gate_ref, keep_ref, sel_ref, tab_ref, w_o, g_ref, p_ref):
    tp = x_ref.shape[0]

    def tokens(i, carry):
        t0 = pl.multiple_of(i * TOK_UNROLL, TOK_UNROLL)
        _gather_rows(idx_ref, t0, tab_ref, g_ref)
        xs = x_ref[pl.ds(t0, TOK_UNROLL)]
        parts = []
        for u in range(TOK_UNROLL):
            g = pltpu.bitcast(g_ref[u], BF16)
            xc = xs[u]
            xt = jnp.concatenate([xc] * (LANES // CHUNKS), axis=0).astype(BF16)
            r = _dot_nt(g, xt) * keep_ref[...]
            parts.append(jnp.sum(r.reshape(NPAIR // 16, 16 * CHUNKS, LANES), axis=1))
        p_ref[pl.ds(pl.multiple_of(t0 * 8, 8 * TOK_UNROLL), 8 * TOK_UNROLL), :] = jnp.concatenate(parts, axis=0)
        return carry

    lax.fori_loop(0, tp // TOK_UNROLL, tokens, 0)
    z = _mm_bf16_rhs(p_ref[...], sel_ref[...], 3)
    grp = lax.broadcasted_iota(I32, z.shape, 0) % 8 == lax.broadcasted_iota(I32, z.shape, 1) // 16
    h = jnp.sum(jnp.where(grp, z, 0.0).reshape(tp, 8, LANES), axis=1)
    w_o[...] = gate_ref[...] * (0.5 * h * (1.0 + lax.erf(h * (2.0 ** -0.5))))


def _peer_u(idx, x3, gate, tab, start, n, tp=128):
    o = start // tp
    rows = jnp.arange(NPAIR * CHUNKS)[:, None]
    lanes = jnp.arange(LANES)[None, :]
    keep = ((lanes % CHUNKS == rows % CHUNKS) & (lanes // CHUNKS == (rows // CHUNKS) % 16)).astype(F32)
    sel = (jnp.arange(LANES)[:, None] // CHUNKS == jnp.arange(LANES)[None, :] % 16).astype(BF16)
    return pl.pallas_call(
        _peer_u_kernel,
        grid=(n // tp,),
        in_specs=[pl.BlockSpec((tp, NPAIR), lambda i: (i + o, 0), memory_space=pltpu.SMEM),
                  pl.BlockSpec((tp, CHUNKS, LANES), lambda i: (i + o, 0, 0)),
                  pl.BlockSpec((tp, NPAIR), lambda i: (i + o, 0)),
                  pl.BlockSpec((NPAIR * CHUNKS, LANES), lambda i: (0, 0)),
                  pl.BlockSpec((LANES, LANES), lambda i: (0, 0)),
                  pl.BlockSpec(memory_space=pltpu.VMEM)],
        out_specs=pl.BlockSpec((tp, NPAIR), lambda i: (i, 0)),
        out_shape=jax.ShapeDtypeStruct((n, NPAIR), F32),
        scratch_shapes=[pltpu.VMEM((TOK_UNROLL, NPAIR * WORD_ROWS, LANES), I32),
                        pltpu.VMEM((tp * 8, LANES), F32)],
        compiler_params=_cparams(("arbitrary",)),
        name="peer_u",
    )(idx, x3, gate, keep, sel, tab)


def _peer_v_kernel(idx_ref, w_ref, rep_ref, diag_ref, tab_ref, o_ref, g_ref, wx_ref):
    tp = w_ref.shape[0]
    wx_ref[...] = _mm_bf16_rhs(w_ref[...], rep_ref[...], 3)

    def tokens(i, carry):
        t0 = pl.multiple_of(i * TOK_UNROLL, TOK_UNROLL)
        _gather_rows(idx_ref, t0, tab_ref, g_ref)
        wx = wx_ref[pl.ds(t0, TOK_UNROLL), :]
        outs = []
        for u in range(TOK_UNROLL):
            g = pltpu.bitcast(g_ref[u], BF16)
            wm = (wx[u:u + 1, :] * diag_ref[...]).astype(BF16)
            outs.append(_dot(wm, g))
        o_ref[pl.ds(t0, TOK_UNROLL)] = jnp.stack(outs, axis=0)
        return carry

    lax.fori_loop(0, tp // TOK_UNROLL, tokens, 0)


def _peer_v(idx, w, tab, start, tp=128):
    n = w.shape[0]
    o = start // tp
    rep = jnp.repeat(jnp.eye(NPAIR, dtype=BF16), CHUNKS, axis=1)
    diag = (jnp.arange(CHUNKS)[:, None] == jnp.arange(NPAIR * CHUNKS)[None, :] % CHUNKS).astype(F32)
    return pl.pallas_call(
        _peer_v_kernel,
        grid=(n // tp,),
        in_specs=[pl.BlockSpec((tp, NPAIR), lambda i: (i + o, 0), memory_space=pltpu.SMEM),
                  pl.BlockSpec((tp, NPAIR), lambda i: (i, 0)),
                  pl.BlockSpec((NPAIR, NPAIR * CHUNKS), lambda i: (0, 0)),
                  pl.BlockSpec((CHUNKS, NPAIR * CHUNKS), lambda i: (0, 0)),
                  pl.BlockSpec(memory_space=pltpu.VMEM)],
        out_specs=pl.BlockSpec((tp, CHUNKS, LANES), lambda i: (i, 0, 0)),
        out_shape=jax.ShapeDtypeStruct((n, CHUNKS, LANES), F32),
        scratch_shapes=[pltpu.VMEM((TOK_UNROLL, NPAIR * WORD_ROWS, LANES), I32),
                        pltpu.VMEM((tp, NPAIR * CHUNKS), F32)],
        compiler_params=_cparams(("arbitrary",)),
        name="peer_v",
    )(idx, w, rep, diag, tab)


SC_LANES = 16
SC_WORKERS = 32
SC_ROWS = 64
SC_TOKENS = 16
SC_GROUP_SHARES, SC_SHARE_DEN = ((4, 7, 13, 8), (4, 7, 10)), 32
SC_COLS = 8


def _peer_v_sc(ids, wts, packed, start):
    m = wts.shape[0]
    per_worker = m // SC_WORKERS
    blocks = NPAIR // SC_ROWS
    words = D_MODEL // 2
    table = packed.reshape(packed.shape[0] // WORD_ROWS, words)
    ids_b = ids.reshape(ids.shape[0] * blocks, SC_ROWS)
    wts_f = wts.reshape(m * NPAIR)
    mesh = plsc.VectorSubcoreMesh(core_axis_name="c", subcore_axis_name="s")

    @functools.partial(
        pl.kernel, mesh=mesh, out_type=jax.ShapeDtypeStruct((m, D_MODEL), F32),
        scratch_types=[pltpu.VMEM((SC_TOKENS * blocks, SC_ROWS), I32), pltpu.VMEM((SC_TOKENS * NPAIR,), F32),
                       pltpu.VMEM((2, SC_ROWS, words), I32), pltpu.VMEM((D_MODEL,), F32),
                       pltpu.SemaphoreType.DMA((2,))],
        compiler_params=pltpu.CompilerParams(needs_layout_passes=False),
        name="peer_v_sc")
    def run(tab_hbm, ids_hbm, w_hbm, out_hbm, ids_v, w_v, rows_v, acc_v, sems):
        wid = lax.axis_index("s") * 2 + lax.axis_index("c")
        base = wid * per_worker
        zero = jnp.zeros((SC_LANES,), F32)

        def gather(u, b):
            return pltpu.make_async_copy(tab_hbm.at[ids_v.at[u * blocks + b]], rows_v.at[b % 2], sems.at[b % 2])

        @pl.loop(0, per_worker // SC_TOKENS)
        def _(tb):
            t0 = base + tb * SC_TOKENS
            pltpu.sync_copy(ids_hbm.at[pl.ds((start + t0) * blocks, SC_TOKENS * blocks)], ids_v)
            pltpu.sync_copy(w_hbm.at[pl.ds(t0 * NPAIR, SC_TOKENS * NPAIR)], w_v)

            @pl.loop(0, SC_TOKENS)
            def _(u):
                for j in range(D_MODEL // SC_LANES):
                    acc_v[pl.ds(j * SC_LANES, SC_LANES)] = zero
                gather(u, 0).start()
                for b in range(blocks):
                    if b + 1 < blocks:
                        gather(u, b + 1).start()
                    gather(u, b).wait()

                    for c0 in range(0, words // SC_LANES, SC_COLS):
                        def row(r, accs, b=b, c0=c0):
                            wk = plsc.load_gather(w_v, [jnp.full((SC_LANES,), u * NPAIR + b * SC_ROWS + r, I32)])
                            out = []
                            for j in range(SC_COLS):
                                w32 = rows_v[b % 2, r, pl.ds((c0 + j) * SC_LANES, SC_LANES)]
                                lo = lax.bitcast_convert_type(lax.shift_left(w32, 16), F32)
                                hi = lax.bitcast_convert_type(w32 & -65536, F32)
                                out += [accs[2 * j] + wk * lo, accs[2 * j + 1] + wk * hi]
                            return tuple(out)
                        accs = lax.fori_loop(0, SC_ROWS, row, (zero,) * (2 * SC_COLS))
                        for j in range(SC_COLS):
                            s_, l0 = divmod((c0 + j) * SC_LANES, LANES)
                            plsc.addupdate(acc_v.at[pl.ds(2 * s_ * LANES + l0, SC_LANES)], accs[2 * j])
                            plsc.addupdate(acc_v.at[pl.ds((2 * s_ + 1) * LANES + l0, SC_LANES)], accs[2 * j + 1])
                pltpu.sync_copy(acc_v, out_hbm.at[t0 + u])

    return run(table, ids_b, wts_f)


def _final_kernel(x1_ref, *refs, ends, has_tc, has_prev):
    refs = list(refs)
    o_ref = refs.pop()
    if has_prev:
        refs.pop()
    g_ref, gt_ref = refs.pop(), refs.pop()
    ptc_ref = refs.pop() if has_tc else None
    i = pl.program_id(0)
    if has_tc:
        p = jnp.concatenate([ptc_ref[:, c, :] for c in range(CHUNKS)], axis=1)
        parts = list(zip(refs, ends))
    else:
        p = refs[-1][...]
        parts = list(zip(refs[:-1], ends[:-1]))
    for ref, end in reversed(parts):
        p = jnp.where(i < end, ref[...], p)
    o_ref[...] = _rms(x1_ref[...] + gt_ref[0] * p, g_ref[...])


def _final(x1, sc_parts, peer_tc, gt, g, out_prev, tile0, n_total, tm=512):
    m = x1.shape[0]
    per_seq = m // gt.shape[0] // tm
    sizes = [p.shape[0] // tm for p in sc_parts]
    ends = [sum(sizes[:j + 1]) for j in range(len(sizes))]
    starts = [e - k for e, k in zip(ends, sizes)]
    tile = pl.BlockSpec((tm, D_MODEL), lambda i: (i, 0))
    part = lambda s0, k: pl.BlockSpec((tm, D_MODEL), lambda i: (jnp.clip(i - s0, 0, k - 1), 0))
    in_specs = [tile] + [part(s0, k) for s0, k in zip(starts, sizes)]
    args = [x1, *sc_parts]
    if peer_tc is not None:
        in_specs.append(pl.BlockSpec((tm, CHUNKS, LANES), lambda i: (jnp.maximum(i - ends[-1], 0), 0, 0)))
        args.append(peer_tc)
    in_specs += [pl.BlockSpec((1, 1, D_MODEL), lambda i: (i // per_seq, 0, 0)),
                 pl.BlockSpec((1, D_MODEL), lambda i: (0, 0))]
    args += [gt, g.reshape(1, D_MODEL)]
    aliases = {}
    if out_prev is not None:
        in_specs.append(pl.BlockSpec(memory_space=pl.ANY))
        aliases = {len(args): 0}
        args.append(out_prev)
    return pl.pallas_call(
        functools.partial(_final_kernel, ends=tuple(ends), has_tc=peer_tc is not None, has_prev=out_prev is not None),
        grid=(m // tm,),
        in_specs=in_specs,
        out_specs=pl.BlockSpec((tm, D_MODEL), lambda i: (i + tile0, 0)),
        out_shape=jax.ShapeDtypeStruct((n_total, D_MODEL), F32),
        input_output_aliases=aliases,
        compiler_params=_cparams(("parallel",)),
        name="final",
    )(*args)


def _block_diag(width, group, value):
    i = jnp.arange(width) // group
    return jnp.where(i[:, None] == i[None, :], value, 0.0).astype(F32)


def _layer(x, mod, final_g, norm_mix_g, w_in, conv_dw_w, conv_dw_b, conv_ln_w, conv_ln_b, rwkv_mu, rwkv_w0, rwkv_w2,
           rwkv_a0, rwkv_a2, rwkv_g2, rwkv_k_k, rwkv_k_a, rwkv_r_k, rwkv_gn_w, rwkv_gn_b, w_out, norm_ffn_g,
           peer_w_q, peer_sub_keys, peer_u, peer_v):
    bsz, s, _ = x.shape
    sh_mix, sc_mix, gt_mix, sh_ffn, sc_ffn, gt_ffn = (
        mod[:, i * D_MODEL:(i + 1) * D_MODEL].reshape(bsz, 1, D_MODEL) for i in range(6))

    w_in_b, w_out_b, w_q_b = w_in.astype(BF16), w_out.astype(BF16), peer_w_q.astype(BF16)
    zpad = jnp.zeros((LORA_W, RWKV_CH), F32)
    w2p, a2p = jnp.concatenate([rwkv_w2, zpad], axis=0), jnp.concatenate([zpad, rwkv_a2], axis=0)
    keys = peer_sub_keys.reshape(2 * PEER_HEADS, PEER_NKEYS, PEER_DQ // 2)
    tab_u, tab_v = _pack_table(peer_u), _pack_table(peer_v)

    nb = bsz // len(SC_GROUP_SHARES)
    n, m = bsz * s, nb * s
    out, after = None, None
    for gi, shares in enumerate(SC_GROUP_SHARES):
        b0 = gi * nb
        grp = lambda t: t[b0:b0 + nb]
        xg = x if after is None else lax.optimization_barrier((x, after))[0]
        yglu, prw = _in_proj(xg, grp(sh_mix), grp(sc_mix), norm_mix_g, w_in_b, b0)
        y_conv = _conv(yglu, conv_dw_w, conv_dw_b, conv_ln_w, conv_ln_b)
        y_rwkv = _rwkv(prw, rwkv_mu, rwkv_w0, w2p, rwkv_a0, a2p, rwkv_g2, rwkv_k_k, rwkv_k_a, rwkv_r_k.reshape(-1),
                       rwkv_gn_w, rwkv_gn_b)
        x1, u2 = _out_proj(y_conv, y_rwkv, w_out_b, xg, grp(gt_mix), norm_ffn_g, grp(sh_ffn), grp(sc_ffn), b0)
        u3 = u2.reshape(m, CHUNKS, LANES)
        idx, gate = _route(u3, w_q_b, keys)
        ids = lax.shift_right_logical(idx, 2)
        start, sc_parts = 0, []
        for share in shares:
            cnt = m * share // SC_SHARE_DEN
            after = _peer_u(idx, u3, gate, tab_u, start, cnt)
            sc_parts.append(_peer_v_sc(ids, after, tab_v, start))
            start += cnt
        peer_tc = None
        if start < m:
            after = _peer_u(idx, u3, gate, tab_u, start, m - start)
            peer_tc = _peer_v(idx, after, tab_v, start)
        out = _final(x1.reshape(m, D_MODEL), sc_parts, peer_tc, grp(gt_ffn), final_g, out, gi * (m // 512), n)
    return out.reshape(bsz, s, D_MODEL)


def kernel(x, c, ada_w, ada_b, norm_mix_g, w_in, conv_dw_w, conv_dw_b, conv_ln_w, conv_ln_b, rwkv_mu, rwkv_w0,
           rwkv_w2, rwkv_a0, rwkv_a2, rwkv_g2, rwkv_k_k, rwkv_k_a, rwkv_r_k, rwkv_gn_w, rwkv_gn_b, w_out,
           norm_ffn_g, peer_w_q, peer_sub_keys, peer_u, peer_v, final_g):
    depth = ada_w.shape[0]
    assert depth == 1, "one layer: the final norm is fused into the last layer's residual"
    mod = _mod(c, ada_w[0], ada_b[0])
    return _layer(x, mod, final_g, norm_mix_g[0], w_in[0], conv_dw_w[0], conv_dw_b[0], conv_ln_w[0],
                        conv_ln_b[0], rwkv_mu[0], rwkv_w0[0], rwkv_w2[0], rwkv_a0[0], rwkv_a2[0], rwkv_g2[0],
                        rwkv_k_k[0], rwkv_k_a[0], rwkv_r_k[0], rwkv_gn_w[0], rwkv_gn_b[0], w_out[0],
                        norm_ffn_g[0], peer_w_q[0], peer_sub_keys[0], peer_u[0], peer_v[0])
```

```python
import functools

import jax
import jax.numpy as jnp
from jax import lax
from jax.experimental import pallas as pl
from jax.experimental.pallas import tpu as pltpu
from jax.experimental.pallas import tpu_sc as plsc

F32 = jnp.float32
BF16 = jnp.bfloat16
I32 = jnp.int32
HI = lax.Precision.HIGHEST

D_MODEL = 1024
CONV_CH = 512
RWKV_CH = 512
HEAD = 64
CONV_WIDTH = 31
LORA_W = 64
LORA_A = 64
LORA_G = 128
RWKV_PROJ = 3 * RWKV_CH + LORA_W + LORA_A + LORA_G
IN_PROJ = 2 * CONV_CH + RWKV_PROJ
PEER_HEADS = 8
PEER_NKEYS = 128
PEER_EXPERTS = PEER_NKEYS * PEER_NKEYS
PEER_DQ = 256
PEER_TOPK = 16
NPAIR = PEER_HEADS * PEER_TOPK
RMS_EPS = 1e-6
LN_EPS = 1e-5
GN_EPS = 64e-5

LANES = 128
CHUNKS = D_MODEL // LANES
WORD_ROWS = CHUNKS // 2
CHUNK_T = 64
VMEM_LIMIT = 56 * 1024 * 1024


def _cparams(sem, vmem=None):
    return pltpu.CompilerParams(dimension_semantics=sem, vmem_limit_bytes=vmem or VMEM_LIMIT)


def _dot(a, b, precision=None):
    return jnp.dot(a, b, precision=precision, preferred_element_type=F32)


def _dot_nt(a, b, precision=None):
    return lax.dot_general(a, b, (((1,), (1,)), ((), ())), precision=precision, preferred_element_type=F32)


def _dot_tn(a, b, precision=None):
    return lax.dot_general(a, b, (((0,), (0,)), ((), ())), precision=precision, preferred_element_type=F32)


def _split(a):
    hi = a.astype(BF16)
    return hi, (a - hi.astype(F32)).astype(BF16)


def _mm(a, b, passes, dot=_dot):
    if passes == 6:
        return dot(a, b, HI)
    if passes == 1:
        return dot(a.astype(BF16), b.astype(BF16))
    ka = 0 if dot is _dot_tn else 1
    kb = 1 if dot is _dot_nt else 0
    ah, al = _split(a)
    bh, bl = _split(b)
    return dot(jnp.concatenate([ah, ah, al], axis=ka), jnp.concatenate([bh, bl, bh], axis=kb))


def _mm_bf16_rhs(a, b_bf16, terms=2):
    parts = []
    for _ in range(terms):
        p = a.astype(BF16)
        parts.append(p)
        a = a - p.astype(F32)
    return _dot(jnp.concatenate(parts, axis=1), jnp.concatenate([b_bf16] * terms, axis=0))


def _head_sums(x, bd_bf16):
    return jnp.concatenate([_mm_bf16_rhs(x[:, i * LANES:(i + 1) * LANES], bd_bf16, 3)
                            for i in range(x.shape[1] // LANES)], axis=1)


def _rms(x, g):
    return x * lax.rsqrt(jnp.mean(x * x, axis=-1, keepdims=True) + RMS_EPS) * g


def _mod_kernel(c_ref, w_ref, b_ref, o_ref):
    c = c_ref[...]
    o_ref[...] = _dot(c * jax.nn.sigmoid(c), w_ref[...], HI) + b_ref[...]


def _mod(c, w, b):
    bsz = c.shape[0]
    n = w.shape[1]
    tn = 1024
    return pl.pallas_call(
        _mod_kernel,
        grid=(n // tn,),
        in_specs=[pl.BlockSpec((bsz, D_MODEL), lambda j: (0, 0)),
                  pl.BlockSpec((D_MODEL, tn), lambda j: (0, j)),
                  pl.BlockSpec((1, tn), lambda j: (0, j))],
        out_specs=pl.BlockSpec((bsz, tn), lambda j: (0, j)),
        out_shape=jax.ShapeDtypeStruct((bsz, n), F32),
        compiler_params=_cparams(("parallel",)),
    )(c, w, b.reshape(1, n))


def _in_proj_kernel(x_ref, sh_ref, sc_ref, g_ref, w_ref, yglu_ref, prw_ref):
    u = _rms(x_ref[0], g_ref[...]) * (1.0 + sc_ref[0]) + sh_ref[0]
    p = _dot(u.astype(BF16), w_ref[...])
    yglu_ref[0] = p[:, :CONV_CH] * jax.nn.sigmoid(p[:, CONV_CH:2 * CONV_CH])
    prw_ref[0] = p[:, 2 * CONV_CH:]


def _in_proj(x, sh, sc, g, w_bf16, b0, tm=256):
    bsz, s = sh.shape[0], x.shape[1]
    vec = pl.BlockSpec((1, 1, D_MODEL), lambda b, i: (b, 0, 0))
    return pl.pallas_call(
        _in_proj_kernel,
        grid=(bsz, s // tm),
        in_specs=[pl.BlockSpec((1, tm, D_MODEL), lambda b, i: (b + b0, i, 0)), vec, vec,
                  pl.BlockSpec((1, D_MODEL), lambda b, i: (0, 0)),
                  pl.BlockSpec((D_MODEL, IN_PROJ), lambda b, i: (0, 0))],
        out_specs=[pl.BlockSpec((1, tm, CONV_CH), lambda b, i: (b, i, 0)),
                   pl.BlockSpec((1, tm, RWKV_PROJ), lambda b, i: (b, i, 0))],
        out_shape=[jax.ShapeDtypeStruct((bsz, s, CONV_CH), F32),
                   jax.ShapeDtypeStruct((bsz, s, RWKV_PROJ), F32)],
        compiler_params=_cparams(("parallel", "parallel")),
    )(x, sh, sc, g.reshape(1, D_MODEL), w_bf16)


CONV_HALO = 32
CONV_ROWS = 64


def _conv_kernel(cur_ref, prev_ref, w_ref, b_ref, lnw_ref, lnb_ref, o_ref, pad_ref):
    tc = cur_ref.shape[1]
    pad_ref[0, 0:CONV_HALO, :] = jnp.where(pl.program_id(1) > 0, prev_ref[0], 0.0)
    pad_ref[0, CONV_HALO:CONV_HALO + tc, :] = cur_ref[0]
    span = CONV_HALO + tc - 8
    for r in range(1, 8):
        pad_ref[r, 0:span, :] = pad_ref[0, r:r + span, :]
    off = CONV_HALO - (CONV_WIDTH - 1)
    for r0 in range(0, tc, CONV_ROWS):
        acc = jnp.zeros((CONV_ROWS, CONV_CH), F32)
        for j in range(CONV_WIDTH):
            q, r = divmod(off + j, 8)
            acc = acc + w_ref[j:j + 1, :] * pad_ref[r, r0 + 8 * q:r0 + 8 * q + CONV_ROWS, :]
        y = acc + b_ref[...]
        mu = jnp.mean(y, axis=-1, keepdims=True)
        yc = y - mu
        var = jnp.mean(yc * yc, axis=-1, keepdims=True)
        yn = yc * lax.rsqrt(var + LN_EPS) * lnw_ref[...] + lnb_ref[...]
        o_ref[0, r0:r0 + CONV_ROWS, :] = (yn * jax.nn.sigmoid(yn)).astype(o_ref.dtype)


def _conv(yglu, w, b, lnw, lnb, tc=256):
    bsz, s, _ = yglu.shape
    hb = tc // CONV_HALO
    row = lambda a: a.reshape(1, CONV_CH)
    const = lambda shp: pl.BlockSpec(shp, lambda bb, i: (0, 0))
    return pl.pallas_call(
        _conv_kernel,
        grid=(bsz, s // tc),
        in_specs=[pl.BlockSpec((1, tc, CONV_CH), lambda bb, i: (bb, i, 0)),
                  pl.BlockSpec((1, CONV_HALO, CONV_CH), lambda bb, i: (bb, jnp.maximum(i * hb - 1, 0), 0)),
                  const((CONV_WIDTH, CONV_CH)), const((1, CONV_CH)), const((1, CONV_CH)), const((1, CONV_CH))],
        out_specs=pl.BlockSpec((1, tc, CONV_CH), lambda bb, i: (bb, i, 0)),
        out_shape=jax.ShapeDtypeStruct((bsz, s, CONV_CH), BF16),
        scratch_shapes=[pltpu.VMEM((8, CONV_HALO + tc, CONV_CH), F32)],
        compiler_params=_cparams(("parallel", "parallel")),
    )(yglu, yglu, w, row(b), row(lnw), row(lnb))


def _softplus(z):
    return jnp.maximum(z, 0.0) + jnp.log1p(jnp.exp(-jnp.abs(z)))


def _rwkv_features(cur, prow, mu, w0, w2p, a0, a2p, g2, k_k, k_a, r_k, bd):
    rows = lax.broadcasted_iota(I32, cur.shape, 0)
    prev = jnp.where(rows == 0, prow, pltpu.roll(cur, 1, axis=0))
    xs = cur + mu * (prev - cur)
    r = xs[:, 0:RWKV_CH]
    k = xs[:, RWKV_CH:2 * RWKV_CH]
    v = xs[:, 2 * RWKV_CH:3 * RWKV_CH]
    wa = xs[:, 3 * RWKV_CH:3 * RWKV_CH + LORA_W + LORA_A]
    gl = xs[:, 3 * RWKV_CH + LORA_W + LORA_A:]
    w = -_softplus(-(w0 + _mm(jnp.tanh(wa), w2p, 3))) - 0.5
    a = jax.nn.sigmoid(a0 + _mm(wa, a2p, 3))
    g = _mm(jax.nn.sigmoid(gl), g2, 3)
    kk = k * k_k
    kkn = kk / jnp.maximum(jnp.sqrt(_head_sums(kk * kk, bd)), 1e-12)
    k2 = k * (1.0 + (a - 1.0) * k_a)
    bonus = _head_sums(r * k2 * r_k, bd) * v
    return r, k2, v, -jnp.exp(w), -kkn, kkn * a, g, bonus


RWKV_PASSES = {"gram": 3, "lakv": 3, "solve": 3, "out": 3, "state": 3}
SOLVE_BLK = 16


def _expand(x, lane_lo):
    return jnp.concatenate([jnp.where(lane_lo, x, 0.0), jnp.where(lane_lo, 0.0, x)], axis=0)


def _rwkv_kernel(p_ref, mu_ref, w0_ref, w2_ref, a0_ref, a2_ref, g2_ref, kk_ref, ka_ref, rk_ref, bd_ref,
                 bdm_ref, gnw_ref, gnb_ref, y_ref, s_ref, last_ref):
    C = CHUNK_T
    H2 = 2 * C

    @pl.when(pl.program_id(1) == 0)
    def _():
        s_ref[...] = jnp.zeros_like(s_ref)
        last_ref[...] = jnp.zeros_like(last_ref)

    cur = p_ref[0]
    r, k, vv, lw, a, b, gate, bonus = _rwkv_features(
        cur, last_ref[...], mu_ref[...], w0_ref[...], w2_ref[...], a0_ref[...], a2_ref[...], g2_ref[...],
        kk_ref[...], ka_ref[...], rk_ref[...], bd_ref[...])
    last_ref[...] = cur[C - 1:C, :]

    tri = (lax.broadcasted_iota(I32, (C, C), 0) >= lax.broadcasted_iota(I32, (C, C), 1)).astype(F32)
    cum = _dot(tri, lw, HI)
    tot = cum[C - 1:C, :]
    e_pos = jnp.exp(cum)
    e_neg = jnp.exp(-cum)
    e_rem = jnp.exp(tot - cum)
    rt = r * e_pos
    at = a * jnp.exp(cum - lw)
    kt = k * e_neg
    bt = b * e_neg
    kp = k * e_rem
    bp = b * e_rem
    pc = jnp.exp(tot)

    lane_lo = lax.broadcasted_iota(I32, (C, LANES), 1) < HEAD
    tt = lax.broadcasted_iota(I32, (H2, H2), 0) % C
    ss = lax.broadcasted_iota(I32, (H2, H2), 1) % C
    strict = tt > ss
    incl = tt >= ss
    near = tt // SOLVE_BLK == ss // SOLVE_BLK
    eye = lax.broadcasted_iota(I32, (LANES, LANES), 0) == lax.broadcasted_iota(I32, (LANES, LANES), 1)

    P = RWKV_PASSES
    pairs = range(RWKV_CH // LANES)
    each = lambda fn, *lists: [fn(*args) for args in zip(*lists)]
    sls = [slice(hp * LANES, (hp + 1) * LANES) for hp in pairs]
    ax, rx, bx, kx, vx, bpx, kpx = ([_expand(t[:, sl], lane_lo) for sl in sls] for t in (at, rt, bt, kt, vv, bp, kp))
    gram = each(lambda a_, r_, b_, k_: _mm(jnp.concatenate([a_, r_], axis=0), jnp.concatenate([b_, k_], axis=0),
                                           P["gram"], _dot_nt), ax, rx, bx, kx)
    l_ab = [jnp.where(strict, g_[:H2, :H2], 0.0) for g_ in gram]
    l_ak = [jnp.where(strict, g_[:H2, H2:], 0.0) for g_ in gram]
    m_r = [jnp.concatenate([jnp.where(incl, g_[H2:, :H2], 0.0), jnp.where(incl, g_[H2:, H2:], 0.0)], axis=1)
           for g_ in gram]
    dg = [jnp.where(near, l_, 0.0) for l_ in l_ab]
    lakv = each(lambda l_, v_: _mm(l_, v_, P["lakv"]), l_ak, vx)
    xf = each(lambda a_, lv_, l_, d_: jnp.concatenate([a_, lv_, l_ - d_], axis=1), ax, lakv, l_ab, dg)
    n_sq = SOLVE_BLK.bit_length() - 1
    for it in range(n_sq):
        xf = each(lambda x_, d_: x_ + _mm(d_, x_, P["solve"]), xf, dg)
        if it + 1 < n_sq:
            dg = [_mm(d_, d_, P["solve"]) for d_ in dg]
    x = [x_[:, :2 * LANES] for x_ in xf]
    f = [x_[:, 2 * LANES:] for x_ in xf]
    n_sq = (C // SOLVE_BLK).bit_length() - 1
    for it in range(n_sq):
        x = each(lambda x_, f_: x_ + _mm(f_, x_, P["solve"]), x, f)
        if it + 1 < n_sq:
            f = [_mm(f_, f_, P["solve"]) for f_ in f]
    zero = jnp.zeros((H2, LANES), F32)
    z = each(lambda x_, v_: jnp.concatenate([x_, jnp.concatenate([zero, v_], axis=1)], axis=0), x, vx)
    w1 = each(lambda m_, z_: _mm(m_, z_, P["out"]), m_r, z)
    w2 = each(lambda b_, k_, z_: _mm(jnp.concatenate([b_, k_], axis=0), z_, P["out"], _dot_tn), bpx, kpx, z)
    ys = []
    for hp in pairs:
        ra = rx[hp] + w1[hp][:, :LANES]
        ra = ra[:C] + ra[C:]
        y0 = w1[hp][:C, LANES:] + w1[hp][C:, LANES:]
        mt = w2[hp][:, :LANES] + jnp.where(eye, pc[:, sls[hp]], 0.0)
        s0 = s_ref[hp]
        ys.append(_mm(ra, s0, P["state"]) + y0)
        s_ref[hp] = _mm(mt, s0, P["state"]) + w2[hp][:, LANES:]
    yc = [y_ - _mm_bf16_rhs(y_, bdm_ref[...]) for y_ in ys]
    ys = [c_ * lax.rsqrt(_mm_bf16_rhs(c_ * c_, bdm_ref[...]) + GN_EPS) for c_ in yc]

    yn = jnp.concatenate(ys, axis=1) * gnw_ref[...] + gnb_ref[...]
    y_ref[0] = ((yn + bonus) * gate).astype(y_ref.dtype)


def _rwkv(prw, mu, w0, w2p, a0, a2p, g2, k_k, k_a, r_k, gnw, gnb):
    bsz, s, _ = prw.shape
    row = lambda t: t.reshape(1, -1)
    const = lambda shp: pl.BlockSpec(shp, lambda bb, c: (0, 0))
    vec = const((1, RWKV_CH))
    lora = const((LORA_W + LORA_A, RWKV_CH))
    return pl.pallas_call(
        _rwkv_kernel,
        grid=(bsz, s // CHUNK_T),
        in_specs=[pl.BlockSpec((1, CHUNK_T, RWKV_PROJ), lambda bb, c: (bb, c, 0)), const((1, RWKV_PROJ)),
                  vec, lora, vec, lora, const((LORA_G, RWKV_CH)), vec, vec, vec,
                  const((LANES, LANES)), const((LANES, LANES)), vec, vec],
        out_specs=pl.BlockSpec((1, CHUNK_T, RWKV_CH), lambda bb, c: (bb, c, 0)),
        out_shape=jax.ShapeDtypeStruct((bsz, s, RWKV_CH), BF16),
        scratch_shapes=[pltpu.VMEM((RWKV_CH // LANES, LANES, LANES), F32), pltpu.VMEM((1, RWKV_PROJ), F32)],
        compiler_params=_cparams(("parallel", "arbitrary")),
        name="rwkv",
    )(prw, row(mu), row(w0), w2p, row(a0), a2p, g2, row(k_k), row(k_a), row(r_k),
      _block_diag(LANES, HEAD, 1.0).astype(BF16), _block_diag(LANES, HEAD, 1.0 / HEAD).astype(BF16),
      row(gnw), row(gnb))


def _out_proj_kernel(yc_ref, yr_ref, wo_ref, x_ref, gt_ref, g_ref, sh_ref, sc_ref, x1_o, u2_o):
    mix = _dot(yc_ref[0], wo_ref[0:CONV_CH, :]) + _dot(yr_ref[0], wo_ref[CONV_CH:, :])
    x1 = x_ref[0] + gt_ref[0] * mix
    x1_o[0] = x1
    u2 = _rms(x1, g_ref[...]) * (1.0 + sc_ref[0]) + sh_ref[0]
    for c in range(CHUNKS):
        u2_o[0, :, c, :] = u2[:, c * LANES:(c + 1) * LANES]


def _out_proj(yc, yr, wo_bf16, x, gt, g, sh, sc, b0, tm=256):
    bsz, s = yc.shape[0], x.shape[1]
    vec = pl.BlockSpec((1, 1, D_MODEL), lambda b, i: (b, 0, 0))
    tile = lambda w: pl.BlockSpec((1, tm, w), lambda b, i: (b, i, 0))
    return pl.pallas_call(
        _out_proj_kernel,
        grid=(bsz, s // tm),
        in_specs=[tile(CONV_CH), tile(RWKV_CH), pl.BlockSpec((D_MODEL, D_MODEL), lambda b, i: (0, 0)),
                  pl.BlockSpec((1, tm, D_MODEL), lambda b, i: (b + b0, i, 0)), vec,
                  pl.BlockSpec((1, D_MODEL), lambda b, i: (0, 0)), vec, vec],
        out_specs=[tile(D_MODEL), pl.BlockSpec((1, tm, CHUNKS, LANES), lambda b, i: (b, i, 0, 0))],
        out_shape=[jax.ShapeDtypeStruct((bsz, s, D_MODEL), F32),
                   jax.ShapeDtypeStruct((bsz, s, CHUNKS, LANES), F32)],
        compiler_params=_cparams(("parallel", "parallel")),
        name="out_proj",
    )(yc, yr, wo_bf16, x, gt, g.reshape(1, D_MODEL), sh, sc)


HEADS_STEP = 4


def _topk_rows(ss, k):
    n, t = ss[0].shape
    rids = [lax.broadcasted_iota(I32, (8, t), 0) + r0 for r0 in range(0, n, 8)]
    ss = [[s[r0:r0 + 8] for r0 in range(0, n, 8)] for s in ss]
    vals, ids = [[] for _ in ss], [[] for _ in ss]
    for _ in range(k):
        tops = [_max_with_tag(s, [rids]) for s in ss]
        ss = [[jnp.where(r == j, -jnp.inf, c) for c, r in zip(s, rids)] for s, (_, (j,)) in zip(ss, tops)]
        for v, i, (m, (j,)) in zip(vals, ids, tops):
            v.append(m)
            i.append(j)
    return [(jnp.concatenate(v, axis=0), jnp.concatenate(i, axis=0)) for v, i in zip(vals, ids)]


def _max_with_tag(chunks, tags):
    vals, tags = list(chunks), [list(tg) for tg in tags]
    while len(vals) > 1:
        nv, nt = [], [[] for _ in tags]
        for a in range(0, len(vals) - 1, 2):
            first = vals[a] >= vals[a + 1]
            nv.append(jnp.maximum(vals[a], vals[a + 1]))
            for dst, tg in zip(nt, tags):
                dst.append(jnp.where(first, tg[a], tg[a + 1]))
        if len(vals) % 2:
            nv.append(vals[-1])
            for dst, tg in zip(nt, tags):
                dst.append(tg[-1])
        vals, tags = nv, nt
    v8 = vals[0]
    m = jnp.max(v8, axis=0, keepdims=True)
    big = jnp.iinfo(jnp.int32).max
    key = jnp.min(jnp.where(v8 == m, tags[0][0], big), axis=0, keepdims=True)
    out = [key]
    for tg in tags[1:]:
        out.append(jnp.sum(jnp.where(tags[0][0] == key, tg[0], 0), axis=0, keepdims=True))
    return m, out


def _route_kernel(u_ref, wq_ref, keys_ref, idx_o, gate_o, q_ref, idx_s, gate_s):
    tm = u_ref.shape[0]
    u = jnp.concatenate([u_ref[:, c, :] for c in range(CHUNKS)], axis=1)
    q = _dot(u.astype(BF16), wq_ref[...])
    for j in range(2 * PEER_HEADS):
        q_ref[j] = q[:, j * LANES:(j + 1) * LANES]
    K = PEER_TOPK
    tt = LANES
    tiles = range(tm // tt)
    row8 = lax.broadcasted_iota(I32, (8, tt), 0)

    def heads(i, carry):
        probs = [(i * HEADS_STEP + dh, lt) for dh in range(HEADS_STEP) for lt in tiles]
        scores = [_mm(keys_ref[2 * h + p], q_ref[2 * h + p, lt * tt:(lt + 1) * tt, :], 3, _dot_nt)
                  for h, lt in probs for p in range(2)]
        tops = _topk_rows(scores, K)
        ss, flats, eids = [], [], []
        for n_ in range(len(probs)):
            (av, ai), (bv, bi) = tops[2 * n_], tops[2 * n_ + 1]
            cs = [av[0:1] + bv[0:8], av[0:1] + bv[8:16]]
            cf = [row8, row8 + 8]
            ce = [ai[0:1] * PEER_NKEYS + bi[0:8], ai[0:1] * PEER_NKEYS + bi[8:16]]
            for x in range(1, 8):
                cs.append(jnp.where(row8 < K // (x + 1), av[x:x + 1] + bv[0:8], -jnp.inf))
                cf.append(x * K + row8)
                ce.append(ai[x:x + 1] * PEER_NKEYS + bi[0:8])
            cs.append(av[8:16] + bv[0:1])
            cf.append((row8 + 8) * K)
            ce.append(ai[8:16] * PEER_NKEYS + bi[0:1])
            ss.append(cs)
            flats.append(cf)
            eids.append(ce)
        best, experts = [[] for _ in probs], [[] for _ in probs]
        for _ in range(K):
            tops2 = [_max_with_tag(s, [fl, ei]) for s, fl, ei in zip(ss, flats, eids)]
            for n_, (m, (f, e)) in enumerate(tops2):
                best[n_].append(m)
                experts[n_].append(e)
            ss = [[jnp.where(fc == f, -jnp.inf, c) for c, fc in zip(s, fl)]
                  for s, fl, (_, (f, _e)) in zip(ss, flats, tops2)]
        for n_, (h, lt) in enumerate(probs):
            b = jnp.concatenate(best[n_], axis=0)
            e = jnp.exp(b - b[0:1])
            gate_s[lt, h] = e / jnp.sum(e, axis=0, keepdims=True)
            idx_s[lt, h] = jnp.concatenate(experts[n_], axis=0) * WORD_ROWS
        return carry

    lax.fori_loop(0, PEER_HEADS // HEADS_STEP, heads, 0)
    for lt in tiles:
        idx_o[lt * tt:(lt + 1) * tt, :] = jnp.transpose(idx_s[lt].reshape(NPAIR, tt))
        gate_o[lt * tt:(lt + 1) * tt, :] = jnp.transpose(gate_s[lt].reshape(NPAIR, tt))


def _route(u2, wq_bf16, keys, tm=256):
    n = u2.shape[0]
    oblk = pl.BlockSpec((tm, NPAIR), lambda i: (i, 0))
    return pl.pallas_call(
        _route_kernel,
        grid=(n // tm,),
        in_specs=[pl.BlockSpec((tm, CHUNKS, LANES), lambda i: (i, 0, 0)),
                  pl.BlockSpec((D_MODEL, PEER_HEADS * PEER_DQ), lambda i: (0, 0)),
                  pl.BlockSpec((2 * PEER_HEADS, PEER_NKEYS, PEER_DQ // 2), lambda i: (0, 0, 0))],
        out_specs=[oblk, oblk],
        out_shape=[jax.ShapeDtypeStruct((n, NPAIR), I32), jax.ShapeDtypeStruct((n, NPAIR), F32)],
        scratch_shapes=[pltpu.VMEM((2 * PEER_HEADS, tm, LANES), F32),
                        pltpu.VMEM((tm // LANES, PEER_HEADS, PEER_TOPK, LANES), I32),
                        pltpu.VMEM((tm // LANES, PEER_HEADS, PEER_TOPK, LANES), F32)],
        compiler_params=_cparams(("parallel",)),
        name="route",
    )(u2, wq_bf16, keys)


TOK_UNROLL = 16


def _pack_kernel(t_ref, o_ref):
    te = t_ref.shape[0]
    bits = lambda v: lax.bitcast_convert_type(v.astype(BF16).astype(F32), I32)
    for s in range(WORD_ROWS):
        lo = bits(t_ref[:, (2 * s) * LANES:(2 * s + 1) * LANES])
        hi = bits(t_ref[:, (2 * s + 1) * LANES:(2 * s + 2) * LANES])
        o_ref[pl.ds(s, te, stride=WORD_ROWS), :] = (hi & -65536) | lax.shift_right_logical(lo, 16)


def _pack_table(t, te=512):
    e = t.shape[0]
    return pl.pallas_call(
        _pack_kernel,
        grid=(e // te,),
        in_specs=[pl.BlockSpec((te, D_MODEL), lambda i: (i, 0))],
        out_specs=pl.BlockSpec((te * WORD_ROWS, LANES), lambda i: (i, 0)),
        out_shape=jax.ShapeDtypeStruct((e * WORD_ROWS, LANES), I32),
        compiler_params=_cparams(("parallel",)),
        name="pack_table",
    )(t)


def _gather_rows(idx_ref, t0, tab_ref, g_ref):
    for u in range(TOK_UNROLL):
        idx_row = idx_ref.at[t0 + u]
        for kk in range(NPAIR):
            row = pl.multiple_of(idx_row[kk], WORD_ROWS)
            g_ref[u, kk * WORD_ROWS:(kk + 1) * WORD_ROWS, :] = tab_ref[pl.ds(row, WORD_ROWS), :]


def _peer_u_kernel(idx_ref, x_ref, gate_ref, keep_ref, sel_ref, tab_ref, w_o, g_ref, p_ref):
    tp = x_ref.shape[0]

    def tokens(i, carry):
        t0 = pl.multiple_of(i * TOK_UNROLL, TOK_UNROLL)
        _gather_rows(idx_ref, t0, tab_ref, g_ref)
        xs = x_ref[pl.ds(t0, TOK_UNROLL)]
        parts = []
        for u in range(TOK_UNROLL):
            g = pltpu.bitcast(g_ref[u], BF16)
            xc = xs[u]
            xt = jnp.concatenate([xc] * (LANES // CHUNKS), axis=0).astype(BF16)
            r = _dot_nt(g, xt) * keep_ref[...]
            parts.append(jnp.sum(r.reshape(NPAIR // 16, 16 * CHUNKS, LANES), axis=1))
        p_ref[pl.ds(pl.multiple_of(t0 * 8, 8 * TOK_UNROLL), 8 * TOK_UNROLL), :] = jnp.concatenate(parts, axis=0)
        return carry

    lax.fori_loop(0, tp // TOK_UNROLL, tokens, 0)
    z = _mm_bf16_rhs(p_ref[...], sel_ref[...], 3)
    grp = lax.broadcasted_iota(I32, z.shape, 0) % 8 == lax.broadcasted_iota(I32, z.shape, 1) // 16
    h = jnp.sum(jnp.where(grp, z, 0.0).reshape(tp, 8, LANES), axis=1)
    w_o[...] = gate_ref[...] * (0.5 * h * (1.0 + lax.erf(h * (2.0 ** -0.5))))


def _peer_u(idx, x3, gate, tab, start, n, tp=128):
    o = start // tp
    rows = jnp.arange(NPAIR * CHUNKS)[:, None]
    lanes = jnp.arange(LANES)[None, :]
    keep = ((lanes % CHUNKS == rows % CHUNKS) & (lanes // CHUNKS == (rows // CHUNKS) % 16)).astype(F32)
    sel = (jnp.arange(LANES)[:, None] // CHUNKS == jnp.arange(LANES)[None, :] % 16).astype(BF16)
    return pl.pallas_call(
        _peer_u_kernel,
        grid=(n // tp,),
        in_specs=[pl.BlockSpec((tp, NPAIR), lambda i: (i + o, 0), memory_space=pltpu.SMEM),
                  pl.BlockSpec((tp, CHUNKS, LANES), lambda i: (i + o, 0, 0)),
                  pl.BlockSpec((tp, NPAIR), lambda i: (i + o, 0)),
                  pl.BlockSpec((NPAIR * CHUNKS, LANES), lambda i: (0, 0)),
                  pl.BlockSpec((LANES, LANES), lambda i: (0, 0)),
                  pl.BlockSpec(memory_space=pltpu.VMEM)],
        out_specs=pl.BlockSpec((tp, NPAIR), lambda i: (i, 0)),
        out_shape=jax.ShapeDtypeStruct((n, NPAIR), F32),
        scratch_shapes=[pltpu.VMEM((TOK_UNROLL, NPAIR * WORD_ROWS, LANES), I32),
                        pltpu.VMEM((tp * 8, LANES), F32)],
        compiler_params=_cparams(("arbitrary",)),
        name="peer_u",
    )(idx, x3, gate, keep, sel, tab)


def _peer_v_kernel(idx_ref, w_ref, rep_ref, diag_ref, tab_ref, o_ref, g_ref, wx_ref):
    tp = w_ref.shape[0]
    wx_ref[...] = _mm_bf16_rhs(w_ref[...], rep_ref[...], 3)

    def tokens(i, carry):
        t0 = pl.multiple_of(i * TOK_UNROLL, TOK_UNROLL)
        _gather_rows(idx_ref, t0, tab_ref, g_ref)
        wx = wx_ref[pl.ds(t0, TOK_UNROLL), :]
        outs = []
        for u in range(TOK_UNROLL):
            g = pltpu.bitcast(g_ref[u], BF16)
            wm = (wx[u:u + 1, :] * diag_ref[...]).astype(BF16)
            outs.append(_dot(wm, g))
        o_ref[pl.ds(t0, TOK_UNROLL)] = jnp.stack(outs, axis=0)
        return carry

    lax.fori_loop(0, tp // TOK_UNROLL, tokens, 0)


def _peer_v(idx, w, tab, start, tp=128):
    n = w.shape[0]
    o = start // tp
    rep = jnp.repeat(jnp.eye(NPAIR, dtype=BF16), CHUNKS, axis=1)
    diag = (jnp.arange(CHUNKS)[:, None] == jnp.arange(NPAIR * CHUNKS)[None, :] % CHUNKS).astype(F32)
    return pl.pallas_call(
        _peer_v_kernel,
        grid=(n // tp,),
        in_specs=[pl.BlockSpec((tp, NPAIR), lambda i: (i + o, 0), memory_space=pltpu.SMEM),
                  pl.BlockSpec((tp, NPAIR), lambda i: (i, 0)),
                  pl.BlockSpec((NPAIR, NPAIR * CHUNKS), lambda i: (0, 0)),
                  pl.BlockSpec((CHUNKS, NPAIR * CHUNKS), lambda i: (0, 0)),
                  pl.BlockSpec(memory_space=pltpu.VMEM)],
        out_specs=pl.BlockSpec((tp, CHUNKS, LANES), lambda i: (i, 0, 0)),
        out_shape=jax.ShapeDtypeStruct((n, CHUNKS, LANES), F32),
        scratch_shapes=[pltpu.VMEM((TOK_UNROLL, NPAIR * WORD_ROWS, LANES), I32),
                        pltpu.VMEM((tp, NPAIR * CHUNKS), F32)],
        compiler_params=_cparams(("arbitrary",)),
        name="peer_v",
    )(idx, w, rep, diag, tab)


SC_LANES = 16
SC_WORKERS = 32
SC_ROWS = 64
SC_TOKENS = 16
SC_GROUP_SHARES, SC_SHARE_DEN = ((2, 3, 5, 6), (2, 3, 5, 6), (2, 3, 5, 6), (2, 3, 5)), 16
SC_COLS = 8


def _peer_v_sc(ids, wts, packed, start):
    m = wts.shape[0]
    per_worker = m // SC_WORKERS
    blocks = NPAIR // SC_ROWS
    words = D_MODEL // 2
    table = packed.reshape(packed.shape[0] // WORD_ROWS, words)
    ids_b = ids.reshape(ids.shape[0] * blocks, SC_ROWS)
    wts_f = wts.reshape(m * NPAIR)
    mesh = plsc.VectorSubcoreMesh(core_axis_name="c", subcore_axis_name="s")

    @functools.partial(
        pl.kernel, mesh=mesh, out_type=jax.ShapeDtypeStruct((m, D_MODEL), F32),
        scratch_types=[pltpu.VMEM((SC_TOKENS * blocks, SC_ROWS), I32), pltpu.VMEM((SC_TOKENS * NPAIR,), F32),
                       pltpu.VMEM((2, SC_ROWS, words), I32), pltpu.VMEM((D_MODEL,), F32),
                       pltpu.SemaphoreType.DMA((2,))],
        compiler_params=pltpu.CompilerParams(needs_layout_passes=False),
        name="peer_v_sc")
    def run(tab_hbm, ids_hbm, w_hbm, out_hbm, ids_v, w_v, rows_v, acc_v, sems):
        wid = lax.axis_index("s") * 2 + lax.axis_index("c")
        base = wid * per_worker
        zero = jnp.zeros((SC_LANES,), F32)

        def gather(u, b):
            return pltpu.make_async_copy(tab_hbm.at[ids_v.at[u * blocks + b]], rows_v.at[b % 2], sems.at[b % 2])

        @pl.loop(0, per_worker // SC_TOKENS)
        def _(tb):
            t0 = base + tb * SC_TOKENS
            pltpu.sync_copy(ids_hbm.at[pl.ds((start + t0) * blocks, SC_TOKENS * blocks)], ids_v)
            pltpu.sync_copy(w_hbm.at[pl.ds(t0 * NPAIR, SC_TOKENS * NPAIR)], w_v)

            @pl.loop(0, SC_TOKENS)
            def _(u):
                for j in range(D_MODEL // SC_LANES):
                    acc_v[pl.ds(j * SC_LANES, SC_LANES)] = zero
                gather(u, 0).start()
                for b in range(blocks):
                    if b + 1 < blocks:
                        gather(u, b + 1).start()
                    gather(u, b).wait()

                    for c0 in range(0, words // SC_LANES, SC_COLS):
                        def row(r, accs, b=b, c0=c0):
                            wk = plsc.load_gather(w_v, [jnp.full((SC_LANES,), u * NPAIR + b * SC_ROWS + r, I32)])
                            out = []
                            for j in range(SC_COLS):
                                w32 = rows_v[b % 2, r, pl.ds((c0 + j) * SC_LANES, SC_LANES)]
                                lo = lax.bitcast_convert_type(lax.shift_left(w32, 16), F32)
                                hi = lax.bitcast_convert_type(w32 & -65536, F32)
                                out += [accs[2 * j] + wk * lo, accs[2 * j + 1] + wk * hi]
                            return tuple(out)
                        accs = lax.fori_loop(0, SC_ROWS, row, (zero,) * (2 * SC_COLS))
                        for j in range(SC_COLS):
                            s_, l0 = divmod((c0 + j) * SC_LANES, LANES)
                            plsc.addupdate(acc_v.at[pl.ds(2 * s_ * LANES + l0, SC_LANES)], accs[2 * j])
                            plsc.addupdate(acc_v.at[pl.ds((2 * s_ + 1) * LANES + l0, SC_LANES)], accs[2 * j + 1])
                pltpu.sync_copy(acc_v, out_hbm.at[t0 + u])

    return run(table, ids_b, wts_f)


def _final_kernel(x1_ref, *refs, ends, has_tc, has_prev):
    refs = list(refs)
    o_ref = refs.pop()
    if has_prev:
        refs.pop()
    g_ref, gt_ref = refs.pop(), refs.pop()
    ptc_ref = refs.pop() if has_tc else None
    i = pl.program_id(0)
    if has_tc:
        p = jnp.concatenate([ptc_ref[:, c, :] for c in range(CHUNKS)], axis=1)
        parts = list(zip(refs, ends))
    else:
        p = refs[-1][...]
        parts = list(zip(refs[:-1], ends[:-1]))
    for ref, end in reversed(parts):
        p = jnp.where(i < end, ref[...], p)
    o_ref[...] = _rms(x1_ref[...] + gt_ref[0] * p, g_ref[...])


def _final(x1, sc_parts, peer_tc, gt, g, out_prev, tile0, n_total, tm=512):
    m = x1.shape[0]
    per_seq = m // gt.shape[0] // tm
    sizes = [p.shape[0] // tm for p in sc_parts]
    ends = [sum(sizes[:j + 1]) for j in range(len(sizes))]
    starts = [e - k for e, k in zip(ends, sizes)]
    tile = pl.BlockSpec((tm, D_MODEL), lambda i: (i, 0))
    part = lambda s0, k: pl.BlockSpec((tm, D_MODEL), lambda i: (jnp.clip(i - s0, 0, k - 1), 0))
    in_specs = [tile] + [part(s0, k) for s0, k in zip(starts, sizes)]
    args = [x1, *sc_parts]
    if peer_tc is not None:
        in_specs.append(pl.BlockSpec((tm, CHUNKS, LANES), lambda i: (jnp.maximum(i - ends[-1], 0), 0, 0)))
        args.append(peer_tc)
    in_specs += [pl.BlockSpec((1, 1, D_MODEL), lambda i: (i // per_seq, 0, 0)),
                 pl.BlockSpec((1, D_MODEL), lambda i: (0, 0))]
    args += [gt, g.reshape(1, D_MODEL)]
    aliases = {}
    if out_prev is not None:
        in_specs.append(pl.BlockSpec(memory_space=pl.ANY))
        aliases = {len(args): 0}
        args.append(out_prev)
    return pl.pallas_call(
        functools.partial(_final_kernel, ends=tuple(ends), has_tc=peer_tc is not None, has_prev=out_prev is not None),
        grid=(m // tm,),
        in_specs=in_specs,
        out_specs=pl.BlockSpec((tm, D_MODEL), lambda i: (i + tile0, 0)),
        out_shape=jax.ShapeDtypeStruct((n_total, D_MODEL), F32),
        input_output_aliases=aliases,
        compiler_params=_cparams(("parallel",)),
        name="final",
    )(*args)


def _block_diag(width, group, value):
    i = jnp.arange(width) // group
    return jnp.where(i[:, None] == i[None, :], value, 0.0).astype(F32)


def _layer(x, mod, final_g, norm_mix_g, w_in, conv_dw_w, conv_dw_b, conv_ln_w, conv_ln_b, rwkv_mu, rwkv_w0, rwkv_w2,
           rwkv_a0, rwkv_a2, rwkv_g2, rwkv_k_k, rwkv_k_a, rwkv_r_k, rwkv_gn_w, rwkv_gn_b, w_out, norm_ffn_g,
           peer_w_q, peer_sub_keys, peer_u, peer_v):
    bsz, s, _ = x.shape
    sh_mix, sc_mix, gt_mix, sh_ffn, sc_ffn, gt_ffn = (
        mod[:, i * D_MODEL:(i + 1) * D_MODEL].reshape(bsz, 1, D_MODEL) for i in range(6))

    w_in_b, w_out_b, w_q_b = w_in.astype(BF16), w_out.astype(BF16), peer_w_q.astype(BF16)
    zpad = jnp.zeros((LORA_W, RWKV_CH), F32)
    w2p, a2p = jnp.concatenate([rwkv_w2, zpad], axis=0), jnp.concatenate([zpad, rwkv_a2], axis=0)
    keys = peer_sub_keys.reshape(2 * PEER_HEADS, PEER_NKEYS, PEER_DQ // 2)
    tab_u, tab_v = _pack_table(peer_u), _pack_table(peer_v)

    nb = bsz // len(SC_GROUP_SHARES)
    n, m = bsz * s, nb * s
    out, after = None, None
    for gi, shares in enumerate(SC_GROUP_SHARES):
        b0 = gi * nb
        grp = lambda t: t[b0:b0 + nb]
        xg = x if after is None else lax.optimization_barrier((x, after))[0]
        yglu, prw = _in_proj(xg, grp(sh_mix), grp(sc_mix), norm_mix_g, w_in_b, b0)
        y_conv = _conv(yglu, conv_dw_w, conv_dw_b, conv_ln_w, conv_ln_b)
        y_rwkv = _rwkv(prw, rwkv_mu, rwkv_w0, w2p, rwkv_a0, a2p, rwkv_g2, rwkv_k_k, rwkv_k_a, rwkv_r_k.reshape(-1),
                       rwkv_gn_w, rwkv_gn_b)
        x1, u2 = _out_proj(y_conv, y_rwkv, w_out_b, xg, grp(gt_mix), norm_ffn_g, grp(sh_ffn), grp(sc_ffn), b0)
        u3 = u2.reshape(m, CHUNKS, LANES)
        idx, gate = _route(u3, w_q_b, keys)
        ids = lax.shift_right_logical(idx, 2)
        start, sc_parts = 0, []
        for share in shares:
            cnt = m * share // SC_SHARE_DEN
            after = _peer_u(idx, u3, gate, tab_u, start, cnt)
            sc_parts.append(_peer_v_sc(ids, after, tab_v, start))
            start += cnt
        peer_tc = None
        if start < m:
            after = _peer_u(idx, u3, gate, tab_u, start, m - start)
            peer_tc = _peer_v(idx, after, tab_v, start)
        out = _final(x1.reshape(m, D_MODEL), sc_parts, peer_tc, grp(gt_ffn), final_g, out, gi * (m // 512), n)
    return out.reshape(bsz, s, D_MODEL)


def kernel(x, c, ada_w, ada_b, norm_mix_g, w_in, conv_dw_w, conv_dw_b, conv_ln_w, conv_ln_b, rwkv_mu, rwkv_w0,
           rwkv_w2, rwkv_a0, rwkv_a2, rwkv_g2, rwkv_k_k, rwkv_k_a, rwkv_r_k, rwkv_gn_w, rwkv_gn_b, w_out,
           norm_ffn_g, peer_w_q, peer_sub_keys, peer_u, peer_v, final_g):
    depth = ada_w.shape[0]
    assert depth == 1, "one layer: the final norm is fused into the last layer's residual"
    mod = _mod(c, ada_w[0], ada_b[0])
    return _layer(x, mod, final_g, norm_mix_g[0], w_in[0], conv_dw_w[0], conv_dw_b[0], conv_ln_w[0],
                        conv_ln_b[0], rwkv_mu[0], rwkv_w0[0], rwkv_w2[0], rwkv_a0[0], rwkv_a2[0], rwkv_g2[0],
                        rwkv_k_k[0], rwkv_k_a[0], rwkv_r_k[0], rwkv_gn_w[0], rwkv_gn_b[0], w_out[0],
                        norm_ffn_g[0], peer_w_q[0], peer_sub_keys[0], peer_u[0], peer_v[0])
```

```python
import functools

import jax
import jax.numpy as jnp
from jax import lax
from jax.experimental import pallas as pl
from jax.experimental.pallas import tpu as pltpu
from jax.experimental.pallas import tpu_sc as plsc

F32 = jnp.float32
BF16 = jnp.bfloat16
I32 = jnp.int32
HI = lax.Precision.HIGHEST

D_MODEL = 1024
CONV_CH = 512
RWKV_CH = 512
HEAD = 64
CONV_WIDTH = 31
LORA_W = 64
LORA_A = 64
LORA_G = 128
RWKV_PROJ = 3 * RWKV_CH + LORA_W + LORA_A + LORA_G
IN_PROJ = 2 * CONV_CH + RWKV_PROJ
PEER_HEADS = 8
PEER_NKEYS = 128
PEER_EXPERTS = PEER_NKEYS * PEER_NKEYS
PEER_DQ = 256
PEER_TOPK = 16
NPAIR = PEER_HEADS * PEER_TOPK
RMS_EPS = 1e-6
LN_EPS = 1e-5
GN_EPS = 64e-5

LANES = 128
CHUNKS = D_MODEL // LANES
WORD_ROWS = CHUNKS // 2
CHUNK_T = 64
VMEM_LIMIT = 56 * 1024 * 1024


def _cparams(sem, vmem=None):
    return pltpu.CompilerParams(dimension_semantics=sem, vmem_limit_bytes=vmem or VMEM_LIMIT)


def _dot(a, b, precision=None):
    return jnp.dot(a, b, precision=precision, preferred_element_type=F32)


def _dot_nt(a, b, precision=None):
    return lax.dot_general(a, b, (((1,), (1,)), ((), ())), precision=precision, preferred_element_type=F32)


def _dot_tn(a, b, precision=None):
    return lax.dot_general(a, b, (((0,), (0,)), ((), ())), precision=precision, preferred_element_type=F32)


def _split(a):
    hi = a.astype(BF16)
    return hi, (a - hi.astype(F32)).astype(BF16)


def _mm(a, b, passes, dot=_dot):
    if passes == 6:
        return dot(a, b, HI)
    if passes == 1:
        return dot(a.astype(BF16), b.astype(BF16))
    ka = 0 if dot is _dot_tn else 1
    kb = 1 if dot is _dot_nt else 0
    ah, al = _split(a)
    bh, bl = _split(b)
    return dot(jnp.concatenate([ah, ah, al], axis=ka), jnp.concatenate([bh, bl, bh], axis=kb))


def _mm_bf16_rhs(a, b_bf16, terms=2):
    parts = []
    for _ in range(terms):
        p = a.astype(BF16)
        parts.append(p)
        a = a - p.astype(F32)
    return _dot(jnp.concatenate(parts, axis=1), jnp.concatenate([b_bf16] * terms, axis=0))


def _head_sums(x, bd_bf16):
    return jnp.concatenate([_mm_bf16_rhs(x[:, i * LANES:(i + 1) * LANES], bd_bf16, 3)
                            for i in range(x.shape[1] // LANES)], axis=1)


def _rms(x, g):
    return x * lax.rsqrt(jnp.mean(x * x, axis=-1, keepdims=True) + RMS_EPS) * g


def _mod_kernel(c_ref, w_ref, b_ref, o_ref):
    c = c_ref[...]
    o_ref[...] = _dot(c * jax.nn.sigmoid(c), w_ref[...], HI) + b_ref[...]


def _mod(c, w, b):
    bsz = c.shape[0]
    n = w.shape[1]
    tn = 1024
    return pl.pallas_call(
        _mod_kernel,
        grid=(n // tn,),
        in_specs=[pl.BlockSpec((bsz, D_MODEL), lambda j: (0, 0)),
                  pl.BlockSpec((D_MODEL, tn), lambda j: (0, j)),
                  pl.BlockSpec((1, tn), lambda j: (0, j))],
        out_specs=pl.BlockSpec((bsz, tn), lambda j: (0, j)),
        out_shape=jax.ShapeDtypeStruct((bsz, n), F32),
        compiler_params=_cparams(("parallel",)),
    )(c, w, b.reshape(1, n))


def _in_proj_kernel(x_ref, sh_ref, sc_ref, g_ref, w_ref, yglu_ref, prw_ref):
    u = _rms(x_ref[0], g_ref[...]) * (1.0 + sc_ref[0]) + sh_ref[0]
    p = _dot(u.astype(BF16), w_ref[...])
    yglu_ref[0] = p[:, :CONV_CH] * jax.nn.sigmoid(p[:, CONV_CH:2 * CONV_CH])
    prw_ref[0] = p[:, 2 * CONV_CH:]


def _in_proj(x, sh, sc, g, w_bf16, b0, tm=256):
    bsz, s = sh.shape[0], x.shape[1]
    vec = pl.BlockSpec((1, 1, D_MODEL), lambda b, i: (b, 0, 0))
    return pl.pallas_call(
        _in_proj_kernel,
        grid=(bsz, s // tm),
        in_specs=[pl.BlockSpec((1, tm, D_MODEL), lambda b, i: (b + b0, i, 0)), vec, vec,
                  pl.BlockSpec((1, D_MODEL), lambda b, i: (0, 0)),
                  pl.BlockSpec((D_MODEL, IN_PROJ), lambda b, i: (0, 0))],
        out_specs=[pl.BlockSpec((1, tm, CONV_CH), lambda b, i: (b, i, 0)),
                   pl.BlockSpec((1, tm, RWKV_PROJ), lambda b, i: (b, i, 0))],
        out_shape=[jax.ShapeDtypeStruct((bsz, s, CONV_CH), F32),
                   jax.ShapeDtypeStruct((bsz, s, RWKV_PROJ), F32)],
        compiler_params=_cparams(("parallel", "parallel")),
    )(x, sh, sc, g.reshape(1, D_MODEL), w_bf16)


CONV_HALO = 32
CONV_ROWS = 64


def _conv_kernel(cur_ref, prev_ref, w_ref, b_ref, lnw_ref, lnb_ref, o_ref, pad_ref):
    tc = cur_ref.shape[1]
    pad_ref[0, 0:CONV_HALO, :] = jnp.where(pl.program_id(1) > 0, prev_ref[0], 0.0)
    pad_ref[0, CONV_HALO:CONV_HALO + tc, :] = cur_ref[0]
    span = CONV_HALO + tc - 8
    for r in range(1, 8):
        pad_ref[r, 0:span, :] = pad_ref[0, r:r + span, :]
    off = CONV_HALO - (CONV_WIDTH - 1)
    for r0 in range(0, tc, CONV_ROWS):
        acc = jnp.zeros((CONV_ROWS, CONV_CH), F32)
        for j in range(CONV_WIDTH):
            q, r = divmod(off + j, 8)
            acc = acc + w_ref[j:j + 1, :] * pad_ref[r, r0 + 8 * q:r0 + 8 * q + CONV_ROWS, :]
        y = acc + b_ref[...]
        mu = jnp.mean(y, axis=-1, keepdims=True)
        yc = y - mu
        var = jnp.mean(yc * yc, axis=-1, keepdims=True)
        yn = yc * lax.rsqrt(var + LN_EPS) * lnw_ref[...] + lnb_ref[...]
        o_ref[0, r0:r0 + CONV_ROWS, :] = (yn * jax.nn.sigmoid(yn)).astype(o_ref.dtype)


def _conv(yglu, w, b, lnw, lnb, tc=256):
    bsz, s, _ = yglu.shape
    hb = tc // CONV_HALO
    row = lambda a: a.reshape(1, CONV_CH)
    const = lambda shp: pl.BlockSpec(shp, lambda bb, i: (0, 0))
    return pl.pallas_call(
        _conv_kernel,
        grid=(bsz, s // tc),
        in_specs=[pl.BlockSpec((1, tc, CONV_CH), lambda bb, i: (bb, i, 0)),
                  pl.BlockSpec((1, CONV_HALO, CONV_CH), lambda bb, i: (bb, jnp.maximum(i * hb - 1, 0), 0)),
                  const((CONV_WIDTH, CONV_CH)), const((1, CONV_CH)), const((1, CONV_CH)), const((1, CONV_CH))],
        out_specs=pl.BlockSpec((1, tc, CONV_CH), lambda bb, i: (bb, i, 0)),
        out_shape=jax.ShapeDtypeStruct((bsz, s, CONV_CH), BF16),
        scratch_shapes=[pltpu.VMEM((8, CONV_HALO + tc, CONV_CH), F32)],
        compiler_params=_cparams(("parallel", "parallel")),
    )(yglu, yglu, w, row(b), row(lnw), row(lnb))


def _softplus(z):
    return jnp.maximum(z, 0.0) + jnp.log1p(jnp.exp(-jnp.abs(z)))


def _rwkv_features(cur, prow, mu, w0, w2p, a0, a2p, g2, k_k, k_a, r_k, bd):
    rows = lax.broadcasted_iota(I32, cur.shape, 0)
    prev = jnp.where(rows == 0, prow, pltpu.roll(cur, 1, axis=0))
    xs = cur + mu * (prev - cur)
    r = xs[:, 0:RWKV_CH]
    k = xs[:, RWKV_CH:2 * RWKV_CH]
    v = xs[:, 2 * RWKV_CH:3 * RWKV_CH]
    wa = xs[:, 3 * RWKV_CH:3 * RWKV_CH + LORA_W + LORA_A]
    gl = xs[:, 3 * RWKV_CH + LORA_W + LORA_A:]
    w = -_softplus(-(w0 + _mm(jnp.tanh(wa), w2p, 3))) - 0.5
    a = jax.nn.sigmoid(a0 + _mm(wa, a2p, 3))
    g = _mm(jax.nn.sigmoid(gl), g2, 3)
    kk = k * k_k
    kkn = kk / jnp.maximum(jnp.sqrt(_head_sums(kk * kk, bd)), 1e-12)
    k2 = k * (1.0 + (a - 1.0) * k_a)
    bonus = _head_sums(r * k2 * r_k, bd) * v
    return r, k2, v, -jnp.exp(w), -kkn, kkn * a, g, bonus


RWKV_PASSES = {"gram": 3, "lakv": 3, "solve": 3, "out": 3, "state": 3}
SOLVE_BLK = 16


def _expand(x, lane_lo):
    return jnp.concatenate([jnp.where(lane_lo, x, 0.0), jnp.where(lane_lo, 0.0, x)], axis=0)


def _rwkv_kernel(p_ref, mu_ref, w0_ref, w2_ref, a0_ref, a2_ref, g2_ref, kk_ref, ka_ref, rk_ref, bd_ref,
                 bdm_ref, gnw_ref, gnb_ref, y_ref, s_ref, last_ref):
    C = CHUNK_T
    H2 = 2 * C

    @pl.when(pl.program_id(1) == 0)
    def _():
        s_ref[...] = jnp.zeros_like(s_ref)
        last_ref[...] = jnp.zeros_like(last_ref)

    cur = p_ref[0]
    r, k, vv, lw, a, b, gate, bonus = _rwkv_features(
        cur, last_ref[...], mu_ref[...], w0_ref[...], w2_ref[...], a0_ref[...], a2_ref[...], g2_ref[...],
        kk_ref[...], ka_ref[...], rk_ref[...], bd_ref[...])
    last_ref[...] = cur[C - 1:C, :]

    tri = (lax.broadcasted_iota(I32, (C, C), 0) >= lax.broadcasted_iota(I32, (C, C), 1)).astype(F32)
    cum = _dot(tri, lw, HI)
    tot = cum[C - 1:C, :]
    e_pos = jnp.exp(cum)
    e_neg = jnp.exp(-cum)
    e_rem = jnp.exp(tot - cum)
    rt = r * e_pos
    at = a * jnp.exp(cum - lw)
    kt = k * e_neg
    bt = b * e_neg
    kp = k * e_rem
    bp = b * e_rem
    pc = jnp.exp(tot)

    lane_lo = lax.broadcasted_iota(I32, (C, LANES), 1) < HEAD
    tt = lax.broadcasted_iota(I32, (H2, H2), 0) % C
    ss = lax.broadcasted_iota(I32, (H2, H2), 1) % C
    strict = tt > ss
    incl = tt >= ss
    near = tt // SOLVE_BLK == ss // SOLVE_BLK
    eye = lax.broadcasted_iota(I32, (LANES, LANES), 0) == lax.broadcasted_iota(I32, (LANES, LANES), 1)

    P = RWKV_PASSES
    pairs = range(RWKV_CH // LANES)
    each = lambda fn, *lists: [fn(*args) for args in zip(*lists)]
    sls = [slice(hp * LANES, (hp + 1) * LANES) for hp in pairs]
    ax, rx, bx, kx, vx, bpx, kpx = ([_expand(t[:, sl], lane_lo) for sl in sls] for t in (at, rt, bt, kt, vv, bp, kp))
    gram = each(lambda a_, r_, b_, k_: _mm(jnp.concatenate([a_, r_], axis=0), jnp.concatenate([b_, k_], axis=0),
                                           P["gram"], _dot_nt), ax, rx, bx, kx)
    l_ab = [jnp.where(strict, g_[:H2, :H2], 0.0) for g_ in gram]
    l_ak = [jnp.where(strict, g_[:H2, H2:], 0.0) for g_ in gram]
    m_r = [jnp.concatenate([jnp.where(incl, g_[H2:, :H2], 0.0), jnp.where(incl, g_[H2:, H2:], 0.0)], axis=1)
           for g_ in gram]
    dg = [jnp.where(near, l_, 0.0) for l_ in l_ab]
    lakv = each(lambda l_, v_: _mm(l_, v_, P["lakv"]), l_ak, vx)
    xf = each(lambda a_, lv_, l_, d_: jnp.concatenate([a_, lv_, l_ - d_], axis=1), ax, lakv, l_ab, dg)
    n_sq = SOLVE_BLK.bit_length() - 1
    for it in range(n_sq):
        xf = each(lambda x_, d_: x_ + _mm(d_, x_, P["solve"]), xf, dg)
        if it + 1 < n_sq:
            dg = [_mm(d_, d_, P["solve"]) for d_ in dg]
    x = [x_[:, :2 * LANES] for x_ in xf]
    f = [x_[:, 2 * LANES:] for x_ in xf]
    n_sq = (C // SOLVE_BLK).bit_length() - 1
    for it in range(n_sq):
        x = each(lambda x_, f_: x_ + _mm(f_, x_, P["solve"]), x, f)
        if it + 1 < n_sq:
            f = [_mm(f_, f_, P["solve"]) for f_ in f]
    zero = jnp.zeros((H2, LANES), F32)
    z = each(lambda x_, v_: jnp.concatenate([x_, jnp.concatenate([zero, v_], axis=1)], axis=0), x, vx)
    w1 = each(lambda m_, z_: _mm(m_, z_, P["out"]), m_r, z)
    w2 = each(lambda b_, k_, z_: _mm(jnp.concatenate([b_, k_], axis=0), z_, P["out"], _dot_tn), bpx, kpx, z)
    ys = []
    for hp in pairs:
        ra = rx[hp] + w1[hp][:, :LANES]
        ra = ra[:C] + ra[C:]
        y0 = w1[hp][:C, LANES:] + w1[hp][C:, LANES:]
        mt = w2[hp][:, :LANES] + jnp.where(eye, pc[:, sls[hp]], 0.0)
        s0 = s_ref[hp]
        ys.append(_mm(ra, s0, P["state"]) + y0)
        s_ref[hp] = _mm(mt, s0, P["state"]) + w2[hp][:, LANES:]
    yc = [y_ - _mm_bf16_rhs(y_, bdm_ref[...]) for y_ in ys]
    ys = [c_ * lax.rsqrt(_mm_bf16_rhs(c_ * c_, bdm_ref[...]) + GN_EPS) for c_ in yc]

    yn = jnp.concatenate(ys, axis=1) * gnw_ref[...] + gnb_ref[...]
    y_ref[0] = ((yn + bonus) * gate).astype(y_ref.dtype)


def _rwkv(prw, mu, w0, w2p, a0, a2p, g2, k_k, k_a, r_k, gnw, gnb):
    bsz, s, _ = prw.shape
    row = lambda t: t.reshape(1, -1)
    const = lambda shp: pl.BlockSpec(shp, lambda bb, c: (0, 0))
    vec = const((1, RWKV_CH))
    lora = const((LORA_W + LORA_A, RWKV_CH))
    return pl.pallas_call(
        _rwkv_kernel,
        grid=(bsz, s // CHUNK_T),
        in_specs=[pl.BlockSpec((1, CHUNK_T, RWKV_PROJ), lambda bb, c: (bb, c, 0)), const((1, RWKV_PROJ)),
                  vec, lora, vec, lora, const((LORA_G, RWKV_CH)), vec, vec, vec,
                  const((LANES, LANES)), const((LANES, LANES)), vec, vec],
        out_specs=pl.BlockSpec((1, CHUNK_T, RWKV_CH), lambda bb, c: (bb, c, 0)),
        out_shape=jax.ShapeDtypeStruct((bsz, s, RWKV_CH), BF16),
        scratch_shapes=[pltpu.VMEM((RWKV_CH // LANES, LANES, LANES), F32), pltpu.VMEM((1, RWKV_PROJ), F32)],
        compiler_params=_cparams(("parallel", "arbitrary")),
        name="rwkv",
    )(prw, row(mu), row(w0), w2p, row(a0), a2p, g2, row(k_k), row(k_a), row(r_k),
      _block_diag(LANES, HEAD, 1.0).astype(BF16), _block_diag(LANES, HEAD, 1.0 / HEAD).astype(BF16),
      row(gnw), row(gnb))


def _out_proj_kernel(yc_ref, yr_ref, wo_ref, x_ref, gt_ref, g_ref, sh_ref, sc_ref, x1_o, u2_o):
    mix = _dot(yc_ref[0], wo_ref[0:CONV_CH, :]) + _dot(yr_ref[0], wo_ref[CONV_CH:, :])
    x1 = x_ref[0] + gt_ref[0] * mix
    x1_o[0] = x1
    u2 = _rms(x1, g_ref[...]) * (1.0 + sc_ref[0]) + sh_ref[0]
    for c in range(CHUNKS):
        u2_o[0, :, c, :] = u2[:, c * LANES:(c + 1) * LANES]


def _out_proj(yc, yr, wo_bf16, x, gt, g, sh, sc, b0, tm=256):
    bsz, s = yc.shape[0], x.shape[1]
    vec = pl.BlockSpec((1, 1, D_MODEL), lambda b, i: (b, 0, 0))
    tile = lambda w: pl.BlockSpec((1, tm, w), lambda b, i: (b, i, 0))
    return pl.pallas_call(
        _out_proj_kernel,
        grid=(bsz, s // tm),
        in_specs=[tile(CONV_CH), tile(RWKV_CH), pl.BlockSpec((D_MODEL, D_MODEL), lambda b, i: (0, 0)),
                  pl.BlockSpec((1, tm, D_MODEL), lambda b, i: (b + b0, i, 0)), vec,
                  pl.BlockSpec((1, D_MODEL), lambda b, i: (0, 0)), vec, vec],
        out_specs=[tile(D_MODEL), pl.BlockSpec((1, tm, CHUNKS, LANES), lambda b, i: (b, i, 0, 0))],
        out_shape=[jax.ShapeDtypeStruct((bsz, s, D_MODEL), F32),
                   jax.ShapeDtypeStruct((bsz, s, CHUNKS, LANES), F32)],
        compiler_params=_cparams(("parallel", "parallel")),
        name="out_proj",
    )(yc, yr, wo_bf16, x, gt, g.reshape(1, D_MODEL), sh, sc)


HEADS_STEP = 4


def _topk_rows(ss, k):
    n, t = ss[0].shape
    rids = [lax.broadcasted_iota(I32, (8, t), 0) + r0 for r0 in range(0, n, 8)]
    ss = [[s[r0:r0 + 8] for r0 in range(0, n, 8)] for s in ss]
    vals, ids = [[] for _ in ss], [[] for _ in ss]
    for _ in range(k):
        tops = [_max_with_tag(s, [rids]) for s in ss]
        ss = [[jnp.where(r == j, -jnp.inf, c) for c, r in zip(s, rids)] for s, (_, (j,)) in zip(ss, tops)]
        for v, i, (m, (j,)) in zip(vals, ids, tops):
            v.append(m)
            i.append(j)
    return [(jnp.concatenate(v, axis=0), jnp.concatenate(i, axis=0)) for v, i in zip(vals, ids)]


def _max_with_tag(chunks, tags):
    vals, tags = list(chunks), [list(tg) for tg in tags]
    while len(vals) > 1:
        nv, nt = [], [[] for _ in tags]
        for a in range(0, len(vals) - 1, 2):
            first = vals[a] >= vals[a + 1]
            nv.append(jnp.maximum(vals[a], vals[a + 1]))
            for dst, tg in zip(nt, tags):
                dst.append(jnp.where(first, tg[a], tg[a + 1]))
        if len(vals) % 2:
            nv.append(vals[-1])
            for dst, tg in zip(nt, tags):
                dst.append(tg[-1])
        vals, tags = nv, nt
    v8 = vals[0]
    m = jnp.max(v8, axis=0, keepdims=True)
    big = jnp.iinfo(jnp.int32).max
    key = jnp.min(jnp.where(v8 == m, tags[0][0], big), axis=0, keepdims=True)
    out = [key]
    for tg in tags[1:]:
        out.append(jnp.sum(jnp.where(tags[0][0] == key, tg[0], 0), axis=0, keepdims=True))
    return m, out


def _route_kernel(u_ref, wq_ref, keys_ref, idx_o, gate_o, q_ref, idx_s, gate_s):
    tm = u_ref.shape[0]
    u = jnp.concatenate([u_ref[:, c, :] for c in range(CHUNKS)], axis=1)
    q = _dot(u.astype(BF16), wq_ref[...])
    for j in range(2 * PEER_HEADS):
        q_ref[j] = q[:, j * LANES:(j + 1) * LANES]
    K = PEER_TOPK
    tt = LANES
    tiles = range(tm // tt)
    row8 = lax.broadcasted_iota(I32, (8, tt), 0)

    def heads(i, carry):
        probs = [(i * HEADS_STEP + dh, lt) for dh in range(HEADS_STEP) for lt in tiles]
        scores = [_mm(keys_ref[2 * h + p], q_ref[2 * h + p, lt * tt:(lt + 1) * tt, :], 3, _dot_nt)
                  for h, lt in probs for p in range(2)]
        tops = _topk_rows(scores, K)
        ss, flats, eids = [], [], []
        for n_ in range(len(probs)):
            (av, ai), (bv, bi) = tops[2 * n_], tops[2 * n_ + 1]
            cs = [av[0:1] + bv[0:8], av[0:1] + bv[8:16]]
            cf = [row8, row8 + 8]
            ce = [ai[0:1] * PEER_NKEYS + bi[0:8], ai[0:1] * PEER_NKEYS + bi[8:16]]
            for x in range(1, 8):
                cs.append(jnp.where(row8 < K // (x + 1), av[x:x + 1] + bv[0:8], -jnp.inf))
                cf.append(x * K + row8)
                ce.append(ai[x:x + 1] * PEER_NKEYS + bi[0:8])
            cs.append(av[8:16] + bv[0:1])
            cf.append((row8 + 8) * K)
            ce.append(ai[8:16] * PEER_NKEYS + bi[0:1])
            ss.append(cs)
            flats.append(cf)
            eids.append(ce)
        best, experts = [[] for _ in probs], [[] for _ in probs]
        for _ in range(K):
            tops2 = [_max_with_tag(s, [fl, ei]) for s, fl, ei in zip(ss, flats, eids)]
            for n_, (m, (f, e)) in enumerate(tops2):
                best[n_].append(m)
                experts[n_].append(e)
            ss = [[jnp.where(fc == f, -jnp.inf, c) for c, fc in zip(s, fl)]
                  for s, fl, (_, (f, _e)) in zip(ss, flats, tops2)]
        for n_, (h, lt) in enumerate(probs):
            b = jnp.concatenate(best[n_], axis=0)
            e = jnp.exp(b - b[0:1])
            gate_s[lt, h] = e / jnp.sum(e, axis=0, keepdims=True)
            idx_s[lt, h] = jnp.concatenate(experts[n_], axis=0) * WORD_ROWS
        return carry

    lax.fori_loop(0, PEER_HEADS // HEADS_STEP, heads, 0)
    for lt in tiles:
        idx_o[lt * tt:(lt + 1) * tt, :] = jnp.transpose(idx_s[lt].reshape(NPAIR, tt))
        gate_o[lt * tt:(lt + 1) * tt, :] = jnp.transpose(gate_s[lt].reshape(NPAIR, tt))


def _route(u2, wq_bf16, keys, tm=256):
    n = u2.shape[0]
    oblk = pl.BlockSpec((tm, NPAIR), lambda i: (i, 0))
    return pl.pallas_call(
        _route_kernel,
        grid=(n // tm,),
        in_specs=[pl.BlockSpec((tm, CHUNKS, LANES), lambda i: (i, 0, 0)),
                  pl.BlockSpec((D_MODEL, PEER_HEADS * PEER_DQ), lambda i: (0, 0)),
                  pl.BlockSpec((2 * PEER_HEADS, PEER_NKEYS, PEER_DQ // 2), lambda i: (0, 0, 0))],
        out_specs=[oblk, oblk],
        out_shape=[jax.ShapeDtypeStruct((n, NPAIR), I32), jax.ShapeDtypeStruct((n, NPAIR), F32)],
        scratch_shapes=[pltpu.VMEM((2 * PEER_HEADS, tm, LANES), F32),
                        pltpu.VMEM((tm // LANES, PEER_HEADS, PEER_TOPK, LANES), I32),
                        pltpu.VMEM((tm // LANES, PEER_HEADS, PEER_TOPK, LANES), F32)],
        compiler_params=_cparams(("parallel",)),
        name="route",
    )(u2, wq_bf16, keys)


TOK_UNROLL = 32


def _pack_kernel(t_ref, o_ref):
    te = t_ref.shape[0]
    bits = lambda v: lax.bitcast_convert_type(v.astype(BF16).astype(F32), I32)
    for s in range(WORD_ROWS):
        lo = bits(t_ref[:, (2 * s) * LANES:(2 * s + 1) * LANES])
        hi = bits(t_ref[:, (2 * s + 1) * LANES:(2 * s + 2) * LANES])
        o_ref[pl.ds(s, te, stride=WORD_ROWS), :] = (hi & -65536) | lax.shift_right_logical(lo, 16)


def _pack_table(t, te=512):
    e = t.shape[0]
    return pl.pallas_call(
        _pack_kernel,
        grid=(e // te,),
        in_specs=[pl.BlockSpec((te, D_MODEL), lambda i: (i, 0))],
        out_specs=pl.BlockSpec((te * WORD_ROWS, LANES), lambda i: (i, 0)),
        out_shape=jax.ShapeDtypeStruct((e * WORD_ROWS, LANES), I32),
        compiler_params=_cparams(("parallel",)),
        name="pack_table",
    )(t)


def _gather_rows(idx_ref, t0, tab_ref, g_ref):
    for u in range(TOK_UNROLL):
        idx_row = idx_ref.at[t0 + u]
        for kk in range(NPAIR):
            row = pl.multiple_of(idx_row[kk], WORD_ROWS)
            g_ref[u, kk * WORD_ROWS:(kk + 1) * WORD_ROWS, :] = tab_ref[pl.ds(row, WORD_ROWS), :]


def _peer_u_kernel(idx_ref, x_ref, gate_ref, keep_ref, sel_ref, tab_ref, w_o, g_ref, p_ref):
    tp = x_ref.shape[0]

    def tokens(i, carry):
        t0 = pl.multiple_of(i * TOK_UNROLL, TOK_UNROLL)
        _gather_rows(idx_ref, t0, tab_ref, g_ref)
        xs = x_ref[pl.ds(t0, TOK_UNROLL)]
        parts = []
        for u in range(TOK_UNROLL):
            g = pltpu.bitcast(g_ref[u], BF16)
            xc = xs[u]
            xt = jnp.concatenate([xc] * (LANES // CHUNKS), axis=0).astype(BF16)
            r = _dot_nt(g, xt) * keep_ref[...]
            parts.append(jnp.sum(r.reshape(NPAIR // 16, 16 * CHUNKS, LANES), axis=1))
        p_ref[pl.ds(pl.multiple_of(t0 * 8, 8 * TOK_UNROLL), 8 * TOK_UNROLL), :] = jnp.concatenate(parts, axis=0)
        return carry

    lax.fori_loop(0, tp // TOK_UNROLL, tokens, 0)
    z = _mm_bf16_rhs(p_ref[...], sel_ref[...], 3)
    grp = lax.broadcasted_iota(I32, z.shape, 0) % 8 == lax.broadcasted_iota(I32, z.shape, 1) // 16
    h = jnp.sum(jnp.where(grp, z, 0.0).reshape(tp, 8, LANES), axis=1)
    w_o[...] = gate_ref[...] * (0.5 * h * (1.0 + lax.erf(h * (2.0 ** -0.5))))


def _peer_u(idx, x3, gate, tab, start, n, tp=128):
    o = start // tp
    rows = jnp.arange(NPAIR * CHUNKS)[:, None]
    lanes = jnp.arange(LANES)[None, :]
    keep = ((lanes % CHUNKS == rows % CHUNKS) & (lanes // CHUNKS == (rows // CHUNKS) % 16)).astype(F32)
    sel = (jnp.arange(LANES)[:, None] // CHUNKS == jnp.arange(LANES)[None, :] % 16).astype(BF16)
    return pl.pallas_call(
        _peer_u_kernel,
        grid=(n // tp,),
        in_specs=[pl.BlockSpec((tp, NPAIR), lambda i: (i + o, 0), memory_space=pltpu.SMEM),
                  pl.BlockSpec((tp, CHUNKS, LANES), lambda i: (i + o, 0, 0)),
                  pl.BlockSpec((tp, NPAIR), lambda i: (i + o, 0)),
                  pl.BlockSpec((NPAIR * CHUNKS, LANES), lambda i: (0, 0)),
                  pl.BlockSpec((LANES, LANES), lambda i: (0, 0)),
                  pl.BlockSpec(memory_space=pltpu.VMEM)],
        out_specs=pl.BlockSpec((tp, NPAIR), lambda i: (i, 0)),
        out_shape=jax.ShapeDtypeStruct((n, NPAIR), F32),
        scratch_shapes=[pltpu.VMEM((TOK_UNROLL, NPAIR * WORD_ROWS, LANES), I32),
                        pltpu.VMEM((tp * 8, LANES), F32)],
        compiler_params=_cparams(("arbitrary",)),
        name="peer_u",
    )(idx, x3, gate, keep, sel, tab)


def _peer_v_kernel(idx_ref, w_ref, rep_ref, diag_ref, tab_ref, o_ref, g_ref, wx_ref):
    tp = w_ref.shape[0]
    wx_ref[...] = _mm_bf16_rhs(w_ref[...], rep_ref[...], 3)

    def tokens(i, carry):
        t0 = pl.multiple_of(i * TOK_UNROLL, TOK_UNROLL)
        _gather_rows(idx_ref, t0, tab_ref, g_ref)
        wx = wx_ref[pl.ds(t0, TOK_UNROLL), :]
        outs = []
        for u in range(TOK_UNROLL):
            g = pltpu.bitcast(g_ref[u], BF16)
            wm = (wx[u:u + 1, :] * diag_ref[...]).astype(BF16)
            outs.append(_dot(wm, g))
        o_ref[pl.ds(t0, TOK_UNROLL)] = jnp.stack(outs, axis=0)
        return carry

    lax.fori_loop(0, tp // TOK_UNROLL, tokens, 0)


def _peer_v(idx, w, tab, start, tp=128):
    n = w.shape[0]
    o = start // tp
    rep = jnp.repeat(jnp.eye(NPAIR, dtype=BF16), CHUNKS, axis=1)
    diag = (jnp.arange(CHUNKS)[:, None] == jnp.arange(NPAIR * CHUNKS)[None, :] % CHUNKS).astype(F32)
    return pl.pallas_call(
        _peer_v_kernel,
        grid=(n // tp,),
        in_specs=[pl.BlockSpec((tp, NPAIR), lambda i: (i + o, 0), memory_space=pltpu.SMEM),
                  pl.BlockSpec((tp, NPAIR), lambda i: (i, 0)),
                  pl.BlockSpec((NPAIR, NPAIR * CHUNKS), lambda i: (0, 0)),
                  pl.BlockSpec((CHUNKS, NPAIR * CHUNKS), lambda i: (0, 0)),
                  pl.BlockSpec(memory_space=pltpu.VMEM)],
        out_specs=pl.BlockSpec((tp, CHUNKS, LANES), lambda i: (i, 0, 0)),
        out_shape=jax.ShapeDtypeStruct((n, CHUNKS, LANES), F32),
        scratch_shapes=[pltpu.VMEM((TOK_UNROLL, NPAIR * WORD_ROWS, LANES), I32),
                        pltpu.VMEM((tp, NPAIR * CHUNKS), F32)],
        compiler_params=_cparams(("arbitrary",)),
        name="peer_v",
    )(idx, w, rep, diag, tab)


SC_LANES = 16
SC_WORKERS = 32
SC_ROWS = 64
SC_TOKENS = 16
SC_GROUP_SHARES, SC_SHARE_DEN = ((2, 3, 5, 6), (2, 3, 5, 6), (2, 3, 5, 6), (2, 3, 5)), 16
SC_COLS = 8


def _peer_v_sc(ids, wts, packed, start):
    m = wts.shape[0]
    per_worker = m // SC_WORKERS
    blocks = NPAIR // SC_ROWS
    words = D_MODEL // 2
    table = packed.reshape(packed.shape[0] // WORD_ROWS, words)
    ids_b = ids.reshape(ids.shape[0] * blocks, SC_ROWS)
    wts_f = wts.reshape(m * NPAIR)
    mesh = plsc.VectorSubcoreMesh(core_axis_name="c", subcore_axis_name="s")

    @functools.partial(
        pl.kernel, mesh=mesh, out_type=jax.ShapeDtypeStruct((m, D_MODEL), F32),
        scratch_types=[pltpu.VMEM((SC_TOKENS * blocks, SC_ROWS), I32), pltpu.VMEM((SC_TOKENS * NPAIR,), F32),
                       pltpu.VMEM((2, SC_ROWS, words), I32), pltpu.VMEM((D_MODEL,), F32),
                       pltpu.SemaphoreType.DMA((2,))],
        compiler_params=pltpu.CompilerParams(needs_layout_passes=False),
        name="peer_v_sc")
    def run(tab_hbm, ids_hbm, w_hbm, out_hbm, ids_v, w_v, rows_v, acc_v, sems):
        wid = lax.axis_index("s") * 2 + lax.axis_index("c")
        base = wid * per_worker
        zero = jnp.zeros((SC_LANES,), F32)

        def gather(u, b):
            return pltpu.make_async_copy(tab_hbm.at[ids_v.at[u * blocks + b]], rows_v.at[b % 2], sems.at[b % 2])

        @pl.loop(0, per_worker // SC_TOKENS)
        def _(tb):
            t0 = base + tb * SC_TOKENS
            pltpu.sync_copy(ids_hbm.at[pl.ds((start + t0) * blocks, SC_TOKENS * blocks)], ids_v)
            pltpu.sync_copy(w_hbm.at[pl.ds(t0 * NPAIR, SC_TOKENS * NPAIR)], w_v)

            @pl.loop(0, SC_TOKENS)
            def _(u):
                for j in range(D_MODEL // SC_LANES):
                    acc_v[pl.ds(j * SC_LANES, SC_LANES)] = zero
                gather(u, 0).start()
                for b in range(blocks):
                    if b + 1 < blocks:
                        gather(u, b + 1).start()
                    gather(u, b).wait()

                    for c0 in range(0, words // SC_LANES, SC_COLS):
                        def row(r, accs, b=b, c0=c0):
                            wk = plsc.load_gather(w_v, [jnp.full((SC_LANES,), u * NPAIR + b * SC_ROWS + r, I32)])
                            out = []
                            for j in range(SC_COLS):
                                w32 = rows_v[b % 2, r, pl.ds((c0 + j) * SC_LANES, SC_LANES)]
                                lo = lax.bitcast_convert_type(lax.shift_left(w32, 16), F32)
                                hi = lax.bitcast_convert_type(w32 & -65536, F32)
                                out += [accs[2 * j] + wk * lo, accs[2 * j + 1] + wk * hi]
                            return tuple(out)
                        accs = lax.fori_loop(0, SC_ROWS, row, (zero,) * (2 * SC_COLS))
                        for j in range(SC_COLS):
                            s_, l0 = divmod((c0 + j) * SC_LANES, LANES)
                            plsc.addupdate(acc_v.at[pl.ds(2 * s_ * LANES + l0, SC_LANES)], accs[2 * j])
                            plsc.addupdate(acc_v.at[pl.ds((2 * s_ + 1) * LANES + l0, SC_LANES)], accs[2 * j + 1])
                pltpu.sync_copy(acc_v, out_hbm.at[t0 + u])

    return run(table, ids_b, wts_f)


def _final_kernel(x1_ref, *refs, ends, has_tc, has_prev):
    refs = list(refs)
    o_ref = refs.pop()
    if has_prev:
        refs.pop()
    g_ref, gt_ref = refs.pop(), refs.pop()
    ptc_ref = refs.pop() if has_tc else None
    i = pl.program_id(0)
    if has_tc:
        p = jnp.concatenate([ptc_ref[:, c, :] for c in range(CHUNKS)], axis=1)
        parts = list(zip(refs, ends))
    else:
        p = refs[-1][...]
        parts = list(zip(refs[:-1], ends[:-1]))
    for ref, end in reversed(parts):
        p = jnp.where(i < end, ref[...], p)
    o_ref[...] = _rms(x1_ref[...] + gt_ref[0] * p, g_ref[...])


def _final(x1, sc_parts, peer_tc, gt, g, out_prev, tile0, n_total, tm=512):
    m = x1.shape[0]
    per_seq = m // gt.shape[0] // tm
    sizes = [p.shape[0] // tm for p in sc_parts]
    ends = [sum(sizes[:j + 1]) for j in range(len(sizes))]
    starts = [e - k for e, k in zip(ends, sizes)]
    tile = pl.BlockSpec((tm, D_MODEL), lambda i: (i, 0))
    part = lambda s0, k: pl.BlockSpec((tm, D_MODEL), lambda i: (jnp.clip(i - s0, 0, k - 1), 0))
    in_specs = [tile] + [part(s0, k) for s0, k in zip(starts, sizes)]
    args = [x1, *sc_parts]
    if peer_tc is not None:
        in_specs.append(pl.BlockSpec((tm, CHUNKS, LANES), lambda i: (jnp.maximum(i - ends[-1], 0), 0, 0)))
        args.append(peer_tc)
    in_specs += [pl.BlockSpec((1, 1, D_MODEL), lambda i: (i // per_seq, 0, 0)),
                 pl.BlockSpec((1, D_MODEL), lambda i: (0, 0))]
    args += [gt, g.reshape(1, D_MODEL)]
    aliases = {}
    if out_prev is not None:
        in_specs.append(pl.BlockSpec(memory_space=pl.ANY))
        aliases = {len(args): 0}
        args.append(out_prev)
    return pl.pallas_call(
        functools.partial(_final_kernel, ends=tuple(ends), has_tc=peer_tc is not None, has_prev=out_prev is not None),
        grid=(m // tm,),
        in_specs=in_specs,
        out_specs=pl.BlockSpec((tm, D_MODEL), lambda i: (i + tile0, 0)),
        out_shape=jax.ShapeDtypeStruct((n_total, D_MODEL), F32),
        input_output_aliases=aliases,
        compiler_params=_cparams(("parallel",)),
        name="final",
    )(*args)


def _block_diag(width, group, value):
    i = jnp.arange(width) // group
    return jnp.where(i[:, None] == i[None, :], value, 0.0).astype(F32)


def _layer(x, mod, final_g, norm_mix_g, w_in, conv_dw_w, conv_dw_b, conv_ln_w, conv_ln_b, rwkv_mu, rwkv_w0, rwkv_w2,
           rwkv_a0, rwkv_a2, rwkv_g2, rwkv_k_k, rwkv_k_a, rwkv_r_k, rwkv_gn_w, rwkv_gn_b, w_out, norm_ffn_g,
           peer_w_q, peer_sub_keys, peer_u, peer_v):
    bsz, s, _ = x.shape
    sh_mix, sc_mix, gt_mix, sh_ffn, sc_ffn, gt_ffn = (
        mod[:, i * D_MODEL:(i + 1) * D_MODEL].reshape(bsz, 1, D_MODEL) for i in range(6))

    w_in_b, w_out_b, w_q_b = w_in.astype(BF16), w_out.astype(BF16), peer_w_q.astype(BF16)
    zpad = jnp.zeros((LORA_W, RWKV_CH), F32)
    w2p, a2p = jnp.concatenate([rwkv_w2, zpad], axis=0), jnp.concatenate([zpad, rwkv_a2], axis=0)
    keys = peer_sub_keys.reshape(2 * PEER_HEADS, PEER_NKEYS, PEER_DQ // 2)
    tab_u, tab_v = _pack_table(peer_u), _pack_table(peer_v)

    nb = bsz // len(SC_GROUP_SHARES)
    n, m = bsz * s, nb * s
    out, after = None, None
    for gi, shares in enumerate(SC_GROUP_SHARES):
        b0 = gi * nb
        grp = lambda t: t[b0:b0 + nb]
        xg = x if after is None else lax.optimization_barrier((x, after))[0]
        yglu, prw = _in_proj(xg, grp(sh_mix), grp(sc_mix), norm_mix_g, w_in_b, b0)
        y_conv = _conv(yglu, conv_dw_w, conv_dw_b, conv_ln_w, conv_ln_b)
        y_rwkv = _rwkv(prw, rwkv_mu, rwkv_w0, w2p, rwkv_a0, a2p, rwkv_g2, rwkv_k_k, rwkv_k_a, rwkv_r_k.reshape(-1),
                       rwkv_gn_w, rwkv_gn_b)
        x1, u2 = _out_proj(y_conv, y_rwkv, w_out_b, xg, grp(gt_mix), norm_ffn_g, grp(sh_ffn), grp(sc_ffn), b0)
        u3 = u2.reshape(m, CHUNKS, LANES)
        idx, gate = _route(u3, w_q_b, keys)
        ids = lax.shift_right_logical(idx, 2)
        start, sc_parts = 0, []
        for share in shares:
            cnt = m * share // SC_SHARE_DEN
            after = _peer_u(idx, u3, gate, tab_u, start, cnt)
            sc_parts.append(_peer_v_sc(ids, after, tab_v, start))
            start += cnt
        peer_tc = None
        if start < m:
            after = _peer_u(idx, u3, gate, tab_u, start, m - start)
            peer_tc = _peer_v(idx, after, tab_v, start)
        out = _final(x1.reshape(m, D_MODEL), sc_parts, peer_tc, grp(gt_ffn), final_g, out, gi * (m // 512), n)
    return out.reshape(bsz, s, D_MODEL)


def kernel(x, c, ada_w, ada_b, norm_mix_g, w_in, conv_dw_w, conv_dw_b, conv_ln_w, conv_ln_b, rwkv_mu, rwkv_w0,
           rwkv_w2, rwkv_a0, rwkv_a2, rwkv_g2, rwkv_k_k, rwkv_k_a, rwkv_r_k, rwkv_gn_w, rwkv_gn_b, w_out,
           norm_ffn_g, peer_w_q, peer_sub_keys, peer_u, peer_v, final_g):
    depth = ada_w.shape[0]
    assert depth == 1, "one layer: the final norm is fused into the last layer's residual"
    mod = _mod(c, ada_w[0], ada_b[0])
    return _layer(x, mod, final_g, norm_mix_g[0], w_in[0], conv_dw_w[0], conv_dw_b[0], conv_ln_w[0],
                        conv_ln_b[0], rwkv_mu[0], rwkv_w0[0], rwkv_w2[0], rwkv_a0[0], rwkv_a2[0], rwkv_g2[0],
                        rwkv_k_k[0], rwkv_k_a[0], rwkv_r_k[0], rwkv_gn_w[0], rwkv_gn_b[0], w_out[0],
                        norm_ffn_g[0], peer_w_q[0], peer_sub_keys[0], peer_u[0], peer_v[0])
```

```python
import functools

import jax
import jax.numpy as jnp
from jax import lax
from jax.experimental import pallas as pl
from jax.experimental.pallas import tpu as pltpu
from jax.experimental.pallas import tpu_sc as plsc

F32 = jnp.float32
BF16 = jnp.bfloat16
I32 = jnp.int32
HI = lax.Precision.HIGHEST

D_MODEL = 1024
CONV_CH = 512
RWKV_CH = 512
HEAD = 64
CONV_WIDTH = 31
LORA_W = 64
LORA_A = 64
LORA_G = 128
RWKV_PROJ = 3 * RWKV_CH + LORA_W + LORA_A + LORA_G
IN_PROJ = 2 * CONV_CH + RWKV_PROJ
PEER_HEADS = 8
PEER_NKEYS = 128
PEER_EXPERTS = PEER_NKEYS * PEER_NKEYS
PEER_DQ = 256
PEER_TOPK = 16
NPAIR = PEER_HEADS * PEER_TOPK
RMS_EPS = 1e-6
LN_EPS = 1e-5
GN_EPS = 64e-5

LANES = 128
CHUNKS = D_MODEL // LANES
WORD_ROWS = CHUNKS // 2
CHUNK_T = 64
VMEM_LIMIT = 56 * 1024 * 1024


def _cparams(sem, vmem=None):
    return pltpu.CompilerParams(dimension_semantics=sem, vmem_limit_bytes=vmem or VMEM_LIMIT)


def _dot(a, b, precision=None):
    return jnp.dot(a, b, precision=precision, preferred_element_type=F32)


def _dot_nt(a, b, precision=None):
    return lax.dot_general(a, b, (((1,), (1,)), ((), ())), precision=precision, preferred_element_type=F32)


def _dot_tn(a, b, precision=None):
    return lax.dot_general(a, b, (((0,), (0,)), ((), ())), precision=precision, preferred_element_type=F32)


def _split(a):
    hi = a.astype(BF16)
    return hi, (a - hi.astype(F32)).astype(BF16)


def _mm(a, b, passes, dot=_dot):
    if passes == 6:
        return dot(a, b, HI)
    if passes == 1:
        return dot(a.astype(BF16), b.astype(BF16))
    ka = 0 if dot is _dot_tn else 1
    kb = 1 if dot is _dot_nt else 0
    ah, al = _split(a)
    bh, bl = _split(b)
    return dot(jnp.concatenate([ah, ah, al], axis=ka), jnp.concatenate([bh, bl, bh], axis=kb))


def _mm_bf16_rhs(a, b_bf16, terms=2):
    parts = []
    for _ in range(terms):
        p = a.astype(BF16)
        parts.append(p)
        a = a - p.astype(F32)
    return _dot(jnp.concatenate(parts, axis=1), jnp.concatenate([b_bf16] * terms, axis=0))


def _head_sums(x, bd_bf16):
    return jnp.concatenate([_mm_bf16_rhs(x[:, i * LANES:(i + 1) * LANES], bd_bf16, 3)
                            for i in range(x.shape[1] // LANES)], axis=1)


def _rms(x, g):
    return x * lax.rsqrt(jnp.mean(x * x, axis=-1, keepdims=True) + RMS_EPS) * g


def _mod_kernel(c_ref, w_ref, b_ref, o_ref):
    c = c_ref[...]
    o_ref[...] = _dot(c * jax.nn.sigmoid(c), w_ref[...], HI) + b_ref[...]


def _mod(c, w, b):
    bsz = c.shape[0]
    n = w.shape[1]
    tn = 1024
    return pl.pallas_call(
        _mod_kernel,
        grid=(n // tn,),
        in_specs=[pl.BlockSpec((bsz, D_MODEL), lambda j: (0, 0)),
                  pl.BlockSpec((D_MODEL, tn), lambda j: (0, j)),
                  pl.BlockSpec((1, tn), lambda j: (0, j))],
        out_specs=pl.BlockSpec((bsz, tn), lambda j: (0, j)),
        out_shape=jax.ShapeDtypeStruct((bsz, n), F32),
        compiler_params=_cparams(("parallel",)),
    )(c, w, b.reshape(1, n))


def _in_proj_kernel(x_ref, sh_ref, sc_ref, g_ref, w_ref, yglu_ref, prw_ref):
    u = _rms(x_ref[0], g_ref[...]) * (1.0 + sc_ref[0]) + sh_ref[0]
    p = _dot(u.astype(BF16), w_ref[...])
    yglu_ref[0] = p[:, :CONV_CH] * jax.nn.sigmoid(p[:, CONV_CH:2 * CONV_CH])
    prw_ref[0] = p[:, 2 * CONV_CH:]


def _in_proj(x, sh, sc, g, w_bf16, b0, tm=256):
    bsz, s = sh.shape[0], x.shape[1]
    vec = pl.BlockSpec((1, 1, D_MODEL), lambda b, i: (b, 0, 0))
    return pl.pallas_call(
        _in_proj_kernel,
        grid=(bsz, s // tm),
        in_specs=[pl.BlockSpec((1, tm, D_MODEL), lambda b, i: (b + b0, i, 0)), vec, vec,
                  pl.BlockSpec((1, D_MODEL), lambda b, i: (0, 0)),
                  pl.BlockSpec((D_MODEL, IN_PROJ), lambda b, i: (0, 0))],
        out_specs=[pl.BlockSpec((1, tm, CONV_CH), lambda b, i: (b, i, 0)),
                   pl.BlockSpec((1, tm, RWKV_PROJ), lambda b, i: (b, i, 0))],
        out_shape=[jax.ShapeDtypeStruct((bsz, s, CONV_CH), F32),
                   jax.ShapeDtypeStruct((bsz, s, RWKV_PROJ), F32)],
        compiler_params=_cparams(("parallel", "parallel")),
    )(x, sh, sc, g.reshape(1, D_MODEL), w_bf16)


CONV_HALO = 32
CONV_ROWS = 64


def _conv_kernel(cur_ref, prev_ref, w_ref, b_ref, lnw_ref, lnb_ref, o_ref, pad_ref):
    tc = cur_ref.shape[1]
    pad_ref[0, 0:CONV_HALO, :] = jnp.where(pl.program_id(1) > 0, prev_ref[0], 0.0)
    pad_ref[0, CONV_HALO:CONV_HALO + tc, :] = cur_ref[0]
    span = CONV_HALO + tc - 8
    for r in range(1, 8):
        pad_ref[r, 0:span, :] = pad_ref[0, r:r + span, :]
    off = CONV_HALO - (CONV_WIDTH - 1)
    for r0 in range(0, tc, CONV_ROWS):
        acc = jnp.zeros((CONV_ROWS, CONV_CH), F32)
        for j in range(CONV_WIDTH):
            q, r = divmod(off + j, 8)
            acc = acc + w_ref[j:j + 1, :] * pad_ref[r, r0 + 8 * q:r0 + 8 * q + CONV_ROWS, :]
        y = acc + b_ref[...]
        mu = jnp.mean(y, axis=-1, keepdims=True)
        yc = y - mu
        var = jnp.mean(yc * yc, axis=-1, keepdims=True)
        yn = yc * lax.rsqrt(var + LN_EPS) * lnw_ref[...] + lnb_ref[...]
        o_ref[0, r0:r0 + CONV_ROWS, :] = (yn * jax.nn.sigmoid(yn)).astype(o_ref.dtype)


def _conv(yglu, w, b, lnw, lnb, tc=256):
    bsz, s, _ = yglu.shape
    hb = tc // CONV_HALO
    row = lambda a: a.reshape(1, CONV_CH)
    const = lambda shp: pl.BlockSpec(shp, lambda bb, i: (0, 0))
    return pl.pallas_call(
        _conv_kernel,
        grid=(bsz, s // tc),
        in_specs=[pl.BlockSpec((1, tc, CONV_CH), lambda bb, i: (bb, i, 0)),
                  pl.BlockSpec((1, CONV_HALO, CONV_CH), lambda bb, i: (bb, jnp.maximum(i * hb - 1, 0), 0)),
                  const((CONV_WIDTH, CONV_CH)), const((1, CONV_CH)), const((1, CONV_CH)), const((1, CONV_CH))],
        out_specs=pl.BlockSpec((1, tc, CONV_CH), lambda bb, i: (bb, i, 0)),
        out_shape=jax.ShapeDtypeStruct((bsz, s, CONV_CH), BF16),
        scratch_shapes=[pltpu.VMEM((8, CONV_HALO + tc, CONV_CH), F32)],
        compiler_params=_cparams(("parallel", "parallel")),
    )(yglu, yglu, w, row(b), row(lnw), row(lnb))


def _softplus(z):
    return jnp.maximum(z, 0.0) + jnp.log1p(jnp.exp(-jnp.abs(z)))


def _rwkv_features(cur, prow, mu, w0, w2p, a0, a2p, g2, k_k, k_a, r_k, bd):
    rows = lax.broadcasted_iota(I32, cur.shape, 0)
    prev = jnp.where(rows == 0, prow, pltpu.roll(cur, 1, axis=0))
    xs = cur + mu * (prev - cur)
    r = xs[:, 0:RWKV_CH]
    k = xs[:, RWKV_CH:2 * RWKV_CH]
    v = xs[:, 2 * RWKV_CH:3 * RWKV_CH]
    wa = xs[:, 3 * RWKV_CH:3 * RWKV_CH + LORA_W + LORA_A]
    gl = xs[:, 3 * RWKV_CH + LORA_W + LORA_A:]
    w = -_softplus(-(w0 + _mm(jnp.tanh(wa), w2p, 3))) - 0.5
    a = jax.nn.sigmoid(a0 + _mm(wa, a2p, 3))
    g = _mm(jax.nn.sigmoid(gl), g2, 3)
    kk = k * k_k
    kkn = kk / jnp.maximum(jnp.sqrt(_head_sums(kk * kk, bd)), 1e-12)
    k2 = k * (1.0 + (a - 1.0) * k_a)
    bonus = _head_sums(r * k2 * r_k, bd) * v
    return r, k2, v, -jnp.exp(w), -kkn, kkn * a, g, bonus


RWKV_PASSES = {"gram": 3, "lakv": 3, "solve": 3, "out": 3, "state": 3}
SOLVE_BLK = 16


def _expand(x, lane_lo):
    return jnp.concatenate([jnp.where(lane_lo, x, 0.0), jnp.where(lane_lo, 0.0, x)], axis=0)


def _rwkv_kernel(p_ref, mu_ref, w0_ref, w2_ref, a0_ref, a2_ref, g2_ref, kk_ref, ka_ref, rk_ref, bd_ref,
                 bdm_ref, gnw_ref, gnb_ref, y_ref, s_ref, last_ref):
    C = CHUNK_T
    H2 = 2 * C

    @pl.when(pl.program_id(1) == 0)
    def _():
        s_ref[...] = jnp.zeros_like(s_ref)
        last_ref[...] = jnp.zeros_like(last_ref)

    cur = p_ref[0]
    r, k, vv, lw, a, b, gate, bonus = _rwkv_features(
        cur, last_ref[...], mu_ref[...], w0_ref[...], w2_ref[...], a0_ref[...], a2_ref[...], g2_ref[...],
        kk_ref[...], ka_ref[...], rk_ref[...], bd_ref[...])
    last_ref[...] = cur[C - 1:C, :]

    tri = (lax.broadcasted_iota(I32, (C, C), 0) >= lax.broadcasted_iota(I32, (C, C), 1)).astype(F32)
    cum = _dot(tri, lw, HI)
    tot = cum[C - 1:C, :]
    e_pos = jnp.exp(cum)
    e_neg = jnp.exp(-cum)
    e_rem = jnp.exp(tot - cum)
    rt = r * e_pos
    at = a * jnp.exp(cum - lw)
    kt = k * e_neg
    bt = b * e_neg
    kp = k * e_rem
    bp = b * e_rem
    pc = jnp.exp(tot)

    lane_lo = lax.broadcasted_iota(I32, (C, LANES), 1) < HEAD
    tt = lax.broadcasted_iota(I32, (H2, H2), 0) % C
    ss = lax.broadcasted_iota(I32, (H2, H2), 1) % C
    strict = tt > ss
    incl = tt >= ss
    near = tt // SOLVE_BLK == ss // SOLVE_BLK
    eye = lax.broadcasted_iota(I32, (LANES, LANES), 0) == lax.broadcasted_iota(I32, (LANES, LANES), 1)

    P = RWKV_PASSES
    pairs = range(RWKV_CH // LANES)
    each = lambda fn, *lists: [fn(*args) for args in zip(*lists)]
    sls = [slice(hp * LANES, (hp + 1) * LANES) for hp in pairs]
    ax, rx, bx, kx, vx, bpx, kpx = ([_expand(t[:, sl], lane_lo) for sl in sls] for t in (at, rt, bt, kt, vv, bp, kp))
    gram = each(lambda a_, r_, b_, k_: _mm(jnp.concatenate([a_, r_], axis=0), jnp.concatenate([b_, k_], axis=0),
                                           P["gram"], _dot_nt), ax, rx, bx, kx)
    l_ab = [jnp.where(strict, g_[:H2, :H2], 0.0) for g_ in gram]
    l_ak = [jnp.where(strict, g_[:H2, H2:], 0.0) for g_ in gram]
    m_r = [jnp.concatenate([jnp.where(incl, g_[H2:, :H2], 0.0), jnp.where(incl, g_[H2:, H2:], 0.0)], axis=1)
           for g_ in gram]
    dg = [jnp.where(near, l_, 0.0) for l_ in l_ab]
    lakv = each(lambda l_, v_: _mm(l_, v_, P["lakv"]), l_ak, vx)
    xf = each(lambda a_, lv_, l_, d_: jnp.concatenate([a_, lv_, l_ - d_], axis=1), ax, lakv, l_ab, dg)
    n_sq = SOLVE_BLK.bit_length() - 1
    for it in range(n_sq):
        xf = each(lambda x_, d_: x_ + _mm(d_, x_, P["solve"]), xf, dg)
        if it + 1 < n_sq:
            dg = [_mm(d_, d_, P["solve"]) for d_ in dg]
    x = [x_[:, :2 * LANES] for x_ in xf]
    f = [x_[:, 2 * LANES:] for x_ in xf]
    n_sq = (C // SOLVE_BLK).bit_length() - 1
    for it in range(n_sq):
        x = each(lambda x_, f_: x_ + _mm(f_, x_, P["solve"]), x, f)
        if it + 1 < n_sq:
            f = [_mm(f_, f_, P["solve"]) for f_ in f]
    zero = jnp.zeros((H2, LANES), F32)
    z = each(lambda x_, v_: jnp.concatenate([x_, jnp.concatenate([zero, v_], axis=1)], axis=0), x, vx)
    w1 = each(lambda m_, z_: _mm(m_, z_, P["out"]), m_r, z)
    w2 = each(lambda b_, k_, z_: _mm(jnp.concatenate([b_, k_], axis=0), z_, P["out"], _dot_tn), bpx, kpx, z)
    ys = []
    for hp in pairs:
        ra = rx[hp] + w1[hp][:, :LANES]
        ra = ra[:C] + ra[C:]
        y0 = w1[hp][:C, LANES:] + w1[hp][C:, LANES:]
        mt = w2[hp][:, :LANES] + jnp.where(eye, pc[:, sls[hp]], 0.0)
        s0 = s_ref[hp]
        ys.append(_mm(ra, s0, P["state"]) + y0)
        s_ref[hp] = _mm(mt, s0, P["state"]) + w2[hp][:, LANES:]
    yc = [y_ - _mm_bf16_rhs(y_, bdm_ref[...]) for y_ in ys]
    ys = [c_ * lax.rsqrt(_mm_bf16_rhs(c_ * c_, bdm_ref[...]) + GN_EPS) for c_ in yc]

    yn = jnp.concatenate(ys, axis=1) * gnw_ref[...] + gnb_ref[...]
    y_ref[0] = ((yn + bonus) * gate).astype(y_ref.dtype)


def _rwkv(prw, mu, w0, w2p, a0, a2p, g2, k_k, k_a, r_k, gnw, gnb):
    bsz, s, _ = prw.shape
    row = lambda t: t.reshape(1, -1)
    const = lambda shp: pl.BlockSpec(shp, lambda bb, c: (0, 0))
    vec = const((1, RWKV_CH))
    lora = const((LORA_W + LORA_A, RWKV_CH))
    return pl.pallas_call(
        _rwkv_kernel,
        grid=(bsz, s // CHUNK_T),
        in_specs=[pl.BlockSpec((1, CHUNK_T, RWKV_PROJ), lambda bb, c: (bb, c, 0)), const((1, RWKV_PROJ)),
                  vec, lora, vec, lora, const((LORA_G, RWKV_CH)), vec, vec, vec,
                  const((LANES, LANES)), const((LANES, LANES)), vec, vec],
        out_specs=pl.BlockSpec((1, CHUNK_T, RWKV_CH), lambda bb, c: (bb, c, 0)),
        out_shape=jax.ShapeDtypeStruct((bsz, s, RWKV_CH), BF16),
        scratch_shapes=[pltpu.VMEM((RWKV_CH // LANES, LANES, LANES), F32), pltpu.VMEM((1, RWKV_PROJ), F32)],
        compiler_params=_cparams(("parallel", "arbitrary")),
        name="rwkv",
    )(prw, row(mu), row(w0), w2p, row(a0), a2p, g2, row(k_k), row(k_a), row(r_k),
      _block_diag(LANES, HEAD, 1.0).astype(BF16), _block_diag(LANES, HEAD, 1.0 / HEAD).astype(BF16),
      row(gnw), row(gnb))


def _out_proj_kernel(yc_ref, yr_ref, wo_ref, x_ref, gt_ref, g_ref, sh_ref, sc_ref, x1_o, u2_o):
    mix = _dot(yc_ref[0], wo_ref[0:CONV_CH, :]) + _dot(yr_ref[0], wo_ref[CONV_CH:, :])
    x1 = x_ref[0] + gt_ref[0] * mix
    x1_o[0] = x1
    u2 = _rms(x1, g_ref[...]) * (1.0 + sc_ref[0]) + sh_ref[0]
    for c in range(CHUNKS):
        u2_o[0, :, c, :] = u2[:, c * LANES:(c + 1) * LANES]


def _out_proj(yc, yr, wo_bf16, x, gt, g, sh, sc, b0, tm=256):
    bsz, s = yc.shape[0], x.shape[1]
    vec = pl.BlockSpec((1, 1, D_MODEL), lambda b, i: (b, 0, 0))
    tile = lambda w: pl.BlockSpec((1, tm, w), lambda b, i: (b, i, 0))
    return pl.pallas_call(
        _out_proj_kernel,
        grid=(bsz, s // tm),
        in_specs=[tile(CONV_CH), tile(RWKV_CH), pl.BlockSpec((D_MODEL, D_MODEL), lambda b, i: (0, 0)),
                  pl.BlockSpec((1, tm, D_MODEL), lambda b, i: (b + b0, i, 0)), vec,
                  pl.BlockSpec((1, D_MODEL), lambda b, i: (0, 0)), vec, vec],
        out_specs=[tile(D_MODEL), pl.BlockSpec((1, tm, CHUNKS, LANES), lambda b, i: (b, i, 0, 0))],
        out_shape=[jax.ShapeDtypeStruct((bsz, s, D_MODEL), F32),
                   jax.ShapeDtypeStruct((bsz, s, CHUNKS, LANES), F32)],
        compiler_params=_cparams(("parallel", "parallel")),
        name="out_proj",
    )(yc, yr, wo_bf16, x, gt, g.reshape(1, D_MODEL), sh, sc)


HEADS_STEP = 4


def _topk_rows(ss, k):
    n, t = ss[0].shape
    rids = [lax.broadcasted_iota(I32, (8, t), 0) + r0 for r0 in range(0, n, 8)]
    ss = [[s[r0:r0 + 8] for r0 in range(0, n, 8)] for s in ss]
    vals, ids = [[] for _ in ss], [[] for _ in ss]
    for _ in range(k):
        tops = [_max_with_tag(s, [rids]) for s in ss]
        ss = [[jnp.where(r == j, -jnp.inf, c) for c, r in zip(s, rids)] for s, (_, (j,)) in zip(ss, tops)]
        for v, i, (m, (j,)) in zip(vals, ids, tops):
            v.append(m)
            i.append(j)
    return [(jnp.concatenate(v, axis=0), jnp.concatenate(i, axis=0)) for v, i in zip(vals, ids)]


def _max_with_tag(chunks, tags):
    vals, tags = list(chunks), [list(tg) for tg in tags]
    while len(vals) > 1:
        nv, nt = [], [[] for _ in tags]
        for a in range(0, len(vals) - 1, 2):
            first = vals[a] >= vals[a + 1]
            nv.append(jnp.maximum(vals[a], vals[a + 1]))
            for dst, tg in zip(nt, tags):
                dst.append(jnp.where(first, tg[a], tg[a + 1]))
        if len(vals) % 2:
            nv.append(vals[-1])
            for dst, tg in zip(nt, tags):
                dst.append(tg[-1])
        vals, tags = nv, nt
    v8 = vals[0]
    m = jnp.max(v8, axis=0, keepdims=True)
    big = jnp.iinfo(jnp.int32).max
    key = jnp.min(jnp.where(v8 == m, tags[0][0], big), axis=0, keepdims=True)
    out = [key]
    for tg in tags[1:]:
        out.append(jnp.sum(jnp.where(tags[0][0] == key, tg[0], 0), axis=0, keepdims=True))
    return m, out


def _route_kernel(u_ref, wq_ref, keys_ref, idx_o, gate_o, q_ref, idx_s, gate_s):
    tm = u_ref.shape[0]
    u = jnp.concatenate([u_ref[:, c, :] for c in range(CHUNKS)], axis=1)
    q = _dot(u.astype(BF16), wq_ref[...])
    for j in range(2 * PEER_HEADS):
        q_ref[j] = q[:, j * LANES:(j + 1) * LANES]
    K = PEER_TOPK
    tt = LANES
    tiles = range(tm // tt)
    row8 = lax.broadcasted_iota(I32, (8, tt), 0)

    def heads(i, carry):
        probs = [(i * HEADS_STEP + dh, lt) for dh in range(HEADS_STEP) for lt in tiles]
        scores = [_mm(keys_ref[2 * h + p], q_ref[2 * h + p, lt * tt:(lt + 1) * tt, :], 3, _dot_nt)
                  for h, lt in probs for p in range(2)]
        tops = _topk_rows(scores, K)
        ss, flats, eids = [], [], []
        for n_ in range(len(probs)):
            (av, ai), (bv, bi) = tops[2 * n_], tops[2 * n_ + 1]
            cs = [av[0:1] + bv[0:8], av[0:1] + bv[8:16]]
            cf = [row8, row8 + 8]
            ce = [ai[0:1] * PEER_NKEYS + bi[0:8], ai[0:1] * PEER_NKEYS + bi[8:16]]
            for x in range(1, 8):
                cs.append(jnp.where(row8 < K // (x + 1), av[x:x + 1] + bv[0:8], -jnp.inf))
                cf.append(x * K + row8)
                ce.append(ai[x:x + 1] * PEER_NKEYS + bi[0:8])
            cs.append(av[8:16] + bv[0:1])
            cf.append((row8 + 8) * K)
            ce.append(ai[8:16] * PEER_NKEYS + bi[0:1])
            ss.append(cs)
            flats.append(cf)
            eids.append(ce)
        best, experts = [[] for _ in probs], [[] for _ in probs]
        for _ in range(K):
            tops2 = [_max_with_tag(s, [fl, ei]) for s, fl, ei in zip(ss, flats, eids)]
            for n_, (m, (f, e)) in enumerate(tops2):
                best[n_].append(m)
                experts[n_].append(e)
            ss = [[jnp.where(fc == f, -jnp.inf, c) for c, fc in zip(s, fl)]
                  for s, fl, (_, (f, _e)) in zip(ss, flats, tops2)]
        for n_, (h, lt) in enumerate(probs):
            b = jnp.concatenate(best[n_], axis=0)
            e = jnp.exp(b - b[0:1])
            gate_s[lt, h] = e / jnp.sum(e, axis=0, keepdims=True)
            idx_s[lt, h] = jnp.concatenate(experts[n_], axis=0) * WORD_ROWS
        return carry

    lax.fori_loop(0, PEER_HEADS // HEADS_STEP, heads, 0)
    for lt in tiles:
        idx_o[lt * tt:(lt + 1) * tt, :] = jnp.transpose(idx_s[lt].reshape(NPAIR, tt))
        gate_o[lt * tt:(lt + 1) * tt, :] = jnp.transpose(gate_s[lt].reshape(NPAIR, tt))


def _route(u2, wq_bf16, keys, tm=256):
    n = u2.shape[0]
    oblk = pl.BlockSpec((tm, NPAIR), lambda i: (i, 0))
    return pl.pallas_call(
        _route_kernel,
        grid=(n // tm,),
        in_specs=[pl.BlockSpec((tm, CHUNKS, LANES), lambda i: (i, 0, 0)),
                  pl.BlockSpec((D_MODEL, PEER_HEADS * PEER_DQ), lambda i: (0, 0)),
                  pl.BlockSpec((2 * PEER_HEADS, PEER_NKEYS, PEER_DQ // 2), lambda i: (0, 0, 0))],
        out_specs=[oblk, oblk],
        out_shape=[jax.ShapeDtypeStruct((n, NPAIR), I32), jax.ShapeDtypeStruct((n, NPAIR), F32)],
        scratch_shapes=[pltpu.VMEM((2 * PEER_HEADS, tm, LANES), F32),
                        pltpu.VMEM((tm // LANES, PEER_HEADS, PEER_TOPK, LANES), I32),
                        pltpu.VMEM((tm // LANES, PEER_HEADS, PEER_TOPK, LANES), F32)],
        compiler_params=_cparams(("parallel",)),
        name="route",
    )(u2, wq_bf16, keys)


TOK_UNROLL = 32


def _pack_kernel(t_ref, o_ref):
    te = t_ref.shape[0]
    bits = lambda v: lax.bitcast_convert_type(v.astype(BF16).astype(F32), I32)
    for s in range(WORD_ROWS):
        lo = bits(t_ref[:, (2 * s) * LANES:(2 * s + 1) * LANES])
        hi = bits(t_ref[:, (2 * s + 1) * LANES:(2 * s + 2) * LANES])
        o_ref[pl.ds(s, te, stride=WORD_ROWS), :] = (hi & -65536) | lax.shift_right_logical(lo, 16)


def _pack_table(t, te=512):
    e = t.shape[0]
    return pl.pallas_call(
        _pack_kernel,
        grid=(e // te,),
        in_specs=[pl.BlockSpec((te, D_MODEL), lambda i: (i, 0))],
        out_specs=pl.BlockSpec((te * WORD_ROWS, LANES), lambda i: (i, 0)),
        out_shape=jax.ShapeDtypeStruct((e * WORD_ROWS, LANES), I32),
        compiler_params=_cparams(("parallel",)),
        name="pack_table",
    )(t)


def _gather_rows(idx_ref, t0, tab_ref, g_ref):
    for u in range(TOK_UNROLL):
        idx_row = idx_ref.at[t0 + u]
        for kk in range(NPAIR):
            row = pl.multiple_of(idx_row[kk], WORD_ROWS)
            g_ref[u, kk * WORD_ROWS:(kk + 1) * WORD_ROWS, :] = tab_ref[pl.ds(row, WORD_ROWS), :]


def _peer_u_kernel(idx_ref, x_ref, gate_ref, keep_ref, sel_ref, tab_ref, w_o, g_ref, p_ref):
    tp = x_ref.shape[0]

    def tokens(i, carry):
        t0 = pl.multiple_of(i * TOK_UNROLL, TOK_UNROLL)
        _gather_rows(idx_ref, t0, tab_ref, g_ref)
        xs = x_ref[pl.ds(t0, TOK_UNROLL)]
        parts = []
        for u in range(TOK_UNROLL):
            g = pltpu.bitcast(g_ref[u], BF16)
            xc = xs[u]
            xt = jnp.concatenate([xc] * (LANES // CHUNKS), axis=0).astype(BF16)
            r = _dot_nt(g, xt) * keep_ref[...]
            parts.append(jnp.sum(r.reshape(NPAIR // 16, 16 * CHUNKS, LANES), axis=1))
        p_ref[pl.ds(pl.multiple_of(t0 * 8, 8 * TOK_UNROLL), 8 * TOK_UNROLL), :] = jnp.concatenate(parts, axis=0)
        return carry

    lax.fori_loop(0, tp // TOK_UNROLL, tokens, 0)
    z = _mm_bf16_rhs(p_ref[...], sel_ref[...], 3)
    grp = lax.broadcasted_iota(I32, z.shape, 0) % 8 == lax.broadcasted_iota(I32, z.shape, 1) // 16
    h = jnp.sum(jnp.where(grp, z, 0.0).reshape(tp, 8, LANES), axis=1)
    w_o[...] = gate_ref[...] * (0.5 * h * (1.0 + lax.erf(h * (2.0 ** -0.5))))


def _peer_u(idx, x3, gate, tab, start, n, tp=128):
    o = start // tp
    rows = jnp.arange(NPAIR * CHUNKS)[:, None]
    lanes = jnp.arange(LANES)[None, :]
    keep = ((lanes % CHUNKS == rows % CHUNKS) & (lanes // CHUNKS == (rows // CHUNKS) % 16)).astype(F32)
    sel = (jnp.arange(LANES)[:, None] // CHUNKS == jnp.arange(LANES)[None, :] % 16).astype(BF16)
    return pl.pallas_call(
        _peer_u_kernel,
        grid=(n // tp,),
        in_specs=[pl.BlockSpec((tp, NPAIR), lambda i: (i + o, 0), memory_space=pltpu.SMEM),
                  pl.BlockSpec((tp, CHUNKS, LANES), lambda i: (i + o, 0, 0)),
                  pl.BlockSpec((tp, NPAIR), lambda i: (i + o, 0)),
                  pl.BlockSpec((NPAIR * CHUNKS, LANES), lambda i: (0, 0)),
                  pl.BlockSpec((LANES, LANES), lambda i: (0, 0)),
                  pl.BlockSpec(memory_space=pltpu.VMEM)],
        out_specs=pl.BlockSpec((tp, NPAIR), lambda i: (i, 0)),
        out_shape=jax.ShapeDtypeStruct((n, NPAIR), F32),
        scratch_shapes=[pltpu.VMEM((TOK_UNROLL, NPAIR * WORD_ROWS, LANES), I32),
                        pltpu.VMEM((tp * 8, LANES), F32)],
        compiler_params=_cparams(("arbitrary",)),
        name="peer_u",
    )(idx, x3, gate, keep, sel, tab)


def _peer_v_kernel(idx_ref, w_ref, rep_ref, diag_ref, tab_ref, o_ref, g_ref, wx_ref):
    tp = w_ref.shape[0]
    wx_ref[...] = _mm_bf16_rhs(w_ref[...], rep_ref[...], 3)

    def tokens(i, carry):
        t0 = pl.multiple_of(i * TOK_UNROLL, TOK_UNROLL)
        _gather_rows(idx_ref, t0, tab_ref, g_ref)
        wx = wx_ref[pl.ds(t0, TOK_UNROLL), :]
        outs = []
        for u in range(TOK_UNROLL):
            g = pltpu.bitcast(g_ref[u], BF16)
            wm = (wx[u:u + 1, :] * diag_ref[...]).astype(BF16)
            outs.append(_dot(wm, g))
        o_ref[pl.ds(t0, TOK_UNROLL)] = jnp.stack(outs, axis=0)
        return carry

    lax.fori_loop(0, tp // TOK_UNROLL, tokens, 0)


def _peer_v(idx, w, tab, start, tp=128):
    n = w.shape[0]
    o = start // tp
    rep = jnp.repeat(jnp.eye(NPAIR, dtype=BF16), CHUNKS, axis=1)
    diag = (jnp.arange(CHUNKS)[:, None] == jnp.arange(NPAIR * CHUNKS)[None, :] % CHUNKS).astype(F32)
    return pl.pallas_call(
        _peer_v_kernel,
        grid=(n // tp,),
        in_specs=[pl.BlockSpec((tp, NPAIR), lambda i: (i + o, 0), memory_space=pltpu.SMEM),
                  pl.BlockSpec((tp, NPAIR), lambda i: (i, 0)),
                  pl.BlockSpec((NPAIR, NPAIR * CHUNKS), lambda i: (0, 0)),
                  pl.BlockSpec((CHUNKS, NPAIR * CHUNKS), lambda i: (0, 0)),
                  pl.BlockSpec(memory_space=pltpu.VMEM)],
        out_specs=pl.BlockSpec((tp, CHUNKS, LANES), lambda i: (i, 0, 0)),
        out_shape=jax.ShapeDtypeStruct((n, CHUNKS, LANES), F32),
        scratch_shapes=[pltpu.VMEM((TOK_UNROLL, NPAIR * WORD_ROWS, LANES), I32),
                        pltpu.VMEM((tp, NPAIR * CHUNKS), F32)],
        compiler_params=_cparams(("arbitrary",)),
        name="peer_v",
    )(idx, w, rep, diag, tab)


SC_LANES = 16
SC_WORKERS = 32
SC_ROWS = 64
SC_TOKENS = 16
SC_GROUP_SHARES, SC_SHARE_DEN = ((2, 3, 5, 6), (2, 3, 5, 6), (2, 3, 5, 6), (2, 3, 5)), 16
SC_COLS = 8


def _peer_v_sc(ids, wts, packed, start):
    m = wts.shape[0]
    per_worker = m // SC_WORKERS
    blocks = NPAIR // SC_ROWS
    words = D_MODEL // 2
    table = packed.reshape(packed.shape[0] // WORD_ROWS, words)
    ids_b = ids.reshape(ids.shape[0] * blocks, SC_ROWS)
    wts_f = wts.reshape(m * NPAIR)
    mesh = plsc.VectorSubcoreMesh(core_axis_name="c", subcore_axis_name="s")

    @functools.partial(
        pl.kernel, mesh=mesh, out_type=jax.ShapeDtypeStruct((m, D_MODEL), F32),
        scratch_types=[pltpu.VMEM((SC_TOKENS * blocks, SC_ROWS), I32), pltpu.VMEM((SC_TOKENS * NPAIR,), F32),
                       pltpu.VMEM((2, SC_ROWS, words), I32), pltpu.VMEM((D_MODEL,), F32),
                       pltpu.SemaphoreType.DMA((2,))],
        compiler_params=pltpu.CompilerParams(needs_layout_passes=False),
        name="peer_v_sc")
    def run(tab_hbm, ids_hbm, w_hbm, out_hbm, ids_v, w_v, rows_v, acc_v, sems):
        wid = lax.axis_index("s") * 2 + lax.axis_index("c")
        base = wid * per_worker
        zero = jnp.zeros((SC_LANES,), F32)

        def gather(u, b):
            return pltpu.make_async_copy(tab_hbm.at[ids_v.at[u * blocks + b]], rows_v.at[b % 2], sems.at[b % 2])

        @pl.loop(0, per_worker // SC_TOKENS)
        def _(tb):
            t0 = base + tb * SC_TOKENS
            pltpu.sync_copy(ids_hbm.at[pl.ds((start + t0) * blocks, SC_TOKENS * blocks)], ids_v)
            pltpu.sync_copy(w_hbm.at[pl.ds(t0 * NPAIR, SC_TOKENS * NPAIR)], w_v)

            @pl.loop(0, SC_TOKENS)
            def _(u):
                for j in range(D_MODEL // SC_LANES):
                    acc_v[pl.ds(j * SC_LANES, SC_LANES)] = zero
                gather(u, 0).start()
                for b in range(blocks):
                    if b + 1 < blocks:
                        gather(u, b + 1).start()
                    gather(u, b).wait()

                    for c0 in range(0, words // SC_LANES, SC_COLS):
                        def row(r, accs, b=b, c0=c0):
                            wk = plsc.load_gather(w_v, [jnp.full((SC_LANES,), u * NPAIR + b * SC_ROWS + r, I32)])
                            out = []
                            for j in range(SC_COLS):
                                w32 = rows_v[b % 2, r, pl.ds((c0 + j) * SC_LANES, SC_LANES)]
                                lo = lax.bitcast_convert_type(lax.shift_left(w32, 16), F32)
                                hi = lax.bitcast_convert_type(w32 & -65536, F32)
                                out += [accs[2 * j] + wk * lo, accs[2 * j + 1] + wk * hi]
                            return tuple(out)
                        accs = lax.fori_loop(0, SC_ROWS, row, (zero,) * (2 * SC_COLS))
                        for j in range(SC_COLS):
                            s_, l0 = divmod((c0 + j) * SC_LANES, LANES)
                            plsc.addupdate(acc_v.at[pl.ds(2 * s_ * LANES + l0, SC_LANES)], accs[2 * j])
                            plsc.addupdate(acc_v.at[pl.ds((2 * s_ + 1) * LANES + l0, SC_LANES)], accs[2 * j + 1])
                pltpu.sync_copy(acc_v, out_hbm.at[t0 + u])

    return run(table, ids_b, wts_f)


def _final_kernel(x1_ref, *refs, ends, has_tc, has_prev):
    refs = list(refs)
    o_ref = refs.pop()
    if has_prev:
        refs.pop()
    g_ref, gt_ref = refs.pop(), refs.pop()
    ptc_ref = refs.pop() if has_tc else None
    i = pl.program_id(0)
    if has_tc:
        p = jnp.concatenate([ptc_ref[:, c, :] for c in range(CHUNKS)], axis=1)
        parts = list(zip(refs, ends))
    else:
        p = refs[-1][...]
        parts = list(zip(refs[:-1], ends[:-1]))
    for ref, end in reversed(parts):
        p = jnp.where(i < end, ref[...], p)
    o_ref[...] = _rms(x1_ref[...] + gt_ref[0] * p, g_ref[...])


FINAL_TILE = 512


def _final(x1, sc_parts, peer_tc, gt, g, out_prev, tile0, n_total, tm=FINAL_TILE):
    m = x1.shape[0]
    per_seq = m // gt.shape[0] // tm
    sizes = [p.shape[0] // tm for p in sc_parts]
    ends = [sum(sizes[:j + 1]) for j in range(len(sizes))]
    starts = [e - k for e, k in zip(ends, sizes)]
    tile = pl.BlockSpec((tm, D_MODEL), lambda i: (i, 0))
    part = lambda s0, k: pl.BlockSpec((tm, D_MODEL), lambda i: (jnp.clip(i - s0, 0, k - 1), 0))
    in_specs = [tile] + [part(s0, k) for s0, k in zip(starts, sizes)]
    args = [x1, *sc_parts]
    if peer_tc is not None:
        in_specs.append(pl.BlockSpec((tm, CHUNKS, LANES), lambda i: (jnp.maximum(i - ends[-1], 0), 0, 0)))
        args.append(peer_tc)
    in_specs += [pl.BlockSpec((1, 1, D_MODEL), lambda i: (i // per_seq, 0, 0)),
                 pl.BlockSpec((1, D_MODEL), lambda i: (0, 0))]
    args += [gt, g.reshape(1, D_MODEL)]
    aliases = {}
    if out_prev is not None:
        in_specs.append(pl.BlockSpec(memory_space=pl.ANY))
        aliases = {len(args): 0}
        args.append(out_prev)
    return pl.pallas_call(
        functools.partial(_final_kernel, ends=tuple(ends), has_tc=peer_tc is not None, has_prev=out_prev is not None),
        grid=(m // tm,),
        in_specs=in_specs,
        out_specs=pl.BlockSpec((tm, D_MODEL), lambda i: (i + tile0, 0)),
        out_shape=jax.ShapeDtypeStruct((n_total, D_MODEL), F32),
        input_output_aliases=aliases,
        compiler_params=_cparams(("parallel",)),
        name="final",
    )(*args)


def _block_diag(width, group, value):
    i = jnp.arange(width) // group
    return jnp.where(i[:, None] == i[None, :], value, 0.0).astype(F32)


def _layer(x, mod, final_g, norm_mix_g, w_in, conv_dw_w, conv_dw_b, conv_ln_w, conv_ln_b, rwkv_mu, rwkv_w0, rwkv_w2,
           rwkv_a0, rwkv_a2, rwkv_g2, rwkv_k_k, rwkv_k_a, rwkv_r_k, rwkv_gn_w, rwkv_gn_b, w_out, norm_ffn_g,
           peer_w_q, peer_sub_keys, peer_u, peer_v):
    bsz, s, _ = x.shape
    sh_mix, sc_mix, gt_mix, sh_ffn, sc_ffn, gt_ffn = (
        mod[:, i * D_MODEL:(i + 1) * D_MODEL].reshape(bsz, 1, D_MODEL) for i in range(6))

    w_in_b, w_out_b, w_q_b = w_in.astype(BF16), w_out.astype(BF16), peer_w_q.astype(BF16)
    zpad = jnp.zeros((LORA_W, RWKV_CH), F32)
    w2p, a2p = jnp.concatenate([rwkv_w2, zpad], axis=0), jnp.concatenate([zpad, rwkv_a2], axis=0)
    keys = peer_sub_keys.reshape(2 * PEER_HEADS, PEER_NKEYS, PEER_DQ // 2)
    tab_u, tab_v = _pack_table(peer_u), _pack_table(peer_v)

    nb = bsz // len(SC_GROUP_SHARES)
    n, m = bsz * s, nb * s
    out, after = None, None
    for gi, shares in enumerate(SC_GROUP_SHARES):
        b0 = gi * nb
        grp = lambda t: t[b0:b0 + nb]
        xg = x if after is None else lax.optimization_barrier((x, after))[0]
        yglu, prw = _in_proj(xg, grp(sh_mix), grp(sc_mix), norm_mix_g, w_in_b, b0)
        y_conv = _conv(yglu, conv_dw_w, conv_dw_b, conv_ln_w, conv_ln_b)
        y_rwkv = _rwkv(prw, rwkv_mu, rwkv_w0, w2p, rwkv_a0, a2p, rwkv_g2, rwkv_k_k, rwkv_k_a, rwkv_r_k.reshape(-1),
                       rwkv_gn_w, rwkv_gn_b)
        x1, u2 = _out_proj(y_conv, y_rwkv, w_out_b, xg, grp(gt_mix), norm_ffn_g, grp(sh_ffn), grp(sc_ffn), b0)
        u3 = u2.reshape(m, CHUNKS, LANES)
        idx, gate = _route(u3, w_q_b, keys)
        ids = lax.shift_right_logical(idx, 2)
        start, sc_parts = 0, []
        for share in shares:
            cnt = m * share // SC_SHARE_DEN
            after = _peer_u(idx, u3, gate, tab_u, start, cnt)
            sc_parts.append(_peer_v_sc(ids, after, tab_v, start))
            start += cnt
        peer_tc = None
        if start < m:
            after = _peer_u(idx, u3, gate, tab_u, start, m - start)
            peer_tc = _peer_v(idx, after, tab_v, start)
        out = _final(x1.reshape(m, D_MODEL), sc_parts, peer_tc, grp(gt_ffn), final_g, out,
                     gi * (m // FINAL_TILE), n)
    return out.reshape(bsz, s, D_MODEL)


def kernel(x, c, ada_w, ada_b, norm_mix_g, w_in, conv_dw_w, conv_dw_b, conv_ln_w, conv_ln_b, rwkv_mu, rwkv_w0,
           rwkv_w2, rwkv_a0, rwkv_a2, rwkv_g2, rwkv_k_k, rwkv_k_a, rwkv_r_k, rwkv_gn_w, rwkv_gn_b, w_out,
           norm_ffn_g, peer_w_q, peer_sub_keys, peer_u, peer_v, final_g):
    depth = ada_w.shape[0]
    assert depth == 1, "one layer: the final norm is fused into the last layer's residual"
    mod = _mod(c, ada_w[0], ada_b[0])
    return _layer(x, mod, final_g, norm_mix_g[0], w_in[0], conv_dw_w[0], conv_dw_b[0], conv_ln_w[0],
                        conv_ln_b[0], rwkv_mu[0], rwkv_w0[0], rwkv_w2[0], rwkv_a0[0], rwkv_a2[0], rwkv_g2[0],
                        rwkv_k_k[0], rwkv_k_a[0], rwkv_r_k[0], rwkv_gn_w[0], rwkv_gn_b[0], w_out[0],
                        norm_ffn_g[0], peer_w_q[0], peer_sub_keys[0], peer_u[0], peer_v[0])
```

```python
import functools

import jax
import jax.numpy as jnp
from jax import lax
from jax.experimental import pallas as pl
from jax.experimental.pallas import tpu as pltpu
from jax.experimental.pallas import tpu_sc as plsc

F32 = jnp.float32
BF16 = jnp.bfloat16
I32 = jnp.int32
HI = lax.Precision.HIGHEST

D_MODEL = 1024
CONV_CH = 512
RWKV_CH = 512
HEAD = 64
CONV_WIDTH = 31
LORA_W = 64
LORA_A = 64
LORA_G = 128
RWKV_PROJ = 3 * RWKV_CH + LORA_W + LORA_A + LORA_G
IN_PROJ = 2 * CONV_CH + RWKV_PROJ
PEER_HEADS = 8
PEER_NKEYS = 128
PEER_EXPERTS = PEER_NKEYS * PEER_NKEYS
PEER_DQ = 256
PEER_TOPK = 16
NPAIR = PEER_HEADS * PEER_TOPK
RMS_EPS = 1e-6
LN_EPS = 1e-5
GN_EPS = 64e-5

LANES = 128
CHUNKS = D_MODEL // LANES
WORD_ROWS = CHUNKS // 2
CHUNK_T = 64
VMEM_LIMIT = 56 * 1024 * 1024


def _cparams(sem, vmem=None):
    return pltpu.CompilerParams(dimension_semantics=sem, vmem_limit_bytes=vmem or VMEM_LIMIT)


def _dot(a, b, precision=None):
    return jnp.dot(a, b, precision=precision, preferred_element_type=F32)


def _dot_nt(a, b, precision=None):
    return lax.dot_general(a, b, (((1,), (1,)), ((), ())), precision=precision, preferred_element_type=F32)


def _dot_tn(a, b, precision=None):
    return lax.dot_general(a, b, (((0,), (0,)), ((), ())), precision=precision, preferred_element_type=F32)


def _split(a):
    hi = a.astype(BF16)
    return hi, (a - hi.astype(F32)).astype(BF16)


def _mm(a, b, passes, dot=_dot):
    if passes == 6:
        return dot(a, b, HI)
    if passes == 1:
        return dot(a.astype(BF16), b.astype(BF16))
    ka = 0 if dot is _dot_tn else 1
    kb = 1 if dot is _dot_nt else 0
    ah, al = _split(a)
    bh, bl = _split(b)
    return dot(jnp.concatenate([ah, ah, al], axis=ka), jnp.concatenate([bh, bl, bh], axis=kb))


def _mm_bf16_rhs(a, b_bf16, terms=2):
    parts = []
    for _ in range(terms):
        p = a.astype(BF16)
        parts.append(p)
        a = a - p.astype(F32)
    return _dot(jnp.concatenate(parts, axis=1), jnp.concatenate([b_bf16] * terms, axis=0))


def _head_sums(x, bd_bf16):
    return jnp.concatenate([_mm_bf16_rhs(x[:, i * LANES:(i + 1) * LANES], bd_bf16, 3)
                            for i in range(x.shape[1] // LANES)], axis=1)


def _rms(x, g):
    return x * lax.rsqrt(jnp.mean(x * x, axis=-1, keepdims=True) + RMS_EPS) * g


def _mod_kernel(c_ref, w_ref, b_ref, o_ref):
    c = c_ref[...]
    o_ref[...] = _dot(c * jax.nn.sigmoid(c), w_ref[...], HI) + b_ref[...]


def _mod(c, w, b):
    bsz = c.shape[0]
    n = w.shape[1]
    tn = 1024
    return pl.pallas_call(
        _mod_kernel,
        grid=(n // tn,),
        in_specs=[pl.BlockSpec((bsz, D_MODEL), lambda j: (0, 0)),
                  pl.BlockSpec((D_MODEL, tn), lambda j: (0, j)),
                  pl.BlockSpec((1, tn), lambda j: (0, j))],
        out_specs=pl.BlockSpec((bsz, tn), lambda j: (0, j)),
        out_shape=jax.ShapeDtypeStruct((bsz, n), F32),
        compiler_params=_cparams(("parallel",)),
    )(c, w, b.reshape(1, n))


def _in_proj_kernel(x_ref, sh_ref, sc_ref, g_ref, w_ref, yglu_ref, prw_ref):
    u = _rms(x_ref[0], g_ref[...]) * (1.0 + sc_ref[0]) + sh_ref[0]
    p = _dot(u.astype(BF16), w_ref[...])
    yglu_ref[0] = p[:, :CONV_CH] * jax.nn.sigmoid(p[:, CONV_CH:2 * CONV_CH])
    prw_ref[0] = p[:, 2 * CONV_CH:]


def _in_proj(x, sh, sc, g, w_bf16, b0, tm=512):
    bsz, s = sh.shape[0], x.shape[1]
    vec = pl.BlockSpec((1, 1, D_MODEL), lambda b, i: (b, 0, 0))
    return pl.pallas_call(
        _in_proj_kernel,
        grid=(bsz, s // tm),
        in_specs=[pl.BlockSpec((1, tm, D_MODEL), lambda b, i: (b + b0, i, 0)), vec, vec,
                  pl.BlockSpec((1, D_MODEL), lambda b, i: (0, 0)),
                  pl.BlockSpec((D_MODEL, IN_PROJ), lambda b, i: (0, 0))],
        out_specs=[pl.BlockSpec((1, tm, CONV_CH), lambda b, i: (b, i, 0)),
                   pl.BlockSpec((1, tm, RWKV_PROJ), lambda b, i: (b, i, 0))],
        out_shape=[jax.ShapeDtypeStruct((bsz, s, CONV_CH), F32),
                   jax.ShapeDtypeStruct((bsz, s, RWKV_PROJ), F32)],
        compiler_params=_cparams(("parallel", "parallel")),
    )(x, sh, sc, g.reshape(1, D_MODEL), w_bf16)


CONV_HALO = 32
CONV_ROWS = 64


def _conv_kernel(cur_ref, prev_ref, w_ref, b_ref, lnw_ref, lnb_ref, o_ref, pad_ref):
    tc = cur_ref.shape[1]
    pad_ref[0, 0:CONV_HALO, :] = jnp.where(pl.program_id(1) > 0, prev_ref[0], 0.0)
    pad_ref[0, CONV_HALO:CONV_HALO + tc, :] = cur_ref[0]
    span = CONV_HALO + tc - 8
    for r in range(1, 8):
        pad_ref[r, 0:span, :] = pad_ref[0, r:r + span, :]
    off = CONV_HALO - (CONV_WIDTH - 1)
    for r0 in range(0, tc, CONV_ROWS):
        acc = jnp.zeros((CONV_ROWS, CONV_CH), F32)
        for j in range(CONV_WIDTH):
            q, r = divmod(off + j, 8)
            acc = acc + w_ref[j:j + 1, :] * pad_ref[r, r0 + 8 * q:r0 + 8 * q + CONV_ROWS, :]
        y = acc + b_ref[...]
        mu = jnp.mean(y, axis=-1, keepdims=True)
        yc = y - mu
        var = jnp.mean(yc * yc, axis=-1, keepdims=True)
        yn = yc * lax.rsqrt(var + LN_EPS) * lnw_ref[...] + lnb_ref[...]
        o_ref[0, r0:r0 + CONV_ROWS, :] = (yn * jax.nn.sigmoid(yn)).astype(o_ref.dtype)


def _conv(yglu, w, b, lnw, lnb, tc=256):
    bsz, s, _ = yglu.shape
    hb = tc // CONV_HALO
    row = lambda a: a.reshape(1, CONV_CH)
    const = lambda shp: pl.BlockSpec(shp, lambda bb, i: (0, 0))
    return pl.pallas_call(
        _conv_kernel,
        grid=(bsz, s // tc),
        in_specs=[pl.BlockSpec((1, tc, CONV_CH), lambda bb, i: (bb, i, 0)),
                  pl.BlockSpec((1, CONV_HALO, CONV_CH), lambda bb, i: (bb, jnp.maximum(i * hb - 1, 0), 0)),
                  const((CONV_WIDTH, CONV_CH)), const((1, CONV_CH)), const((1, CONV_CH)), const((1, CONV_CH))],
        out_specs=pl.BlockSpec((1, tc, CONV_CH), lambda bb, i: (bb, i, 0)),
        out_shape=jax.ShapeDtypeStruct((bsz, s, CONV_CH), BF16),
        scratch_shapes=[pltpu.VMEM((8, CONV_HALO + tc, CONV_CH), F32)],
        compiler_params=_cparams(("parallel", "parallel")),
    )(yglu, yglu, w, row(b), row(lnw), row(lnb))


def _softplus(z):
    return jnp.maximum(z, 0.0) + jnp.log1p(jnp.exp(-jnp.abs(z)))


def _rwkv_features(cur, prow, mu, w0, w2p, a0, a2p, g2, k_k, k_a, r_k, bd):
    rows = lax.broadcasted_iota(I32, cur.shape, 0)
    prev = jnp.where(rows == 0, prow, pltpu.roll(cur, 1, axis=0))
    xs = cur + mu * (prev - cur)
    r = xs[:, 0:RWKV_CH]
    k = xs[:, RWKV_CH:2 * RWKV_CH]
    v = xs[:, 2 * RWKV_CH:3 * RWKV_CH]
    wa = xs[:, 3 * RWKV_CH:3 * RWKV_CH + LORA_W + LORA_A]
    gl = xs[:, 3 * RWKV_CH + LORA_W + LORA_A:]
    w = -_softplus(-(w0 + _mm(jnp.tanh(wa), w2p, 3))) - 0.5
    a = jax.nn.sigmoid(a0 + _mm(wa, a2p, 3))
    g = _mm(jax.nn.sigmoid(gl), g2, 3)
    kk = k * k_k
    kkn = kk / jnp.maximum(jnp.sqrt(_head_sums(kk * kk, bd)), 1e-12)
    k2 = k * (1.0 + (a - 1.0) * k_a)
    bonus = _head_sums(r * k2 * r_k, bd) * v
    return r, k2, v, -jnp.exp(w), -kkn, kkn * a, g, bonus


RWKV_PASSES = {"gram": 3, "lakv": 3, "solve": 3, "out": 3, "state": 3}
SOLVE_BLK = 16


def _expand(x, lane_lo):
    return jnp.concatenate([jnp.where(lane_lo, x, 0.0), jnp.where(lane_lo, 0.0, x)], axis=0)


def _rwkv_kernel(p_ref, mu_ref, w0_ref, w2_ref, a0_ref, a2_ref, g2_ref, kk_ref, ka_ref, rk_ref, bd_ref,
                 bdm_ref, gnw_ref, gnb_ref, y_ref, s_ref, last_ref):
    C = CHUNK_T
    H2 = 2 * C

    @pl.when(pl.program_id(1) == 0)
    def _():
        s_ref[...] = jnp.zeros_like(s_ref)
        last_ref[...] = jnp.zeros_like(last_ref)

    cur = p_ref[0]
    r, k, vv, lw, a, b, gate, bonus = _rwkv_features(
        cur, last_ref[...], mu_ref[...], w0_ref[...], w2_ref[...], a0_ref[...], a2_ref[...], g2_ref[...],
        kk_ref[...], ka_ref[...], rk_ref[...], bd_ref[...])
    last_ref[...] = cur[C - 1:C, :]

    tri = (lax.broadcasted_iota(I32, (C, C), 0) >= lax.broadcasted_iota(I32, (C, C), 1)).astype(F32)
    cum = _dot(tri, lw, HI)
    tot = cum[C - 1:C, :]
    e_pos = jnp.exp(cum)
    e_neg = jnp.exp(-cum)
    e_rem = jnp.exp(tot - cum)
    rt = r * e_pos
    at = a * jnp.exp(cum - lw)
    kt = k * e_neg
    bt = b * e_neg
    kp = k * e_rem
    bp = b * e_rem
    pc = jnp.exp(tot)

    lane_lo = lax.broadcasted_iota(I32, (C, LANES), 1) < HEAD
    tt = lax.broadcasted_iota(I32, (H2, H2), 0) % C
    ss = lax.broadcasted_iota(I32, (H2, H2), 1) % C
    strict = tt > ss
    incl = tt >= ss
    near = tt // SOLVE_BLK == ss // SOLVE_BLK
    eye = lax.broadcasted_iota(I32, (LANES, LANES), 0) == lax.broadcasted_iota(I32, (LANES, LANES), 1)

    P = RWKV_PASSES
    pairs = range(RWKV_CH // LANES)
    each = lambda fn, *lists: [fn(*args) for args in zip(*lists)]
    sls = [slice(hp * LANES, (hp + 1) * LANES) for hp in pairs]
    ax, rx, bx, kx, vx, bpx, kpx = ([_expand(t[:, sl], lane_lo) for sl in sls] for t in (at, rt, bt, kt, vv, bp, kp))
    gram = each(lambda a_, r_, b_, k_: _mm(jnp.concatenate([a_, r_], axis=0), jnp.concatenate([b_, k_], axis=0),
                                           P["gram"], _dot_nt), ax, rx, bx, kx)
    l_ab = [jnp.where(strict, g_[:H2, :H2], 0.0) for g_ in gram]
    l_ak = [jnp.where(strict, g_[:H2, H2:], 0.0) for g_ in gram]
    m_r = [jnp.concatenate([jnp.where(incl, g_[H2:, :H2], 0.0), jnp.where(incl, g_[H2:, H2:], 0.0)], axis=1)
           for g_ in gram]
    dg = [jnp.where(near, l_, 0.0) for l_ in l_ab]
    lakv = each(lambda l_, v_: _mm(l_, v_, P["lakv"]), l_ak, vx)
    xf = each(lambda a_, lv_, l_, d_: jnp.concatenate([a_, lv_, l_ - d_], axis=1), ax, lakv, l_ab, dg)
    n_sq = SOLVE_BLK.bit_length() - 1
    for it in range(n_sq):
        xf = each(lambda x_, d_: x_ + _mm(d_, x_, P["solve"]), xf, dg)
        if it + 1 < n_sq:
            dg = [_mm(d_, d_, P["solve"]) for d_ in dg]
    x = [x_[:, :2 * LANES] for x_ in xf]
    f = [x_[:, 2 * LANES:] for x_ in xf]
    n_sq = (C // SOLVE_BLK).bit_length() - 1
    for it in range(n_sq):
        x = each(lambda x_, f_: x_ + _mm(f_, x_, P["solve"]), x, f)
        if it + 1 < n_sq:
            f = [_mm(f_, f_, P["solve"]) for f_ in f]
    zero = jnp.zeros((H2, LANES), F32)
    z = each(lambda x_, v_: jnp.concatenate([x_, jnp.concatenate([zero, v_], axis=1)], axis=0), x, vx)
    w1 = each(lambda m_, z_: _mm(m_, z_, P["out"]), m_r, z)
    w2 = each(lambda b_, k_, z_: _mm(jnp.concatenate([b_, k_], axis=0), z_, P["out"], _dot_tn), bpx, kpx, z)
    ys = []
    for hp in pairs:
        ra = rx[hp] + w1[hp][:, :LANES]
        ra = ra[:C] + ra[C:]
        y0 = w1[hp][:C, LANES:] + w1[hp][C:, LANES:]
        mt = w2[hp][:, :LANES] + jnp.where(eye, pc[:, sls[hp]], 0.0)
        s0 = s_ref[hp]
        ys.append(_mm(ra, s0, P["state"]) + y0)
        s_ref[hp] = _mm(mt, s0, P["state"]) + w2[hp][:, LANES:]
    yc = [y_ - _mm_bf16_rhs(y_, bdm_ref[...]) for y_ in ys]
    ys = [c_ * lax.rsqrt(_mm_bf16_rhs(c_ * c_, bdm_ref[...]) + GN_EPS) for c_ in yc]

    yn = jnp.concatenate(ys, axis=1) * gnw_ref[...] + gnb_ref[...]
    y_ref[0] = ((yn + bonus) * gate).astype(y_ref.dtype)


def _rwkv(prw, mu, w0, w2p, a0, a2p, g2, k_k, k_a, r_k, gnw, gnb):
    bsz, s, _ = prw.shape
    row = lambda t: t.reshape(1, -1)
    const = lambda shp: pl.BlockSpec(shp, lambda bb, c: (0, 0))
    vec = const((1, RWKV_CH))
    lora = const((LORA_W + LORA_A, RWKV_CH))
    return pl.pallas_call(
        _rwkv_kernel,
        grid=(bsz, s // CHUNK_T),
        in_specs=[pl.BlockSpec((1, CHUNK_T, RWKV_PROJ), lambda bb, c: (bb, c, 0)), const((1, RWKV_PROJ)),
                  vec, lora, vec, lora, const((LORA_G, RWKV_CH)), vec, vec, vec,
                  const((LANES, LANES)), const((LANES, LANES)), vec, vec],
        out_specs=pl.BlockSpec((1, CHUNK_T, RWKV_CH), lambda bb, c: (bb, c, 0)),
        out_shape=jax.ShapeDtypeStruct((bsz, s, RWKV_CH), BF16),
        scratch_shapes=[pltpu.VMEM((RWKV_CH // LANES, LANES, LANES), F32), pltpu.VMEM((1, RWKV_PROJ), F32)],
        compiler_params=_cparams(("parallel", "arbitrary")),
        name="rwkv",
    )(prw, row(mu), row(w0), w2p, row(a0), a2p, g2, row(k_k), row(k_a), row(r_k),
      _block_diag(LANES, HEAD, 1.0).astype(BF16), _block_diag(LANES, HEAD, 1.0 / HEAD).astype(BF16),
      row(gnw), row(gnb))


def _out_proj_kernel(yc_ref, yr_ref, wo_ref, x_ref, gt_ref, g_ref, sh_ref, sc_ref, x1_o, u2_o):
    mix = _dot(yc_ref[0], wo_ref[0:CONV_CH, :]) + _dot(yr_ref[0], wo_ref[CONV_CH:, :])
    x1 = x_ref[0] + gt_ref[0] * mix
    x1_o[0] = x1
    u2 = _rms(x1, g_ref[...]) * (1.0 + sc_ref[0]) + sh_ref[0]
    for c in range(CHUNKS):
        u2_o[0, :, c, :] = u2[:, c * LANES:(c + 1) * LANES]


def _out_proj(yc, yr, wo_bf16, x, gt, g, sh, sc, b0, tm=512):
    bsz, s = yc.shape[0], x.shape[1]
    vec = pl.BlockSpec((1, 1, D_MODEL), lambda b, i: (b, 0, 0))
    tile = lambda w: pl.BlockSpec((1, tm, w), lambda b, i: (b, i, 0))
    return pl.pallas_call(
        _out_proj_kernel,
        grid=(bsz, s // tm),
        in_specs=[tile(CONV_CH), tile(RWKV_CH), pl.BlockSpec((D_MODEL, D_MODEL), lambda b, i: (0, 0)),
                  pl.BlockSpec((1, tm, D_MODEL), lambda b, i: (b + b0, i, 0)), vec,
                  pl.BlockSpec((1, D_MODEL), lambda b, i: (0, 0)), vec, vec],
        out_specs=[tile(D_MODEL), pl.BlockSpec((1, tm, CHUNKS, LANES), lambda b, i: (b, i, 0, 0))],
        out_shape=[jax.ShapeDtypeStruct((bsz, s, D_MODEL), F32),
                   jax.ShapeDtypeStruct((bsz, s, CHUNKS, LANES), F32)],
        compiler_params=_cparams(("parallel", "parallel")),
        name="out_proj",
    )(yc, yr, wo_bf16, x, gt, g.reshape(1, D_MODEL), sh, sc)


HEADS_STEP = 4


def _topk_rows(ss, k):
    n, t = ss[0].shape
    rids = [lax.broadcasted_iota(I32, (8, t), 0) + r0 for r0 in range(0, n, 8)]
    ss = [[s[r0:r0 + 8] for r0 in range(0, n, 8)] for s in ss]
    vals, ids = [[] for _ in ss], [[] for _ in ss]
    for _ in range(k):
        tops = [_max_with_tag(s, [rids]) for s in ss]
        ss = [[jnp.where(r == j, -jnp.inf, c) for c, r in zip(s, rids)] for s, (_, (j,)) in zip(ss, tops)]
        for v, i, (m, (j,)) in zip(vals, ids, tops):
            v.append(m)
            i.append(j)
    return [(jnp.concatenate(v, axis=0), jnp.concatenate(i, axis=0)) for v, i in zip(vals, ids)]


def _max_with_tag(chunks, tags):
    vals, tags = list(chunks), [list(tg) for tg in tags]
    while len(vals) > 1:
        nv, nt = [], [[] for _ in tags]
        for a in range(0, len(vals) - 1, 2):
            first = vals[a] >= vals[a + 1]
            nv.append(jnp.maximum(vals[a], vals[a + 1]))
            for dst, tg in zip(nt, tags):
                dst.append(jnp.where(first, tg[a], tg[a + 1]))
        if len(vals) % 2:
            nv.append(vals[-1])
            for dst, tg in zip(nt, tags):
                dst.append(tg[-1])
        vals, tags = nv, nt
    v8 = vals[0]
    m = jnp.max(v8, axis=0, keepdims=True)
    big = jnp.iinfo(jnp.int32).max
    key = jnp.min(jnp.where(v8 == m, tags[0][0], big), axis=0, keepdims=True)
    out = [key]
    for tg in tags[1:]:
        out.append(jnp.sum(jnp.where(tags[0][0] == key, tg[0], 0), axis=0, keepdims=True))
    return m, out


def _route_kernel(u_ref, wq_ref, keys_ref, idx_o, gate_o, q_ref, idx_s, gate_s):
    tm = u_ref.shape[0]
    u = jnp.concatenate([u_ref[:, c, :] for c in range(CHUNKS)], axis=1)
    q = _dot(u.astype(BF16), wq_ref[...])
    for j in range(2 * PEER_HEADS):
        q_ref[j] = q[:, j * LANES:(j + 1) * LANES]
    K = PEER_TOPK
    tt = LANES
    tiles = range(tm // tt)
    row8 = lax.broadcasted_iota(I32, (8, tt), 0)

    def heads(i, carry):
        probs = [(i * HEADS_STEP + dh, lt) for dh in range(HEADS_STEP) for lt in tiles]
        scores = [_mm(keys_ref[2 * h + p], q_ref[2 * h + p, lt * tt:(lt + 1) * tt, :], 3, _dot_nt)
                  for h, lt in probs for p in range(2)]
        tops = _topk_rows(scores, K)
        ss, flats, eids = [], [], []
        for n_ in range(len(probs)):
            (av, ai), (bv, bi) = tops[2 * n_], tops[2 * n_ + 1]
            cs = [av[0:1] + bv[0:8], av[0:1] + bv[8:16]]
            cf = [row8, row8 + 8]
            ce = [ai[0:1] * PEER_NKEYS + bi[0:8], ai[0:1] * PEER_NKEYS + bi[8:16]]
            for x in range(1, 8):
                cs.append(jnp.where(row8 < K // (x + 1), av[x:x + 1] + bv[0:8], -jnp.inf))
                cf.append(x * K + row8)
                ce.append(ai[x:x + 1] * PEER_NKEYS + bi[0:8])
            cs.append(av[8:16] + bv[0:1])
            cf.append((row8 + 8) * K)
            ce.append(ai[8:16] * PEER_NKEYS + bi[0:1])
            ss.append(cs)
            flats.append(cf)
            eids.append(ce)
        best, experts = [[] for _ in probs], [[] for _ in probs]
        for _ in range(K):
            tops2 = [_max_with_tag(s, [fl, ei]) for s, fl, ei in zip(ss, flats, eids)]
            for n_, (m, (f, e)) in enumerate(tops2):
                best[n_].append(m)
                experts[n_].append(e)
            ss = [[jnp.where(fc == f, -jnp.inf, c) for c, fc in zip(s, fl)]
                  for s, fl, (_, (f, _e)) in zip(ss, flats, tops2)]
        for n_, (h, lt) in enumerate(probs):
            b = jnp.concatenate(best[n_], axis=0)
            e = jnp.exp(b - b[0:1])
            gate_s[lt, h] = e / jnp.sum(e, axis=0, keepdims=True)
            idx_s[lt, h] = jnp.concatenate(experts[n_], axis=0) * WORD_ROWS
        return carry

    lax.fori_loop(0, PEER_HEADS // HEADS_STEP, heads, 0)
    for lt in tiles:
        idx_o[lt * tt:(lt + 1) * tt, :] = jnp.transpose(idx_s[lt].reshape(NPAIR, tt))
        gate_o[lt * tt:(lt + 1) * tt, :] = jnp.transpose(gate_s[lt].reshape(NPAIR, tt))


def _route(u2, wq_bf16, keys, tm=256):
    n = u2.shape[0]
    oblk = pl.BlockSpec((tm, NPAIR), lambda i: (i, 0))
    return pl.pallas_call(
        _route_kernel,
        grid=(n // tm,),
        in_specs=[pl.BlockSpec((tm, CHUNKS, LANES), lambda i: (i, 0, 0)),
                  pl.BlockSpec((D_MODEL, PEER_HEADS * PEER_DQ), lambda i: (0, 0)),
                  pl.BlockSpec((2 * PEER_HEADS, PEER_NKEYS, PEER_DQ // 2), lambda i: (0, 0, 0))],
        out_specs=[oblk, oblk],
        out_shape=[jax.ShapeDtypeStruct((n, NPAIR), I32), jax.ShapeDtypeStruct((n, NPAIR), F32)],
        scratch_shapes=[pltpu.VMEM((2 * PEER_HEADS, tm, LANES), F32),
                        pltpu.VMEM((tm // LANES, PEER_HEADS, PEER_TOPK, LANES), I32),
                        pltpu.VMEM((tm // LANES, PEER_HEADS, PEER_TOPK, LANES), F32)],
        compiler_params=_cparams(("parallel",)),
        name="route",
    )(u2, wq_bf16, keys)


TOK_UNROLL = 32


def _pack_kernel(t_ref, o_ref):
    te = t_ref.shape[0]
    bits = lambda v: lax.bitcast_convert_type(v.astype(BF16).astype(F32), I32)
    for s in range(WORD_ROWS):
        lo = bits(t_ref[:, (2 * s) * LANES:(2 * s + 1) * LANES])
        hi = bits(t_ref[:, (2 * s + 1) * LANES:(2 * s + 2) * LANES])
        o_ref[pl.ds(s, te, stride=WORD_ROWS), :] = (hi & -65536) | lax.shift_right_logical(lo, 16)


def _pack_table(t, te=512):
    e = t.shape[0]
    return pl.pallas_call(
        _pack_kernel,
        grid=(e // te,),
        in_specs=[pl.BlockSpec((te, D_MODEL), lambda i: (i, 0))],
        out_specs=pl.BlockSpec((te * WORD_ROWS, LANES), lambda i: (i, 0)),
        out_shape=jax.ShapeDtypeStruct((e * WORD_ROWS, LANES), I32),
        compiler_params=_cparams(("parallel",)),
        name="pack_table",
    )(t)


def _gather_rows(idx_ref, t0, tab_ref, g_ref):
    for u in range(TOK_UNROLL):
        idx_row = idx_ref.at[t0 + u]
        for kk in range(NPAIR):
            row = pl.multiple_of(idx_row[kk], WORD_ROWS)
            g_ref[u, kk * WORD_ROWS:(kk + 1) * WORD_ROWS, :] = tab_ref[pl.ds(row, WORD_ROWS), :]


def _peer_u_kernel(idx_ref, x_ref, gate_ref, keep_ref, sel_ref, tab_ref, w_o, g_ref, p_ref):
    tp = x_ref.shape[0]

    def tokens(i, carry):
        t0 = pl.multiple_of(i * TOK_UNROLL, TOK_UNROLL)
        _gather_rows(idx_ref, t0, tab_ref, g_ref)
        xs = x_ref[pl.ds(t0, TOK_UNROLL)]
        parts = []
        for u in range(TOK_UNROLL):
            g = pltpu.bitcast(g_ref[u], BF16)
            xc = xs[u]
            xt = jnp.concatenate([xc] * (LANES // CHUNKS), axis=0).astype(BF16)
            r = _dot_nt(g, xt) * keep_ref[...]
            parts.append(jnp.sum(r.reshape(NPAIR // 16, 16 * CHUNKS, LANES), axis=1))
        p_ref[pl.ds(pl.multiple_of(t0 * 8, 8 * TOK_UNROLL), 8 * TOK_UNROLL), :] = jnp.concatenate(parts, axis=0)
        return carry

    lax.fori_loop(0, tp // TOK_UNROLL, tokens, 0)
    z = _mm_bf16_rhs(p_ref[...], sel_ref[...], 3)
    grp = lax.broadcasted_iota(I32, z.shape, 0) % 8 == lax.broadcasted_iota(I32, z.shape, 1) // 16
    h = jnp.sum(jnp.where(grp, z, 0.0).reshape(tp, 8, LANES), axis=1)
    w_o[...] = gate_ref[...] * (0.5 * h * (1.0 + lax.erf(h * (2.0 ** -0.5))))


def _peer_u(idx, x3, gate, tab, start, n, tp=128):
    o = start // tp
    rows = jnp.arange(NPAIR * CHUNKS)[:, None]
    lanes = jnp.arange(LANES)[None, :]
    keep = ((lanes % CHUNKS == rows % CHUNKS) & (lanes // CHUNKS == (rows // CHUNKS) % 16)).astype(F32)
    sel = (jnp.arange(LANES)[:, None] // CHUNKS == jnp.arange(LANES)[None, :] % 16).astype(BF16)
    return pl.pallas_call(
        _peer_u_kernel,
        grid=(n // tp,),
        in_specs=[pl.BlockSpec((tp, NPAIR), lambda i: (i + o, 0), memory_space=pltpu.SMEM),
                  pl.BlockSpec((tp, CHUNKS, LANES), lambda i: (i + o, 0, 0)),
                  pl.BlockSpec((tp, NPAIR), lambda i: (i + o, 0)),
                  pl.BlockSpec((NPAIR * CHUNKS, LANES), lambda i: (0, 0)),
                  pl.BlockSpec((LANES, LANES), lambda i: (0, 0)),
                  pl.BlockSpec(memory_space=pltpu.VMEM)],
        out_specs=pl.BlockSpec((tp, NPAIR), lambda i: (i, 0)),
        out_shape=jax.ShapeDtypeStruct((n, NPAIR), F32),
        scratch_shapes=[pltpu.VMEM((TOK_UNROLL, NPAIR * WORD_ROWS, LANES), I32),
                        pltpu.VMEM((tp * 8, LANES), F32)],
        compiler_params=_cparams(("arbitrary",)),
        name="peer_u",
    )(idx, x3, gate, keep, sel, tab)


def _peer_v_kernel(idx_ref, w_ref, rep_ref, diag_ref, tab_ref, o_ref, g_ref, wx_ref):
    tp = w_ref.shape[0]
    wx_ref[...] = _mm_bf16_rhs(w_ref[...], rep_ref[...], 3)

    def tokens(i, carry):
        t0 = pl.multiple_of(i * TOK_UNROLL, TOK_UNROLL)
        _gather_rows(idx_ref, t0, tab_ref, g_ref)
        wx = wx_ref[pl.ds(t0, TOK_UNROLL), :]
        outs = []
        for u in range(TOK_UNROLL):
            g = pltpu.bitcast(g_ref[u], BF16)
            wm = (wx[u:u + 1, :] * diag_ref[...]).astype(BF16)
            outs.append(_dot(wm, g))
        o_ref[pl.ds(t0, TOK_UNROLL)] = jnp.stack(outs, axis=0)
        return carry

    lax.fori_loop(0, tp // TOK_UNROLL, tokens, 0)


def _peer_v(idx, w, tab, start, tp=128):
    n = w.shape[0]
    o = start // tp
    rep = jnp.repeat(jnp.eye(NPAIR, dtype=BF16), CHUNKS, axis=1)
    diag = (jnp.arange(CHUNKS)[:, None] == jnp.arange(NPAIR * CHUNKS)[None, :] % CHUNKS).astype(F32)
    return pl.pallas_call(
        _peer_v_kernel,
        grid=(n // tp,),
        in_specs=[pl.BlockSpec((tp, NPAIR), lambda i: (i + o, 0), memory_space=pltpu.SMEM),
                  pl.BlockSpec((tp, NPAIR), lambda i: (i, 0)),
                  pl.BlockSpec((NPAIR, NPAIR * CHUNKS), lambda i: (0, 0)),
                  pl.BlockSpec((CHUNKS, NPAIR * CHUNKS), lambda i: (0, 0)),
                  pl.BlockSpec(memory_space=pltpu.VMEM)],
        out_specs=pl.BlockSpec((tp, CHUNKS, LANES), lambda i: (i, 0, 0)),
        out_shape=jax.ShapeDtypeStruct((n, CHUNKS, LANES), F32),
        scratch_shapes=[pltpu.VMEM((TOK_UNROLL, NPAIR * WORD_ROWS, LANES), I32),
                        pltpu.VMEM((tp, NPAIR * CHUNKS), F32)],
        compiler_params=_cparams(("arbitrary",)),
        name="peer_v",
    )(idx, w, rep, diag, tab)


SC_LANES = 16
SC_WORKERS = 32
SC_ROWS = 64
SC_TOKENS = 16
SC_GROUP_SHARES, SC_SHARE_DEN = ((2, 3, 5, 6), (2, 3, 5, 6), (2, 3, 5, 6), (2, 3, 5)), 16
SC_COLS = 8


def _peer_v_sc(ids, wts, packed, start):
    m = wts.shape[0]
    per_worker = m // SC_WORKERS
    blocks = NPAIR // SC_ROWS
    words = D_MODEL // 2
    table = packed.reshape(packed.shape[0] // WORD_ROWS, words)
    ids_b = ids.reshape(ids.shape[0] * blocks, SC_ROWS)
    wts_f = wts.reshape(m * NPAIR)
    mesh = plsc.VectorSubcoreMesh(core_axis_name="c", subcore_axis_name="s")

    @functools.partial(
        pl.kernel, mesh=mesh, out_type=jax.ShapeDtypeStruct((m, D_MODEL), F32),
        scratch_types=[pltpu.VMEM((SC_TOKENS * blocks, SC_ROWS), I32), pltpu.VMEM((SC_TOKENS * NPAIR,), F32),
                       pltpu.VMEM((2, SC_ROWS, words), I32), pltpu.VMEM((D_MODEL,), F32),
                       pltpu.SemaphoreType.DMA((2,))],
        compiler_params=pltpu.CompilerParams(needs_layout_passes=False),
        name="peer_v_sc")
    def run(tab_hbm, ids_hbm, w_hbm, out_hbm, ids_v, w_v, rows_v, acc_v, sems):
        wid = lax.axis_index("s") * 2 + lax.axis_index("c")
        base = wid * per_worker
        zero = jnp.zeros((SC_LANES,), F32)

        def gather(u, b):
            return pltpu.make_async_copy(tab_hbm.at[ids_v.at[u * blocks + b]], rows_v.at[b % 2], sems.at[b % 2])

        @pl.loop(0, per_worker // SC_TOKENS)
        def _(tb):
            t0 = base + tb * SC_TOKENS
            pltpu.sync_copy(ids_hbm.at[pl.ds((start + t0) * blocks, SC_TOKENS * blocks)], ids_v)
            pltpu.sync_copy(w_hbm.at[pl.ds(t0 * NPAIR, SC_TOKENS * NPAIR)], w_v)

            @pl.loop(0, SC_TOKENS)
            def _(u):
                for j in range(D_MODEL // SC_LANES):
                    acc_v[pl.ds(j * SC_LANES, SC_LANES)] = zero
                gather(u, 0).start()
                for b in range(blocks):
                    if b + 1 < blocks:
                        gather(u, b + 1).start()
                    gather(u, b).wait()

                    for c0 in range(0, words // SC_LANES, SC_COLS):
                        def row(r, accs, b=b, c0=c0):
                            wk = plsc.load_gather(w_v, [jnp.full((SC_LANES,), u * NPAIR + b * SC_ROWS + r, I32)])
                            out = []
                            for j in range(SC_COLS):
                                w32 = rows_v[b % 2, r, pl.ds((c0 + j) * SC_LANES, SC_LANES)]
                                lo = lax.bitcast_convert_type(lax.shift_left(w32, 16), F32)
                                hi = lax.bitcast_convert_type(w32 & -65536, F32)
                                out += [accs[2 * j] + wk * lo, accs[2 * j + 1] + wk * hi]
                            return tuple(out)
                        accs = lax.fori_loop(0, SC_ROWS, row, (zero,) * (2 * SC_COLS))
                        for j in range(SC_COLS):
                            s_, l0 = divmod((c0 + j) * SC_LANES, LANES)
                            plsc.addupdate(acc_v.at[pl.ds(2 * s_ * LANES + l0, SC_LANES)], accs[2 * j])
                            plsc.addupdate(acc_v.at[pl.ds((2 * s_ + 1) * LANES + l0, SC_LANES)], accs[2 * j + 1])
                pltpu.sync_copy(acc_v, out_hbm.at[t0 + u])

    return run(table, ids_b, wts_f)


def _final_kernel(x1_ref, *refs, ends, has_tc, has_prev):
    refs = list(refs)
    o_ref = refs.pop()
    if has_prev:
        refs.pop()
    g_ref, gt_ref = refs.pop(), refs.pop()
    ptc_ref = refs.pop() if has_tc else None
    i = pl.program_id(0)
    if has_tc:
        p = jnp.concatenate([ptc_ref[:, c, :] for c in range(CHUNKS)], axis=1)
        parts = list(zip(refs, ends))
    else:
        p = refs[-1][...]
        parts = list(zip(refs[:-1], ends[:-1]))
    for ref, end in reversed(parts):
        p = jnp.where(i < end, ref[...], p)
    o_ref[...] = _rms(x1_ref[...] + gt_ref[0] * p, g_ref[...])


FINAL_TILE = 512


def _final(x1, sc_parts, peer_tc, gt, g, out_prev, tile0, n_total, tm=FINAL_TILE):
    m = x1.shape[0]
    per_seq = m // gt.shape[0] // tm
    sizes = [p.shape[0] // tm for p in sc_parts]
    ends = [sum(sizes[:j + 1]) for j in range(len(sizes))]
    starts = [e - k for e, k in zip(ends, sizes)]
    tile = pl.BlockSpec((tm, D_MODEL), lambda i: (i, 0))
    part = lambda s0, k: pl.BlockSpec((tm, D_MODEL), lambda i: (jnp.clip(i - s0, 0, k - 1), 0))
    in_specs = [tile] + [part(s0, k) for s0, k in zip(starts, sizes)]
    args = [x1, *sc_parts]
    if peer_tc is not None:
        in_specs.append(pl.BlockSpec((tm, CHUNKS, LANES), lambda i: (jnp.maximum(i - ends[-1], 0), 0, 0)))
        args.append(peer_tc)
    in_specs += [pl.BlockSpec((1, 1, D_MODEL), lambda i: (i // per_seq, 0, 0)),
                 pl.BlockSpec((1, D_MODEL), lambda i: (0, 0))]
    args += [gt, g.reshape(1, D_MODEL)]
    aliases = {}
    if out_prev is not None:
        in_specs.append(pl.BlockSpec(memory_space=pl.ANY))
        aliases = {len(args): 0}
        args.append(out_prev)
    return pl.pallas_call(
        functools.partial(_final_kernel, ends=tuple(ends), has_tc=peer_tc is not None, has_prev=out_prev is not None),
        grid=(m // tm,),
        in_specs=in_specs,
        out_specs=pl.BlockSpec((tm, D_MODEL), lambda i: (i + tile0, 0)),
        out_shape=jax.ShapeDtypeStruct((n_total, D_MODEL), F32),
        input_output_aliases=aliases,
        compiler_params=_cparams(("parallel",)),
        name="final",
    )(*args)


def _block_diag(width, group, value):
    i = jnp.arange(width) // group
    return jnp.where(i[:, None] == i[None, :], value, 0.0).astype(F32)


def _layer(x, mod, final_g, norm_mix_g, w_in, conv_dw_w, conv_dw_b, conv_ln_w, conv_ln_b, rwkv_mu, rwkv_w0, rwkv_w2,
           rwkv_a0, rwkv_a2, rwkv_g2, rwkv_k_k, rwkv_k_a, rwkv_r_k, rwkv_gn_w, rwkv_gn_b, w_out, norm_ffn_g,
           peer_w_q, peer_sub_keys, peer_u, peer_v):
    bsz, s, _ = x.shape
    sh_mix, sc_mix, gt_mix, sh_ffn, sc_ffn, gt_ffn = (
        mod[:, i * D_MODEL:(i + 1) * D_MODEL].reshape(bsz, 1, D_MODEL) for i in range(6))

    w_in_b, w_out_b, w_q_b = w_in.astype(BF16), w_out.astype(BF16), peer_w_q.astype(BF16)
    zpad = jnp.zeros((LORA_W, RWKV_CH), F32)
    w2p, a2p = jnp.concatenate([rwkv_w2, zpad], axis=0), jnp.concatenate([zpad, rwkv_a2], axis=0)
    keys = peer_sub_keys.reshape(2 * PEER_HEADS, PEER_NKEYS, PEER_DQ // 2)
    tab_u, tab_v = _pack_table(peer_u), _pack_table(peer_v)

    nb = bsz // len(SC_GROUP_SHARES)
    n, m = bsz * s, nb * s
    out, after = None, None
    for gi, shares in enumerate(SC_GROUP_SHARES):
        b0 = gi * nb
        grp = lambda t: t[b0:b0 + nb]
        xg = x if after is None else lax.optimization_barrier((x, after))[0]
        yglu, prw = _in_proj(xg, grp(sh_mix), grp(sc_mix), norm_mix_g, w_in_b, b0)
        y_conv = _conv(yglu, conv_dw_w, conv_dw_b, conv_ln_w, conv_ln_b)
        y_rwkv = _rwkv(prw, rwkv_mu, rwkv_w0, w2p, rwkv_a0, a2p, rwkv_g2, rwkv_k_k, rwkv_k_a, rwkv_r_k.reshape(-1),
                       rwkv_gn_w, rwkv_gn_b)
        x1, u2 = _out_proj(y_conv, y_rwkv, w_out_b, xg, grp(gt_mix), norm_ffn_g, grp(sh_ffn), grp(sc_ffn), b0)
        u3 = u2.reshape(m, CHUNKS, LANES)
        idx, gate = _route(u3, w_q_b, keys)
        ids = lax.shift_right_logical(idx, 2)
        start, sc_parts = 0, []
        for share in shares:
            cnt = m * share // SC_SHARE_DEN
            after = _peer_u(idx, u3, gate, tab_u, start, cnt)
            sc_parts.append(_peer_v_sc(ids, after, tab_v, start))
            start += cnt
        peer_tc = None
        if start < m:
            after = _peer_u(idx, u3, gate, tab_u, start, m - start)
            peer_tc = _peer_v(idx, after, tab_v, start)
        out = _final(x1.reshape(m, D_MODEL), sc_parts, peer_tc, grp(gt_ffn), final_g, out,
                     gi * (m // FINAL_TILE), n)
    return out.reshape(bsz, s, D_MODEL)


def kernel(x, c, ada_w, ada_b, norm_mix_g, w_in, conv_dw_w, conv_dw_b, conv_ln_w, conv_ln_b, rwkv_mu, rwkv_w0,
           rwkv_w2, rwkv_a0, rwkv_a2, rwkv_g2, rwkv_k_k, rwkv_k_a, rwkv_r_k, rwkv_gn_w, rwkv_gn_b, w_out,
           norm_ffn_g, peer_w_q, peer_sub_keys, peer_u, peer_v, final_g):
    depth = ada_w.shape[0]
    assert depth == 1, "one layer: the final norm is fused into the last layer's residual"
    mod = _mod(c, ada_w[0], ada_b[0])
    return _layer(x, mod, final_g, norm_mix_g[0], w_in[0], conv_dw_w[0], conv_dw_b[0], conv_ln_w[0],
                        conv_ln_b[0], rwkv_mu[0], rwkv_w0[0], rwkv_w2[0], rwkv_a0[0], rwkv_a2[0], rwkv_g2[0],
                        rwkv_k_k[0], rwkv_k_a[0], rwkv_r_k[0], rwkv_gn_w[0], rwkv_gn_b[0], w_out[0],
                        norm_ffn_g[0], peer_w_q[0], peer_sub_keys[0], peer_u[0], peer_v[0])
```
